```python
import jax
import jax.numpy as jnp
from jax import lax
import numpy as np

D_MODEL = 2048
BATCH = 16
SEQ = 2048
DEPTH = 2

HEAD_DIM = 64
NORM_EPS = 1e-6
FFN_HIDDEN = ((8 * D_MODEL + 3 * 256 - 1) // (3 * 256)) * 256

MOBA_HEADS = D_MODEL // (2 * HEAD_DIM)
MOBA_BLOCK = 256
MOBA_TOPK = 3
MOBA_Q_CHUNK = 16
RWKV_HEADS = D_MODEL // (2 * HEAD_DIM)
RWKV_DECAY_LORA = 64
RWKV_ICLR_LORA = 64
RWKV_GATE_LORA = 128
RWKV_GN_EPS = 6.4e-4
MOBA_W = MOBA_HEADS * HEAD_DIM
RWKV_W = RWKV_HEADS * HEAD_DIM
RWKV_SPLITS = (RWKV_W, RWKV_W, RWKV_W, RWKV_DECAY_LORA, RWKV_ICLR_LORA, RWKV_GATE_LORA)
RWKV_SHIFT_W = sum(RWKV_SPLITS)
EVEN_SPLITS = (MOBA_W, MOBA_W, MOBA_W, RWKV_SHIFT_W)
EVEN_IN = sum(EVEN_SPLITS)

RET_QK_DIM = HEAD_DIM
RET_V_DIM = 2 * HEAD_DIM
RET_HEADS = D_MODEL // (2 * RET_V_DIM)
RET_CHUNK = 128
RET_GN_EPS = 1e-6
ROPE_BASE = 10000.0
RET_W = RET_HEADS * RET_V_DIM
NSA_HEADS = D_MODEL // (2 * HEAD_DIM)
NSA_KV_GROUPS = 4
NSA_CMP_BLOCK = 32
NSA_CMP_STRIDE = 16
NSA_CMP_HIDDEN = 128
NSA_SLC_BLOCK = 64
NSA_SLC_TOPN = 16
NSA_WINDOW = 512
NSA_Q_CHUNK = 32
WIN_Q_BLOCK = 128
NSA_W = NSA_HEADS * HEAD_DIM
NSA_KV_W = NSA_KV_GROUPS * HEAD_DIM
ODD_SPLITS = (RET_HEADS * RET_QK_DIM, RET_HEADS * RET_QK_DIM, RET_W, RET_W, NSA_W) + (NSA_KV_W,) * 6 + (3 * NSA_HEADS,)
ODD_IN = sum(ODD_SPLITS)
N_EVEN = (DEPTH + 1) // 2
N_ODD = DEPTH // 2

kernel_name = 'moba_rwkv7_retnet_nsa_hybrid'


def split_cols(z, sizes):
    return jnp.split(z, [int(c) for c in np.cumsum(sizes)[:-1]], axis=-1)


def rms_norm(x, g):
    xf = x.astype(jnp.float32)
    y = xf * lax.rsqrt(jnp.mean(xf * xf, axis=-1, keepdims=True) + NORM_EPS)
    return (y * g.astype(jnp.float32)).astype(x.dtype)


def head_group_norm(y, eps):
    yf = y.astype(jnp.float32)
    mu = jnp.mean(yf, axis=-1, keepdims=True)
    var = jnp.mean(jnp.square(yf - mu), axis=-1, keepdims=True)
    return (yf - mu) * lax.rsqrt(var + eps)


def masked_softmax(s, mask):
    s = jnp.where(mask, s.astype(jnp.float32), -jnp.inf)
    m = jnp.max(s, axis=-1, keepdims=True)
    p = jnp.exp(s - jnp.where(jnp.isfinite(m), m, 0.0))
    den = jnp.sum(p, axis=-1, keepdims=True)
    return p / jnp.where(den > 0.0, den, 1.0)


def token_shift(z):
    return jnp.pad(z[:, :-1], ((0, 0), (1, 0), (0, 0)))


def swiglu(h, w1, w3, w2):
    return (jax.nn.silu(h @ w1) * (h @ w3)) @ w2


def rotary(z):
    S, d = z.shape[1], z.shape[-1]
    half = d // 2
    inv = ROPE_BASE ** (-jnp.arange(half, dtype=jnp.float32) / half)
    ang = jnp.arange(S, dtype=jnp.float32)[:, None] * inv
    cos, sin = jnp.cos(ang)[None, :, None, :], jnp.sin(ang)[None, :, None, :]
    z1, z2 = z[..., :half], z[..., half:]
    return jnp.concatenate([z1 * cos - z2 * sin, z1 * sin + z2 * cos], axis=-1)


def moba_attention(q, k, v):
    B, S, H, Dh = q.shape
    L, Qc = MOBA_BLOCK, MOBA_Q_CHUNK
    nb = -(-S // L)
    n_sel = min(MOBA_TOPK, nb - 1)
    nq = S // Qc
    pad = nb * L - S
    qh = q.transpose(0, 2, 1, 3) * (Dh ** -0.5)
    pad_blocks = lambda z: jnp.pad(z.transpose(0, 2, 1, 3), ((0, 0), (0, 0), (0, pad), (0, 0))).reshape(B, H, nb, L, Dh)
    kb, vb = pad_blocks(k), pad_blocks(v)
    to_chunks = lambda z: jnp.moveaxis(z.reshape(B, H, nq, Qc, *z.shape[3:]), 2, 0)
    xs = [jnp.arange(nq), to_chunks(qh)]
    if n_sel > 0:
        t = jnp.arange(S)
        gate = jnp.einsum('bhtd,bhnd->bhtn', qh, jnp.mean(kb, axis=3)).astype(jnp.float32)
        past = jnp.arange(nb)[None, :] < (t // L)[:, None]
        sel_score, sel_idx = lax.top_k(jnp.where(past, gate, -jnp.inf), n_sel)
        xs += [to_chunks(sel_idx), to_chunks(sel_score > -jnp.inf)]
    bi = jnp.arange(B)[:, None, None, None]
    hi = jnp.arange(H)[None, :, None, None]

    def chunk(args):
        c, q_c = args[0], args[1]
        t_c = c * Qc + jnp.arange(Qc)
        blk = (c * Qc) // L
        k_own = lax.dynamic_index_in_dim(kb, blk, axis=2, keepdims=False)
        v_own = lax.dynamic_index_in_dim(vb, blk, axis=2, keepdims=False)
        scores = [jnp.einsum('bhqd,bhld->bhql', q_c, k_own)]
        masks = [jnp.broadcast_to(blk * L + jnp.arange(L) <= t_c[:, None], (B, H, Qc, L))]
        if n_sel > 0:
            idx_c, ok_c = args[2], args[3]
            k_sel = kb[bi, hi, idx_c]
            v_sel = vb[bi, hi, idx_c]
            scores.append(jnp.einsum('bhqd,bhqnld->bhqnl', q_c, k_sel).reshape(B, H, Qc, n_sel * L))
            masks.append(jnp.repeat(ok_c, L, axis=-1))
        p = masked_softmax(jnp.concatenate(scores, axis=-1), jnp.concatenate(masks, axis=-1))
        out = jnp.einsum('bhql,bhld->bhqd', p[..., :L], v_own)
        if n_sel > 0:
            out = out + jnp.einsum('bhqnl,bhqnld->bhqd', p[..., L:].reshape(B, H, Qc, n_sel, L), v_sel)
        return out

    o = lax.map(chunk, tuple(xs))
    o = jnp.moveaxis(o, 0, 2).reshape(B, H, S, Dh)
    return o.transpose(0, 2, 1, 3).reshape(B, S, H * Dh)


def rwkv7_time_mix(r, k, v, w_lo, a_lo, g_lo, w0, w_up, a0, a_up, g_up, k_k, k_a, r_k, ln_g, ln_b):
    B, S, C = r.shape
    H, N = RWKV_HEADS, C // RWKV_HEADS
    f32 = jnp.float32
    w = -jax.nn.softplus(-(w0 + jnp.tanh(w_lo) @ w_up)) - 0.5
    decay = jnp.exp(-jnp.exp(w.astype(f32)))
    a = jax.nn.sigmoid(a0 + a_lo @ a_up)
    g = jax.nn.sigmoid(g_lo) @ g_up
    hd = lambda u: u.reshape(B, S, H, N).astype(f32)
    kk = hd(k * k_k)
    kk = kk / jnp.maximum(jnp.sqrt(jnp.sum(kk * kk, axis=-1, keepdims=True)), 1e-12)
    k = k * (1.0 + (a - 1.0) * k_a)
    rh, kh, vh, ah, wh = hd(r), hd(k), hd(v), hd(a), hd(decay)

    def step(state, inp):
        r_t, w_t, k_t, v_t, kk_t, a_t = inp
        sa = jnp.einsum('bhij,bhj->bhi', state, -kk_t)
        state = (state * w_t[:, :, None, :]
                 + sa[..., None] * (kk_t * a_t)[:, :, None, :]
                 + v_t[..., None] * k_t[:, :, None, :])
        return state, jnp.einsum('bhij,bhj->bhi', state, r_t)

    xs = tuple(jnp.moveaxis(u, 1, 0) for u in (rh, wh, kh, vh, kk, ah))
    _, y = lax.scan(step, jnp.zeros((B, H, N, N), f32), xs)
    y = jnp.moveaxis(y, 0, 1)
    y = head_group_norm(y, RWKV_GN_EPS) * ln_g.reshape(H, N) + ln_b.reshape(H, N)
    bonus = jnp.sum(rh * kh * r_k, axis=-1, keepdims=True) * vh
    return ((y + bonus).reshape(B, S, C) * g).astype(r.dtype)


def retention(q, k, v):
    B, S, H, dk = q.shape
    dv = v.shape[-1]
    C = RET_CHUNK
    nc = S // C
    f32 = jnp.float32
    q = rotary(q.astype(f32))
    k = rotary(k.astype(f32)) * (dk ** -0.5)
    v = v.astype(f32)
    log_g = jnp.asarray(np.log(1.0 - 2.0 ** (-5.0 - np.arange(H))), f32)
    n = jnp.arange(C, dtype=f32)
    diff = n[:, None] - n[None, :]
    decay_in = jnp.where(diff >= 0, jnp.exp(jnp.maximum(diff, 0.0) * log_g[:, None, None]), 0.0)
    decay_q = jnp.exp((n + 1.0) * log_g[:, None])[None, :, :, None]
    decay_k = jnp.exp((C - 1.0 - n) * log_g[:, None])[None, :, :, None]
    decay_c = jnp.exp(C * log_g)[None, :, None, None]
    chunks = lambda z: z.reshape(B, nc, C, H, z.shape[-1]).transpose(1, 0, 3, 2, 4)

    def step(state, inp):
        qi, ki, vi = inp
        inner = jnp.einsum('bhnd,bhmd->bhnm', qi, ki) * decay_in
        o = jnp.einsum('bhnm,bhme->bhne', inner, vi) + jnp.einsum('bhnd,bhde->bhne', qi, state) * decay_q
        state = jnp.einsum('bhmd,bhme->bhde', ki * decay_k, vi) + decay_c * state
        return state, o

    _, o = lax.scan(step, jnp.zeros((B, H, dk, dv), f32), (chunks(q), chunks(k), chunks(v)))
    o = o.transpose(1, 0, 3, 2, 4).reshape(B, S, H, dv)
    return head_group_norm(o, RET_GN_EPS)


def nsa_attention(q, k_cmp_in, v_cmp_in, k_slc, v_slc, k_win, v_win, gates, pe_k, w1_k, w2_k, pe_v, w1_v, w2_v):
    B, S, H, Dh = q.shape
    G = NSA_KV_GROUPS
    R = H // G
    Lc, st, Ls, W = NSA_CMP_BLOCK, NSA_CMP_STRIDE, NSA_SLC_BLOCK, NSA_WINDOW
    t = jnp.arange(S)
    qg = (q * (Dh ** -0.5)).reshape(B, S, G, R, Dh).transpose(0, 2, 3, 1, 4)
    kv = lambda z: z.transpose(0, 2, 1, 3)

    nc = (S - Lc) // st + 1
    c_start = np.arange(nc) * st
    c_idx = c_start[:, None] + np.arange(Lc)[None, :]

    def compress(z, pe, w1, w2):
        blocks = kv(z)[:, :, c_idx] + pe
        return jax.nn.gelu(blocks.reshape(B, G, nc, Lc * Dh) @ w1) @ w2

    k_c = compress(k_cmp_in, pe_k, w1_k, w2_k)
    v_c = compress(v_cmp_in, pe_v, w1_v, w2_v)
    c_end = jnp.asarray(c_start + Lc - 1)
    p_cmp = masked_softmax(jnp.einsum('bgrtd,bgcd->bgrtc', qg, k_c), c_end[None, :] <= t[:, None])
    o_cmp = jnp.einsum('bgrtc,bgcd->bgrtd', p_cmp, v_c)

    ns = S // Ls
    s_start = np.arange(ns) * Ls
    overlap = ((c_start[:, None] <= s_start[None, :] + Ls - 1)
               & (c_start[:, None] + Lc - 1 >= s_start[None, :])).astype(np.float32)
    p_slc = jnp.einsum('bgrtc,cj->bgtj', p_cmp, jnp.asarray(overlap))
    j = jnp.arange(ns)[None, :]
    own = (t // Ls)[:, None]
    score = jnp.where((j == own) | (j == 0), jnp.inf, jnp.where(j > own, -jnp.inf, p_slc))
    n_top = min(NSA_SLC_TOPN, ns)
    sel_score, sel_idx = lax.top_k(score, n_top)
    sel_ok = sel_score > -jnp.inf
    ks_b = kv(k_slc).reshape(B, G, ns, Ls, Dh)
    vs_b = kv(v_slc).reshape(B, G, ns, Ls, Dh)
    Qc = NSA_Q_CHUNK
    nq = S // Qc
    bi = jnp.arange(B)[:, None, None, None]
    gi = jnp.arange(G)[None, :, None, None]

    def sel_chunk(args):
        c, q_c, idx_c, ok_c = args
        t_c = c * Qc + jnp.arange(Qc)
        k_s = ks_b[bi, gi, idx_c]
        v_s = vs_b[bi, gi, idx_c]
        pos = idx_c[..., None] * Ls + jnp.arange(Ls)
        mask = ok_c[..., None] & (pos <= t_c[None, None, :, None, None])
        s = jnp.einsum('bgrqd,bgqnld->bgrqnl', q_c, k_s).reshape(B, G, R, Qc, n_top * Ls)
        p = masked_softmax(s, mask.reshape(B, G, 1, Qc, n_top * Ls))
        return jnp.einsum('bgrqnl,bgqnld->bgrqd', p.reshape(B, G, R, Qc, n_top, Ls), v_s)

    o_slc = lax.map(sel_chunk, (jnp.arange(nq),
                                jnp.moveaxis(qg.reshape(B, G, R, nq, Qc, Dh), 3, 0),
                                jnp.moveaxis(sel_idx.reshape(B, G, nq, Qc, n_top), 2, 0),
                                jnp.moveaxis(sel_ok.reshape(B, G, nq, Qc, n_top), 2, 0)))
    o_slc = jnp.moveaxis(o_slc, 0, 3).reshape(B, G, R, S, Dh)

    Qw = WIN_Q_BLOCK
    nw = S // Qw
    kw_p = jnp.pad(kv(k_win), ((0, 0), (0, 0), (W, 0), (0, 0)))
    vw_p = jnp.pad(kv(v_win), ((0, 0), (0, 0), (W, 0), (0, 0)))

    def win_block(args):
        c, q_c = args
        s0 = c * Qw
        k_w = lax.dynamic_slice_in_dim(kw_p, s0, Qw + W, axis=2)
        v_w = lax.dynamic_slice_in_dim(vw_p, s0, Qw + W, axis=2)
        t_c = s0 + jnp.arange(Qw)
        s_pos = s0 - W + jnp.arange(Qw + W)
        d = t_c[:, None] - s_pos[None, :]
        mask = (d >= 0) & (d < W) & (s_pos[None, :] >= 0)
        p = masked_softmax(jnp.einsum('bgrqd,bgkd->bgrqk', q_c, k_w), mask)
        return jnp.einsum('bgrqk,bgkd->bgrqd', p, v_w)

    o_win = lax.map(win_block, (jnp.arange(nw), jnp.moveaxis(qg.reshape(B, G, R, nw, Qw, Dh), 3, 0)))
    o_win = jnp.moveaxis(o_win, 0, 3).reshape(B, G, R, S, Dh)

    g = jnp.moveaxis(gates, 1, 2).reshape(B, G, R, S, 3)
    o = g[..., 0:1] * o_cmp + g[..., 1:2] * o_slc + g[..., 2:3] * o_win
    return o.reshape(B, H, S, Dh).transpose(0, 2, 1, 3).reshape(B, S, H * Dh)


def even_mixer(h, w_in, shift_mu, w0, w_up, a0, a_up, g_up, k_k, k_a, r_k, ln_g, ln_b, w_out):
    B, S, _ = h.shape
    qa, ka, va, zb = split_cols(h @ w_in, EVEN_SPLITS)
    heads = lambda u: u.reshape(B, S, -1, HEAD_DIM)
    o_a = moba_attention(heads(qa), heads(ka), heads(va))
    zb = zb + (token_shift(zb) - zb) * shift_mu
    r, kb, vb, w_lo, a_lo, g_lo = split_cols(zb, RWKV_SPLITS)
    o_b = rwkv7_time_mix(r, kb, vb, w_lo, a_lo, g_lo, w0, w_up, a0, a_up, g_up, k_k, k_a, r_k, ln_g, ln_b)
    return (jnp.concatenate([o_a, o_b], axis=-1) @ w_out).astype(h.dtype)


def odd_mixer(h, w_in, pe_k, w1_k, w2_k, pe_v, w1_v, w2_v, w_out):
    B, S, _ = h.shape
    rq, rk, rv, rg, nq, kc, vc, ks, vs, kw, vw, ng = split_cols(h @ w_in, ODD_SPLITS)
    o_c = retention(rq.reshape(B, S, RET_HEADS, RET_QK_DIM), rk.reshape(B, S, RET_HEADS, RET_QK_DIM),
                    rv.reshape(B, S, RET_HEADS, RET_V_DIM))
    o_c = jax.nn.silu(rg) * o_c.reshape(B, S, RET_W)
    hd = lambda u: u.reshape(B, S, -1, HEAD_DIM)
    gates = jax.nn.sigmoid(ng.reshape(B, S, NSA_HEADS, 3))
    o_d = nsa_attention(hd(nq), hd(kc), hd(vc), hd(ks), hd(vs), hd(kw), hd(vw), gates,
                        pe_k, w1_k, w2_k, pe_v, w1_v, w2_v)
    return (jnp.concatenate([o_c, o_d], axis=-1) @ w_out).astype(h.dtype)


def setup_inputs(seed: int = 0) -> dict:
    key = jax.random.key(seed)
    keys = list(jax.random.split(key, 32))

    def nrm(shape, scale):
        return jax.random.normal(keys.pop(), shape, jnp.float32) * scale

    def uni(shape, lo, hi):
        return jax.random.uniform(keys.pop(), shape, jnp.float32, lo, hi)

    E, O, D = N_EVEN, N_ODD, D_MODEL
    cmp_in = NSA_CMP_BLOCK * HEAD_DIM
    return {
        'x': nrm((BATCH, SEQ, D), 1.0),
        'mix_norm': 1.0 + nrm((DEPTH, D), 0.02),
        'ffn_norm': 1.0 + nrm((DEPTH, D), 0.02),
        'even_w_in': nrm((E, D, EVEN_IN), D ** -0.5),
        'even_shift_mu': uni((E, RWKV_SHIFT_W), 0.0, 1.0),
        'even_w0': uni((E, RWKV_W), -6.0, -1.0),
        'even_w_up': nrm((E, RWKV_DECAY_LORA, RWKV_W), 0.1 * RWKV_DECAY_LORA ** -0.5),
        'even_a0': nrm((E, RWKV_W), 0.1),
        'even_a_up': nrm((E, RWKV_ICLR_LORA, RWKV_W), 0.1 * RWKV_ICLR_LORA ** -0.5),
        'even_g_up': nrm((E, RWKV_GATE_LORA, RWKV_W), RWKV_GATE_LORA ** -0.5),
        'even_k_k': 0.85 + nrm((E, RWKV_W), 0.05),
        'even_k_a': 1.0 + nrm((E, RWKV_W), 0.05),
        'even_r_k': nrm((E, RWKV_HEADS, HEAD_DIM), 0.1),
        'even_ln_g': 1.0 + nrm((E, RWKV_W), 0.02),
        'even_ln_b': nrm((E, RWKV_W), 0.02),
        'even_w_out': nrm((E, MOBA_W + RWKV_W, D), (MOBA_W + RWKV_W) ** -0.5),
        'odd_w_in': nrm((O, D, ODD_IN), D ** -0.5),
        'odd_cmp_pe_k': nrm((O, NSA_CMP_BLOCK, HEAD_DIM), 0.02),
        'odd_cmp_w1_k': nrm((O, cmp_in, NSA_CMP_HIDDEN), cmp_in ** -0.5),
        'odd_cmp_w2_k': nrm((O, NSA_CMP_HIDDEN, HEAD_DIM), NSA_CMP_HIDDEN ** -0.5),
        'odd_cmp_pe_v': nrm((O, NSA_CMP_BLOCK, HEAD_DIM), 0.02),
        'odd_cmp_w1_v': nrm((O, cmp_in, NSA_CMP_HIDDEN), cmp_in ** -0.5),
        'odd_cmp_w2_v': nrm((O, NSA_CMP_HIDDEN, HEAD_DIM), NSA_CMP_HIDDEN ** -0.5),
        'odd_w_out': nrm((O, RET_W + NSA_W, D), (RET_W + NSA_W) ** -0.5),
        'ffn_w1': nrm((DEPTH, D, FFN_HIDDEN), D ** -0.5),
        'ffn_w3': nrm((DEPTH, D, FFN_HIDDEN), D ** -0.5),
        'ffn_w2': nrm((DEPTH, FFN_HIDDEN, D), FFN_HIDDEN ** -0.5),
        'final_norm': 1.0 + nrm((D,), 0.02),
    }


def reference(x, mix_norm, ffn_norm, even_w_in, even_shift_mu, even_w0, even_w_up, even_a0, even_a_up,
              even_g_up, even_k_k, even_k_a, even_r_k, even_ln_g, even_ln_b, even_w_out, odd_w_in,
              odd_cmp_pe_k, odd_cmp_w1_k, odd_cmp_w2_k, odd_cmp_pe_v, odd_cmp_w1_v, odd_cmp_w2_v, odd_w_out,
              ffn_w1, ffn_w3, ffn_w2, final_norm):
    for layer in range(DEPTH):
        i = layer // 2
        h = rms_norm(x, mix_norm[layer])
        if layer % 2 == 0:
            mix = even_mixer(h, even_w_in[i], even_shift_mu[i], even_w0[i], even_w_up[i], even_a0[i],
                             even_a_up[i], even_g_up[i], even_k_k[i], even_k_a[i], even_r_k[i],
                             even_ln_g[i], even_ln_b[i], even_w_out[i])
        else:
            mix = odd_mixer(h, odd_w_in[i], odd_cmp_pe_k[i], odd_cmp_w1_k[i], odd_cmp_w2_k[i],
                            odd_cmp_pe_v[i], odd_cmp_w1_v[i], odd_cmp_w2_v[i], odd_w_out[i])
        x = x + mix
        x = x + swiglu(rms_norm(x, ffn_norm[layer]), ffn_w1[layer], ffn_w3[layer], ffn_w2[layer]).astype(x.dtype)
    return rms_norm(x, final_norm)
```

```python
import functools

import jax
import jax.numpy as jnp
import numpy as np
from jax import lax
from jax.experimental import pallas as pl
from jax.experimental.pallas import tpu as pltpu

F32 = jnp.float32
BF16 = jnp.bfloat16

V7X_LANES = 128
V7X_VMEM_BYTES = 64 * 1024 * 1024
VMEM_LIMIT = 56 * 1024 * 1024

NORM_EPS = 1e-6
HEAD_DIM = 64

MOBA_BLOCK = 256
MOBA_TOPK = 3
MOBA_Q_CHUNK = 16
RWKV_HEADS = 16
RWKV_GN_EPS = 6.4e-4

RET_HEADS = 8
RET_QK_DIM = 64
RET_V_DIM = 128
RET_CHUNK = 128
RET_GN_EPS = 1e-6
ROPE_BASE = 10000.0
NSA_HEADS = 16
NSA_KV_GROUPS = 4
NSA_CMP_BLOCK = 32
NSA_CMP_STRIDE = 16
NSA_SLC_BLOCK = 64
NSA_SLC_TOPN = 16
NSA_WINDOW = 512
NSA_Q_CHUNK = 32
WIN_Q_BLOCK = 128


def _params(*semantics):
    return pltpu.CompilerParams(dimension_semantics=semantics, vmem_limit_bytes=VMEM_LIMIT)


def _rms(x, g):
    return x * lax.rsqrt(jnp.mean(x * x, axis=-1, keepdims=True) + NORM_EPS) * g


def _norm_matmul_kernel(x_ref, g_ref, w_ref, o_ref, h_ref):
    @pl.when(pl.program_id(1) == 0)
    def _():
        h_ref[...] = _rms(x_ref[...], g_ref[...]).astype(BF16)

    o_ref[...] = jnp.dot(h_ref[...], w_ref[...], preferred_element_type=F32)


def norm_matmul(x, g, w, *, tm=512, tn=640):
    T, D = x.shape
    N = w.shape[1]
    assert T % tm == 0 and N % tn == 0
    return pl.pallas_call(
        _norm_matmul_kernel,
        grid=(T // tm, N // tn),
        in_specs=[
            pl.BlockSpec((tm, D), lambda i, j: (i, 0)),
            pl.BlockSpec((1, D), lambda i, j: (0, 0)),
            pl.BlockSpec((D, tn), lambda i, j: (0, j)),
        ],
        out_specs=pl.BlockSpec((tm, tn), lambda i, j: (i, j)),
        out_shape=jax.ShapeDtypeStruct((T, N), F32),
        scratch_shapes=[pltpu.VMEM((tm, D), BF16)],
        compiler_params=_params("parallel", "arbitrary"),
        name="norm_matmul",
    )(x, g.reshape(1, D), w)


def _out_proj_kernel(a_ref, b_ref, wa_ref, wb_ref, x_ref, o_ref):
    acc = jnp.dot(a_ref[...].astype(BF16), wa_ref[...], preferred_element_type=F32)
    acc += jnp.dot(b_ref[...].astype(BF16), wb_ref[...], preferred_element_type=F32)
    o_ref[...] = x_ref[...] + acc


def out_proj_residual(a, b, w, x, *, tm=512):
    T, D = x.shape
    Ka, Kb = a.shape[1], b.shape[1]
    assert T % tm == 0 and w.shape == (Ka + Kb, D)
    return pl.pallas_call(
        _out_proj_kernel,
        grid=(T // tm,),
        in_specs=[
            pl.BlockSpec((tm, Ka), lambda i: (i, 0)),
            pl.BlockSpec((tm, Kb), lambda i: (i, 0)),
            pl.BlockSpec((Ka, D), lambda i: (0, 0)),
            pl.BlockSpec((Kb, D), lambda i: (0, 0)),
            pl.BlockSpec((tm, D), lambda i: (i, 0)),
        ],
        out_specs=pl.BlockSpec((tm, D), lambda i: (i, 0)),
        out_shape=jax.ShapeDtypeStruct((T, D), F32),
        compiler_params=_params("parallel"),
        name="out_proj_residual",
    )(a, b, w[:Ka], w[Ka:], x)


def _ffn_kernel(x_ref, g_ref, w1_ref, w3_ref, w2_ref, gf_ref, o_ref, h_ref, acc_ref, *, final_norm):
    j = pl.program_id(1)

    @pl.when(j == 0)
    def _():
        h_ref[...] = _rms(x_ref[...], g_ref[...]).astype(BF16)
        acc_ref[...] = jnp.zeros_like(acc_ref)

    h = h_ref[...]
    a = jnp.dot(h, w1_ref[...], preferred_element_type=F32)
    b = jnp.dot(h, w3_ref[...], preferred_element_type=F32)
    act = (a * jax.nn.sigmoid(a) * b).astype(BF16)
    acc_ref[...] += jnp.dot(act, w2_ref[...], preferred_element_type=F32)

    @pl.when(j == pl.num_programs(1) - 1)
    def _():
        y = x_ref[...] + acc_ref[...]
        if final_norm:
            y = _rms(y, gf_ref[...])
        o_ref[...] = y


def ffn_residual(x, g, w1, w3, w2, g_final=None, *, tm=512, tf=512):
    T, D = x.shape
    Fh = w1.shape[1]
    assert T % tm == 0 and Fh % tf == 0
    final_norm = g_final is not None
    gf = (g_final if final_norm else g).reshape(1, D)
    return pl.pallas_call(
        functools.partial(_ffn_kernel, final_norm=final_norm),
        grid=(T // tm, Fh // tf),
        in_specs=[
            pl.BlockSpec((tm, D), lambda i, j: (i, 0)),
            pl.BlockSpec((1, D), lambda i, j: (0, 0)),
            pl.BlockSpec((D, tf), lambda i, j: (0, j)),
            pl.BlockSpec((D, tf), lambda i, j: (0, j)),
            pl.BlockSpec((tf, D), lambda i, j: (j, 0)),
            pl.BlockSpec((1, D), lambda i, j: (0, 0)),
        ],
        out_specs=pl.BlockSpec((tm, D), lambda i, j: (i, 0)),
        out_shape=jax.ShapeDtypeStruct((T, D), F32),
        scratch_shapes=[pltpu.VMEM((tm, D), BF16), pltpu.VMEM((tm, D), F32)],
        compiler_params=_params("parallel", "arbitrary"),
        name="ffn_residual",
    )(x, g.reshape(1, D), w1, w3, w2, gf)


def _split_cols(z, sizes):
    return jnp.split(z, [int(c) for c in np.cumsum(sizes)[:-1]], axis=-1)


def _head_group_norm(y, eps):
    mu = jnp.mean(y, axis=-1, keepdims=True)
    var = jnp.mean(jnp.square(y - mu), axis=-1, keepdims=True)
    return (y - mu) * lax.rsqrt(var + eps)


def _masked_softmax(s, mask):
    s = jnp.where(mask, s.astype(F32), -jnp.inf)
    m = jnp.max(s, axis=-1, keepdims=True)
    p = jnp.exp(s - jnp.where(jnp.isfinite(m), m, 0.0))
    den = jnp.sum(p, axis=-1, keepdims=True)
    return p / jnp.where(den > 0.0, den, 1.0)


def _token_shift(z):
    return jnp.pad(z[:, :-1], ((0, 0), (1, 0), (0, 0)))


def _rotary(z):
    S, d = z.shape[1], z.shape[-1]
    half = d // 2
    inv = ROPE_BASE ** (-jnp.arange(half, dtype=F32) / half)
    ang = jnp.arange(S, dtype=F32)[:, None] * inv
    cos, sin = jnp.cos(ang)[None, :, None, :], jnp.sin(ang)[None, :, None, :]
    z1, z2 = z[..., :half], z[..., half:]
    return jnp.concatenate([z1 * cos - z2 * sin, z1 * sin + z2 * cos], axis=-1)


def _moba_attention(q, k, v):
    B, S, H, Dh = q.shape
    L, Qc = MOBA_BLOCK, MOBA_Q_CHUNK
    nb = -(-S // L)
    n_sel = min(MOBA_TOPK, nb - 1)
    nq = S // Qc
    pad = nb * L - S
    qh = q.transpose(0, 2, 1, 3) * (Dh ** -0.5)
    pad_blocks = lambda z: jnp.pad(z.transpose(0, 2, 1, 3), ((0, 0), (0, 0), (0, pad), (0, 0))).reshape(B, H, nb, L, Dh)
    kb, vb = pad_blocks(k), pad_blocks(v)
    to_chunks = lambda z: jnp.moveaxis(z.reshape(B, H, nq, Qc, *z.shape[3:]), 2, 0)
    xs = [jnp.arange(nq), to_chunks(qh)]
    t = jnp.arange(S)
    gate = jnp.einsum('bhtd,bhnd->bhtn', qh, jnp.mean(kb, axis=3)).astype(F32)
    past = jnp.arange(nb)[None, :] < (t // L)[:, None]
    sel_score, sel_idx = lax.top_k(jnp.where(past, gate, -jnp.inf), n_sel)
    xs += [to_chunks(sel_idx), to_chunks(sel_score > -jnp.inf)]
    bi = jnp.arange(B)[:, None, None, None]
    hi = jnp.arange(H)[None, :, None, None]

    def chunk(args):
        c, q_c = args[0], args[1]
        t_c = c * Qc + jnp.arange(Qc)
        blk = (c * Qc) // L
        k_own = lax.dynamic_index_in_dim(kb, blk, axis=2, keepdims=False)
        v_own = lax.dynamic_index_in_dim(vb, blk, axis=2, keepdims=False)
        scores = [jnp.einsum('bhqd,bhld->bhql', q_c, k_own)]
        masks = [jnp.broadcast_to(blk * L + jnp.arange(L) <= t_c[:, None], (B, H, Qc, L))]
        idx_c, ok_c = args[2], args[3]
        k_sel = kb[bi, hi, idx_c]
        v_sel = vb[bi, hi, idx_c]
        scores.append(jnp.einsum('bhqd,bhqnld->bhqnl', q_c, k_sel).reshape(B, H, Qc, n_sel * L))
        masks.append(jnp.repeat(ok_c, L, axis=-1))
        p = _masked_softmax(jnp.concatenate(scores, axis=-1), jnp.concatenate(masks, axis=-1))
        out = jnp.einsum('bhql,bhld->bhqd', p[..., :L], v_own)
        out = out + jnp.einsum('bhqnl,bhqnld->bhqd', p[..., L:].reshape(B, H, Qc, n_sel, L), v_sel)
        return out

    o = lax.map(chunk, tuple(xs))
    o = jnp.moveaxis(o, 0, 2).reshape(B, H, S, Dh)
    return o.transpose(0, 2, 1, 3).reshape(B, S, H * Dh)


def _rwkv7_time_mix(r, k, v, w_lo, a_lo, g_lo, w0, w_up, a0, a_up, g_up, k_k, k_a, r_k, ln_g, ln_b):
    B, S, C = r.shape
    H, N = RWKV_HEADS, C // RWKV_HEADS
    w = -jax.nn.softplus(-(w0 + jnp.tanh(w_lo) @ w_up)) - 0.5
    decay = jnp.exp(-jnp.exp(w.astype(F32)))
    a = jax.nn.sigmoid(a0 + a_lo @ a_up)
    g = jax.nn.sigmoid(g_lo) @ g_up
    hd = lambda u: u.reshape(B, S, H, N).astype(F32)
    kk = hd(k * k_k)
    kk = kk / jnp.maximum(jnp.sqrt(jnp.sum(kk * kk, axis=-1, keepdims=True)), 1e-12)
    k = k * (1.0 + (a - 1.0) * k_a)
    rh, kh, vh, ah, wh = hd(r), hd(k), hd(v), hd(a), hd(decay)

    def step(state, inp):
        r_t, w_t, k_t, v_t, kk_t, a_t = inp
        sa = jnp.einsum('bhij,bhj->bhi', state, -kk_t)
        state = (state * w_t[:, :, None, :]
                 + sa[..., None] * (kk_t * a_t)[:, :, None, :]
                 + v_t[..., None] * k_t[:, :, None, :])
        return state, jnp.einsum('bhij,bhj->bhi', state, r_t)

    xs = tuple(jnp.moveaxis(u, 1, 0) for u in (rh, wh, kh, vh, kk, ah))
    _, y = lax.scan(step, jnp.zeros((B, H, N, N), F32), xs)
    y = jnp.moveaxis(y, 0, 1)
    y = _head_group_norm(y, RWKV_GN_EPS) * ln_g.reshape(H, N) + ln_b.reshape(H, N)
    bonus = jnp.sum(rh * kh * r_k, axis=-1, keepdims=True) * vh
    return ((y + bonus).reshape(B, S, C) * g).astype(r.dtype)


def _retention(q, k, v):
    B, S, H, dk = q.shape
    dv = v.shape[-1]
    C = RET_CHUNK
    nc = S // C
    q = _rotary(q)
    k = _rotary(k) * (dk ** -0.5)
    log_g = jnp.asarray(np.log(1.0 - 2.0 ** (-5.0 - np.arange(H))), F32)
    n = jnp.arange(C, dtype=F32)
    diff = n[:, None] - n[None, :]
    decay_in = jnp.where(diff >= 0, jnp.exp(jnp.maximum(diff, 0.0) * log_g[:, None, None]), 0.0)
    decay_q = jnp.exp((n + 1.0) * log_g[:, None])[None, :, :, None]
    decay_k = jnp.exp((C - 1.0 - n) * log_g[:, None])[None, :, :, None]
    decay_c = jnp.exp(C * log_g)[None, :, None, None]
    chunks = lambda z: z.reshape(B, nc, C, H, z.shape[-1]).transpose(1, 0, 3, 2, 4)

    def step(state, inp):
        qi, ki, vi = inp
        inner = jnp.einsum('bhnd,bhmd->bhnm', qi, ki) * decay_in
        o = jnp.einsum('bhnm,bhme->bhne', inner, vi) + jnp.einsum('bhnd,bhde->bhne', qi, state) * decay_q
        state = jnp.einsum('bhmd,bhme->bhde', ki * decay_k, vi) + decay_c * state
        return state, o

    _, o = lax.scan(step, jnp.zeros((B, H, dk, dv), F32), (chunks(q), chunks(k), chunks(v)))
    o = o.transpose(1, 0, 3, 2, 4).reshape(B, S, H, dv)
    return _head_group_norm(o, RET_GN_EPS)


def _nsa_attention(q, k_cmp_in, v_cmp_in, k_slc, v_slc, k_win, v_win, gates, pe_k, w1_k, w2_k, pe_v, w1_v, w2_v):
    B, S, H, Dh = q.shape
    G = NSA_KV_GROUPS
    R = H // G
    Lc, st, Ls, W = NSA_CMP_BLOCK, NSA_CMP_STRIDE, NSA_SLC_BLOCK, NSA_WINDOW
    t = jnp.arange(S)
    qg = (q * (Dh ** -0.5)).reshape(B, S, G, R, Dh).transpose(0, 2, 3, 1, 4)
    kv = lambda z: z.transpose(0, 2, 1, 3)
    nc = (S - Lc) // st + 1
    c_start = np.arange(nc) * st
    c_idx = c_start[:, None] + np.arange(Lc)[None, :]

    def compress(z, pe, w1, w2):
        blocks = kv(z)[:, :, c_idx] + pe
        return jax.nn.gelu(blocks.reshape(B, G, nc, Lc * Dh) @ w1) @ w2

    k_c = compress(k_cmp_in, pe_k, w1_k, w2_k)
    v_c = compress(v_cmp_in, pe_v, w1_v, w2_v)
    c_end = jnp.asarray(c_start + Lc - 1)
    p_cmp = _masked_softmax(jnp.einsum('bgrtd,bgcd->bgrtc', qg, k_c), c_end[None, :] <= t[:, None])
    o_cmp = jnp.einsum('bgrtc,bgcd->bgrtd', p_cmp, v_c)
    ns = S // Ls
    s_start = np.arange(ns) * Ls
    overlap = ((c_start[:, None] <= s_start[None, :] + Ls - 1)
               & (c_start[:, None] + Lc - 1 >= s_start[None, :])).astype(np.float32)
    p_slc = jnp.einsum('bgrtc,cj->bgtj', p_cmp, jnp.asarray(overlap))
    j = jnp.arange(ns)[None, :]
    own = (t // Ls)[:, None]
    score = jnp.where((j == own) | (j == 0), jnp.inf, jnp.where(j > own, -jnp.inf, p_slc))
    n_top = min(NSA_SLC_TOPN, ns)
    sel_score, sel_idx = lax.top_k(score, n_top)
    sel_ok = sel_score > -jnp.inf
    ks_b = kv(k_slc).reshape(B, G, ns, Ls, Dh)
    vs_b = kv(v_slc).reshape(B, G, ns, Ls, Dh)
    Qc = NSA_Q_CHUNK
    nq = S // Qc
    bi = jnp.arange(B)[:, None, None, None]
    gi = jnp.arange(G)[None, :, None, None]

    def sel_chunk(args):
        c, q_c, idx_c, ok_c = args
        t_c = c * Qc + jnp.arange(Qc)
        k_s = ks_b[bi, gi, idx_c]
        v_s = vs_b[bi, gi, idx_c]
        pos = idx_c[..., None] * Ls + jnp.arange(Ls)
        mask = ok_c[..., None] & (pos <= t_c[None, None, :, None, None])
        s = jnp.einsum('bgrqd,bgqnld->bgrqnl', q_c, k_s).reshape(B, G, R, Qc, n_top * Ls)
        p = _masked_softmax(s, mask.reshape(B, G, 1, Qc, n_top * Ls))
        return jnp.einsum('bgrqnl,bgqnld->bgrqd', p.reshape(B, G, R, Qc, n_top, Ls), v_s)

    o_slc = lax.map(sel_chunk, (jnp.arange(nq),
                                jnp.moveaxis(qg.reshape(B, G, R, nq, Qc, Dh), 3, 0),
                                jnp.moveaxis(sel_idx.reshape(B, G, nq, Qc, n_top), 2, 0),
                                jnp.moveaxis(sel_ok.reshape(B, G, nq, Qc, n_top), 2, 0)))
    o_slc = jnp.moveaxis(o_slc, 0, 3).reshape(B, G, R, S, Dh)
    Qw = WIN_Q_BLOCK
    nw = S // Qw
    kw_p = jnp.pad(kv(k_win), ((0, 0), (0, 0), (W, 0), (0, 0)))
    vw_p = jnp.pad(kv(v_win), ((0, 0), (0, 0), (W, 0), (0, 0)))

    def win_block(args):
        c, q_c = args
        s0 = c * Qw
        k_w = lax.dynamic_slice_in_dim(kw_p, s0, Qw + W, axis=2)
        v_w = lax.dynamic_slice_in_dim(vw_p, s0, Qw + W, axis=2)
        t_c = s0 + jnp.arange(Qw)
        s_pos = s0 - W + jnp.arange(Qw + W)
        d = t_c[:, None] - s_pos[None, :]
        mask = (d >= 0) & (d < W) & (s_pos[None, :] >= 0)
        p = _masked_softmax(jnp.einsum('bgrqd,bgkd->bgrqk', q_c, k_w), mask)
        return jnp.einsum('bgrqk,bgkd->bgrqd', p, v_w)

    o_win = lax.map(win_block, (jnp.arange(nw), jnp.moveaxis(qg.reshape(B, G, R, nw, Qw, Dh), 3, 0)))
    o_win = jnp.moveaxis(o_win, 0, 3).reshape(B, G, R, S, Dh)
    g = jnp.moveaxis(gates, 1, 2).reshape(B, G, R, S, 3)
    o = g[..., 0:1] * o_cmp + g[..., 1:2] * o_slc + g[..., 2:3] * o_win
    return o.reshape(B, H, S, Dh).transpose(0, 2, 1, 3).reshape(B, S, H * Dh)


def _even_mixer(x, g_norm, w_in, shift_mu, w0, w_up, a0, a_up, g_up, k_k, k_a, r_k, ln_g, ln_b):
    B, S, D = x.shape
    z = norm_matmul(x.reshape(B * S, D), g_norm, w_in.astype(BF16)).reshape(B, S, -1)
    MW = 16 * HEAD_DIM
    qa, ka, va, zb = _split_cols(z, (MW, MW, MW, z.shape[-1] - 3 * MW))
    heads = lambda u: u.reshape(B, S, -1, HEAD_DIM)
    o_a = _moba_attention(heads(qa), heads(ka), heads(va))
    zb = zb + (_token_shift(zb) - zb) * shift_mu
    r, kb, vb, w_lo, a_lo, g_lo = _split_cols(zb, (MW, MW, MW, 64, 64, 128))
    o_b = _rwkv7_time_mix(r, kb, vb, w_lo, a_lo, g_lo, w0, w_up, a0, a_up, g_up, k_k, k_a, r_k, ln_g, ln_b)
    return o_a, o_b


def _odd_mixer(x, g_norm, w_in, pe_k, w1_k, w2_k, pe_v, w1_v, w2_v):
    B, S, D = x.shape
    n_in = w_in.shape[1]
    n_pad = -(-n_in // 640) * 640
    w_p = jnp.pad(w_in, ((0, 0), (0, n_pad - n_in))).astype(BF16)
    z = norm_matmul(x.reshape(B * S, D), g_norm, w_p).reshape(B, S, -1)[..., :n_in]
    RW = RET_HEADS * RET_V_DIM
    splits = (RET_HEADS * RET_QK_DIM, RET_HEADS * RET_QK_DIM, RW, RW, NSA_HEADS * HEAD_DIM) + (NSA_KV_GROUPS * HEAD_DIM,) * 6 + (3 * NSA_HEADS,)
    rq, rk, rv, rg, nq, kc, vc, ks, vs, kw, vw, ng = _split_cols(z, splits)
    o_c = _retention(rq.reshape(B, S, RET_HEADS, RET_QK_DIM), rk.reshape(B, S, RET_HEADS, RET_QK_DIM),
                     rv.reshape(B, S, RET_HEADS, RET_V_DIM))
    o_c = jax.nn.silu(rg) * o_c.reshape(B, S, RW)
    hd = lambda u: u.reshape(B, S, -1, HEAD_DIM)
    gates = jax.nn.sigmoid(ng.reshape(B, S, NSA_HEADS, 3))
    o_d = _nsa_attention(hd(nq), hd(kc), hd(vc), hd(ks), hd(vs), hd(kw), hd(vw), gates,
                         pe_k, w1_k, w2_k, pe_v, w1_v, w2_v)
    return o_c, o_d


def kernel(x, mix_norm, ffn_norm, even_w_in, even_shift_mu, even_w0, even_w_up, even_a0, even_a_up, even_g_up, even_k_k, even_k_a, even_r_k, even_ln_g, even_ln_b, even_w_out, odd_w_in, odd_cmp_pe_k, odd_cmp_w1_k, odd_cmp_w2_k, odd_cmp_pe_v, odd_cmp_w1_v, odd_cmp_w2_v, odd_w_out, ffn_w1, ffn_w3, ffn_w2, final_norm):
    B, S, D = x.shape
    depth = mix_norm.shape[0]
    for layer in range(depth):
        i = layer // 2
        if layer % 2 == 0:
            o1, o2 = _even_mixer(x, mix_norm[layer], even_w_in[i], even_shift_mu[i], even_w0[i], even_w_up[i],
                                 even_a0[i], even_a_up[i], even_g_up[i], even_k_k[i], even_k_a[i], even_r_k[i],
                                 even_ln_g[i], even_ln_b[i])
            w_out = even_w_out[i]
        else:
            o1, o2 = _odd_mixer(x, mix_norm[layer], odd_w_in[i], odd_cmp_pe_k[i], odd_cmp_w1_k[i], odd_cmp_w2_k[i],
                                odd_cmp_pe_v[i], odd_cmp_w1_v[i], odd_cmp_w2_v[i])
            w_out = odd_w_out[i]
        T = B * S
        x2 = out_proj_residual(o1.reshape(T, -1), o2.reshape(T, -1), w_out.astype(BF16), x.reshape(T, D))
        x2 = ffn_residual(x2, ffn_norm[layer], ffn_w1[layer].astype(BF16), ffn_w3[layer].astype(BF16),
                          ffn_w2[layer].astype(BF16), final_norm if layer == depth - 1 else None)
        x = x2.reshape(B, S, D)
    return x
```

```python
import functools

import jax
import jax.numpy as jnp
import numpy as np
from jax import lax
from jax.experimental import pallas as pl
from jax.experimental.pallas import tpu as pltpu

F32 = jnp.float32
BF16 = jnp.bfloat16

V7X_LANES = 128
V7X_VMEM_BYTES = 64 * 1024 * 1024
VMEM_LIMIT = 56 * 1024 * 1024

NORM_EPS = 1e-6
HEAD_DIM = 64

MOBA_BLOCK = 256
MOBA_TOPK = 3
MOBA_Q_CHUNK = 16
RWKV_HEADS = 16
RWKV_GN_EPS = 6.4e-4

RET_HEADS = 8
RET_QK_DIM = 64
RET_V_DIM = 128
RET_CHUNK = 128
RET_GN_EPS = 1e-6
ROPE_BASE = 10000.0
NSA_HEADS = 16
NSA_KV_GROUPS = 4
NSA_CMP_BLOCK = 32
NSA_CMP_STRIDE = 16
NSA_SLC_BLOCK = 64
NSA_SLC_TOPN = 16
NSA_WINDOW = 512
NSA_Q_CHUNK = 32
WIN_Q_BLOCK = 128


def _params(*semantics):
    return pltpu.CompilerParams(dimension_semantics=semantics, vmem_limit_bytes=VMEM_LIMIT)


def _rms(x, g):
    return x * lax.rsqrt(jnp.mean(x * x, axis=-1, keepdims=True) + NORM_EPS) * g


def _norm_matmul_kernel(x_ref, g_ref, w_ref, o_ref, h_ref):
    @pl.when(pl.program_id(1) == 0)
    def _():
        h_ref[...] = _rms(x_ref[...], g_ref[...]).astype(BF16)

    o_ref[...] = jnp.dot(h_ref[...], w_ref[...], preferred_element_type=F32)


def norm_matmul(x, g, w, *, tm=512, tn=640):
    T, D = x.shape
    N = w.shape[1]
    assert T % tm == 0 and N % tn == 0
    return pl.pallas_call(
        _norm_matmul_kernel,
        grid=(T // tm, N // tn),
        in_specs=[
            pl.BlockSpec((tm, D), lambda i, j: (i, 0)),
            pl.BlockSpec((1, D), lambda i, j: (0, 0)),
            pl.BlockSpec((D, tn), lambda i, j: (0, j)),
        ],
        out_specs=pl.BlockSpec((tm, tn), lambda i, j: (i, j)),
        out_shape=jax.ShapeDtypeStruct((T, N), F32),
        scratch_shapes=[pltpu.VMEM((tm, D), BF16)],
        compiler_params=_params("parallel", "arbitrary"),
        name="norm_matmul",
    )(x, g.reshape(1, D), w)


def _out_proj_kernel(a_ref, b_ref, wa_ref, wb_ref, x_ref, o_ref):
    acc = jnp.dot(a_ref[...].astype(BF16), wa_ref[...], preferred_element_type=F32)
    acc += jnp.dot(b_ref[...].astype(BF16), wb_ref[...], preferred_element_type=F32)
    o_ref[...] = x_ref[...] + acc


def out_proj_residual(a, b, w, x, *, tm=512):
    T, D = x.shape
    Ka, Kb = a.shape[1], b.shape[1]
    assert T % tm == 0 and w.shape == (Ka + Kb, D)
    return pl.pallas_call(
        _out_proj_kernel,
        grid=(T // tm,),
        in_specs=[
            pl.BlockSpec((tm, Ka), lambda i: (i, 0)),
            pl.BlockSpec((tm, Kb), lambda i: (i, 0)),
            pl.BlockSpec((Ka, D), lambda i: (0, 0)),
            pl.BlockSpec((Kb, D), lambda i: (0, 0)),
            pl.BlockSpec((tm, D), lambda i: (i, 0)),
        ],
        out_specs=pl.BlockSpec((tm, D), lambda i: (i, 0)),
        out_shape=jax.ShapeDtypeStruct((T, D), F32),
        compiler_params=_params("parallel"),
        name="out_proj_residual",
    )(a, b, w[:Ka], w[Ka:], x)


def _ffn_kernel(x_ref, g_ref, w1_ref, w3_ref, w2_ref, gf_ref, o_ref, h_ref, acc_ref, *, final_norm):
    j = pl.program_id(1)

    @pl.when(j == 0)
    def _():
        h_ref[...] = _rms(x_ref[...], g_ref[...]).astype(BF16)
        acc_ref[...] = jnp.zeros_like(acc_ref)

    h = h_ref[...]
    a = jnp.dot(h, w1_ref[...], preferred_element_type=F32)
    b = jnp.dot(h, w3_ref[...], preferred_element_type=F32)
    act = (a * jax.nn.sigmoid(a) * b).astype(BF16)
    acc_ref[...] += jnp.dot(act, w2_ref[...], preferred_element_type=F32)

    @pl.when(j == pl.num_programs(1) - 1)
    def _():
        y = x_ref[...] + acc_ref[...]
        if final_norm:
            y = _rms(y, gf_ref[...])
        o_ref[...] = y


def ffn_residual(x, g, w1, w3, w2, g_final=None, *, tm=512, tf=512):
    T, D = x.shape
    Fh = w1.shape[1]
    assert T % tm == 0 and Fh % tf == 0
    final_norm = g_final is not None
    gf = (g_final if final_norm else g).reshape(1, D)
    return pl.pallas_call(
        functools.partial(_ffn_kernel, final_norm=final_norm),
        grid=(T // tm, Fh // tf),
        in_specs=[
            pl.BlockSpec((tm, D), lambda i, j: (i, 0)),
            pl.BlockSpec((1, D), lambda i, j: (0, 0)),
            pl.BlockSpec((D, tf), lambda i, j: (0, j)),
            pl.BlockSpec((D, tf), lambda i, j: (0, j)),
            pl.BlockSpec((tf, D), lambda i, j: (j, 0)),
            pl.BlockSpec((1, D), lambda i, j: (0, 0)),
        ],
        out_specs=pl.BlockSpec((tm, D), lambda i, j: (i, 0)),
        out_shape=jax.ShapeDtypeStruct((T, D), F32),
        scratch_shapes=[pltpu.VMEM((tm, D), BF16), pltpu.VMEM((tm, D), F32)],
        compiler_params=_params("parallel", "arbitrary"),
        name="ffn_residual",
    )(x, g.reshape(1, D), w1, w3, w2, gf)


NEG_BIG = -1e30
_NT = (((1,), (1,)), ((), ()))


def _moba_kernel(q_ref, k_ref, v_ref, o_ref, kb_ref, vb_ref, km_ref, acc_ref):
    L = MOBA_BLOCK
    nb = k_ref.shape[1] // L
    qi = pl.program_id(2)
    lane = lax.broadcasted_iota(jnp.int32, (L, V7X_LANES), 1)

    @pl.when(qi == 0)
    def _():
        k = k_ref[0]
        kb_ref[...] = k.astype(BF16)
        vb_ref[...] = v_ref[0].astype(BF16)
        km_ref[...] = jnp.zeros_like(km_ref)
        km_ref[0:nb, :] = jnp.mean(k.reshape(nb, L, V7X_LANES), axis=1)

    q = q_ref[0] * (HEAD_DIM ** -0.5)
    row = lax.broadcasted_iota(jnp.int32, (L, L), 0)
    col = lax.broadcasted_iota(jnp.int32, (L, L), 1)
    causal = col <= row
    past = lane < qi
    outs = []
    for h in range(2):
        in_head = (lane >= h * HEAD_DIM) & (lane < (h + 1) * HEAD_DIM)
        qm = jnp.where(in_head, q, 0.0)
        gate = lax.dot_general(qm, km_ref[...], _NT, precision=lax.Precision.HIGHEST,
                               preferred_element_type=F32)
        selmat = jnp.zeros((L, V7X_LANES), F32)
        for n in range(nb):
            g_n = jnp.sum(jnp.where(lane == n, gate, 0.0), axis=1, keepdims=True)
            beats = (gate > g_n) | ((gate == g_n) & (lane < n))
            rank = jnp.sum(jnp.where(past & beats, 1.0, 0.0), axis=1, keepdims=True)
            selmat = jnp.where((lane == n) & (rank < MOBA_TOPK), 1.0, selmat)
        selmat = jnp.where(past, selmat, 0.0)
        qb = qm.astype(BF16)

        start = pl.multiple_of(qi * L, L)
        s = lax.dot_general(qb, kb_ref[pl.ds(start, L), :], _NT, preferred_element_type=F32)
        s = jnp.where(causal, s, NEG_BIG)
        m0 = jnp.max(s, axis=1, keepdims=True)
        p = jnp.exp(s - m0)
        l0 = jnp.sum(p, axis=1, keepdims=True)
        acc_ref[h] = jnp.dot(p.astype(BF16), vb_ref[pl.ds(start, L), :], preferred_element_type=F32)

        def body(n, carry, h=h, qb=qb, selmat=selmat):
            m_prev, l_prev = carry
            st = pl.multiple_of(n * L, L)
            s = lax.dot_general(qb, kb_ref[pl.ds(st, L), :], _NT, preferred_element_type=F32)
            sel_n = jnp.sum(jnp.where(lane == n, selmat, 0.0), axis=1, keepdims=True) > 0.5
            s = jnp.where(sel_n, s, NEG_BIG)
            m_new = jnp.maximum(m_prev, jnp.max(s, axis=1, keepdims=True))
            alpha = jnp.exp(m_prev - m_new)
            p = jnp.exp(s - m_new)
            l_new = alpha * l_prev + jnp.sum(p, axis=1, keepdims=True)
            acc_ref[h] = alpha * acc_ref[h] + jnp.dot(p.astype(BF16), vb_ref[pl.ds(st, L), :],
                                                      preferred_element_type=F32)
            return m_new, l_new

        _, l_fin = lax.fori_loop(0, qi, body, (m0, l0))
        outs.append(acc_ref[h] / l_fin)
    o_ref[0] = jnp.where(lane < HEAD_DIM, outs[0], outs[1])


def moba_attention(z, *, n_heads=16):
    B, S, _ = z.shape
    L = MOBA_BLOCK
    assert S % L == 0 and n_heads % 2 == 0
    npair = n_heads // 2
    return pl.pallas_call(
        _moba_kernel,
        grid=(B, npair, S // L),
        in_specs=[
            pl.BlockSpec((1, L, V7X_LANES), lambda b, p, i: (b, i, p)),
            pl.BlockSpec((1, S, V7X_LANES), lambda b, p, i: (b, 0, npair + p)),
            pl.BlockSpec((1, S, V7X_LANES), lambda b, p, i: (b, 0, 2 * npair + p)),
        ],
        out_specs=pl.BlockSpec((1, L, V7X_LANES), lambda b, p, i: (b, i, p)),
        out_shape=jax.ShapeDtypeStruct((B, S, n_heads * HEAD_DIM), F32),
        scratch_shapes=[
            pltpu.VMEM((S, V7X_LANES), BF16),
            pltpu.VMEM((S, V7X_LANES), BF16),
            pltpu.VMEM((V7X_LANES, V7X_LANES), F32),
            pltpu.VMEM((2, L, V7X_LANES), F32),
        ],
        compiler_params=_params("parallel", "parallel", "arbitrary"),
        name="moba_attention",
    )(z, z, z)


NSA_TQ = 256
BIG = 3.0e38


def _gelu_tanh(x):
    return 0.5 * x * (1.0 + jnp.tanh(0.7978845608028654 * (x + 0.044715 * x * x * x)))


def _nsa_compress_kernel(xk0_ref, xk1_ref, xv0_ref, xv1_ref, pek_ref, pev_ref, w1k_ref, w1v_ref, w2k_ref, w2v_ref,
                         o1_ref, o2_ref):
    G, Lc, st = NSA_KV_GROUPS, NSA_CMP_BLOCK, NSA_CMP_STRIDE
    nrow = xk0_ref.shape[1] // st
    lane = lax.broadcasted_iota(jnp.int32, (nrow, G * HEAD_DIM), 1)

    def hidden(x_refs, pe_ref, w1_ref):
        acc = [jnp.zeros((G * nrow, V7X_LANES), F32) for _ in range(Lc // st)]
        for l in range(Lc):
            u, m = divmod(l, st)
            x = jnp.concatenate([r[0, pl.ds(m, nrow, stride=st), :] for r in x_refs], axis=1) + pe_ref[l:l + 1, :]
            xs = jnp.concatenate(
                [jnp.where((lane >= g * HEAD_DIM) & (lane < (g + 1) * HEAD_DIM), x, 0.0) for g in range(G)],
                axis=0).astype(BF16)
            acc[u] = acc[u] + jnp.dot(xs, w1_ref[l], preferred_element_type=F32)
        nxt = jnp.concatenate([pltpu.roll(acc[1][g * nrow:(g + 1) * nrow], nrow - 1, axis=0) for g in range(G)],
                              axis=0)
        return _gelu_tanh(acc[0] + nxt).astype(BF16)

    hk = hidden((xk0_ref, xk1_ref), pek_ref, w1k_ref)
    hv = hidden((xv0_ref, xv1_ref), pev_ref, w1v_ref)
    kc = jnp.dot(hk, w2k_ref[...], preferred_element_type=F32)
    vc = jnp.dot(hv, w2v_ref[...], preferred_element_type=F32)
    kv = kc + vc
    vk = pltpu.roll(kv, HEAD_DIM, axis=1)
    for g in range(G):
        o1_ref[0, :, g * V7X_LANES:(g + 1) * V7X_LANES] = kv[g * nrow:(g + 1) * nrow]
        o2_ref[0, :, g * V7X_LANES:(g + 1) * V7X_LANES] = vk[g * nrow:(g + 1) * nrow]


def nsa_compress(z, col_k, col_v, pe_k, w1_k, w2_k, pe_v, w1_v, w2_v):
    B, S, _ = z.shape
    G, Lc, st = NSA_KV_GROUPS, NSA_CMP_BLOCK, NSA_CMP_STRIDE
    GW = G * HEAD_DIM
    nrow = S // st
    hid = w1_k.shape[1]
    assert hid == V7X_LANES and col_k % GW == 0 and col_v % GW == 0
    tile_pe = lambda pe: jnp.tile(pe, (1, G))
    tile_w1 = lambda w: jnp.tile(w.reshape(Lc, 1, HEAD_DIM, hid), (1, G, 1, 1)).reshape(Lc, GW, hid).astype(BF16)
    w2k = jnp.pad(w2_k, ((0, 0), (0, HEAD_DIM))).astype(BF16)
    w2v = jnp.pad(w2_v, ((0, 0), (HEAD_DIM, 0))).astype(BF16)
    const = lambda shape: pl.BlockSpec(shape, lambda b: (0,) * len(shape))
    out = jax.ShapeDtypeStruct((B, nrow, G * V7X_LANES), F32)
    return pl.pallas_call(
        _nsa_compress_kernel,
        grid=(B,),
        in_specs=[
            pl.BlockSpec((1, S, V7X_LANES), lambda b: (b, 0, col_k // V7X_LANES)),
            pl.BlockSpec((1, S, V7X_LANES), lambda b: (b, 0, col_k // V7X_LANES + 1)),
            pl.BlockSpec((1, S, V7X_LANES), lambda b: (b, 0, col_v // V7X_LANES)),
            pl.BlockSpec((1, S, V7X_LANES), lambda b: (b, 0, col_v // V7X_LANES + 1)),
            const((Lc, GW)), const((Lc, GW)),
            const((Lc, GW, hid)), const((Lc, GW, hid)),
            const((hid, V7X_LANES)), const((hid, V7X_LANES)),
        ],
        out_specs=[pl.BlockSpec((1, nrow, G * V7X_LANES), lambda b: (b, 0, 0))] * 2,
        out_shape=[out, out],
        compiler_params=_params("parallel"),
        name="nsa_compress",
    )(z, z, z, z, tile_pe(pe_k), tile_pe(pe_v), tile_w1(w1_k), tile_w1(w1_v), w2k, w2v)


def _flash_step(qb, kv_s, kv_o, mask, m_prev, l_prev, acc_prev):
    s = lax.dot_general(qb, kv_s, _NT, preferred_element_type=F32)
    s = jnp.where(mask, s, NEG_BIG)
    m_new = jnp.maximum(m_prev, jnp.max(s, axis=1, keepdims=True))
    alpha = jnp.exp(m_prev - m_new)
    p = jnp.exp(s - m_new)
    l_new = alpha * l_prev + jnp.sum(p, axis=1, keepdims=True)
    acc_new = alpha * acc_prev + jnp.dot(p.astype(BF16), kv_o, preferred_element_type=F32)
    return m_new, l_new, acc_new


def _nsa_kernel(q_ref, c1_ref, c2_ref, s_ref, w_ref, g_ref, ov_ref, e_ref, o_ref,
                s1_ref, s2_ref, w1_ref, w2_ref, selx_ref, acc_ref):
    TQ = NSA_TQ
    R = NSA_HEADS // NSA_KV_GROUPS
    grp = pl.program_id(1)
    qi = pl.program_id(2)
    lane = lax.broadcasted_iota(jnp.int32, (TQ, V7X_LANES), 1)

    @pl.when(qi == 0)
    def _():
        kv = s_ref[0]
        s1_ref[...] = kv.astype(BF16)
        s2_ref[...] = pltpu.roll(kv, HEAD_DIM, axis=1).astype(BF16)
        kv = w_ref[0]
        w1_ref[...] = kv.astype(BF16)
        w2_ref[...] = pltpu.roll(kv, HEAD_DIM, axis=1).astype(BF16)

    q0 = pl.multiple_of(qi * TQ, TQ)
    row = lax.broadcasted_iota(jnp.int32, (TQ, TQ), 0)
    col = lax.broadcasted_iota(jnp.int32, (TQ, TQ), 1)
    causal = col <= row
    qi_mat = jnp.zeros((TQ, TQ), jnp.int32) + qi
    t_abs = q0 + lax.broadcasted_iota(jnp.int32, (TQ, V7X_LANES), 0)
    even_lanes = lane < HEAD_DIM

    c_kv, c_vk = c1_ref[0], c2_ref[0]
    c_kv_b, c_vk_b = c_kv.astype(BF16), c_vk.astype(BF16)
    cmask = lane * NSA_CMP_STRIDE + (NSA_CMP_BLOCK - 1) <= t_abs
    qbs, o_cmp = [], []
    p_sum = jnp.zeros((TQ, V7X_LANES), F32)
    for r in range(R):
        tile = q_ref[0, :, (r // 2) * V7X_LANES:(r // 2 + 1) * V7X_LANES] * (HEAD_DIM ** -0.5)
        qm = jnp.where(even_lanes if r % 2 == 0 else ~even_lanes, tile, 0.0)
        s = lax.dot_general(qm, c_kv if r % 2 == 0 else c_vk, _NT, precision=lax.Precision.HIGHEST,
                            preferred_element_type=F32)
        s = jnp.where(cmask, s, NEG_BIG)
        p = jnp.where(cmask, jnp.exp(s - jnp.max(s, axis=1, keepdims=True)), 0.0)
        den = jnp.sum(p, axis=1, keepdims=True)
        p = p / jnp.where(den > 0.0, den, 1.0)
        p_sum = p_sum + p
        o_cmp.append(jnp.dot(p.astype(BF16), c_vk_b if r % 2 == 0 else c_kv_b, preferred_element_type=F32))
        qbs.append(qm.astype(BF16))

    p_slc = jnp.dot(p_sum, ov_ref[...], precision=lax.Precision.HIGHEST, preferred_element_type=F32)
    own = t_abs // NSA_SLC_BLOCK
    score = jnp.where((lane == own) | (lane == 0), BIG, jnp.where(lane > own, -BIG, p_slc))
    sel = jnp.zeros((TQ, V7X_LANES), F32)
    for j in range(s_ref.shape[1] // NSA_SLC_BLOCK):
        s_j = jnp.sum(jnp.where(lane == j, score, 0.0), axis=1, keepdims=True)
        beats = (score > s_j) | ((score == s_j) & (lane < j))
        rank = jnp.sum(jnp.where(beats, 1.0, 0.0), axis=1, keepdims=True)
        sel = jnp.where((lane == j) & (rank < NSA_SLC_TOPN), 1.0, sel)
    sel = jnp.where(lane <= own, sel, 0.0).astype(BF16)

    def expand(kb, c):
        selx_ref[kb] = jnp.dot(sel, e_ref[kb], preferred_element_type=F32)
        return c

    lax.fori_loop(0, qi + 1, expand, 0)

    gates = jax.nn.sigmoid(g_ref[0])
    outs = []
    for r in range(R):
        ev = r % 2 == 0
        qb = qbs[r]
        ks_ref, vs_ref = (s1_ref, s2_ref) if ev else (s2_ref, s1_ref)
        kw_ref, vw_ref = (w1_ref, w2_ref) if ev else (w2_ref, w1_ref)
        neg = jnp.full((TQ, 1), NEG_BIG, F32)
        zero = jnp.zeros((TQ, 1), F32)
        zacc = jnp.zeros((TQ, V7X_LANES), F32)

        m, l, acc = _flash_step(qb, ks_ref[pl.ds(q0, TQ), :], vs_ref[pl.ds(q0, TQ), :],
                                causal & (selx_ref[qi] > 0.5), neg, zero, zacc)
        acc_ref[r] = acc

        def body(kb, carry, r=r, qb=qb, ks_ref=ks_ref, vs_ref=vs_ref):
            st = pl.multiple_of(kb * TQ, TQ)
            m2, l2, acc2 = _flash_step(qb, ks_ref[pl.ds(st, TQ), :], vs_ref[pl.ds(st, TQ), :],
                                       selx_ref[kb] > 0.5, carry[0], carry[1], acc_ref[r])
            acc_ref[r] = acc2
            return m2, l2

        _, l = lax.fori_loop(0, qi, body, (m, l))
        o_slc = acc_ref[r] / l

        m, l, acc = _flash_step(qb, kw_ref[pl.ds(q0, TQ), :], vw_ref[pl.ds(q0, TQ), :], causal, neg, zero, zacc)
        st1 = pl.multiple_of(jnp.maximum(qi - 1, 0) * TQ, TQ)
        m, l, acc = _flash_step(qb, kw_ref[pl.ds(st1, TQ), :], vw_ref[pl.ds(st1, TQ), :],
                                qi_mat >= 1, m, l, acc)
        st2 = pl.multiple_of(jnp.maximum(qi - 2, 0) * TQ, TQ)
        m, l, acc = _flash_step(qb, kw_ref[pl.ds(st2, TQ), :], vw_ref[pl.ds(st2, TQ), :],
                                (col > row) & (qi_mat >= 2), m, l, acc)
        o_win = acc / l

        c0 = (grp * R + r) * 3
        gate = lambda c: jnp.sum(jnp.where(lane == c, gates, 0.0), axis=1, keepdims=True)
        outs.append(gate(c0) * o_cmp[r] + gate(c0 + 1) * o_slc + gate(c0 + 2) * o_win)
    for p2 in range(R // 2):
        o_ref[0, :, p2 * V7X_LANES:(p2 + 1) * V7X_LANES] = jnp.where(even_lanes, outs[2 * p2], outs[2 * p2 + 1])


def nsa_attention(z, cmp_kv, cmp_vk, col_q, col_slc, col_win, col_gate):
    B, S, _ = z.shape
    G, TQ = NSA_KV_GROUPS, NSA_TQ
    R = NSA_HEADS // G
    QW = R * HEAD_DIM
    nkb = S // TQ
    ncmp = cmp_kv.shape[1]
    assert S % TQ == 0 and ncmp == V7X_LANES and S // NSA_SLC_BLOCK <= V7X_LANES
    assert col_q % QW == 0 and col_slc % V7X_LANES == 0 and col_win % V7X_LANES == 0 and col_gate % V7X_LANES == 0
    nc = (S - NSA_CMP_BLOCK) // NSA_CMP_STRIDE + 1
    c_start = np.arange(V7X_LANES) * NSA_CMP_STRIDE
    s_start = np.arange(V7X_LANES) * NSA_SLC_BLOCK
    overlap = ((c_start[:, None] <= s_start[None, :] + NSA_SLC_BLOCK - 1)
               & (c_start[:, None] + NSA_CMP_BLOCK - 1 >= s_start[None, :])
               & (np.arange(V7X_LANES)[:, None] < nc) & (np.arange(V7X_LANES)[None, :] < S // NSA_SLC_BLOCK))
    key_blk = (np.arange(nkb)[:, None, None] * TQ + np.arange(TQ)[None, None, :]) // NSA_SLC_BLOCK
    expand = (np.arange(V7X_LANES)[None, :, None] == key_blk)
    const = lambda shape: pl.BlockSpec(shape, lambda b, g, i: (0,) * len(shape))
    return pl.pallas_call(
        _nsa_kernel,
        grid=(B, G, S // TQ),
        in_specs=[
            pl.BlockSpec((1, TQ, QW), lambda b, g, i: (b, i, col_q // QW + g)),
            pl.BlockSpec((1, ncmp, V7X_LANES), lambda b, g, i: (b, 0, g)),
            pl.BlockSpec((1, ncmp, V7X_LANES), lambda b, g, i: (b, 0, g)),
            pl.BlockSpec((1, S, V7X_LANES), lambda b, g, i: (b, 0, col_slc // V7X_LANES + g)),
            pl.BlockSpec((1, S, V7X_LANES), lambda b, g, i: (b, 0, col_win // V7X_LANES + g)),
            pl.BlockSpec((1, TQ, V7X_LANES), lambda b, g, i: (b, i, col_gate // V7X_LANES)),
            const((V7X_LANES, V7X_LANES)),
            const((nkb, V7X_LANES, TQ)),
        ],
        out_specs=pl.BlockSpec((1, TQ, QW), lambda b, g, i: (b, i, g)),
        out_shape=jax.ShapeDtypeStruct((B, S, NSA_HEADS * HEAD_DIM), F32),
        scratch_shapes=[pltpu.VMEM((S, V7X_LANES), BF16)] * 4 + [
            pltpu.VMEM((nkb, TQ, TQ), F32),
            pltpu.VMEM((R, TQ, V7X_LANES), F32),
        ],
        compiler_params=_params("parallel", "parallel", "arbitrary"),
        name="nsa_attention",
    )(z, cmp_kv, cmp_vk, z, z, z, jnp.asarray(overlap, F32), jnp.asarray(expand, BF16))


def _split_cols(z, sizes):
    return jnp.split(z, [int(c) for c in np.cumsum(sizes)[:-1]], axis=-1)


def _head_group_norm(y, eps):
    mu = jnp.mean(y, axis=-1, keepdims=True)
    var = jnp.mean(jnp.square(y - mu), axis=-1, keepdims=True)
    return (y - mu) * lax.rsqrt(var + eps)


def _masked_softmax(s, mask):
    s = jnp.where(mask, s.astype(F32), -jnp.inf)
    m = jnp.max(s, axis=-1, keepdims=True)
    p = jnp.exp(s - jnp.where(jnp.isfinite(m), m, 0.0))
    den = jnp.sum(p, axis=-1, keepdims=True)
    return p / jnp.where(den > 0.0, den, 1.0)


def _token_shift(z):
    return jnp.pad(z[:, :-1], ((0, 0), (1, 0), (0, 0)))


def _rotary(z):
    S, d = z.shape[1], z.shape[-1]
    half = d // 2
    inv = ROPE_BASE ** (-jnp.arange(half, dtype=F32) / half)
    ang = jnp.arange(S, dtype=F32)[:, None] * inv
    cos, sin = jnp.cos(ang)[None, :, None, :], jnp.sin(ang)[None, :, None, :]
    z1, z2 = z[..., :half], z[..., half:]
    return jnp.concatenate([z1 * cos - z2 * sin, z1 * sin + z2 * cos], axis=-1)


def _moba_attention(q, k, v):
    B, S, H, Dh = q.shape
    L, Qc = MOBA_BLOCK, MOBA_Q_CHUNK
    nb = -(-S // L)
    n_sel = min(MOBA_TOPK, nb - 1)
    nq = S // Qc
    pad = nb * L - S
    qh = q.transpose(0, 2, 1, 3) * (Dh ** -0.5)
    pad_blocks = lambda z: jnp.pad(z.transpose(0, 2, 1, 3), ((0, 0), (0, 0), (0, pad), (0, 0))).reshape(B, H, nb, L, Dh)
    kb, vb = pad_blocks(k), pad_blocks(v)
    to_chunks = lambda z: jnp.moveaxis(z.reshape(B, H, nq, Qc, *z.shape[3:]), 2, 0)
    xs = [jnp.arange(nq), to_chunks(qh)]
    t = jnp.arange(S)
    gate = jnp.einsum('bhtd,bhnd->bhtn', qh, jnp.mean(kb, axis=3)).astype(F32)
    past = jnp.arange(nb)[None, :] < (t // L)[:, None]
    sel_score, sel_idx = lax.top_k(jnp.where(past, gate, -jnp.inf), n_sel)
    xs += [to_chunks(sel_idx), to_chunks(sel_score > -jnp.inf)]
    bi = jnp.arange(B)[:, None, None, None]
    hi = jnp.arange(H)[None, :, None, None]

    def chunk(args):
        c, q_c = args[0], args[1]
        t_c = c * Qc + jnp.arange(Qc)
        blk = (c * Qc) // L
        k_own = lax.dynamic_index_in_dim(kb, blk, axis=2, keepdims=False)
        v_own = lax.dynamic_index_in_dim(vb, blk, axis=2, keepdims=False)
        scores = [jnp.einsum('bhqd,bhld->bhql', q_c, k_own)]
        masks = [jnp.broadcast_to(blk * L + jnp.arange(L) <= t_c[:, None], (B, H, Qc, L))]
        idx_c, ok_c = args[2], args[3]
        k_sel = kb[bi, hi, idx_c]
        v_sel = vb[bi, hi, idx_c]
        scores.append(jnp.einsum('bhqd,bhqnld->bhqnl', q_c, k_sel).reshape(B, H, Qc, n_sel * L))
        masks.append(jnp.repeat(ok_c, L, axis=-1))
        p = _masked_softmax(jnp.concatenate(scores, axis=-1), jnp.concatenate(masks, axis=-1))
        out = jnp.einsum('bhql,bhld->bhqd', p[..., :L], v_own)
        out = out + jnp.einsum('bhqnl,bhqnld->bhqd', p[..., L:].reshape(B, H, Qc, n_sel, L), v_sel)
        return out

    o = lax.map(chunk, tuple(xs))
    o = jnp.moveaxis(o, 0, 2).reshape(B, H, S, Dh)
    return o.transpose(0, 2, 1, 3).reshape(B, S, H * Dh)


def _rwkv7_time_mix(r, k, v, w_lo, a_lo, g_lo, w0, w_up, a0, a_up, g_up, k_k, k_a, r_k, ln_g, ln_b):
    B, S, C = r.shape
    H, N = RWKV_HEADS, C // RWKV_HEADS
    w = -jax.nn.softplus(-(w0 + jnp.tanh(w_lo) @ w_up)) - 0.5
    decay = jnp.exp(-jnp.exp(w.astype(F32)))
    a = jax.nn.sigmoid(a0 + a_lo @ a_up)
    g = jax.nn.sigmoid(g_lo) @ g_up
    hd = lambda u: u.reshape(B, S, H, N).astype(F32)
    kk = hd(k * k_k)
    kk = kk / jnp.maximum(jnp.sqrt(jnp.sum(kk * kk, axis=-1, keepdims=True)), 1e-12)
    k = k * (1.0 + (a - 1.0) * k_a)
    rh, kh, vh, ah, wh = hd(r), hd(k), hd(v), hd(a), hd(decay)

    def step(state, inp):
        r_t, w_t, k_t, v_t, kk_t, a_t = inp
        sa = jnp.einsum('bhij,bhj->bhi', state, -kk_t)
        state = (state * w_t[:, :, None, :]
                 + sa[..., None] * (kk_t * a_t)[:, :, None, :]
                 + v_t[..., None] * k_t[:, :, None, :])
        return state, jnp.einsum('bhij,bhj->bhi', state, r_t)

    xs = tuple(jnp.moveaxis(u, 1, 0) for u in (rh, wh, kh, vh, kk, ah))
    _, y = lax.scan(step, jnp.zeros((B, H, N, N), F32), xs)
    y = jnp.moveaxis(y, 0, 1)
    y = _head_group_norm(y, RWKV_GN_EPS) * ln_g.reshape(H, N) + ln_b.reshape(H, N)
    bonus = jnp.sum(rh * kh * r_k, axis=-1, keepdims=True) * vh
    return ((y + bonus).reshape(B, S, C) * g).astype(r.dtype)


def _retention(q, k, v):
    B, S, H, dk = q.shape
    dv = v.shape[-1]
    C = RET_CHUNK
    nc = S // C
    q = _rotary(q)
    k = _rotary(k) * (dk ** -0.5)
    log_g = jnp.asarray(np.log(1.0 - 2.0 ** (-5.0 - np.arange(H))), F32)
    n = jnp.arange(C, dtype=F32)
    diff = n[:, None] - n[None, :]
    decay_in = jnp.where(diff >= 0, jnp.exp(jnp.maximum(diff, 0.0) * log_g[:, None, None]), 0.0)
    decay_q = jnp.exp((n + 1.0) * log_g[:, None])[None, :, :, None]
    decay_k = jnp.exp((C - 1.0 - n) * log_g[:, None])[None, :, :, None]
    decay_c = jnp.exp(C * log_g)[None, :, None, None]
    chunks = lambda z: z.reshape(B, nc, C, H, z.shape[-1]).transpose(1, 0, 3, 2, 4)

    def step(state, inp):
        qi, ki, vi = inp
        inner = jnp.einsum('bhnd,bhmd->bhnm', qi, ki) * decay_in
        o = jnp.einsum('bhnm,bhme->bhne', inner, vi) + jnp.einsum('bhnd,bhde->bhne', qi, state) * decay_q
        state = jnp.einsum('bhmd,bhme->bhde', ki * decay_k, vi) + decay_c * state
        return state, o

    _, o = lax.scan(step, jnp.zeros((B, H, dk, dv), F32), (chunks(q), chunks(k), chunks(v)))
    o = o.transpose(1, 0, 3, 2, 4).reshape(B, S, H, dv)
    return _head_group_norm(o, RET_GN_EPS)


def _nsa_attention(q, k_cmp_in, v_cmp_in, k_slc, v_slc, k_win, v_win, gates, pe_k, w1_k, w2_k, pe_v, w1_v, w2_v):
    B, S, H, Dh = q.shape
    G = NSA_KV_GROUPS
    R = H // G
    Lc, st, Ls, W = NSA_CMP_BLOCK, NSA_CMP_STRIDE, NSA_SLC_BLOCK, NSA_WINDOW
    t = jnp.arange(S)
    qg = (q * (Dh ** -0.5)).reshape(B, S, G, R, Dh).transpose(0, 2, 3, 1, 4)
    kv = lambda z: z.transpose(0, 2, 1, 3)
    nc = (S - Lc) // st + 1
    c_start = np.arange(nc) * st
    c_idx = c_start[:, None] + np.arange(Lc)[None, :]

    def compress(z, pe, w1, w2):
        blocks = kv(z)[:, :, c_idx] + pe
        return jax.nn.gelu(blocks.reshape(B, G, nc, Lc * Dh) @ w1) @ w2

    k_c = compress(k_cmp_in, pe_k, w1_k, w2_k)
    v_c = compress(v_cmp_in, pe_v, w1_v, w2_v)
    c_end = jnp.asarray(c_start + Lc - 1)
    p_cmp = _masked_softmax(jnp.einsum('bgrtd,bgcd->bgrtc', qg, k_c), c_end[None, :] <= t[:, None])
    o_cmp = jnp.einsum('bgrtc,bgcd->bgrtd', p_cmp, v_c)
    ns = S // Ls
    s_start = np.arange(ns) * Ls
    overlap = ((c_start[:, None] <= s_start[None, :] + Ls - 1)
               & (c_start[:, None] + Lc - 1 >= s_start[None, :])).astype(np.float32)
    p_slc = jnp.einsum('bgrtc,cj->bgtj', p_cmp, jnp.asarray(overlap))
    j = jnp.arange(ns)[None, :]
    own = (t // Ls)[:, None]
    score = jnp.where((j == own) | (j == 0), jnp.inf, jnp.where(j > own, -jnp.inf, p_slc))
    n_top = min(NSA_SLC_TOPN, ns)
    sel_score, sel_idx = lax.top_k(score, n_top)
    sel_ok = sel_score > -jnp.inf
    ks_b = kv(k_slc).reshape(B, G, ns, Ls, Dh)
    vs_b = kv(v_slc).reshape(B, G, ns, Ls, Dh)
    Qc = NSA_Q_CHUNK
    nq = S // Qc
    bi = jnp.arange(B)[:, None, None, None]
    gi = jnp.arange(G)[None, :, None, None]

    def sel_chunk(args):
        c, q_c, idx_c, ok_c = args
        t_c = c * Qc + jnp.arange(Qc)
        k_s = ks_b[bi, gi, idx_c]
        v_s = vs_b[bi, gi, idx_c]
        pos = idx_c[..., None] * Ls + jnp.arange(Ls)
        mask = ok_c[..., None] & (pos <= t_c[None, None, :, None, None])
        s = jnp.einsum('bgrqd,bgqnld->bgrqnl', q_c, k_s).reshape(B, G, R, Qc, n_top * Ls)
        p = _masked_softmax(s, mask.reshape(B, G, 1, Qc, n_top * Ls))
        return jnp.einsum('bgrqnl,bgqnld->bgrqd', p.reshape(B, G, R, Qc, n_top, Ls), v_s)

    o_slc = lax.map(sel_chunk, (jnp.arange(nq),
                                jnp.moveaxis(qg.reshape(B, G, R, nq, Qc, Dh), 3, 0),
                                jnp.moveaxis(sel_idx.reshape(B, G, nq, Qc, n_top), 2, 0),
                                jnp.moveaxis(sel_ok.reshape(B, G, nq, Qc, n_top), 2, 0)))
    o_slc = jnp.moveaxis(o_slc, 0, 3).reshape(B, G, R, S, Dh)
    Qw = WIN_Q_BLOCK
    nw = S // Qw
    kw_p = jnp.pad(kv(k_win), ((0, 0), (0, 0), (W, 0), (0, 0)))
    vw_p = jnp.pad(kv(v_win), ((0, 0), (0, 0), (W, 0), (0, 0)))

    def win_block(args):
        c, q_c = args
        s0 = c * Qw
        k_w = lax.dynamic_slice_in_dim(kw_p, s0, Qw + W, axis=2)
        v_w = lax.dynamic_slice_in_dim(vw_p, s0, Qw + W, axis=2)
        t_c = s0 + jnp.arange(Qw)
        s_pos = s0 - W + jnp.arange(Qw + W)
        d = t_c[:, None] - s_pos[None, :]
        mask = (d >= 0) & (d < W) & (s_pos[None, :] >= 0)
        p = _masked_softmax(jnp.einsum('bgrqd,bgkd->bgrqk', q_c, k_w), mask)
        return jnp.einsum('bgrqk,bgkd->bgrqd', p, v_w)

    o_win = lax.map(win_block, (jnp.arange(nw), jnp.moveaxis(qg.reshape(B, G, R, nw, Qw, Dh), 3, 0)))
    o_win = jnp.moveaxis(o_win, 0, 3).reshape(B, G, R, S, Dh)
    g = jnp.moveaxis(gates, 1, 2).reshape(B, G, R, S, 3)
    o = g[..., 0:1] * o_cmp + g[..., 1:2] * o_slc + g[..., 2:3] * o_win
    return o.reshape(B, H, S, Dh).transpose(0, 2, 1, 3).reshape(B, S, H * Dh)


def _even_mixer(x, g_norm, w_in, shift_mu, w0, w_up, a0, a_up, g_up, k_k, k_a, r_k, ln_g, ln_b):
    B, S, D = x.shape
    z = norm_matmul(x.reshape(B * S, D), g_norm, w_in.astype(BF16)).reshape(B, S, -1)
    MW = 16 * HEAD_DIM
    qa, ka, va, zb = _split_cols(z, (MW, MW, MW, z.shape[-1] - 3 * MW))
    o_a = moba_attention(z)
    zb = zb + (_token_shift(zb) - zb) * shift_mu
    r, kb, vb, w_lo, a_lo, g_lo = _split_cols(zb, (MW, MW, MW, 64, 64, 128))
    o_b = _rwkv7_time_mix(r, kb, vb, w_lo, a_lo, g_lo, w0, w_up, a0, a_up, g_up, k_k, k_a, r_k, ln_g, ln_b)
    return o_a, o_b


def _odd_mixer(x, g_norm, w_in, pe_k, w1_k, w2_k, pe_v, w1_v, w2_v):
    B, S, D = x.shape
    perm, col = _odd_layout()
    w_p = jnp.take(jnp.pad(w_in, ((0, 0), (0, 1))), perm, axis=1).astype(BF16)
    z = norm_matmul(x.reshape(B * S, D), g_norm, w_p).reshape(B, S, -1)
    RW = RET_HEADS * RET_V_DIM
    rq, rk, rv, rg = _split_cols(z[..., :col["nq"]], (RET_HEADS * RET_QK_DIM, RET_HEADS * RET_QK_DIM, RW, RW))
    o_c = _retention(rq.reshape(B, S, RET_HEADS, RET_QK_DIM), rk.reshape(B, S, RET_HEADS, RET_QK_DIM),
                     rv.reshape(B, S, RET_HEADS, RET_V_DIM))
    o_c = jax.nn.silu(rg) * o_c.reshape(B, S, RW)
    cmp_kv, cmp_vk = nsa_compress(z, col["kc"], col["vc"], pe_k, w1_k, w2_k, pe_v, w1_v, w2_v)
    o_d = nsa_attention(z, cmp_kv, cmp_vk, col["nq"], col["slc"], col["win"], col["gate"])
    return o_c, o_d


def _odd_layout():
    G, Dh = NSA_KV_GROUPS, HEAD_DIM
    sizes = (RET_HEADS * RET_QK_DIM, RET_HEADS * RET_QK_DIM, RET_HEADS * RET_V_DIM, RET_HEADS * RET_V_DIM,
             NSA_HEADS * Dh) + (G * Dh,) * 6 + (3 * NSA_HEADS,)
    off = np.concatenate([[0], np.cumsum(sizes)])
    rq, rk, rv, rg, nq, kc, vc, ks, vs, kw, vw, ng = off[:-1]
    n_in = int(off[-1])
    pair = lambda a, b: np.concatenate([np.concatenate([a + g * Dh + np.arange(Dh), b + g * Dh + np.arange(Dh)])
                                        for g in range(G)])
    perm = np.concatenate([np.arange(ks), pair(ks, vs), pair(kw, vw), ng + np.arange(3 * NSA_HEADS)])
    n_pad = -(-len(perm) // 640) * 640
    perm = np.concatenate([perm, np.full(n_pad - len(perm), n_in)]).astype(np.int32)
    col = {"nq": int(nq), "kc": int(kc), "vc": int(vc), "slc": int(ks), "win": int(ks) + 2 * G * Dh,
           "gate": int(ks) + 4 * G * Dh}
    return perm, col


def kernel(x, mix_norm, ffn_norm, even_w_in, even_shift_mu, even_w0, even_w_up, even_a0, even_a_up, even_g_up, even_k_k, even_k_a, even_r_k, even_ln_g, even_ln_b, even_w_out, odd_w_in, odd_cmp_pe_k, odd_cmp_w1_k, odd_cmp_w2_k, odd_cmp_pe_v, odd_cmp_w1_v, odd_cmp_w2_v, odd_w_out, ffn_w1, ffn_w3, ffn_w2, final_norm):
    B, S, D = x.shape
    depth = mix_norm.shape[0]
    for layer in range(depth):
        i = layer // 2
        if layer % 2 == 0:
            o1, o2 = _even_mixer(x, mix_norm[layer], even_w_in[i], even_shift_mu[i], even_w0[i], even_w_up[i],
                                 even_a0[i], even_a_up[i], even_g_up[i], even_k_k[i], even_k_a[i], even_r_k[i],
                                 even_ln_g[i], even_ln_b[i])
            w_out = even_w_out[i]
        else:
            o1, o2 = _odd_mixer(x, mix_norm[layer], odd_w_in[i], odd_cmp_pe_k[i], odd_cmp_w1_k[i], odd_cmp_w2_k[i],
                                odd_cmp_pe_v[i], odd_cmp_w1_v[i], odd_cmp_w2_v[i])
            w_out = odd_w_out[i]
        T = B * S
        x2 = out_proj_residual(o1.reshape(T, -1), o2.reshape(T, -1), w_out.astype(BF16), x.reshape(T, D))
        x2 = ffn_residual(x2, ffn_norm[layer], ffn_w1[layer].astype(BF16), ffn_w3[layer].astype(BF16),
                          ffn_w2[layer].astype(BF16), final_norm if layer == depth - 1 else None)
        x = x2.reshape(B, S, D)
    return x
```

```python
import functools

import jax
import jax.numpy as jnp
import numpy as np
from jax import lax
from jax.experimental import pallas as pl
from jax.experimental.pallas import tpu as pltpu

F32 = jnp.float32
BF16 = jnp.bfloat16

V7X_LANES = 128
V7X_VMEM_BYTES = 64 * 1024 * 1024
VMEM_LIMIT = 56 * 1024 * 1024

NORM_EPS = 1e-6
HEAD_DIM = 64

MOBA_BLOCK = 256
MOBA_TOPK = 3
MOBA_Q_CHUNK = 16
RWKV_HEADS = 16
RWKV_GN_EPS = 6.4e-4

RET_HEADS = 8
RET_QK_DIM = 64
RET_V_DIM = 128
RET_CHUNK = 128
RET_GN_EPS = 1e-6
ROPE_BASE = 10000.0
NSA_HEADS = 16
NSA_KV_GROUPS = 4
NSA_CMP_BLOCK = 32
NSA_CMP_STRIDE = 16
NSA_SLC_BLOCK = 64
NSA_SLC_TOPN = 16
NSA_WINDOW = 512
NSA_Q_CHUNK = 32
WIN_Q_BLOCK = 128


def _params(*semantics):
    return pltpu.CompilerParams(dimension_semantics=semantics, vmem_limit_bytes=VMEM_LIMIT)


def _rms(x, g):
    return x * lax.rsqrt(jnp.mean(x * x, axis=-1, keepdims=True) + NORM_EPS) * g


def _norm_matmul_kernel(x_ref, g_ref, w_ref, o_ref, h_ref):
    @pl.when(pl.program_id(1) == 0)
    def _():
        h_ref[...] = _rms(x_ref[...], g_ref[...]).astype(BF16)

    o_ref[...] = jnp.dot(h_ref[...], w_ref[...], preferred_element_type=F32)


def norm_matmul(x, g, w, *, tm=512, tn=640):
    T, D = x.shape
    N = w.shape[1]
    assert T % tm == 0 and N % tn == 0
    return pl.pallas_call(
        _norm_matmul_kernel,
        grid=(T // tm, N // tn),
        in_specs=[
            pl.BlockSpec((tm, D), lambda i, j: (i, 0)),
            pl.BlockSpec((1, D), lambda i, j: (0, 0)),
            pl.BlockSpec((D, tn), lambda i, j: (0, j)),
        ],
        out_specs=pl.BlockSpec((tm, tn), lambda i, j: (i, j)),
        out_shape=jax.ShapeDtypeStruct((T, N), F32),
        scratch_shapes=[pltpu.VMEM((tm, D), BF16)],
        compiler_params=_params("parallel", "arbitrary"),
        name="norm_matmul",
    )(x, g.reshape(1, D), w)


def _out_proj_kernel(a_ref, b_ref, wa_ref, wb_ref, x_ref, o_ref):
    acc = jnp.dot(a_ref[...].astype(BF16), wa_ref[...], preferred_element_type=F32)
    acc += jnp.dot(b_ref[...].astype(BF16), wb_ref[...], preferred_element_type=F32)
    o_ref[...] = x_ref[...] + acc


def out_proj_residual(a, b, w, x, *, tm=512):
    T, D = x.shape
    Ka, Kb = a.shape[1], b.shape[1]
    assert T % tm == 0 and w.shape == (Ka + Kb, D)
    return pl.pallas_call(
        _out_proj_kernel,
        grid=(T // tm,),
        in_specs=[
            pl.BlockSpec((tm, Ka), lambda i: (i, 0)),
            pl.BlockSpec((tm, Kb), lambda i: (i, 0)),
            pl.BlockSpec((Ka, D), lambda i: (0, 0)),
            pl.BlockSpec((Kb, D), lambda i: (0, 0)),
            pl.BlockSpec((tm, D), lambda i: (i, 0)),
        ],
        out_specs=pl.BlockSpec((tm, D), lambda i: (i, 0)),
        out_shape=jax.ShapeDtypeStruct((T, D), F32),
        compiler_params=_params("parallel"),
        name="out_proj_residual",
    )(a, b, w[:Ka], w[Ka:], x)


def _ffn_kernel(x_ref, g_ref, w1_ref, w3_ref, w2_ref, gf_ref, o_ref, h_ref, acc_ref, *, final_norm):
    j = pl.program_id(1)

    @pl.when(j == 0)
    def _():
        h_ref[...] = _rms(x_ref[...], g_ref[...]).astype(BF16)
        acc_ref[...] = jnp.zeros_like(acc_ref)

    h = h_ref[...]
    a = jnp.dot(h, w1_ref[...], preferred_element_type=F32)
    b = jnp.dot(h, w3_ref[...], preferred_element_type=F32)
    act = (a * jax.nn.sigmoid(a) * b).astype(BF16)
    acc_ref[...] += jnp.dot(act, w2_ref[...], preferred_element_type=F32)

    @pl.when(j == pl.num_programs(1) - 1)
    def _():
        y = x_ref[...] + acc_ref[...]
        if final_norm:
            y = _rms(y, gf_ref[...])
        o_ref[...] = y


def ffn_residual(x, g, w1, w3, w2, g_final=None, *, tm=512, tf=512):
    T, D = x.shape
    Fh = w1.shape[1]
    assert T % tm == 0 and Fh % tf == 0
    final_norm = g_final is not None
    gf = (g_final if final_norm else g).reshape(1, D)
    return pl.pallas_call(
        functools.partial(_ffn_kernel, final_norm=final_norm),
        grid=(T // tm, Fh // tf),
        in_specs=[
            pl.BlockSpec((tm, D), lambda i, j: (i, 0)),
            pl.BlockSpec((1, D), lambda i, j: (0, 0)),
            pl.BlockSpec((D, tf), lambda i, j: (0, j)),
            pl.BlockSpec((D, tf), lambda i, j: (0, j)),
            pl.BlockSpec((tf, D), lambda i, j: (j, 0)),
            pl.BlockSpec((1, D), lambda i, j: (0, 0)),
        ],
        out_specs=pl.BlockSpec((tm, D), lambda i, j: (i, 0)),
        out_shape=jax.ShapeDtypeStruct((T, D), F32),
        scratch_shapes=[pltpu.VMEM((tm, D), BF16), pltpu.VMEM((tm, D), F32)],
        compiler_params=_params("parallel", "arbitrary"),
        name="ffn_residual",
    )(x, g.reshape(1, D), w1, w3, w2, gf)


NEG_BIG = -1e30
_NT = (((1,), (1,)), ((), ()))


def _moba_kernel(q_ref, k_ref, v_ref, o_ref, kb_ref, vb_ref, km_ref, acc_ref):
    L = MOBA_BLOCK
    nb = k_ref.shape[1] // L
    qi = pl.program_id(2)
    lane = lax.broadcasted_iota(jnp.int32, (L, V7X_LANES), 1)

    @pl.when(qi == 0)
    def _():
        k = k_ref[0]
        kb_ref[...] = k.astype(BF16)
        vb_ref[...] = v_ref[0].astype(BF16)
        km_ref[...] = jnp.zeros_like(km_ref)
        km_ref[0:nb, :] = jnp.mean(k.reshape(nb, L, V7X_LANES), axis=1)

    q = q_ref[0] * (HEAD_DIM ** -0.5)
    row = lax.broadcasted_iota(jnp.int32, (L, L), 0)
    col = lax.broadcasted_iota(jnp.int32, (L, L), 1)
    causal = col <= row
    past = lane < qi
    outs = []
    for h in range(2):
        in_head = (lane >= h * HEAD_DIM) & (lane < (h + 1) * HEAD_DIM)
        qm = jnp.where(in_head, q, 0.0)
        gate = lax.dot_general(qm, km_ref[...], _NT, precision=lax.Precision.HIGHEST,
                               preferred_element_type=F32)
        selmat = jnp.zeros((L, V7X_LANES), F32)
        for n in range(nb):
            g_n = jnp.sum(jnp.where(lane == n, gate, 0.0), axis=1, keepdims=True)
            beats = (gate > g_n) | ((gate == g_n) & (lane < n))
            rank = jnp.sum(jnp.where(past & beats, 1.0, 0.0), axis=1, keepdims=True)
            selmat = jnp.where((lane == n) & (rank < MOBA_TOPK), 1.0, selmat)
        selmat = jnp.where(past, selmat, 0.0)
        qb = qm.astype(BF16)

        start = pl.multiple_of(qi * L, L)
        s = lax.dot_general(qb, kb_ref[pl.ds(start, L), :], _NT, preferred_element_type=F32)
        s = jnp.where(causal, s, NEG_BIG)
        m0 = jnp.max(s, axis=1, keepdims=True)
        p = jnp.exp(s - m0)
        l0 = jnp.sum(p, axis=1, keepdims=True)
        acc_ref[h] = jnp.dot(p.astype(BF16), vb_ref[pl.ds(start, L), :], preferred_element_type=F32)

        def body(n, carry, h=h, qb=qb, selmat=selmat):
            m_prev, l_prev = carry
            st = pl.multiple_of(n * L, L)
            s = lax.dot_general(qb, kb_ref[pl.ds(st, L), :], _NT, preferred_element_type=F32)
            sel_n = jnp.sum(jnp.where(lane == n, selmat, 0.0), axis=1, keepdims=True) > 0.5
            s = jnp.where(sel_n, s, NEG_BIG)
            m_new = jnp.maximum(m_prev, jnp.max(s, axis=1, keepdims=True))
            alpha = jnp.exp(m_prev - m_new)
            p = jnp.exp(s - m_new)
            l_new = alpha * l_prev + jnp.sum(p, axis=1, keepdims=True)
            acc_ref[h] = alpha * acc_ref[h] + jnp.dot(p.astype(BF16), vb_ref[pl.ds(st, L), :],
                                                      preferred_element_type=F32)
            return m_new, l_new

        _, l_fin = lax.fori_loop(0, qi, body, (m0, l0))
        outs.append(acc_ref[h] / l_fin)
    o_ref[0] = jnp.where(lane < HEAD_DIM, outs[0], outs[1])


def moba_attention(z, *, n_heads=16):
    B, S, _ = z.shape
    L = MOBA_BLOCK
    assert S % L == 0 and n_heads % 2 == 0
    npair = n_heads // 2
    return pl.pallas_call(
        _moba_kernel,
        grid=(B, npair, S // L),
        in_specs=[
            pl.BlockSpec((1, L, V7X_LANES), lambda b, p, i: (b, i, p)),
            pl.BlockSpec((1, S, V7X_LANES), lambda b, p, i: (b, 0, npair + p)),
            pl.BlockSpec((1, S, V7X_LANES), lambda b, p, i: (b, 0, 2 * npair + p)),
        ],
        out_specs=pl.BlockSpec((1, L, V7X_LANES), lambda b, p, i: (b, i, p)),
        out_shape=jax.ShapeDtypeStruct((B, S, n_heads * HEAD_DIM), F32),
        scratch_shapes=[
            pltpu.VMEM((S, V7X_LANES), BF16),
            pltpu.VMEM((S, V7X_LANES), BF16),
            pltpu.VMEM((V7X_LANES, V7X_LANES), F32),
            pltpu.VMEM((2, L, V7X_LANES), F32),
        ],
        compiler_params=_params("parallel", "parallel", "arbitrary"),
        name="moba_attention",
    )(z, z, z)


NSA_TQ = 256
BIG = 3.0e38


def _gelu_tanh(x):
    return 0.5 * x * (1.0 + jnp.tanh(0.7978845608028654 * (x + 0.044715 * x * x * x)))


def _nsa_compress_kernel(xk0_ref, xk1_ref, xv0_ref, xv1_ref, pek_ref, pev_ref, w1k_ref, w1v_ref, w2k_ref, w2v_ref,
                         o1_ref, o2_ref):
    G, Lc, st = NSA_KV_GROUPS, NSA_CMP_BLOCK, NSA_CMP_STRIDE
    nrow = xk0_ref.shape[1] // st
    lane = lax.broadcasted_iota(jnp.int32, (nrow, G * HEAD_DIM), 1)

    def hidden(x_refs, pe_ref, w1_ref):
        acc = [jnp.zeros((G * nrow, V7X_LANES), F32) for _ in range(Lc // st)]
        for l in range(Lc):
            u, m = divmod(l, st)
            x = jnp.concatenate([r[0, pl.ds(m, nrow, stride=st), :] for r in x_refs], axis=1) + pe_ref[l:l + 1, :]
            xs = jnp.concatenate(
                [jnp.where((lane >= g * HEAD_DIM) & (lane < (g + 1) * HEAD_DIM), x, 0.0) for g in range(G)],
                axis=0).astype(BF16)
            acc[u] = acc[u] + jnp.dot(xs, w1_ref[l], preferred_element_type=F32)
        nxt = jnp.concatenate([pltpu.roll(acc[1][g * nrow:(g + 1) * nrow], nrow - 1, axis=0) for g in range(G)],
                              axis=0)
        return _gelu_tanh(acc[0] + nxt).astype(BF16)

    hk = hidden((xk0_ref, xk1_ref), pek_ref, w1k_ref)
    hv = hidden((xv0_ref, xv1_ref), pev_ref, w1v_ref)
    kc = jnp.dot(hk, w2k_ref[...], preferred_element_type=F32)
    vc = jnp.dot(hv, w2v_ref[...], preferred_element_type=F32)
    kv = kc + vc
    vk = pltpu.roll(kv, HEAD_DIM, axis=1)
    for g in range(G):
        o1_ref[0, :, g * V7X_LANES:(g + 1) * V7X_LANES] = kv[g * nrow:(g + 1) * nrow]
        o2_ref[0, :, g * V7X_LANES:(g + 1) * V7X_LANES] = vk[g * nrow:(g + 1) * nrow]


def nsa_compress(z, col_k, col_v, pe_k, w1_k, w2_k, pe_v, w1_v, w2_v):
    B, S, _ = z.shape
    G, Lc, st = NSA_KV_GROUPS, NSA_CMP_BLOCK, NSA_CMP_STRIDE
    GW = G * HEAD_DIM
    nrow = S // st
    hid = w1_k.shape[1]
    assert hid == V7X_LANES and col_k % GW == 0 and col_v % GW == 0
    tile_pe = lambda pe: jnp.tile(pe, (1, G))
    tile_w1 = lambda w: jnp.tile(w.reshape(Lc, 1, HEAD_DIM, hid), (1, G, 1, 1)).reshape(Lc, GW, hid).astype(BF16)
    w2k = jnp.pad(w2_k, ((0, 0), (0, HEAD_DIM))).astype(BF16)
    w2v = jnp.pad(w2_v, ((0, 0), (HEAD_DIM, 0))).astype(BF16)
    const = lambda shape: pl.BlockSpec(shape, lambda b: (0,) * len(shape))
    out = jax.ShapeDtypeStruct((B, nrow, G * V7X_LANES), F32)
    return pl.pallas_call(
        _nsa_compress_kernel,
        grid=(B,),
        in_specs=[
            pl.BlockSpec((1, S, V7X_LANES), lambda b: (b, 0, col_k // V7X_LANES)),
            pl.BlockSpec((1, S, V7X_LANES), lambda b: (b, 0, col_k // V7X_LANES + 1)),
            pl.BlockSpec((1, S, V7X_LANES), lambda b: (b, 0, col_v // V7X_LANES)),
            pl.BlockSpec((1, S, V7X_LANES), lambda b: (b, 0, col_v // V7X_LANES + 1)),
            const((Lc, GW)), const((Lc, GW)),
            const((Lc, GW, hid)), const((Lc, GW, hid)),
            const((hid, V7X_LANES)), const((hid, V7X_LANES)),
        ],
        out_specs=[pl.BlockSpec((1, nrow, G * V7X_LANES), lambda b: (b, 0, 0))] * 2,
        out_shape=[out, out],
        compiler_params=_params("parallel"),
        name="nsa_compress",
    )(z, z, z, z, tile_pe(pe_k), tile_pe(pe_v), tile_w1(w1_k), tile_w1(w1_v), w2k, w2v)


def _flash_step(qb, kv_s, kv_o, mask, m_prev, l_prev, acc_prev):
    s = lax.dot_general(qb, kv_s, _NT, preferred_element_type=F32)
    s = jnp.where(mask, s, NEG_BIG)
    m_new = jnp.maximum(m_prev, jnp.max(s, axis=1, keepdims=True))
    alpha = jnp.exp(m_prev - m_new)
    p = jnp.exp(s - m_new)
    l_new = alpha * l_prev + jnp.sum(p, axis=1, keepdims=True)
    acc_new = alpha * acc_prev + jnp.dot(p.astype(BF16), kv_o, preferred_element_type=F32)
    return m_new, l_new, acc_new


def _nsa_kernel(q_ref, c1_ref, c2_ref, s_ref, w_ref, g_ref, ov_ref, e_ref, o_ref,
                s1_ref, s2_ref, w1_ref, w2_ref, selx_ref, acc_ref):
    TQ = NSA_TQ
    R = NSA_HEADS // NSA_KV_GROUPS
    grp = pl.program_id(1)
    qi = pl.program_id(2)
    lane = lax.broadcasted_iota(jnp.int32, (TQ, V7X_LANES), 1)

    @pl.when(qi == 0)
    def _():
        kv = s_ref[0]
        s1_ref[...] = kv.astype(BF16)
        s2_ref[...] = pltpu.roll(kv, HEAD_DIM, axis=1).astype(BF16)
        kv = w_ref[0]
        w1_ref[...] = kv.astype(BF16)
        w2_ref[...] = pltpu.roll(kv, HEAD_DIM, axis=1).astype(BF16)

    q0 = pl.multiple_of(qi * TQ, TQ)
    row = lax.broadcasted_iota(jnp.int32, (TQ, TQ), 0)
    col = lax.broadcasted_iota(jnp.int32, (TQ, TQ), 1)
    causal = col <= row
    qi_mat = jnp.zeros((TQ, TQ), jnp.int32) + qi
    t_abs = q0 + lax.broadcasted_iota(jnp.int32, (TQ, V7X_LANES), 0)
    even_lanes = lane < HEAD_DIM

    c_kv, c_vk = c1_ref[0], c2_ref[0]
    c_kv_b, c_vk_b = c_kv.astype(BF16), c_vk.astype(BF16)
    cmask = lane * NSA_CMP_STRIDE + (NSA_CMP_BLOCK - 1) <= t_abs
    qbs, o_cmp = [], []
    p_sum = jnp.zeros((TQ, V7X_LANES), F32)
    for r in range(R):
        tile = q_ref[0, :, (r // 2) * V7X_LANES:(r // 2 + 1) * V7X_LANES] * (HEAD_DIM ** -0.5)
        qm = jnp.where(even_lanes if r % 2 == 0 else ~even_lanes, tile, 0.0)
        s = lax.dot_general(qm, c_kv if r % 2 == 0 else c_vk, _NT, precision=lax.Precision.HIGHEST,
                            preferred_element_type=F32)
        s = jnp.where(cmask, s, NEG_BIG)
        p = jnp.where(cmask, jnp.exp(s - jnp.max(s, axis=1, keepdims=True)), 0.0)
        den = jnp.sum(p, axis=1, keepdims=True)
        p = p / jnp.where(den > 0.0, den, 1.0)
        p_sum = p_sum + p
        o_cmp.append(jnp.dot(p.astype(BF16), c_vk_b if r % 2 == 0 else c_kv_b, preferred_element_type=F32))
        qbs.append(qm.astype(BF16))

    p_slc = jnp.dot(p_sum, ov_ref[...], precision=lax.Precision.HIGHEST, preferred_element_type=F32)
    own = t_abs // NSA_SLC_BLOCK
    score = jnp.where((lane == own) | (lane == 0), BIG, jnp.where(lane > own, -BIG, p_slc))
    sel = jnp.zeros((TQ, V7X_LANES), F32)
    for j in range(s_ref.shape[1] // NSA_SLC_BLOCK):
        s_j = jnp.sum(jnp.where(lane == j, score, 0.0), axis=1, keepdims=True)
        beats = (score > s_j) | ((score == s_j) & (lane < j))
        rank = jnp.sum(jnp.where(beats, 1.0, 0.0), axis=1, keepdims=True)
        sel = jnp.where((lane == j) & (rank < NSA_SLC_TOPN), 1.0, sel)
    sel = jnp.where(lane <= own, sel, 0.0).astype(BF16)

    def expand(kb, c):
        selx_ref[kb] = jnp.dot(sel, e_ref[kb], preferred_element_type=F32)
        return c

    lax.fori_loop(0, qi + 1, expand, 0)

    gates = jax.nn.sigmoid(g_ref[0])
    outs = []
    for r in range(R):
        ev = r % 2 == 0
        qb = qbs[r]
        ks_ref, vs_ref = (s1_ref, s2_ref) if ev else (s2_ref, s1_ref)
        kw_ref, vw_ref = (w1_ref, w2_ref) if ev else (w2_ref, w1_ref)
        neg = jnp.full((TQ, 1), NEG_BIG, F32)
        zero = jnp.zeros((TQ, 1), F32)
        zacc = jnp.zeros((TQ, V7X_LANES), F32)

        m, l, acc = _flash_step(qb, ks_ref[pl.ds(q0, TQ), :], vs_ref[pl.ds(q0, TQ), :],
                                causal & (selx_ref[qi] > 0.5), neg, zero, zacc)
        acc_ref[r] = acc

        def body(kb, carry, r=r, qb=qb, ks_ref=ks_ref, vs_ref=vs_ref):
            st = pl.multiple_of(kb * TQ, TQ)
            m2, l2, acc2 = _flash_step(qb, ks_ref[pl.ds(st, TQ), :], vs_ref[pl.ds(st, TQ), :],
                                       selx_ref[kb] > 0.5, carry[0], carry[1], acc_ref[r])
            acc_ref[r] = acc2
            return m2, l2

        _, l = lax.fori_loop(0, qi, body, (m, l))
        o_slc = acc_ref[r] / l

        m, l, acc = _flash_step(qb, kw_ref[pl.ds(q0, TQ), :], vw_ref[pl.ds(q0, TQ), :], causal, neg, zero, zacc)
        st1 = pl.multiple_of(jnp.maximum(qi - 1, 0) * TQ, TQ)
        m, l, acc = _flash_step(qb, kw_ref[pl.ds(st1, TQ), :], vw_ref[pl.ds(st1, TQ), :],
                                qi_mat >= 1, m, l, acc)
        st2 = pl.multiple_of(jnp.maximum(qi - 2, 0) * TQ, TQ)
        m, l, acc = _flash_step(qb, kw_ref[pl.ds(st2, TQ), :], vw_ref[pl.ds(st2, TQ), :],
                                (col > row) & (qi_mat >= 2), m, l, acc)
        o_win = acc / l

        c0 = (grp * R + r) * 3
        gate = lambda c: jnp.sum(jnp.where(lane == c, gates, 0.0), axis=1, keepdims=True)
        outs.append(gate(c0) * o_cmp[r] + gate(c0 + 1) * o_slc + gate(c0 + 2) * o_win)
    for p2 in range(R // 2):
        o_ref[0, :, p2 * V7X_LANES:(p2 + 1) * V7X_LANES] = jnp.where(even_lanes, outs[2 * p2], outs[2 * p2 + 1])


def nsa_attention(z, cmp_kv, cmp_vk, col_q, col_slc, col_win, col_gate):
    B, S, _ = z.shape
    G, TQ = NSA_KV_GROUPS, NSA_TQ
    R = NSA_HEADS // G
    QW = R * HEAD_DIM
    nkb = S // TQ
    ncmp = cmp_kv.shape[1]
    assert S % TQ == 0 and ncmp == V7X_LANES and S // NSA_SLC_BLOCK <= V7X_LANES
    assert col_q % QW == 0 and col_slc % V7X_LANES == 0 and col_win % V7X_LANES == 0 and col_gate % V7X_LANES == 0
    nc = (S - NSA_CMP_BLOCK) // NSA_CMP_STRIDE + 1
    c_start = np.arange(V7X_LANES) * NSA_CMP_STRIDE
    s_start = np.arange(V7X_LANES) * NSA_SLC_BLOCK
    overlap = ((c_start[:, None] <= s_start[None, :] + NSA_SLC_BLOCK - 1)
               & (c_start[:, None] + NSA_CMP_BLOCK - 1 >= s_start[None, :])
               & (np.arange(V7X_LANES)[:, None] < nc) & (np.arange(V7X_LANES)[None, :] < S // NSA_SLC_BLOCK))
    key_blk = (np.arange(nkb)[:, None, None] * TQ + np.arange(TQ)[None, None, :]) // NSA_SLC_BLOCK
    expand = (np.arange(V7X_LANES)[None, :, None] == key_blk)
    const = lambda shape: pl.BlockSpec(shape, lambda b, g, i: (0,) * len(shape))
    return pl.pallas_call(
        _nsa_kernel,
        grid=(B, G, S // TQ),
        in_specs=[
            pl.BlockSpec((1, TQ, QW), lambda b, g, i: (b, i, col_q // QW + g)),
            pl.BlockSpec((1, ncmp, V7X_LANES), lambda b, g, i: (b, 0, g)),
            pl.BlockSpec((1, ncmp, V7X_LANES), lambda b, g, i: (b, 0, g)),
            pl.BlockSpec((1, S, V7X_LANES), lambda b, g, i: (b, 0, col_slc // V7X_LANES + g)),
            pl.BlockSpec((1, S, V7X_LANES), lambda b, g, i: (b, 0, col_win // V7X_LANES + g)),
            pl.BlockSpec((1, TQ, V7X_LANES), lambda b, g, i: (b, i, col_gate // V7X_LANES)),
            const((V7X_LANES, V7X_LANES)),
            const((nkb, V7X_LANES, TQ)),
        ],
        out_specs=pl.BlockSpec((1, TQ, QW), lambda b, g, i: (b, i, g)),
        out_shape=jax.ShapeDtypeStruct((B, S, NSA_HEADS * HEAD_DIM), F32),
        scratch_shapes=[pltpu.VMEM((S, V7X_LANES), BF16)] * 4 + [
            pltpu.VMEM((nkb, TQ, TQ), F32),
            pltpu.VMEM((R, TQ, V7X_LANES), F32),
        ],
        compiler_params=_params("parallel", "parallel", "arbitrary"),
        name="nsa_attention",
    )(z, cmp_kv, cmp_vk, z, z, z, jnp.asarray(overlap, F32), jnp.asarray(expand, BF16))


RWKV_CHUNK = 64
RWKV_ROWS = 256


def _mm(a, b, exact=False, dims=None):
    dims = dims or (((1,), (0,)), ((), ()))
    if exact:
        return lax.dot_general(a, b, dims, precision=lax.Precision.HIGHEST, preferred_element_type=F32)
    return lax.dot_general(a.astype(BF16), b.astype(BF16), dims, preferred_element_type=F32)


def _head_sum(x, low):
    s0 = jnp.sum(jnp.where(low, x, 0.0), axis=1, keepdims=True)
    s1 = jnp.sum(jnp.where(low, 0.0, x), axis=1, keepdims=True)
    return jnp.where(low, s0, s1)


def _rwkv_kernel(r_ref, k_ref, v_ref, lo_ref, glo_ref, pp_ref, pl_ref, wup_ref, aup_ref, gup_ref, o_ref,
                 rs, ws, ks, vs, als, bes, gs, ys, hs):
    S = r_ref.shape[1]
    C, RB = RWKV_CHUNK, RWKV_ROWS
    pp = pp_ref[...]
    mu_r, mu_k, mu_v, w0, a0, k_k, k_a, r_k, ln_g, ln_b = [pp[i:i + 1, :] for i in range(10)]
    mu_lo, mu_g = pl_ref[0:1, :], pl_ref[1:2, :]
    low = lax.broadcasted_iota(jnp.int32, (RB, V7X_LANES), 1) < HEAD_DIM
    first = lax.broadcasted_iota(jnp.int32, (RB, V7X_LANES), 0) == 0

    def prologue(i, c):
        t0 = pl.multiple_of(i * RB, RB)
        tp = jnp.maximum(t0 - 1, 0)
        keep = jnp.where(i > 0, 1.0, 0.0)

        def shifted(ref, mu):
            x = ref[0, pl.ds(t0, RB), :]
            prev = jnp.where(first, ref[0, pl.ds(tp, 1), :] * keep, pltpu.roll(x, 1, axis=0))
            return x + (prev - x) * mu

        r, k, v = shifted(r_ref, mu_r), shifted(k_ref, mu_k), shifted(v_ref, mu_v)
        lo, glo = shifted(lo_ref, mu_lo), shifted(glo_ref, mu_g)
        wp = -(w0 + _mm(jnp.tanh(lo), wup_ref[...], exact=True))
        w = -(jnp.maximum(wp, 0.0) + jnp.log(1.0 + jnp.exp(-jnp.abs(wp)))) - 0.5
        a = jax.nn.sigmoid(a0 + _mm(lo, aup_ref[...], exact=True))
        kk = k * k_k
        kk = kk / jnp.maximum(jnp.sqrt(_head_sum(kk * kk, low)), 1e-12)
        k2 = k * (1.0 + (a - 1.0) * k_a)
        rs[pl.ds(t0, RB), :] = r
        ws[pl.ds(t0, RB), :] = -jnp.exp(w)
        ks[pl.ds(t0, RB), :] = k2
        vs[pl.ds(t0, RB), :] = v
        als[pl.ds(t0, RB), :] = -kk
        bes[pl.ds(t0, RB), :] = kk * a
        gs[pl.ds(t0, RB), :] = _mm(jax.nn.sigmoid(glo), gup_ref[...])
        o_ref[0, pl.ds(t0, RB), :] = _head_sum(r * k2 * r_k, low) * v
        return c

    lax.fori_loop(0, S // RB, prologue, 0)

    W2 = 2 * C
    row = lax.broadcasted_iota(jnp.int32, (W2, W2), 0)
    col = lax.broadcasted_iota(jnp.int32, (W2, W2), 1)
    t_idx, s_idx = row % C, col % C
    top, left = row < C, col < C
    same = top == left
    eye = jnp.where(row == col, 1.0, 0.0)
    tri = jnp.where(lax.broadcasted_iota(jnp.int32, (C, C), 1) <= lax.broadcasted_iota(jnp.int32, (C, C), 0), 1.0, 0.0)
    low_c = lax.broadcasted_iota(jnp.int32, (C, V7X_LANES), 1) < HEAD_DIM
    fold = lambda x: x[0:C] + x[C:W2]
    stack_heads = lambda x: jnp.concatenate([jnp.where(low_c, x, 0.0), jnp.where(low_c, 0.0, x)], axis=0)
    block_diag = lambda x: jnp.where(top, jnp.where(left, x, 0.0), jnp.where(left, 0.0, pltpu.roll(x, C, axis=1)))

    def chunk(c, carry):
        H = hs[...]
        t0 = pl.multiple_of(c * C, C)
        sl = pl.ds(t0, C)
        r, lw, k2, v, al, be = rs[sl, :], ws[sl, :], ks[sl, :], vs[sl, :], als[sl, :], bes[sl, :]
        logp = _mm(tri, lw, exact=True)
        P, Pinv = jnp.exp(logp), jnp.exp(-logp)
        At, Rt, Bt, Kt = al * jnp.exp(logp - lw), r * P, be * Pinv, k2 * Pinv
        PC = P[C - 1:C, :]
        A_bd, R_bd = stack_heads(At), stack_heads(Rt)
        Yt = jnp.concatenate([Bt, Kt], axis=0)
        A1 = jnp.where(s_idx < t_idx, _mm(A_bd, Yt, dims=_NT), 0.0)
        A2 = jnp.where(s_idx <= t_idx, _mm(R_bd, Yt, dims=_NT), 0.0)
        Aab, Arb = block_diag(A1), block_diag(A2)
        T, X = eye + Aab, Aab
        for _ in range(5):
            X = _mm(X, X)
            T = T + _mm(T, X)
        V0 = jnp.concatenate([jnp.zeros_like(v), v], axis=0)
        TA = _mm(T, A_bd)
        U0 = _mm(T, jnp.where(same, _mm(A1, V0), 0.0))
        AR = _mm(Arb, jnp.concatenate([TA, U0], axis=1))
        Rq = Rt + fold(AR[:, 0:W2])
        Y0 = fold(AR[:, W2:2 * W2] + jnp.where(same, _mm(A2, V0), 0.0))
        TAj, U0j = fold(TA), fold(U0)
        M = eye * PC + jnp.where(same, _mm((Bt * PC).T, TAj), 0.0)
        N0 = jnp.where(same, _mm(jnp.concatenate([Bt * PC, Kt * PC], axis=0).T,
                                 jnp.concatenate([U0j, v], axis=0)), 0.0)
        ys[sl, :] = _mm(Rq, H, exact=True) + Y0
        hs[...] = _mm(M, H, exact=True) + N0
        return carry

    hs[...] = jnp.zeros((W2, W2), F32)
    lax.fori_loop(0, S // C, chunk, 0)

    def epilogue(i, c):
        sl = pl.ds(pl.multiple_of(i * RB, RB), RB)
        y = ys[sl, :]
        d = y - _head_sum(y, low) * (1.0 / HEAD_DIM)
        var = _head_sum(d * d, low) * (1.0 / HEAD_DIM)
        yn = d * lax.rsqrt(var + RWKV_GN_EPS) * ln_g + ln_b
        o_ref[0, sl, :] = (yn + o_ref[0, sl, :]) * gs[sl, :]
        return c

    lax.fori_loop(0, S // RB, epilogue, 0)


def rwkv7_mixer(z, shift_mu, w0, w_up, a0, a_up, g_up, k_k, k_a, r_k, ln_g, ln_b):
    B, S, _ = z.shape
    CW = RWKV_HEADS * HEAD_DIM
    npair = CW // V7X_LANES
    base = 3 * CW // V7X_LANES
    lora = w_up.shape[0] + a_up.shape[0]
    assert lora == V7X_LANES and g_up.shape[0] == V7X_LANES and S % RWKV_ROWS == 0
    pp = jnp.stack([shift_mu[0:CW], shift_mu[CW:2 * CW], shift_mu[2 * CW:3 * CW], w0, a0, k_k, k_a,
                    r_k.reshape(CW), ln_g, ln_b])
    pp = jnp.pad(pp, ((0, 16 - pp.shape[0]), (0, 0)))
    pl2 = jnp.pad(shift_mu[3 * CW:].reshape(2, V7X_LANES), ((0, 6), (0, 0)))
    wup = jnp.pad(w_up, ((0, a_up.shape[0]), (0, 0)))
    aup = jnp.pad(a_up, ((w_up.shape[0], 0), (0, 0)))
    tile = lambda off: pl.BlockSpec((1, S, V7X_LANES), lambda b, p: (b, 0, base + off * npair + p))
    fixed = lambda off: pl.BlockSpec((1, S, V7X_LANES), lambda b, p: (b, 0, base + 3 * npair + off))
    seq = pltpu.VMEM((S, V7X_LANES), F32)
    return pl.pallas_call(
        _rwkv_kernel,
        grid=(B, npair),
        in_specs=[
            tile(0), tile(1), tile(2), fixed(0), fixed(1),
            pl.BlockSpec((16, V7X_LANES), lambda b, p: (0, p)),
            pl.BlockSpec((8, V7X_LANES), lambda b, p: (0, 0)),
            pl.BlockSpec((V7X_LANES, V7X_LANES), lambda b, p: (0, p)),
            pl.BlockSpec((V7X_LANES, V7X_LANES), lambda b, p: (0, p)),
            pl.BlockSpec((V7X_LANES, V7X_LANES), lambda b, p: (0, p)),
        ],
        out_specs=pl.BlockSpec((1, S, V7X_LANES), lambda b, p: (b, 0, p)),
        out_shape=jax.ShapeDtypeStruct((B, S, CW), F32),
        scratch_shapes=[seq] * 8 + [pltpu.VMEM((V7X_LANES, V7X_LANES), F32)],
        compiler_params=_params("parallel", "parallel"),
        name="rwkv7_mixer",
    )(z, z, z, z, z, pp, pl2, wup, aup, g_up.astype(BF16))


def _ret_kernel(q_ref, k_ref, v_ref, g_ref, cos_ref, sin_ref, din_ref, dq_ref, dk_ref, dc_ref, o_ref, st_ref):
    S = q_ref.shape[1]
    C, DV = RET_CHUNK, RET_V_DIM
    lane = lax.broadcasted_iota(jnp.int32, (C, V7X_LANES), 1)
    first_half = (lane % RET_QK_DIM) < RET_QK_DIM // 2
    st_ref[...] = jnp.zeros_like(st_ref)

    def chunk(c, carry):
        sl = pl.ds(pl.multiple_of(c * C, C), C)
        cos, sin = cos_ref[sl, :], sin_ref[sl, :]

        def rot(z):
            swapped = jnp.where(first_half, pltpu.roll(z, V7X_LANES - RET_QK_DIM // 2, axis=1),
                                pltpu.roll(z, RET_QK_DIM // 2, axis=1))
            return z * cos + swapped * sin

        q = rot(q_ref[0, sl, :])
        k = rot(k_ref[0, sl, :]) * (RET_QK_DIM ** -0.5)
        for h in range(2):
            in_head = (lane >= h * RET_QK_DIM) & (lane < (h + 1) * RET_QK_DIM)
            qm, km = jnp.where(in_head, q, 0.0), jnp.where(in_head, k, 0.0)
            v = v_ref[0, sl, h * DV:(h + 1) * DV]
            st = st_ref[h]
            inner = _mm(qm, k, dims=_NT) * din_ref[h]
            o = _mm(inner, v) + _mm(qm, st) * dq_ref[h]
            st_ref[h] = _mm((km * dk_ref[h]).T, v) + dc_ref[h, 0:1, :] * st
            d = o - jnp.mean(o, axis=1, keepdims=True)
            on = d * lax.rsqrt(jnp.mean(d * d, axis=1, keepdims=True) + RET_GN_EPS)
            gate = g_ref[0, sl, h * DV:(h + 1) * DV]
            o_ref[0, sl, h * DV:(h + 1) * DV] = gate * jax.nn.sigmoid(gate) * on
        return carry

    lax.fori_loop(0, S // C, chunk, 0)


def retention_mixer(z):
    B, S, _ = z.shape
    H, C, DK, DV = RET_HEADS, RET_CHUNK, RET_QK_DIM, RET_V_DIM
    assert S % C == 0 and 2 * DK == V7X_LANES and DV == V7X_LANES
    npair = H // 2
    half = DK // 2
    inv = ROPE_BASE ** (-jnp.arange(half, dtype=F32) / half)
    ang = jnp.arange(S, dtype=F32)[:, None] * inv
    cos = jnp.tile(jnp.cos(ang), (1, 4))
    sin = jnp.tile(jnp.concatenate([-jnp.sin(ang), jnp.sin(ang)], axis=1), (1, 2))
    log_g = jnp.asarray(np.log(1.0 - 2.0 ** (-5.0 - np.arange(H))), F32)
    n = jnp.arange(C, dtype=F32)
    diff = n[:, None] - n[None, :]
    d_in = jnp.where(diff >= 0, jnp.exp(jnp.maximum(diff, 0.0) * log_g[:, None, None]), 0.0)
    lanes = lambda t: jnp.broadcast_to(t[..., None], t.shape + (V7X_LANES,))
    d_q = lanes(jnp.exp((n + 1.0) * log_g[:, None]))
    d_k = lanes(jnp.exp((C - 1.0 - n) * log_g[:, None]))
    d_c = lanes(jnp.broadcast_to(jnp.exp(C * log_g)[:, None], (H, 8)))
    qk_tiles = H * DK // V7X_LANES
    return pl.pallas_call(
        _ret_kernel,
        grid=(B, npair),
        in_specs=[
            pl.BlockSpec((1, S, V7X_LANES), lambda b, p: (b, 0, p)),
            pl.BlockSpec((1, S, V7X_LANES), lambda b, p: (b, 0, qk_tiles + p)),
            pl.BlockSpec((1, S, 2 * DV), lambda b, p: (b, 0, 2 * qk_tiles * V7X_LANES // (2 * DV) + p)),
            pl.BlockSpec((1, S, 2 * DV), lambda b, p: (b, 0, (2 * qk_tiles * V7X_LANES + H * DV) // (2 * DV) + p)),
            pl.BlockSpec((S, V7X_LANES), lambda b, p: (0, 0)),
            pl.BlockSpec((S, V7X_LANES), lambda b, p: (0, 0)),
            pl.BlockSpec((2, C, C), lambda b, p: (p, 0, 0)),
            pl.BlockSpec((2, C, V7X_LANES), lambda b, p: (p, 0, 0)),
            pl.BlockSpec((2, C, V7X_LANES), lambda b, p: (p, 0, 0)),
            pl.BlockSpec((2, 8, V7X_LANES), lambda b, p: (p, 0, 0)),
        ],
        out_specs=pl.BlockSpec((1, S, 2 * DV), lambda b, p: (b, 0, p)),
        out_shape=jax.ShapeDtypeStruct((B, S, H * DV), F32),
        scratch_shapes=[pltpu.VMEM((2, V7X_LANES, DV), F32)],
        compiler_params=_params("parallel", "parallel"),
        name="retention_mixer",
    )(z, z, z, z, cos, sin, d_in, d_q, d_k, d_c)


def _split_cols(z, sizes):
    return jnp.split(z, [int(c) for c in np.cumsum(sizes)[:-1]], axis=-1)


def _head_group_norm(y, eps):
    mu = jnp.mean(y, axis=-1, keepdims=True)
    var = jnp.mean(jnp.square(y - mu), axis=-1, keepdims=True)
    return (y - mu) * lax.rsqrt(var + eps)


def _masked_softmax(s, mask):
    s = jnp.where(mask, s.astype(F32), -jnp.inf)
    m = jnp.max(s, axis=-1, keepdims=True)
    p = jnp.exp(s - jnp.where(jnp.isfinite(m), m, 0.0))
    den = jnp.sum(p, axis=-1, keepdims=True)
    return p / jnp.where(den > 0.0, den, 1.0)


def _token_shift(z):
    return jnp.pad(z[:, :-1], ((0, 0), (1, 0), (0, 0)))


def _rotary(z):
    S, d = z.shape[1], z.shape[-1]
    half = d // 2
    inv = ROPE_BASE ** (-jnp.arange(half, dtype=F32) / half)
    ang = jnp.arange(S, dtype=F32)[:, None] * inv
    cos, sin = jnp.cos(ang)[None, :, None, :], jnp.sin(ang)[None, :, None, :]
    z1, z2 = z[..., :half], z[..., half:]
    return jnp.concatenate([z1 * cos - z2 * sin, z1 * sin + z2 * cos], axis=-1)


def _moba_attention(q, k, v):
    B, S, H, Dh = q.shape
    L, Qc = MOBA_BLOCK, MOBA_Q_CHUNK
    nb = -(-S // L)
    n_sel = min(MOBA_TOPK, nb - 1)
    nq = S // Qc
    pad = nb * L - S
    qh = q.transpose(0, 2, 1, 3) * (Dh ** -0.5)
    pad_blocks = lambda z: jnp.pad(z.transpose(0, 2, 1, 3), ((0, 0), (0, 0), (0, pad), (0, 0))).reshape(B, H, nb, L, Dh)
    kb, vb = pad_blocks(k), pad_blocks(v)
    to_chunks = lambda z: jnp.moveaxis(z.reshape(B, H, nq, Qc, *z.shape[3:]), 2, 0)
    xs = [jnp.arange(nq), to_chunks(qh)]
    t = jnp.arange(S)
    gate = jnp.einsum('bhtd,bhnd->bhtn', qh, jnp.mean(kb, axis=3)).astype(F32)
    past = jnp.arange(nb)[None, :] < (t // L)[:, None]
    sel_score, sel_idx = lax.top_k(jnp.where(past, gate, -jnp.inf), n_sel)
    xs += [to_chunks(sel_idx), to_chunks(sel_score > -jnp.inf)]
    bi = jnp.arange(B)[:, None, None, None]
    hi = jnp.arange(H)[None, :, None, None]

    def chunk(args):
        c, q_c = args[0], args[1]
        t_c = c * Qc + jnp.arange(Qc)
        blk = (c * Qc) // L
        k_own = lax.dynamic_index_in_dim(kb, blk, axis=2, keepdims=False)
        v_own = lax.dynamic_index_in_dim(vb, blk, axis=2, keepdims=False)
        scores = [jnp.einsum('bhqd,bhld->bhql', q_c, k_own)]
        masks = [jnp.broadcast_to(blk * L + jnp.arange(L) <= t_c[:, None], (B, H, Qc, L))]
        idx_c, ok_c = args[2], args[3]
        k_sel = kb[bi, hi, idx_c]
        v_sel = vb[bi, hi, idx_c]
        scores.append(jnp.einsum('bhqd,bhqnld->bhqnl', q_c, k_sel).reshape(B, H, Qc, n_sel * L))
        masks.append(jnp.repeat(ok_c, L, axis=-1))
        p = _masked_softmax(jnp.concatenate(scores, axis=-1), jnp.concatenate(masks, axis=-1))
        out = jnp.einsum('bhql,bhld->bhqd', p[..., :L], v_own)
        out = out + jnp.einsum('bhqnl,bhqnld->bhqd', p[..., L:].reshape(B, H, Qc, n_sel, L), v_sel)
        return out

    o = lax.map(chunk, tuple(xs))
    o = jnp.moveaxis(o, 0, 2).reshape(B, H, S, Dh)
    return o.transpose(0, 2, 1, 3).reshape(B, S, H * Dh)


def _rwkv7_time_mix(r, k, v, w_lo, a_lo, g_lo, w0, w_up, a0, a_up, g_up, k_k, k_a, r_k, ln_g, ln_b):
    B, S, C = r.shape
    H, N = RWKV_HEADS, C // RWKV_HEADS
    w = -jax.nn.softplus(-(w0 + jnp.tanh(w_lo) @ w_up)) - 0.5
    decay = jnp.exp(-jnp.exp(w.astype(F32)))
    a = jax.nn.sigmoid(a0 + a_lo @ a_up)
    g = jax.nn.sigmoid(g_lo) @ g_up
    hd = lambda u: u.reshape(B, S, H, N).astype(F32)
    kk = hd(k * k_k)
    kk = kk / jnp.maximum(jnp.sqrt(jnp.sum(kk * kk, axis=-1, keepdims=True)), 1e-12)
    k = k * (1.0 + (a - 1.0) * k_a)
    rh, kh, vh, ah, wh = hd(r), hd(k), hd(v), hd(a), hd(decay)

    def step(state, inp):
        r_t, w_t, k_t, v_t, kk_t, a_t = inp
        sa = jnp.einsum('bhij,bhj->bhi', state, -kk_t)
        state = (state * w_t[:, :, None, :]
                 + sa[..., None] * (kk_t * a_t)[:, :, None, :]
                 + v_t[..., None] * k_t[:, :, None, :])
        return state, jnp.einsum('bhij,bhj->bhi', state, r_t)

    xs = tuple(jnp.moveaxis(u, 1, 0) for u in (rh, wh, kh, vh, kk, ah))
    _, y = lax.scan(step, jnp.zeros((B, H, N, N), F32), xs)
    y = jnp.moveaxis(y, 0, 1)
    y = _head_group_norm(y, RWKV_GN_EPS) * ln_g.reshape(H, N) + ln_b.reshape(H, N)
    bonus = jnp.sum(rh * kh * r_k, axis=-1, keepdims=True) * vh
    return ((y + bonus).reshape(B, S, C) * g).astype(r.dtype)


def _retention(q, k, v):
    B, S, H, dk = q.shape
    dv = v.shape[-1]
    C = RET_CHUNK
    nc = S // C
    q = _rotary(q)
    k = _rotary(k) * (dk ** -0.5)
    log_g = jnp.asarray(np.log(1.0 - 2.0 ** (-5.0 - np.arange(H))), F32)
    n = jnp.arange(C, dtype=F32)
    diff = n[:, None] - n[None, :]
    decay_in = jnp.where(diff >= 0, jnp.exp(jnp.maximum(diff, 0.0) * log_g[:, None, None]), 0.0)
    decay_q = jnp.exp((n + 1.0) * log_g[:, None])[None, :, :, None]
    decay_k = jnp.exp((C - 1.0 - n) * log_g[:, None])[None, :, :, None]
    decay_c = jnp.exp(C * log_g)[None, :, None, None]
    chunks = lambda z: z.reshape(B, nc, C, H, z.shape[-1]).transpose(1, 0, 3, 2, 4)

    def step(state, inp):
        qi, ki, vi = inp
        inner = jnp.einsum('bhnd,bhmd->bhnm', qi, ki) * decay_in
        o = jnp.einsum('bhnm,bhme->bhne', inner, vi) + jnp.einsum('bhnd,bhde->bhne', qi, state) * decay_q
        state = jnp.einsum('bhmd,bhme->bhde', ki * decay_k, vi) + decay_c * state
        return state, o

    _, o = lax.scan(step, jnp.zeros((B, H, dk, dv), F32), (chunks(q), chunks(k), chunks(v)))
    o = o.transpose(1, 0, 3, 2, 4).reshape(B, S, H, dv)
    return _head_group_norm(o, RET_GN_EPS)


def _nsa_attention(q, k_cmp_in, v_cmp_in, k_slc, v_slc, k_win, v_win, gates, pe_k, w1_k, w2_k, pe_v, w1_v, w2_v):
    B, S, H, Dh = q.shape
    G = NSA_KV_GROUPS
    R = H // G
    Lc, st, Ls, W = NSA_CMP_BLOCK, NSA_CMP_STRIDE, NSA_SLC_BLOCK, NSA_WINDOW
    t = jnp.arange(S)
    qg = (q * (Dh ** -0.5)).reshape(B, S, G, R, Dh).transpose(0, 2, 3, 1, 4)
    kv = lambda z: z.transpose(0, 2, 1, 3)
    nc = (S - Lc) // st + 1
    c_start = np.arange(nc) * st
    c_idx = c_start[:, None] + np.arange(Lc)[None, :]

    def compress(z, pe, w1, w2):
        blocks = kv(z)[:, :, c_idx] + pe
        return jax.nn.gelu(blocks.reshape(B, G, nc, Lc * Dh) @ w1) @ w2

    k_c = compress(k_cmp_in, pe_k, w1_k, w2_k)
    v_c = compress(v_cmp_in, pe_v, w1_v, w2_v)
    c_end = jnp.asarray(c_start + Lc - 1)
    p_cmp = _masked_softmax(jnp.einsum('bgrtd,bgcd->bgrtc', qg, k_c), c_end[None, :] <= t[:, None])
    o_cmp = jnp.einsum('bgrtc,bgcd->bgrtd', p_cmp, v_c)
    ns = S // Ls
    s_start = np.arange(ns) * Ls
    overlap = ((c_start[:, None] <= s_start[None, :] + Ls - 1)
               & (c_start[:, None] + Lc - 1 >= s_start[None, :])).astype(np.float32)
    p_slc = jnp.einsum('bgrtc,cj->bgtj', p_cmp, jnp.asarray(overlap))
    j = jnp.arange(ns)[None, :]
    own = (t // Ls)[:, None]
    score = jnp.where((j == own) | (j == 0), jnp.inf, jnp.where(j > own, -jnp.inf, p_slc))
    n_top = min(NSA_SLC_TOPN, ns)
    sel_score, sel_idx = lax.top_k(score, n_top)
    sel_ok = sel_score > -jnp.inf
    ks_b = kv(k_slc).reshape(B, G, ns, Ls, Dh)
    vs_b = kv(v_slc).reshape(B, G, ns, Ls, Dh)
    Qc = NSA_Q_CHUNK
    nq = S // Qc
    bi = jnp.arange(B)[:, None, None, None]
    gi = jnp.arange(G)[None, :, None, None]

    def sel_chunk(args):
        c, q_c, idx_c, ok_c = args
        t_c = c * Qc + jnp.arange(Qc)
        k_s = ks_b[bi, gi, idx_c]
        v_s = vs_b[bi, gi, idx_c]
        pos = idx_c[..., None] * Ls + jnp.arange(Ls)
        mask = ok_c[..., None] & (pos <= t_c[None, None, :, None, None])
        s = jnp.einsum('bgrqd,bgqnld->bgrqnl', q_c, k_s).reshape(B, G, R, Qc, n_top * Ls)
        p = _masked_softmax(s, mask.reshape(B, G, 1, Qc, n_top * Ls))
        return jnp.einsum('bgrqnl,bgqnld->bgrqd', p.reshape(B, G, R, Qc, n_top, Ls), v_s)

    o_slc = lax.map(sel_chunk, (jnp.arange(nq),
                                jnp.moveaxis(qg.reshape(B, G, R, nq, Qc, Dh), 3, 0),
                                jnp.moveaxis(sel_idx.reshape(B, G, nq, Qc, n_top), 2, 0),
                                jnp.moveaxis(sel_ok.reshape(B, G, nq, Qc, n_top), 2, 0)))
    o_slc = jnp.moveaxis(o_slc, 0, 3).reshape(B, G, R, S, Dh)
    Qw = WIN_Q_BLOCK
    nw = S // Qw
    kw_p = jnp.pad(kv(k_win), ((0, 0), (0, 0), (W, 0), (0, 0)))
    vw_p = jnp.pad(kv(v_win), ((0, 0), (0, 0), (W, 0), (0, 0)))

    def win_block(args):
        c, q_c = args
        s0 = c * Qw
        k_w = lax.dynamic_slice_in_dim(kw_p, s0, Qw + W, axis=2)
        v_w = lax.dynamic_slice_in_dim(vw_p, s0, Qw + W, axis=2)
        t_c = s0 + jnp.arange(Qw)
        s_pos = s0 - W + jnp.arange(Qw + W)
        d = t_c[:, None] - s_pos[None, :]
        mask = (d >= 0) & (d < W) & (s_pos[None, :] >= 0)
        p = _masked_softmax(jnp.einsum('bgrqd,bgkd->bgrqk', q_c, k_w), mask)
        return jnp.einsum('bgrqk,bgkd->bgrqd', p, v_w)

    o_win = lax.map(win_block, (jnp.arange(nw), jnp.moveaxis(qg.reshape(B, G, R, nw, Qw, Dh), 3, 0)))
    o_win = jnp.moveaxis(o_win, 0, 3).reshape(B, G, R, S, Dh)
    g = jnp.moveaxis(gates, 1, 2).reshape(B, G, R, S, 3)
    o = g[..., 0:1] * o_cmp + g[..., 1:2] * o_slc + g[..., 2:3] * o_win
    return o.reshape(B, H, S, Dh).transpose(0, 2, 1, 3).reshape(B, S, H * Dh)


def _even_mixer(x, g_norm, w_in, shift_mu, w0, w_up, a0, a_up, g_up, k_k, k_a, r_k, ln_g, ln_b):
    B, S, D = x.shape
    z = norm_matmul(x.reshape(B * S, D), g_norm, w_in.astype(BF16)).reshape(B, S, -1)
    o_a = moba_attention(z)
    o_b = rwkv7_mixer(z, shift_mu, w0, w_up, a0, a_up, g_up, k_k, k_a, r_k, ln_g, ln_b)
    return o_a, o_b


def _odd_mixer(x, g_norm, w_in, pe_k, w1_k, w2_k, pe_v, w1_v, w2_v):
    B, S, D = x.shape
    perm, col = _odd_layout()
    w_p = jnp.take(jnp.pad(w_in, ((0, 0), (0, 1))), perm, axis=1).astype(BF16)
    z = norm_matmul(x.reshape(B * S, D), g_norm, w_p).reshape(B, S, -1)
    o_c = retention_mixer(z)
    cmp_kv, cmp_vk = nsa_compress(z, col["kc"], col["vc"], pe_k, w1_k, w2_k, pe_v, w1_v, w2_v)
    o_d = nsa_attention(z, cmp_kv, cmp_vk, col["nq"], col["slc"], col["win"], col["gate"])
    return o_c, o_d


def _odd_layout():
    G, Dh = NSA_KV_GROUPS, HEAD_DIM
    sizes = (RET_HEADS * RET_QK_DIM, RET_HEADS * RET_QK_DIM, RET_HEADS * RET_V_DIM, RET_HEADS * RET_V_DIM,
             NSA_HEADS * Dh) + (G * Dh,) * 6 + (3 * NSA_HEADS,)
    off = np.concatenate([[0], np.cumsum(sizes)])
    rq, rk, rv, rg, nq, kc, vc, ks, vs, kw, vw, ng = off[:-1]
    n_in = int(off[-1])
    pair = lambda a, b: np.concatenate([np.concatenate([a + g * Dh + np.arange(Dh), b + g * Dh + np.arange(Dh)])
                                        for g in range(G)])
    perm = np.concatenate([np.arange(ks), pair(ks, vs), pair(kw, vw), ng + np.arange(3 * NSA_HEADS)])
    n_pad = -(-len(perm) // 640) * 640
    perm = np.concatenate([perm, np.full(n_pad - len(perm), n_in)]).astype(np.int32)
    col = {"nq": int(nq), "kc": int(kc), "vc": int(vc), "slc": int(ks), "win": int(ks) + 2 * G * Dh,
           "gate": int(ks) + 4 * G * Dh}
    return perm, col


def kernel(x, mix_norm, ffn_norm, even_w_in, even_shift_mu, even_w0, even_w_up, even_a0, even_a_up, even_g_up, even_k_k, even_k_a, even_r_k, even_ln_g, even_ln_b, even_w_out, odd_w_in, odd_cmp_pe_k, odd_cmp_w1_k, odd_cmp_w2_k, odd_cmp_pe_v, odd_cmp_w1_v, odd_cmp_w2_v, odd_w_out, ffn_w1, ffn_w3, ffn_w2, final_norm):
    B, S, D = x.shape
    depth = mix_norm.shape[0]
    for layer in range(depth):
        i = layer // 2
        if layer % 2 == 0:
            o1, o2 = _even_mixer(x, mix_norm[layer], even_w_in[i], even_shift_mu[i], even_w0[i], even_w_up[i],
                                 even_a0[i], even_a_up[i], even_g_up[i], even_k_k[i], even_k_a[i], even_r_k[i],
                                 even_ln_g[i], even_ln_b[i])
            w_out = even_w_out[i]
        else:
            o1, o2 = _odd_mixer(x, mix_norm[layer], odd_w_in[i], odd_cmp_pe_k[i], odd_cmp_w1_k[i], odd_cmp_w2_k[i],
                                odd_cmp_pe_v[i], odd_cmp_w1_v[i], odd_cmp_w2_v[i])
            w_out = odd_w_out[i]
        T = B * S
        x2 = out_proj_residual(o1.reshape(T, -1), o2.reshape(T, -1), w_out.astype(BF16), x.reshape(T, D))
        x2 = ffn_residual(x2, ffn_norm[layer], ffn_w1[layer].astype(BF16), ffn_w3[layer].astype(BF16),
                          ffn_w2[layer].astype(BF16), final_norm if layer == depth - 1 else None)
        x = x2.reshape(B, S, D)
    return x
```

```python
import functools

import jax
import jax.numpy as jnp
import numpy as np
from jax import lax
from jax.experimental import pallas as pl
from jax.experimental.pallas import tpu as pltpu

F32 = jnp.float32
BF16 = jnp.bfloat16

V7X_LANES = 128
V7X_VMEM_BYTES = 64 * 1024 * 1024
VMEM_LIMIT = 56 * 1024 * 1024

NORM_EPS = 1e-6
HEAD_DIM = 64

MOBA_BLOCK = 256
MOBA_TOPK = 3
RWKV_HEADS = 16
RWKV_GN_EPS = 6.4e-4

RET_HEADS = 8
RET_QK_DIM = 64
RET_V_DIM = 128
RET_CHUNK = 128
RET_GN_EPS = 1e-6
ROPE_BASE = 10000.0
NSA_HEADS = 16
NSA_KV_GROUPS = 4
NSA_CMP_BLOCK = 32
NSA_CMP_STRIDE = 16
NSA_SLC_BLOCK = 64
NSA_SLC_TOPN = 16
NSA_WINDOW = 512


def _params(*semantics):
    return pltpu.CompilerParams(dimension_semantics=semantics, vmem_limit_bytes=VMEM_LIMIT)


def _rms(x, g):
    return x * lax.rsqrt(jnp.mean(x * x, axis=-1, keepdims=True) + NORM_EPS) * g


def _norm_matmul_kernel(x_ref, g_ref, w_ref, o_ref, h_ref):
    @pl.when(pl.program_id(1) == 0)
    def _():
        h_ref[...] = _rms(x_ref[...], g_ref[...]).astype(BF16)

    o_ref[...] = jnp.dot(h_ref[...], w_ref[...], preferred_element_type=F32)


def norm_matmul(x, g, w, *, tm=512, tn=640):
    T, D = x.shape
    N = w.shape[1]
    assert T % tm == 0 and N % tn == 0
    return pl.pallas_call(
        _norm_matmul_kernel,
        grid=(T // tm, N // tn),
        in_specs=[
            pl.BlockSpec((tm, D), lambda i, j: (i, 0)),
            pl.BlockSpec((1, D), lambda i, j: (0, 0)),
            pl.BlockSpec((D, tn), lambda i, j: (0, j)),
        ],
        out_specs=pl.BlockSpec((tm, tn), lambda i, j: (i, j)),
        out_shape=jax.ShapeDtypeStruct((T, N), F32),
        scratch_shapes=[pltpu.VMEM((tm, D), BF16)],
        compiler_params=_params("parallel", "arbitrary"),
        name="norm_matmul",
    )(x, g.reshape(1, D), w)


def _out_proj_kernel(a_ref, b_ref, wa_ref, wb_ref, x_ref, o_ref):
    acc = jnp.dot(a_ref[...].astype(BF16), wa_ref[...], preferred_element_type=F32)
    acc += jnp.dot(b_ref[...].astype(BF16), wb_ref[...], preferred_element_type=F32)
    o_ref[...] = x_ref[...] + acc


def out_proj_residual(a, b, w, x, *, tm=512):
    T, D = x.shape
    Ka, Kb = a.shape[1], b.shape[1]
    assert T % tm == 0 and w.shape == (Ka + Kb, D)
    return pl.pallas_call(
        _out_proj_kernel,
        grid=(T // tm,),
        in_specs=[
            pl.BlockSpec((tm, Ka), lambda i: (i, 0)),
            pl.BlockSpec((tm, Kb), lambda i: (i, 0)),
            pl.BlockSpec((Ka, D), lambda i: (0, 0)),
            pl.BlockSpec((Kb, D), lambda i: (0, 0)),
            pl.BlockSpec((tm, D), lambda i: (i, 0)),
        ],
        out_specs=pl.BlockSpec((tm, D), lambda i: (i, 0)),
        out_shape=jax.ShapeDtypeStruct((T, D), F32),
        compiler_params=_params("parallel"),
        name="out_proj_residual",
    )(a, b, w[:Ka], w[Ka:], x)


def _ffn_kernel(x_ref, g_ref, w1_ref, w3_ref, w2_ref, gf_ref, o_ref, h_ref, acc_ref, *, final_norm):
    j = pl.program_id(1)

    @pl.when(j == 0)
    def _():
        h_ref[...] = _rms(x_ref[...], g_ref[...]).astype(BF16)
        acc_ref[...] = jnp.zeros_like(acc_ref)

    h = h_ref[...]
    a = jnp.dot(h, w1_ref[...], preferred_element_type=F32)
    b = jnp.dot(h, w3_ref[...], preferred_element_type=F32)
    act = (a * jax.nn.sigmoid(a) * b).astype(BF16)
    acc_ref[...] += jnp.dot(act, w2_ref[...], preferred_element_type=F32)

    @pl.when(j == pl.num_programs(1) - 1)
    def _():
        y = x_ref[...] + acc_ref[...]
        if final_norm:
            y = _rms(y, gf_ref[...])
        o_ref[...] = y


def ffn_residual(x, g, w1, w3, w2, g_final=None, *, tm=512, tf=512):
    T, D = x.shape
    Fh = w1.shape[1]
    assert T % tm == 0 and Fh % tf == 0
    final_norm = g_final is not None
    gf = (g_final if final_norm else g).reshape(1, D)
    return pl.pallas_call(
        functools.partial(_ffn_kernel, final_norm=final_norm),
        grid=(T // tm, Fh // tf),
        in_specs=[
            pl.BlockSpec((tm, D), lambda i, j: (i, 0)),
            pl.BlockSpec((1, D), lambda i, j: (0, 0)),
            pl.BlockSpec((D, tf), lambda i, j: (0, j)),
            pl.BlockSpec((D, tf), lambda i, j: (0, j)),
            pl.BlockSpec((tf, D), lambda i, j: (j, 0)),
            pl.BlockSpec((1, D), lambda i, j: (0, 0)),
        ],
        out_specs=pl.BlockSpec((tm, D), lambda i, j: (i, 0)),
        out_shape=jax.ShapeDtypeStruct((T, D), F32),
        scratch_shapes=[pltpu.VMEM((tm, D), BF16), pltpu.VMEM((tm, D), F32)],
        compiler_params=_params("parallel", "arbitrary"),
        name="ffn_residual",
    )(x, g.reshape(1, D), w1, w3, w2, gf)


NEG_BIG = -1e30
_NT = (((1,), (1,)), ((), ()))


def _moba_kernel(q_ref, k_ref, v_ref, o_ref, kb_ref, vb_ref, km_ref, acc_ref):
    L = MOBA_BLOCK
    nb = k_ref.shape[1] // L
    qi = pl.program_id(2)
    lane = lax.broadcasted_iota(jnp.int32, (L, V7X_LANES), 1)

    @pl.when(qi == 0)
    def _():
        k = k_ref[0]
        kb_ref[...] = k.astype(BF16)
        vb_ref[...] = v_ref[0].astype(BF16)
        km_ref[...] = jnp.zeros_like(km_ref)
        km_ref[0:nb, :] = jnp.mean(k.reshape(nb, L, V7X_LANES), axis=1)

    q = q_ref[0] * (HEAD_DIM ** -0.5)
    row = lax.broadcasted_iota(jnp.int32, (L, L), 0)
    col = lax.broadcasted_iota(jnp.int32, (L, L), 1)
    causal = col <= row
    jrow = lax.broadcasted_iota(jnp.int32, (nb, L), 0)
    past = jrow < qi
    start = pl.multiple_of(qi * L, L)
    qbs, selmats = [], []
    for h in range(2):
        in_head = (lane >= h * HEAD_DIM) & (lane < (h + 1) * HEAD_DIM)
        qm = jnp.where(in_head, q, 0.0)
        gate = lax.dot_general(km_ref[...], qm, _NT, precision=lax.Precision.HIGHEST,
                               preferred_element_type=F32)[0:nb]
        sel_t = jnp.zeros((nb, L), F32)
        for n in range(nb):
            g_n = gate[n:n + 1, :]
            beats = (gate > g_n) | ((gate == g_n) & (jrow < n))
            rank = jnp.sum(jnp.where(past & beats, 1.0, 0.0), axis=0, keepdims=True)
            sel_t = jnp.where((jrow == n) & (rank < MOBA_TOPK), 1.0, sel_t)
        sel_t = jnp.where(past, sel_t, 0.0)
        selmats.append(jnp.concatenate([sel_t, jnp.zeros((V7X_LANES - nb, L), F32)], axis=0).T)
        qbs.append(qm.astype(BF16))

    m, l, acc = _flash_steps(qbs, [kb_ref[pl.ds(start, L), :]] * 2, [vb_ref[pl.ds(start, L), :]] * 2, [causal] * 2,
                             [jnp.full((L, 1), NEG_BIG, F32)] * 2, [jnp.zeros((L, 1), F32)] * 2,
                             [jnp.zeros((L, V7X_LANES), F32)] * 2)
    acc_ref[0], acc_ref[1] = acc

    def body(n, carry):
        st = pl.multiple_of(n * L, L)
        sel_n = [jnp.sum(jnp.where(lane == n, selmats[h], 0.0), axis=1, keepdims=True) > 0.5 for h in range(2)]
        m2, l2, acc2 = _flash_steps(qbs, [kb_ref[pl.ds(st, L), :]] * 2, [vb_ref[pl.ds(st, L), :]] * 2, sel_n,
                                    carry[:2], carry[2:], [acc_ref[0], acc_ref[1]])
        acc_ref[0], acc_ref[1] = acc2
        return tuple(m2) + tuple(l2)

    fin = lax.fori_loop(0, qi, body, tuple(m) + tuple(l))
    o_ref[0] = jnp.where(lane < HEAD_DIM, acc_ref[0] / fin[2], acc_ref[1] / fin[3])


def moba_attention(z, *, n_heads=16):
    B, S, _ = z.shape
    L = MOBA_BLOCK
    assert S % L == 0 and n_heads % 2 == 0
    npair = n_heads // 2
    return pl.pallas_call(
        _moba_kernel,
        grid=(B, npair, S // L),
        in_specs=[
            pl.BlockSpec((1, L, V7X_LANES), lambda b, p, i: (b, i, p)),
            pl.BlockSpec((1, S, V7X_LANES), lambda b, p, i: (b, 0, npair + p)),
            pl.BlockSpec((1, S, V7X_LANES), lambda b, p, i: (b, 0, 2 * npair + p)),
        ],
        out_specs=pl.BlockSpec((1, L, V7X_LANES), lambda b, p, i: (b, i, p)),
        out_shape=jax.ShapeDtypeStruct((B, S, n_heads * HEAD_DIM), F32),
        scratch_shapes=[
            pltpu.VMEM((S, V7X_LANES), BF16),
            pltpu.VMEM((S, V7X_LANES), BF16),
            pltpu.VMEM((V7X_LANES, V7X_LANES), F32),
            pltpu.VMEM((2, L, V7X_LANES), F32),
        ],
        compiler_params=_params("parallel", "parallel", "arbitrary"),
        name="moba_attention",
    )(z, z, z)


NSA_TQ = 256
BIG = 3.0e38


def _gelu_tanh(x):
    return 0.5 * x * (1.0 + jnp.tanh(0.7978845608028654 * (x + 0.044715 * x * x * x)))


def _nsa_compress_kernel(xk0_ref, xk1_ref, xv0_ref, xv1_ref, pek_ref, pev_ref, w1k_ref, w1v_ref, w2k_ref, w2v_ref,
                         o1_ref, o2_ref):
    G, Lc, st = NSA_KV_GROUPS, NSA_CMP_BLOCK, NSA_CMP_STRIDE
    nrow = xk0_ref.shape[1] // st
    lane = lax.broadcasted_iota(jnp.int32, (nrow, G * HEAD_DIM), 1)

    def hidden(x_refs, pe_ref, w1_ref):
        acc = [jnp.zeros((G * nrow, V7X_LANES), F32) for _ in range(Lc // st)]
        for l in range(Lc):
            u, m = divmod(l, st)
            x = jnp.concatenate([r[0, pl.ds(m, nrow, stride=st), :] for r in x_refs], axis=1) + pe_ref[l:l + 1, :]
            xs = jnp.concatenate(
                [jnp.where((lane >= g * HEAD_DIM) & (lane < (g + 1) * HEAD_DIM), x, 0.0) for g in range(G)],
                axis=0).astype(BF16)
            acc[u] = acc[u] + jnp.dot(xs, w1_ref[l], preferred_element_type=F32)
        nxt = jnp.concatenate([pltpu.roll(acc[1][g * nrow:(g + 1) * nrow], nrow - 1, axis=0) for g in range(G)],
                              axis=0)
        return _gelu_tanh(acc[0] + nxt).astype(BF16)

    hk = hidden((xk0_ref, xk1_ref), pek_ref, w1k_ref)
    hv = hidden((xv0_ref, xv1_ref), pev_ref, w1v_ref)
    kc = jnp.dot(hk, w2k_ref[...], preferred_element_type=F32)
    vc = jnp.dot(hv, w2v_ref[...], preferred_element_type=F32)
    kv = kc + vc
    vk = pltpu.roll(kv, HEAD_DIM, axis=1)
    for g in range(G):
        o1_ref[0, :, g * V7X_LANES:(g + 1) * V7X_LANES] = kv[g * nrow:(g + 1) * nrow]
        o2_ref[0, :, g * V7X_LANES:(g + 1) * V7X_LANES] = vk[g * nrow:(g + 1) * nrow]


def nsa_compress(z, col_k, col_v, pe_k, w1_k, w2_k, pe_v, w1_v, w2_v):
    B, S, _ = z.shape
    G, Lc, st = NSA_KV_GROUPS, NSA_CMP_BLOCK, NSA_CMP_STRIDE
    GW = G * HEAD_DIM
    nrow = S // st
    hid = w1_k.shape[1]
    assert hid == V7X_LANES and col_k % GW == 0 and col_v % GW == 0
    tile_pe = lambda pe: jnp.tile(pe, (1, G))
    tile_w1 = lambda w: jnp.tile(w.reshape(Lc, 1, HEAD_DIM, hid), (1, G, 1, 1)).reshape(Lc, GW, hid).astype(BF16)
    w2k = jnp.pad(w2_k, ((0, 0), (0, HEAD_DIM))).astype(BF16)
    w2v = jnp.pad(w2_v, ((0, 0), (HEAD_DIM, 0))).astype(BF16)
    const = lambda shape: pl.BlockSpec(shape, lambda b: (0,) * len(shape))
    out = jax.ShapeDtypeStruct((B, nrow, G * V7X_LANES), F32)
    return pl.pallas_call(
        _nsa_compress_kernel,
        grid=(B,),
        in_specs=[
            pl.BlockSpec((1, S, V7X_LANES), lambda b: (b, 0, col_k // V7X_LANES)),
            pl.BlockSpec((1, S, V7X_LANES), lambda b: (b, 0, col_k // V7X_LANES + 1)),
            pl.BlockSpec((1, S, V7X_LANES), lambda b: (b, 0, col_v // V7X_LANES)),
            pl.BlockSpec((1, S, V7X_LANES), lambda b: (b, 0, col_v // V7X_LANES + 1)),
            const((Lc, GW)), const((Lc, GW)),
            const((Lc, GW, hid)), const((Lc, GW, hid)),
            const((hid, V7X_LANES)), const((hid, V7X_LANES)),
        ],
        out_specs=[pl.BlockSpec((1, nrow, G * V7X_LANES), lambda b: (b, 0, 0))] * 2,
        out_shape=[out, out],
        compiler_params=_params("parallel"),
        name="nsa_compress",
    )(z, z, z, z, tile_pe(pe_k), tile_pe(pe_v), tile_w1(w1_k), tile_w1(w1_v), w2k, w2v)


def _flash_steps(qbs, kv_s, kv_o, masks, m_prev, l_prev, acc_prev):
    hs = range(len(qbs))
    s = [lax.dot_general(qbs[h], kv_s[h], _NT, preferred_element_type=F32) for h in hs]
    s = [jnp.where(masks[h], s[h], NEG_BIG) for h in hs]
    m_new = [jnp.maximum(m_prev[h], jnp.max(s[h], axis=1, keepdims=True)) for h in hs]
    alpha = [jnp.exp(m_prev[h] - m_new[h]) for h in hs]
    p = [jnp.exp(s[h] - m_new[h]) for h in hs]
    l_new = [alpha[h] * l_prev[h] + jnp.sum(p[h], axis=1, keepdims=True) for h in hs]
    pv = [jnp.dot(p[h].astype(BF16), kv_o[h], preferred_element_type=F32) for h in hs]
    acc_new = [alpha[h] * acc_prev[h] + pv[h] for h in hs]
    return m_new, l_new, acc_new


def _nsa_kernel(q_ref, c1_ref, c2_ref, s_ref, w_ref, g_ref, ovt_ref, e_ref, o_ref,
                s1_ref, s2_ref, w1_ref, w2_ref, selx_ref, acc_ref):
    TQ = NSA_TQ
    R = NSA_HEADS // NSA_KV_GROUPS
    grp = pl.program_id(1)
    qi = pl.program_id(2)
    lane = lax.broadcasted_iota(jnp.int32, (TQ, V7X_LANES), 1)

    @pl.when(qi == 0)
    def _():
        kv = s_ref[0]
        s1_ref[...] = kv.astype(BF16)
        s2_ref[...] = pltpu.roll(kv, HEAD_DIM, axis=1).astype(BF16)
        kv = w_ref[0]
        w1_ref[...] = kv.astype(BF16)
        w2_ref[...] = pltpu.roll(kv, HEAD_DIM, axis=1).astype(BF16)

    q0 = pl.multiple_of(qi * TQ, TQ)
    row = lax.broadcasted_iota(jnp.int32, (TQ, TQ), 0)
    col = lax.broadcasted_iota(jnp.int32, (TQ, TQ), 1)
    causal = col <= row
    qi_mat = jnp.zeros((TQ, TQ), jnp.int32) + qi
    t_abs = q0 + lax.broadcasted_iota(jnp.int32, (TQ, V7X_LANES), 0)
    even_lanes = lane < HEAD_DIM

    c_kv, c_vk = c1_ref[0], c2_ref[0]
    c_kv_b, c_vk_b = c_kv.astype(BF16), c_vk.astype(BF16)
    cmask = lane * NSA_CMP_STRIDE + (NSA_CMP_BLOCK - 1) <= t_abs
    qbs, o_cmp = [], []
    p_sum = jnp.zeros((TQ, V7X_LANES), F32)
    for r in range(R):
        tile = q_ref[0, :, (r // 2) * V7X_LANES:(r // 2 + 1) * V7X_LANES] * (HEAD_DIM ** -0.5)
        qm = jnp.where(even_lanes if r % 2 == 0 else ~even_lanes, tile, 0.0)
        s = lax.dot_general(qm, c_kv if r % 2 == 0 else c_vk, _NT, precision=lax.Precision.HIGHEST,
                            preferred_element_type=F32)
        s = jnp.where(cmask, s, NEG_BIG)
        p = jnp.where(cmask, jnp.exp(s - jnp.max(s, axis=1, keepdims=True)), 0.0)
        den = jnp.sum(p, axis=1, keepdims=True)
        p = p / jnp.where(den > 0.0, den, 1.0)
        p_sum = p_sum + p
        o_cmp.append(jnp.dot(p.astype(BF16), c_vk_b if r % 2 == 0 else c_kv_b, preferred_element_type=F32))
        qbs.append(qm.astype(BF16))

    nblk = s_ref.shape[1] // NSA_SLC_BLOCK
    p_slc = lax.dot_general(ovt_ref[...], p_sum, _NT, precision=lax.Precision.HIGHEST,
                            preferred_element_type=F32)[0:nblk]
    jrow = lax.broadcasted_iota(jnp.int32, (nblk, TQ), 0)
    own = (q0 + lax.broadcasted_iota(jnp.int32, (nblk, TQ), 1)) // NSA_SLC_BLOCK
    score = jnp.where((jrow == own) | (jrow == 0), BIG, jnp.where(jrow > own, -BIG, p_slc))
    sel_t = jnp.zeros((nblk, TQ), F32)
    for j in range(nblk):
        s_j = score[j:j + 1, :]
        beats = (score > s_j) | ((score == s_j) & (jrow < j))
        rank = jnp.sum(jnp.where(beats, 1.0, 0.0), axis=0, keepdims=True)
        sel_t = jnp.where((jrow == j) & (rank < NSA_SLC_TOPN), 1.0, sel_t)
    sel_t = jnp.where(jrow <= own, sel_t, 0.0)
    sel = jnp.concatenate([sel_t, jnp.zeros((V7X_LANES - nblk, TQ), F32)], axis=0).T.astype(BF16)

    def expand(kb, c):
        selx_ref[kb] = jnp.dot(sel, e_ref[kb], preferred_element_type=F32)
        return c

    lax.fori_loop(0, qi + 1, expand, 0)

    neg = [jnp.full((TQ, 1), NEG_BIG, F32)] * R
    zero = [jnp.zeros((TQ, 1), F32)] * R
    zacc = [jnp.zeros((TQ, V7X_LANES), F32)] * R

    def kv_blocks(kv_ref, vk_ref, start):
        kv, vk = kv_ref[pl.ds(start, TQ), :], vk_ref[pl.ds(start, TQ), :]
        return [kv if r % 2 == 0 else vk for r in range(R)], [vk if r % 2 == 0 else kv for r in range(R)]

    ks, vs = kv_blocks(s1_ref, s2_ref, q0)
    m, l, acc = _flash_steps(qbs, ks, vs, [causal & (selx_ref[qi] > 0.5)] * R, neg, zero, zacc)
    for r in range(R):
        acc_ref[r] = acc[r]

    def body(kb, carry):
        ks, vs = kv_blocks(s1_ref, s2_ref, pl.multiple_of(kb * TQ, TQ))
        m2, l2, acc2 = _flash_steps(qbs, ks, vs, [selx_ref[kb] > 0.5] * R, carry[:R], carry[R:],
                                    [acc_ref[r] for r in range(R)])
        for r in range(R):
            acc_ref[r] = acc2[r]
        return tuple(m2) + tuple(l2)

    fin = lax.fori_loop(0, qi, body, tuple(m) + tuple(l))

    ks, vs = kv_blocks(w1_ref, w2_ref, q0)
    m, l, acc = _flash_steps(qbs, ks, vs, [causal] * R, neg, zero, zacc)
    ks, vs = kv_blocks(w1_ref, w2_ref, pl.multiple_of(jnp.maximum(qi - 1, 0) * TQ, TQ))
    m, l, acc = _flash_steps(qbs, ks, vs, [qi_mat >= 1] * R, m, l, acc)
    ks, vs = kv_blocks(w1_ref, w2_ref, pl.multiple_of(jnp.maximum(qi - 2, 0) * TQ, TQ))
    m, l, acc = _flash_steps(qbs, ks, vs, [(col > row) & (qi_mat >= 2)] * R, m, l, acc)

    gates = jax.nn.sigmoid(g_ref[0])
    outs = []
    for r in range(R):
        o_slc = acc_ref[r] / fin[R + r]
        o_win = acc[r] / l[r]
        c0 = (grp * R + r) * 3
        gate = lambda c: jnp.sum(jnp.where(lane == c, gates, 0.0), axis=1, keepdims=True)
        outs.append(gate(c0) * o_cmp[r] + gate(c0 + 1) * o_slc + gate(c0 + 2) * o_win)
    for p2 in range(R // 2):
        o_ref[0, :, p2 * V7X_LANES:(p2 + 1) * V7X_LANES] = jnp.where(even_lanes, outs[2 * p2], outs[2 * p2 + 1])


def nsa_attention(z, cmp_kv, cmp_vk, col_q, col_slc, col_win, col_gate):
    B, S, _ = z.shape
    G, TQ = NSA_KV_GROUPS, NSA_TQ
    R = NSA_HEADS // G
    QW = R * HEAD_DIM
    nkb = S // TQ
    ncmp = cmp_kv.shape[1]
    assert S % TQ == 0 and ncmp == V7X_LANES and S // NSA_SLC_BLOCK <= V7X_LANES and NSA_WINDOW == 2 * TQ
    assert col_q % QW == 0 and col_slc % V7X_LANES == 0 and col_win % V7X_LANES == 0 and col_gate % V7X_LANES == 0
    nc = (S - NSA_CMP_BLOCK) // NSA_CMP_STRIDE + 1
    c_start = np.arange(V7X_LANES) * NSA_CMP_STRIDE
    s_start = np.arange(V7X_LANES) * NSA_SLC_BLOCK
    overlap = ((c_start[:, None] <= s_start[None, :] + NSA_SLC_BLOCK - 1)
               & (c_start[:, None] + NSA_CMP_BLOCK - 1 >= s_start[None, :])
               & (np.arange(V7X_LANES)[:, None] < nc) & (np.arange(V7X_LANES)[None, :] < S // NSA_SLC_BLOCK))
    key_blk = (np.arange(nkb)[:, None, None] * TQ + np.arange(TQ)[None, None, :]) // NSA_SLC_BLOCK
    expand = (np.arange(V7X_LANES)[None, :, None] == key_blk)
    const = lambda shape: pl.BlockSpec(shape, lambda b, g, i: (0,) * len(shape))
    return pl.pallas_call(
        _nsa_kernel,
        grid=(B, G, S // TQ),
        in_specs=[
            pl.BlockSpec((1, TQ, QW), lambda b, g, i: (b, i, col_q // QW + g)),
            pl.BlockSpec((1, ncmp, V7X_LANES), lambda b, g, i: (b, 0, g)),
            pl.BlockSpec((1, ncmp, V7X_LANES), lambda b, g, i: (b, 0, g)),
            pl.BlockSpec((1, S, V7X_LANES), lambda b, g, i: (b, 0, col_slc // V7X_LANES + g)),
            pl.BlockSpec((1, S, V7X_LANES), lambda b, g, i: (b, 0, col_win // V7X_LANES + g)),
            pl.BlockSpec((1, TQ, V7X_LANES), lambda b, g, i: (b, i, col_gate // V7X_LANES)),
            const((V7X_LANES, V7X_LANES)),
            const((nkb, V7X_LANES, TQ)),
        ],
        out_specs=pl.BlockSpec((1, TQ, QW), lambda b, g, i: (b, i, g)),
        out_shape=jax.ShapeDtypeStruct((B, S, NSA_HEADS * HEAD_DIM), F32),
        scratch_shapes=[pltpu.VMEM((S, V7X_LANES), BF16)] * 4 + [
            pltpu.VMEM((nkb, TQ, TQ), F32),
            pltpu.VMEM((R, TQ, V7X_LANES), F32),
        ],
        compiler_params=_params("parallel", "parallel", "arbitrary"),
        name="nsa_attention",
    )(z, cmp_kv, cmp_vk, z, z, z, jnp.asarray(overlap.T, F32), jnp.asarray(expand, BF16))


RWKV_CHUNK = 64
RWKV_ROWS = 256
RWKV_INTERLEAVE = 4


def _mm(a, b, exact=False, dims=None):
    dims = dims or (((1,), (0,)), ((), ()))
    if exact:
        return lax.dot_general(a, b, dims, precision=lax.Precision.HIGHEST, preferred_element_type=F32)
    return lax.dot_general(a.astype(BF16), b.astype(BF16), dims, preferred_element_type=F32)


def _head_sum(x, low):
    s0 = jnp.sum(jnp.where(low, x, 0.0), axis=1, keepdims=True)
    s1 = jnp.sum(jnp.where(low, 0.0, x), axis=1, keepdims=True)
    return jnp.where(low, s0, s1)


def _rwkv_kernel(r_ref, k_ref, v_ref, lo_ref, glo_ref, pp_ref, pl_ref, wup_ref, aup_ref, gup_ref, o_ref,
                 rs, ws, ks, vs, als, bes, gs, ys, hs, rqs, ms, ns):
    S = r_ref.shape[1]
    C, RB = RWKV_CHUNK, RWKV_ROWS
    pp = pp_ref[...]
    mu_r, mu_k, mu_v, w0, a0, k_k, k_a, r_k, ln_g, ln_b = [pp[i:i + 1, :] for i in range(10)]
    mu_lo, mu_g = pl_ref[0:1, :], pl_ref[1:2, :]
    low = lax.broadcasted_iota(jnp.int32, (RB, V7X_LANES), 1) < HEAD_DIM
    first = lax.broadcasted_iota(jnp.int32, (RB, V7X_LANES), 0) == 0

    def prologue(i, c):
        t0 = pl.multiple_of(i * RB, RB)
        tp = jnp.maximum(t0 - 1, 0)
        keep = jnp.where(i > 0, 1.0, 0.0)

        def shifted(ref, mu):
            x = ref[0, pl.ds(t0, RB), :]
            prev = jnp.where(first, ref[0, pl.ds(tp, 1), :] * keep, pltpu.roll(x, 1, axis=0))
            return x + (prev - x) * mu

        r, k, v = shifted(r_ref, mu_r), shifted(k_ref, mu_k), shifted(v_ref, mu_v)
        lo, glo = shifted(lo_ref, mu_lo), shifted(glo_ref, mu_g)
        wp = -(w0 + _mm(jnp.tanh(lo), wup_ref[...], exact=True))
        w = -(jnp.maximum(wp, 0.0) + jnp.log(1.0 + jnp.exp(-jnp.abs(wp)))) - 0.5
        a = jax.nn.sigmoid(a0 + _mm(lo, aup_ref[...], exact=True))
        kk = k * k_k
        kk = kk / jnp.maximum(jnp.sqrt(_head_sum(kk * kk, low)), 1e-12)
        k2 = k * (1.0 + (a - 1.0) * k_a)
        rs[pl.ds(t0, RB), :] = r
        ws[pl.ds(t0, RB), :] = -jnp.exp(w)
        ks[pl.ds(t0, RB), :] = k2
        vs[pl.ds(t0, RB), :] = v
        als[pl.ds(t0, RB), :] = -kk
        bes[pl.ds(t0, RB), :] = kk * a
        gs[pl.ds(t0, RB), :] = _mm(jax.nn.sigmoid(glo), gup_ref[...])
        o_ref[0, pl.ds(t0, RB), :] = _head_sum(r * k2 * r_k, low) * v
        return c

    lax.fori_loop(0, S // RB, prologue, 0)

    W2 = 2 * C
    row = lax.broadcasted_iota(jnp.int32, (W2, W2), 0)
    col = lax.broadcasted_iota(jnp.int32, (W2, W2), 1)
    t_idx, s_idx = row % C, col % C
    top, left = row < C, col < C
    same = top == left
    eye = jnp.where(row == col, 1.0, 0.0)
    tri = jnp.where(lax.broadcasted_iota(jnp.int32, (C, C), 1) <= lax.broadcasted_iota(jnp.int32, (C, C), 0), 1.0, 0.0)
    low_c = lax.broadcasted_iota(jnp.int32, (C, V7X_LANES), 1) < HEAD_DIM
    fold = lambda x: x[0:C] + x[C:W2]
    stack_heads = lambda x: jnp.concatenate([jnp.where(low_c, x, 0.0), jnp.where(low_c, 0.0, x)], axis=0)
    block_diag = lambda x: jnp.where(top, jnp.where(left, x, 0.0), jnp.where(left, 0.0, pltpu.roll(x, C, axis=1)))

    def transfers(i, carry):
        each = lambda f, *xs: [f(*a) for a in zip(*xs)]
        cs = [i * RWKV_INTERLEAVE + u for u in range(RWKV_INTERLEAVE)]
        sls = [pl.ds(pl.multiple_of(c * C, C), C) for c in cs]
        r, lw, k2, v, al, be = ([ref[sl, :] for sl in sls] for ref in (rs, ws, ks, vs, als, bes))
        logp = each(lambda x: _mm(tri, x, exact=True), lw)
        P = each(jnp.exp, logp)
        Pinv = each(lambda x: jnp.exp(-x), logp)
        At = each(lambda a_, lp, w_: a_ * jnp.exp(lp - w_), al, logp, lw)
        Rt, Bt, Kt = each(jnp.multiply, r, P), each(jnp.multiply, be, Pinv), each(jnp.multiply, k2, Pinv)
        PC = each(lambda p: p[C - 1:C, :], P)
        A_bd, R_bd = each(stack_heads, At), each(stack_heads, Rt)
        Yt = each(lambda b, k: jnp.concatenate([b, k], axis=0), Bt, Kt)
        A1 = each(lambda a, y: jnp.where(s_idx < t_idx, _mm(a, y, dims=_NT), 0.0), A_bd, Yt)
        A2 = each(lambda a, y: jnp.where(s_idx <= t_idx, _mm(a, y, dims=_NT), 0.0), R_bd, Yt)
        X, Arb = each(block_diag, A1), each(block_diag, A2)
        T = each(lambda x: eye + x, X)
        for _ in range(5):
            X = each(lambda x: _mm(x, x), X)
            T = each(lambda t, x: t + _mm(t, x), T, X)
        V0 = each(lambda x: jnp.concatenate([jnp.zeros_like(x), x], axis=0), v)
        TA = each(_mm, T, A_bd)
        AkV = each(lambda a, x: jnp.where(same, _mm(a, x), 0.0), A1, V0)
        U0 = each(_mm, T, AkV)
        AR = each(lambda a, t, u: _mm(a, jnp.concatenate([t, u], axis=1)), Arb, TA, U0)
        ArkV = each(lambda a, x: jnp.where(same, _mm(a, x), 0.0), A2, V0)
        Mx = each(lambda b, p, t: _mm((b * p).T, fold(t)), Bt, PC, TA)
        Nx = each(lambda b, k, p, u, x: _mm(jnp.concatenate([b * p, k * p], axis=0).T,
                                            jnp.concatenate([fold(u), x], axis=0)), Bt, Kt, PC, U0, v)
        for u in range(RWKV_INTERLEAVE):
            ys[sls[u], :] = fold(AR[u][:, W2:2 * W2] + ArkV[u])
            rqs[cs[u]] = Rt[u] + fold(AR[u][:, 0:W2])
            ms[cs[u]] = eye * PC[u] + jnp.where(same, Mx[u], 0.0)
            ns[cs[u]] = jnp.where(same, Nx[u], 0.0)
        return carry

    lax.fori_loop(0, S // C // RWKV_INTERLEAVE, transfers, 0)

    def advance(c, carry):
        H = hs[...]
        sl = pl.ds(pl.multiple_of(c * C, C), C)
        ys[sl, :] += _mm(rqs[c], H, exact=True)
        hs[...] = _mm(ms[c], H, exact=True) + ns[c]
        return carry

    hs[...] = jnp.zeros((W2, W2), F32)
    lax.fori_loop(0, S // C, advance, 0)

    def epilogue(i, c):
        sl = pl.ds(pl.multiple_of(i * RB, RB), RB)
        y = ys[sl, :]
        d = y - _head_sum(y, low) * (1.0 / HEAD_DIM)
        var = _head_sum(d * d, low) * (1.0 / HEAD_DIM)
        yn = d * lax.rsqrt(var + RWKV_GN_EPS) * ln_g + ln_b
        o_ref[0, sl, :] = (yn + o_ref[0, sl, :]) * gs[sl, :]
        return c

    lax.fori_loop(0, S // RB, epilogue, 0)


def rwkv7_mixer(z, shift_mu, w0, w_up, a0, a_up, g_up, k_k, k_a, r_k, ln_g, ln_b):
    B, S, _ = z.shape
    CW = RWKV_HEADS * HEAD_DIM
    npair = CW // V7X_LANES
    base = 3 * CW // V7X_LANES
    lora = w_up.shape[0] + a_up.shape[0]
    assert lora == V7X_LANES and g_up.shape[0] == V7X_LANES and S % RWKV_ROWS == 0
    pp = jnp.stack([shift_mu[0:CW], shift_mu[CW:2 * CW], shift_mu[2 * CW:3 * CW], w0, a0, k_k, k_a,
                    r_k.reshape(CW), ln_g, ln_b])
    pp = jnp.pad(pp, ((0, 16 - pp.shape[0]), (0, 0)))
    pl2 = jnp.pad(shift_mu[3 * CW:].reshape(2, V7X_LANES), ((0, 6), (0, 0)))
    wup = jnp.pad(w_up, ((0, a_up.shape[0]), (0, 0)))
    aup = jnp.pad(a_up, ((w_up.shape[0], 0), (0, 0)))
    tile = lambda off: pl.BlockSpec((1, S, V7X_LANES), lambda b, p: (b, 0, base + off * npair + p))
    fixed = lambda off: pl.BlockSpec((1, S, V7X_LANES), lambda b, p: (b, 0, base + 3 * npair + off))
    seq = pltpu.VMEM((S, V7X_LANES), F32)
    return pl.pallas_call(
        _rwkv_kernel,
        grid=(B, npair),
        in_specs=[
            tile(0), tile(1), tile(2), fixed(0), fixed(1),
            pl.BlockSpec((16, V7X_LANES), lambda b, p: (0, p)),
            pl.BlockSpec((8, V7X_LANES), lambda b, p: (0, 0)),
            pl.BlockSpec((V7X_LANES, V7X_LANES), lambda b, p: (0, p)),
            pl.BlockSpec((V7X_LANES, V7X_LANES), lambda b, p: (0, p)),
            pl.BlockSpec((V7X_LANES, V7X_LANES), lambda b, p: (0, p)),
        ],
        out_specs=pl.BlockSpec((1, S, V7X_LANES), lambda b, p: (b, 0, p)),
        out_shape=jax.ShapeDtypeStruct((B, S, CW), F32),
        scratch_shapes=[seq] * 8 + [
            pltpu.VMEM((V7X_LANES, V7X_LANES), F32),
            pltpu.VMEM((S // RWKV_CHUNK, RWKV_CHUNK, V7X_LANES), F32),
            pltpu.VMEM((S // RWKV_CHUNK, V7X_LANES, V7X_LANES), F32),
            pltpu.VMEM((S // RWKV_CHUNK, V7X_LANES, V7X_LANES), F32),
        ],
        compiler_params=_params("parallel", "parallel"),
        name="rwkv7_mixer",
    )(z, z, z, z, z, pp, pl2, wup, aup, g_up.astype(BF16))


def _ret_kernel(q_ref, k_ref, v_ref, g_ref, cos_ref, sin_ref, din_ref, dq_ref, dk_ref, dc_ref, o_ref, st_ref):
    S = q_ref.shape[1]
    C, DV = RET_CHUNK, RET_V_DIM
    lane = lax.broadcasted_iota(jnp.int32, (C, V7X_LANES), 1)
    first_half = (lane % RET_QK_DIM) < RET_QK_DIM // 2
    st_ref[...] = jnp.zeros_like(st_ref)

    def chunk(c, carry):
        sl = pl.ds(pl.multiple_of(c * C, C), C)
        cos, sin = cos_ref[sl, :], sin_ref[sl, :]

        def rot(z):
            swapped = jnp.where(first_half, pltpu.roll(z, V7X_LANES - RET_QK_DIM // 2, axis=1),
                                pltpu.roll(z, RET_QK_DIM // 2, axis=1))
            return z * cos + swapped * sin

        q = rot(q_ref[0, sl, :])
        k = rot(k_ref[0, sl, :]) * (RET_QK_DIM ** -0.5)
        for h in range(2):
            in_head = (lane >= h * RET_QK_DIM) & (lane < (h + 1) * RET_QK_DIM)
            qm, km = jnp.where(in_head, q, 0.0), jnp.where(in_head, k, 0.0)
            v = v_ref[0, sl, h * DV:(h + 1) * DV]
            st = st_ref[h]
            inner = _mm(qm, k, dims=_NT) * din_ref[h]
            o = _mm(inner, v) + _mm(qm, st) * dq_ref[h]
            st_ref[h] = _mm((km * dk_ref[h]).T, v) + dc_ref[h, 0:1, :] * st
            d = o - jnp.mean(o, axis=1, keepdims=True)
            on = d * lax.rsqrt(jnp.mean(d * d, axis=1, keepdims=True) + RET_GN_EPS)
            gate = g_ref[0, sl, h * DV:(h + 1) * DV]
            o_ref[0, sl, h * DV:(h + 1) * DV] = gate * jax.nn.sigmoid(gate) * on
        return carry

    lax.fori_loop(0, S // C, chunk, 0)


def retention_mixer(z):
    B, S, _ = z.shape
    H, C, DK, DV = RET_HEADS, RET_CHUNK, RET_QK_DIM, RET_V_DIM
    assert S % C == 0 and 2 * DK == V7X_LANES and DV == V7X_LANES
    npair = H // 2
    half = DK // 2
    inv = ROPE_BASE ** (-jnp.arange(half, dtype=F32) / half)
    ang = jnp.arange(S, dtype=F32)[:, None] * inv
    cos = jnp.tile(jnp.cos(ang), (1, 4))
    sin = jnp.tile(jnp.concatenate([-jnp.sin(ang), jnp.sin(ang)], axis=1), (1, 2))
    log_g = jnp.asarray(np.log(1.0 - 2.0 ** (-5.0 - np.arange(H))), F32)
    n = jnp.arange(C, dtype=F32)
    diff = n[:, None] - n[None, :]
    d_in = jnp.where(diff >= 0, jnp.exp(jnp.maximum(diff, 0.0) * log_g[:, None, None]), 0.0)
    lanes = lambda t: jnp.broadcast_to(t[..., None], t.shape + (V7X_LANES,))
    d_q = lanes(jnp.exp((n + 1.0) * log_g[:, None]))
    d_k = lanes(jnp.exp((C - 1.0 - n) * log_g[:, None]))
    d_c = lanes(jnp.broadcast_to(jnp.exp(C * log_g)[:, None], (H, 8)))
    qk_tiles = H * DK // V7X_LANES
    return pl.pallas_call(
        _ret_kernel,
        grid=(B, npair),
        in_specs=[
            pl.BlockSpec((1, S, V7X_LANES), lambda b, p: (b, 0, p)),
            pl.BlockSpec((1, S, V7X_LANES), lambda b, p: (b, 0, qk_tiles + p)),
            pl.BlockSpec((1, S, 2 * DV), lambda b, p: (b, 0, 2 * qk_tiles * V7X_LANES // (2 * DV) + p)),
            pl.BlockSpec((1, S, 2 * DV), lambda b, p: (b, 0, (2 * qk_tiles * V7X_LANES + H * DV) // (2 * DV) + p)),
            pl.BlockSpec((S, V7X_LANES), lambda b, p: (0, 0)),
            pl.BlockSpec((S, V7X_LANES), lambda b, p: (0, 0)),
            pl.BlockSpec((2, C, C), lambda b, p: (p, 0, 0)),
            pl.BlockSpec((2, C, V7X_LANES), lambda b, p: (p, 0, 0)),
            pl.BlockSpec((2, C, V7X_LANES), lambda b, p: (p, 0, 0)),
            pl.BlockSpec((2, 8, V7X_LANES), lambda b, p: (p, 0, 0)),
        ],
        out_specs=pl.BlockSpec((1, S, 2 * DV), lambda b, p: (b, 0, p)),
        out_shape=jax.ShapeDtypeStruct((B, S, H * DV), F32),
        scratch_shapes=[pltpu.VMEM((2, V7X_LANES, DV), F32)],
        compiler_params=_params("parallel", "parallel"),
        name="retention_mixer",
    )(z, z, z, z, cos, sin, d_in, d_q, d_k, d_c)


def _even_mixer(x, g_norm, w_in, shift_mu, w0, w_up, a0, a_up, g_up, k_k, k_a, r_k, ln_g, ln_b):
    B, S, D = x.shape
    z = norm_matmul(x.reshape(B * S, D), g_norm, w_in.astype(BF16)).reshape(B, S, -1)
    o_a = moba_attention(z)
    o_b = rwkv7_mixer(z, shift_mu, w0, w_up, a0, a_up, g_up, k_k, k_a, r_k, ln_g, ln_b)
    return o_a, o_b


def _odd_mixer(x, g_norm, w_in, pe_k, w1_k, w2_k, pe_v, w1_v, w2_v):
    B, S, D = x.shape
    perm, col = _odd_layout()
    w_p = jnp.take(jnp.pad(w_in, ((0, 0), (0, 1))), perm, axis=1).astype(BF16)
    z = norm_matmul(x.reshape(B * S, D), g_norm, w_p).reshape(B, S, -1)
    o_c = retention_mixer(z)
    cmp_kv, cmp_vk = nsa_compress(z, col["kc"], col["vc"], pe_k, w1_k, w2_k, pe_v, w1_v, w2_v)
    o_d = nsa_attention(z, cmp_kv, cmp_vk, col["nq"], col["slc"], col["win"], col["gate"])
    return o_c, o_d


def _odd_layout():
    G, Dh = NSA_KV_GROUPS, HEAD_DIM
    sizes = (RET_HEADS * RET_QK_DIM, RET_HEADS * RET_QK_DIM, RET_HEADS * RET_V_DIM, RET_HEADS * RET_V_DIM,
             NSA_HEADS * Dh) + (G * Dh,) * 6 + (3 * NSA_HEADS,)
    off = np.concatenate([[0], np.cumsum(sizes)])
    rq, rk, rv, rg, nq, kc, vc, ks, vs, kw, vw, ng = off[:-1]
    n_in = int(off[-1])
    pair = lambda a, b: np.concatenate([np.concatenate([a + g * Dh + np.arange(Dh), b + g * Dh + np.arange(Dh)])
                                        for g in range(G)])
    perm = np.concatenate([np.arange(ks), pair(ks, vs), pair(kw, vw), ng + np.arange(3 * NSA_HEADS)])
    n_pad = -(-len(perm) // 640) * 640
    perm = np.concatenate([perm, np.full(n_pad - len(perm), n_in)]).astype(np.int32)
    col = {"nq": int(nq), "kc": int(kc), "vc": int(vc), "slc": int(ks), "win": int(ks) + 2 * G * Dh,
           "gate": int(ks) + 4 * G * Dh}
    return perm, col


def kernel(x, mix_norm, ffn_norm, even_w_in, even_shift_mu, even_w0, even_w_up, even_a0, even_a_up, even_g_up, even_k_k, even_k_a, even_r_k, even_ln_g, even_ln_b, even_w_out, odd_w_in, odd_cmp_pe_k, odd_cmp_w1_k, odd_cmp_w2_k, odd_cmp_pe_v, odd_cmp_w1_v, odd_cmp_w2_v, odd_w_out, ffn_w1, ffn_w3, ffn_w2, final_norm):
    B, S, D = x.shape
    depth = mix_norm.shape[0]
    for layer in range(depth):
        i = layer // 2
        if layer % 2 == 0:
            o1, o2 = _even_mixer(x, mix_norm[layer], even_w_in[i], even_shift_mu[i], even_w0[i], even_w_up[i],
                                 even_a0[i], even_a_up[i], even_g_up[i], even_k_k[i], even_k_a[i], even_r_k[i],
                                 even_ln_g[i], even_ln_b[i])
            w_out = even_w_out[i]
        else:
            o1, o2 = _odd_mixer(x, mix_norm[layer], odd_w_in[i], odd_cmp_pe_k[i], odd_cmp_w1_k[i], odd_cmp_w2_k[i],
                                odd_cmp_pe_v[i], odd_cmp_w1_v[i], odd_cmp_w2_v[i])
            w_out = odd_w_out[i]
        T = B * S
        x2 = out_proj_residual(o1.reshape(T, -1), o2.reshape(T, -1), w_out.astype(BF16), x.reshape(T, D))
        x2 = ffn_residual(x2, ffn_norm[layer], ffn_w1[layer].astype(BF16), ffn_w3[layer].astype(BF16),
                          ffn_w2[layer].astype(BF16), final_norm if layer == depth - 1 else None)
        x = x2.reshape(B, S, D)
    return x
```

```python
import functools

import jax
import jax.numpy as jnp
import numpy as np
from jax import lax
from jax.experimental import pallas as pl
from jax.experimental.pallas import tpu as pltpu

F32 = jnp.float32
BF16 = jnp.bfloat16

V7X_LANES = 128
V7X_MXU_DIM = 256
V7X_VMEM_BYTES = 64 * 1024 * 1024
VMEM_LIMIT = V7X_VMEM_BYTES * 7 // 8

NORM_EPS = 1e-6
HEAD_DIM = 64

MOBA_BLOCK = 256
MOBA_TOPK = 3
RWKV_HEADS = 16
RWKV_GN_EPS = 6.4e-4

RET_HEADS = 8
RET_QK_DIM = 64
RET_V_DIM = 128
RET_CHUNK = 128
RET_GN_EPS = 1e-6
ROPE_BASE = 10000.0
NSA_HEADS = 16
NSA_KV_GROUPS = 4
NSA_CMP_BLOCK = 32
NSA_CMP_STRIDE = 16
NSA_SLC_BLOCK = 64
NSA_SLC_TOPN = 16
NSA_WINDOW = 512


def _params(*semantics):
    return pltpu.CompilerParams(dimension_semantics=semantics, vmem_limit_bytes=VMEM_LIMIT)


def _rms(x, g):
    return x * lax.rsqrt(jnp.mean(x * x, axis=-1, keepdims=True) + NORM_EPS) * g


def _norm_matmul_kernel(x_ref, g_ref, w_ref, o_ref, h_ref):
    @pl.when(pl.program_id(1) == 0)
    def _():
        h_ref[...] = _rms(x_ref[...], g_ref[...]).astype(BF16)

    o_ref[...] = jnp.dot(h_ref[...], w_ref[...], preferred_element_type=F32)


def _proj_tile(n):
    assert n % V7X_MXU_DIM == 0
    k = n // V7X_MXU_DIM
    return V7X_MXU_DIM * max(d for d in range(1, 7) if k % d == 0)


def norm_matmul(x, g, w, *, tm=512):
    T, D = x.shape
    N = w.shape[1]
    tn = _proj_tile(N)
    assert T % tm == 0 and N % tn == 0
    return pl.pallas_call(
        _norm_matmul_kernel,
        grid=(T // tm, N // tn),
        in_specs=[
            pl.BlockSpec((tm, D), lambda i, j: (i, 0)),
            pl.BlockSpec((1, D), lambda i, j: (0, 0)),
            pl.BlockSpec((D, tn), lambda i, j: (0, j)),
        ],
        out_specs=pl.BlockSpec((tm, tn), lambda i, j: (i, j)),
        out_shape=jax.ShapeDtypeStruct((T, N), F32),
        scratch_shapes=[pltpu.VMEM((tm, D), BF16)],
        compiler_params=_params("parallel", "arbitrary"),
        name="norm_matmul",
    )(x, g.reshape(1, D), w)


def _out_proj_kernel(a_ref, b_ref, wa_ref, wb_ref, x_ref, o_ref):
    acc = jnp.dot(a_ref[...].astype(BF16), wa_ref[...], preferred_element_type=F32)
    acc += jnp.dot(b_ref[...].astype(BF16), wb_ref[...], preferred_element_type=F32)
    o_ref[...] = x_ref[...] + acc


def out_proj_residual(a, b, w, x, *, tm=512):
    T, D = x.shape
    Ka, Kb = a.shape[1], b.shape[1]
    assert T % tm == 0 and w.shape == (Ka + Kb, D)
    return pl.pallas_call(
        _out_proj_kernel,
        grid=(T // tm,),
        in_specs=[
            pl.BlockSpec((tm, Ka), lambda i: (i, 0)),
            pl.BlockSpec((tm, Kb), lambda i: (i, 0)),
            pl.BlockSpec((Ka, D), lambda i: (0, 0)),
            pl.BlockSpec((Kb, D), lambda i: (0, 0)),
            pl.BlockSpec((tm, D), lambda i: (i, 0)),
        ],
        out_specs=pl.BlockSpec((tm, D), lambda i: (i, 0)),
        out_shape=jax.ShapeDtypeStruct((T, D), F32),
        compiler_params=_params("parallel"),
        name="out_proj_residual",
    )(a, b, w[:Ka], w[Ka:], x)


def _ffn_kernel(x_ref, g_ref, w1_ref, w3_ref, w2_ref, gf_ref, o_ref, h_ref, acc_ref, *, final_norm):
    j = pl.program_id(1)

    @pl.when(j == 0)
    def _():
        h_ref[...] = _rms(x_ref[...], g_ref[...]).astype(BF16)
        acc_ref[...] = jnp.zeros_like(acc_ref)

    h = h_ref[...]
    a = jnp.dot(h, w1_ref[...], preferred_element_type=F32)
    b = jnp.dot(h, w3_ref[...], preferred_element_type=F32)
    act = (a * jax.nn.sigmoid(a) * b).astype(BF16)
    acc_ref[...] += jnp.dot(act, w2_ref[...], preferred_element_type=F32)

    @pl.when(j == pl.num_programs(1) - 1)
    def _():
        y = x_ref[...] + acc_ref[...]
        if final_norm:
            y = _rms(y, gf_ref[...])
        o_ref[...] = y


def ffn_residual(x, g, w1, w3, w2, g_final=None, *, tm=512, tf=512):
    T, D = x.shape
    Fh = w1.shape[1]
    assert T % tm == 0 and Fh % tf == 0
    final_norm = g_final is not None
    gf = (g_final if final_norm else g).reshape(1, D)
    return pl.pallas_call(
        functools.partial(_ffn_kernel, final_norm=final_norm),
        grid=(T // tm, Fh // tf),
        in_specs=[
            pl.BlockSpec((tm, D), lambda i, j: (i, 0)),
            pl.BlockSpec((1, D), lambda i, j: (0, 0)),
            pl.BlockSpec((D, tf), lambda i, j: (0, j)),
            pl.BlockSpec((D, tf), lambda i, j: (0, j)),
            pl.BlockSpec((tf, D), lambda i, j: (j, 0)),
            pl.BlockSpec((1, D), lambda i, j: (0, 0)),
        ],
        out_specs=pl.BlockSpec((tm, D), lambda i, j: (i, 0)),
        out_shape=jax.ShapeDtypeStruct((T, D), F32),
        scratch_shapes=[pltpu.VMEM((tm, D), BF16), pltpu.VMEM((tm, D), F32)],
        compiler_params=_params("parallel", "arbitrary"),
        name="ffn_residual",
    )(x, g.reshape(1, D), w1, w3, w2, gf)


NEG_BIG = -1e30
_NT = (((1,), (1,)), ((), ()))


def _moba_kernel(q_ref, k_ref, v_ref, o_ref, kb_ref, vb_ref, km_ref, acc_ref):
    L = MOBA_BLOCK
    nb = k_ref.shape[1] // L
    qi = pl.program_id(2)
    lane = lax.broadcasted_iota(jnp.int32, (L, V7X_LANES), 1)

    @pl.when(qi == 0)
    def _():
        k = k_ref[0]
        kb_ref[...] = k.astype(BF16)
        vb_ref[...] = v_ref[0].astype(BF16)
        km_ref[...] = jnp.zeros_like(km_ref)
        km_ref[0:nb, :] = jnp.mean(k.reshape(nb, L, V7X_LANES), axis=1)

    q = q_ref[0] * (HEAD_DIM ** -0.5)
    row = lax.broadcasted_iota(jnp.int32, (L, L), 0)
    col = lax.broadcasted_iota(jnp.int32, (L, L), 1)
    causal = col <= row
    jrow = lax.broadcasted_iota(jnp.int32, (nb, L), 0)
    past = jrow < qi
    start = pl.multiple_of(qi * L, L)
    qbs, selmats = [], []
    for h in range(2):
        in_head = (lane >= h * HEAD_DIM) & (lane < (h + 1) * HEAD_DIM)
        qm = jnp.where(in_head, q, 0.0)
        gate = lax.dot_general(km_ref[...], qm, _NT, precision=lax.Precision.HIGHEST,
                               preferred_element_type=F32)[0:nb]
        sel_t = jnp.zeros((nb, L), F32)
        for n in range(nb):
            g_n = gate[n:n + 1, :]
            beats = (gate > g_n) | ((gate == g_n) & (jrow < n))
            rank = jnp.sum(jnp.where(past & beats, 1.0, 0.0), axis=0, keepdims=True)
            sel_t = jnp.where((jrow == n) & (rank < MOBA_TOPK), 1.0, sel_t)
        sel_t = jnp.where(past, sel_t, 0.0)
        selmats.append(jnp.concatenate([sel_t, jnp.zeros((V7X_LANES - nb, L), F32)], axis=0).T)
        qbs.append(qm.astype(BF16))

    m, l, acc = _flash_steps(qbs, [kb_ref[pl.ds(start, L), :]] * 2, [vb_ref[pl.ds(start, L), :]] * 2, [causal] * 2,
                             [jnp.full((L, 1), NEG_BIG, F32)] * 2, [jnp.zeros((L, 1), F32)] * 2,
                             [jnp.zeros((L, V7X_LANES), F32)] * 2)
    acc_ref[0], acc_ref[1] = acc

    def body(n, carry):
        st = pl.multiple_of(n * L, L)
        sel_n = [jnp.sum(jnp.where(lane == n, selmats[h], 0.0), axis=1, keepdims=True) > 0.5 for h in range(2)]
        m2, l2, acc2 = _flash_steps(qbs, [kb_ref[pl.ds(st, L), :]] * 2, [vb_ref[pl.ds(st, L), :]] * 2, sel_n,
                                    carry[:2], carry[2:], [acc_ref[0], acc_ref[1]])
        acc_ref[0], acc_ref[1] = acc2
        return tuple(m2) + tuple(l2)

    fin = lax.fori_loop(0, qi, body, tuple(m) + tuple(l))
    o_ref[0] = jnp.where(lane < HEAD_DIM, acc_ref[0] / fin[2], acc_ref[1] / fin[3])


def moba_attention(z, *, n_heads=16):
    B, S, _ = z.shape
    L = MOBA_BLOCK
    assert S % L == 0 and n_heads % 2 == 0
    npair = n_heads // 2
    return pl.pallas_call(
        _moba_kernel,
        grid=(B, npair, S // L),
        in_specs=[
            pl.BlockSpec((1, L, V7X_LANES), lambda b, p, i: (b, i, p)),
            pl.BlockSpec((1, S, V7X_LANES), lambda b, p, i: (b, 0, npair + p)),
            pl.BlockSpec((1, S, V7X_LANES), lambda b, p, i: (b, 0, 2 * npair + p)),
        ],
        out_specs=pl.BlockSpec((1, L, V7X_LANES), lambda b, p, i: (b, i, p)),
        out_shape=jax.ShapeDtypeStruct((B, S, n_heads * HEAD_DIM), F32),
        scratch_shapes=[
            pltpu.VMEM((S, V7X_LANES), BF16),
            pltpu.VMEM((S, V7X_LANES), BF16),
            pltpu.VMEM((V7X_LANES, V7X_LANES), F32),
            pltpu.VMEM((2, L, V7X_LANES), F32),
        ],
        compiler_params=_params("parallel", "parallel", "arbitrary"),
        name="moba_attention",
    )(z, z, z)


NSA_TQ = 256
BIG = 3.0e38


def _gelu_tanh(x):
    return 0.5 * x * (1.0 + jnp.tanh(0.7978845608028654 * (x + 0.044715 * x * x * x)))


def _nsa_compress_kernel(xk0_ref, xk1_ref, xv0_ref, xv1_ref, pek_ref, pev_ref, w1k_ref, w1v_ref, w2k_ref, w2v_ref,
                         o1_ref, o2_ref):
    G, Lc, st = NSA_KV_GROUPS, NSA_CMP_BLOCK, NSA_CMP_STRIDE
    nrow = xk0_ref.shape[1] // st
    lane = lax.broadcasted_iota(jnp.int32, (nrow, G * HEAD_DIM), 1)

    def hidden(x_refs, pe_ref, w1_ref):
        acc = [jnp.zeros((G * nrow, V7X_LANES), F32) for _ in range(Lc // st)]
        for l in range(Lc):
            u, m = divmod(l, st)
            x = jnp.concatenate([r[0, pl.ds(m, nrow, stride=st), :] for r in x_refs], axis=1) + pe_ref[l:l + 1, :]
            xs = jnp.concatenate(
                [jnp.where((lane >= g * HEAD_DIM) & (lane < (g + 1) * HEAD_DIM), x, 0.0) for g in range(G)],
                axis=0).astype(BF16)
            acc[u] = acc[u] + jnp.dot(xs, w1_ref[l], preferred_element_type=F32)
        nxt = jnp.concatenate([pltpu.roll(acc[1][g * nrow:(g + 1) * nrow], nrow - 1, axis=0) for g in range(G)],
                              axis=0)
        return _gelu_tanh(acc[0] + nxt).astype(BF16)

    hk = hidden((xk0_ref, xk1_ref), pek_ref, w1k_ref)
    hv = hidden((xv0_ref, xv1_ref), pev_ref, w1v_ref)
    kc = jnp.dot(hk, w2k_ref[...], preferred_element_type=F32)
    vc = jnp.dot(hv, w2v_ref[...], preferred_element_type=F32)
    kv = kc + vc
    vk = pltpu.roll(kv, HEAD_DIM, axis=1)
    for g in range(G):
        o1_ref[0, :, g * V7X_LANES:(g + 1) * V7X_LANES] = kv[g * nrow:(g + 1) * nrow]
        o2_ref[0, :, g * V7X_LANES:(g + 1) * V7X_LANES] = vk[g * nrow:(g + 1) * nrow]


def nsa_compress(z, col_k, col_v, pe_k, w1_k, w2_k, pe_v, w1_v, w2_v):
    B, S, _ = z.shape
    G, Lc, st = NSA_KV_GROUPS, NSA_CMP_BLOCK, NSA_CMP_STRIDE
    GW = G * HEAD_DIM
    nrow = S // st
    hid = w1_k.shape[1]
    assert hid == V7X_LANES and col_k % GW == 0 and col_v % GW == 0
    tile_pe = lambda pe: jnp.tile(pe, (1, G))
    tile_w1 = lambda w: jnp.tile(w.reshape(Lc, 1, HEAD_DIM, hid), (1, G, 1, 1)).reshape(Lc, GW, hid).astype(BF16)
    w2k = jnp.pad(w2_k, ((0, 0), (0, HEAD_DIM))).astype(BF16)
    w2v = jnp.pad(w2_v, ((0, 0), (HEAD_DIM, 0))).astype(BF16)
    const = lambda shape: pl.BlockSpec(shape, lambda b: (0,) * len(shape))
    out = jax.ShapeDtypeStruct((B, nrow, G * V7X_LANES), F32)
    return pl.pallas_call(
        _nsa_compress_kernel,
        grid=(B,),
        in_specs=[
            pl.BlockSpec((1, S, V7X_LANES), lambda b: (b, 0, col_k // V7X_LANES)),
            pl.BlockSpec((1, S, V7X_LANES), lambda b: (b, 0, col_k // V7X_LANES + 1)),
            pl.BlockSpec((1, S, V7X_LANES), lambda b: (b, 0, col_v // V7X_LANES)),
            pl.BlockSpec((1, S, V7X_LANES), lambda b: (b, 0, col_v // V7X_LANES + 1)),
            const((Lc, GW)), const((Lc, GW)),
            const((Lc, GW, hid)), const((Lc, GW, hid)),
            const((hid, V7X_LANES)), const((hid, V7X_LANES)),
        ],
        out_specs=[pl.BlockSpec((1, nrow, G * V7X_LANES), lambda b: (b, 0, 0))] * 2,
        out_shape=[out, out],
        compiler_params=_params("parallel"),
        name="nsa_compress",
    )(z, z, z, z, tile_pe(pe_k), tile_pe(pe_v), tile_w1(w1_k), tile_w1(w1_v), w2k, w2v)


def _flash_steps(qbs, kv_s, kv_o, masks, m_prev, l_prev, acc_prev):
    hs = range(len(qbs))
    s = [lax.dot_general(qbs[h], kv_s[h], _NT, preferred_element_type=F32) for h in hs]
    s = [jnp.where(masks[h], s[h], NEG_BIG) for h in hs]
    m_new = [jnp.maximum(m_prev[h], jnp.max(s[h], axis=1, keepdims=True)) for h in hs]
    alpha = [jnp.exp(m_prev[h] - m_new[h]) for h in hs]
    p = [jnp.exp(s[h] - m_new[h]) for h in hs]
    l_new = [alpha[h] * l_prev[h] + jnp.sum(p[h], axis=1, keepdims=True) for h in hs]
    pv = [jnp.dot(p[h].astype(BF16), kv_o[h], preferred_element_type=F32) for h in hs]
    acc_new = [alpha[h] * acc_prev[h] + pv[h] for h in hs]
    return m_new, l_new, acc_new


def _nsa_kernel(q_ref, c1_ref, c2_ref, s_ref, w_ref, g_ref, ovt_ref, e_ref, o_ref,
                s1_ref, s2_ref, w1_ref, w2_ref, selx_ref, acc_ref):
    TQ = NSA_TQ
    R = NSA_HEADS // NSA_KV_GROUPS
    grp = pl.program_id(1)
    qi = pl.program_id(2)
    lane = lax.broadcasted_iota(jnp.int32, (TQ, V7X_LANES), 1)

    @pl.when(qi == 0)
    def _():
        kv = s_ref[0]
        s1_ref[...] = kv.astype(BF16)
        s2_ref[...] = pltpu.roll(kv, HEAD_DIM, axis=1).astype(BF16)
        kv = w_ref[0]
        w1_ref[...] = kv.astype(BF16)
        w2_ref[...] = pltpu.roll(kv, HEAD_DIM, axis=1).astype(BF16)

    q0 = pl.multiple_of(qi * TQ, TQ)
    row = lax.broadcasted_iota(jnp.int32, (TQ, TQ), 0)
    col = lax.broadcasted_iota(jnp.int32, (TQ, TQ), 1)
    causal = col <= row
    qi_mat = jnp.zeros((TQ, TQ), jnp.int32) + qi
    t_abs = q0 + lax.broadcasted_iota(jnp.int32, (TQ, V7X_LANES), 0)
    even_lanes = lane < HEAD_DIM

    c_kv, c_vk = c1_ref[0], c2_ref[0]
    c_kv_b, c_vk_b = c_kv.astype(BF16), c_vk.astype(BF16)
    cmask = lane * NSA_CMP_STRIDE + (NSA_CMP_BLOCK - 1) <= t_abs
    qbs, o_cmp = [], []
    p_sum = jnp.zeros((TQ, V7X_LANES), F32)
    for r in range(R):
        tile = q_ref[0, :, (r // 2) * V7X_LANES:(r // 2 + 1) * V7X_LANES] * (HEAD_DIM ** -0.5)
        qm = jnp.where(even_lanes if r % 2 == 0 else ~even_lanes, tile, 0.0)
        qbs.append(qm.astype(BF16))
        s = lax.dot_general(qbs[r], c_kv_b if r % 2 == 0 else c_vk_b, _NT, preferred_element_type=F32)
        s = jnp.where(cmask, s, NEG_BIG)
        p = jnp.where(cmask, jnp.exp(s - jnp.max(s, axis=1, keepdims=True)), 0.0)
        den = jnp.sum(p, axis=1, keepdims=True)
        p = p / jnp.where(den > 0.0, den, 1.0)
        p_sum = p_sum + p
        o_cmp.append(jnp.dot(p.astype(BF16), c_vk_b if r % 2 == 0 else c_kv_b, preferred_element_type=F32))

    nblk = s_ref.shape[1] // NSA_SLC_BLOCK
    p_slc = lax.dot_general(ovt_ref[...], p_sum, _NT, precision=lax.Precision.HIGHEST,
                            preferred_element_type=F32)[0:nblk]
    jrow = lax.broadcasted_iota(jnp.int32, (nblk, TQ), 0)
    own = (q0 + lax.broadcasted_iota(jnp.int32, (nblk, TQ), 1)) // NSA_SLC_BLOCK
    score = jnp.where((jrow == own) | (jrow == 0), BIG, jnp.where(jrow > own, -BIG, p_slc))
    sel_t = jnp.zeros((nblk, TQ), F32)
    for j in range(nblk):
        s_j = score[j:j + 1, :]
        beats = (score > s_j) | ((score == s_j) & (jrow < j))
        rank = jnp.sum(jnp.where(beats, 1.0, 0.0), axis=0, keepdims=True)
        sel_t = jnp.where((jrow == j) & (rank < NSA_SLC_TOPN), 1.0, sel_t)
    sel_t = jnp.where(jrow <= own, sel_t, 0.0)
    sel = jnp.concatenate([sel_t, jnp.zeros((V7X_LANES - nblk, TQ), F32)], axis=0).T.astype(BF16)

    def expand(kb, c):
        selx_ref[kb] = jnp.dot(sel, e_ref[kb], preferred_element_type=F32)
        return c

    lax.fori_loop(0, qi + 1, expand, 0)

    neg = [jnp.full((TQ, 1), NEG_BIG, F32)] * R
    zero = [jnp.zeros((TQ, 1), F32)] * R
    zacc = [jnp.zeros((TQ, V7X_LANES), F32)] * R

    def kv_blocks(kv_ref, vk_ref, start):
        kv, vk = kv_ref[pl.ds(start, TQ), :], vk_ref[pl.ds(start, TQ), :]
        return [kv if r % 2 == 0 else vk for r in range(R)], [vk if r % 2 == 0 else kv for r in range(R)]

    ks, vs = kv_blocks(s1_ref, s2_ref, q0)
    m, l, acc = _flash_steps(qbs, ks, vs, [causal & (selx_ref[qi] > 0.5)] * R, neg, zero, zacc)
    for r in range(R):
        acc_ref[r] = acc[r]

    def body(kb, carry):
        ks, vs = kv_blocks(s1_ref, s2_ref, pl.multiple_of(kb * TQ, TQ))
        m2, l2, acc2 = _flash_steps(qbs, ks, vs, [selx_ref[kb] > 0.5] * R, carry[:R], carry[R:],
                                    [acc_ref[r] for r in range(R)])
        for r in range(R):
            acc_ref[r] = acc2[r]
        return tuple(m2) + tuple(l2)

    fin = lax.fori_loop(0, qi, body, tuple(m) + tuple(l))

    ks, vs = kv_blocks(w1_ref, w2_ref, q0)
    m, l, acc = _flash_steps(qbs, ks, vs, [causal] * R, neg, zero, zacc)
    ks, vs = kv_blocks(w1_ref, w2_ref, pl.multiple_of(jnp.maximum(qi - 1, 0) * TQ, TQ))
    m, l, acc = _flash_steps(qbs, ks, vs, [qi_mat >= 1] * R, m, l, acc)
    ks, vs = kv_blocks(w1_ref, w2_ref, pl.multiple_of(jnp.maximum(qi - 2, 0) * TQ, TQ))
    m, l, acc = _flash_steps(qbs, ks, vs, [(col > row) & (qi_mat >= 2)] * R, m, l, acc)

    gates = jax.nn.sigmoid(g_ref[0])
    outs = []
    for r in range(R):
        o_slc = acc_ref[r] / fin[R + r]
        o_win = acc[r] / l[r]
        c0 = (grp * R + r) * 3
        gate = lambda c: jnp.sum(jnp.where(lane == c, gates, 0.0), axis=1, keepdims=True)
        outs.append(gate(c0) * o_cmp[r] + gate(c0 + 1) * o_slc + gate(c0 + 2) * o_win)
    for p2 in range(R // 2):
        o_ref[0, :, p2 * V7X_LANES:(p2 + 1) * V7X_LANES] = jnp.where(even_lanes, outs[2 * p2], outs[2 * p2 + 1])


def nsa_attention(z, cmp_kv, cmp_vk, col_q, col_slc, col_win, col_gate):
    B, S, _ = z.shape
    G, TQ = NSA_KV_GROUPS, NSA_TQ
    R = NSA_HEADS // G
    QW = R * HEAD_DIM
    nkb = S // TQ
    ncmp = cmp_kv.shape[1]
    assert S % TQ == 0 and ncmp == V7X_LANES and S // NSA_SLC_BLOCK <= V7X_LANES and NSA_WINDOW == 2 * TQ
    assert col_q % QW == 0 and col_slc % V7X_LANES == 0 and col_win % V7X_LANES == 0 and col_gate % V7X_LANES == 0
    nc = (S - NSA_CMP_BLOCK) // NSA_CMP_STRIDE + 1
    c_start = np.arange(V7X_LANES) * NSA_CMP_STRIDE
    s_start = np.arange(V7X_LANES) * NSA_SLC_BLOCK
    overlap = ((c_start[:, None] <= s_start[None, :] + NSA_SLC_BLOCK - 1)
               & (c_start[:, None] + NSA_CMP_BLOCK - 1 >= s_start[None, :])
               & (np.arange(V7X_LANES)[:, None] < nc) & (np.arange(V7X_LANES)[None, :] < S // NSA_SLC_BLOCK))
    key_blk = (np.arange(nkb)[:, None, None] * TQ + np.arange(TQ)[None, None, :]) // NSA_SLC_BLOCK
    expand = (np.arange(V7X_LANES)[None, :, None] == key_blk)
    const = lambda shape: pl.BlockSpec(shape, lambda b, g, i: (0,) * len(shape))
    return pl.pallas_call(
        _nsa_kernel,
        grid=(B, G, S // TQ),
        in_specs=[
            pl.BlockSpec((1, TQ, QW), lambda b, g, i: (b, i, col_q // QW + g)),
            pl.BlockSpec((1, ncmp, V7X_LANES), lambda b, g, i: (b, 0, g)),
            pl.BlockSpec((1, ncmp, V7X_LANES), lambda b, g, i: (b, 0, g)),
            pl.BlockSpec((1, S, V7X_LANES), lambda b, g, i: (b, 0, col_slc // V7X_LANES + g)),
            pl.BlockSpec((1, S, V7X_LANES), lambda b, g, i: (b, 0, col_win // V7X_LANES + g)),
            pl.BlockSpec((1, TQ, V7X_LANES), lambda b, g, i: (b, i, col_gate // V7X_LANES)),
            const((V7X_LANES, V7X_LANES)),
            const((nkb, V7X_LANES, TQ)),
        ],
        out_specs=pl.BlockSpec((1, TQ, QW), lambda b, g, i: (b, i, g)),
        out_shape=jax.ShapeDtypeStruct((B, S, NSA_HEADS * HEAD_DIM), F32),
        scratch_shapes=[pltpu.VMEM((S, V7X_LANES), BF16)] * 4 + [
            pltpu.VMEM((nkb, TQ, TQ), F32),
            pltpu.VMEM((R, TQ, V7X_LANES), F32),
        ],
        compiler_params=_params("parallel", "parallel", "arbitrary"),
        name="nsa_attention",
    )(z, cmp_kv, cmp_vk, z, z, z, jnp.asarray(overlap.T, F32), jnp.asarray(expand, BF16))


RWKV_CHUNK = 64
RWKV_ROWS = 256
RWKV_INTERLEAVE = 4


def _mm(a, b, exact=False, dims=None):
    dims = dims or (((1,), (0,)), ((), ()))
    if exact:
        return lax.dot_general(a, b, dims, precision=lax.Precision.HIGHEST, preferred_element_type=F32)
    return lax.dot_general(a.astype(BF16), b.astype(BF16), dims, preferred_element_type=F32)


def _head_sum(x, low):
    s0 = jnp.sum(jnp.where(low, x, 0.0), axis=1, keepdims=True)
    s1 = jnp.sum(jnp.where(low, 0.0, x), axis=1, keepdims=True)
    return jnp.where(low, s0, s1)


def _rwkv_kernel(r_ref, k_ref, v_ref, lo_ref, glo_ref, pp_ref, pl_ref, wup_ref, aup_ref, gup_ref, o_ref,
                 rs, ws, ks, vs, als, bes, gs, ys, hs, rqs, ms, ns):
    S = r_ref.shape[1]
    C, RB = RWKV_CHUNK, RWKV_ROWS
    pp = pp_ref[...]
    mu_r, mu_k, mu_v, w0, a0, k_k, k_a, r_k, ln_g, ln_b = [pp[i:i + 1, :] for i in range(10)]
    mu_lo, mu_g = pl_ref[0:1, :], pl_ref[1:2, :]
    low = lax.broadcasted_iota(jnp.int32, (RB, V7X_LANES), 1) < HEAD_DIM
    first = lax.broadcasted_iota(jnp.int32, (RB, V7X_LANES), 0) == 0

    def prologue(i, c):
        t0 = pl.multiple_of(i * RB, RB)
        tp = jnp.maximum(t0 - 1, 0)
        keep = jnp.where(i > 0, 1.0, 0.0)

        def shifted(ref, mu):
            x = ref[0, pl.ds(t0, RB), :]
            prev = jnp.where(first, ref[0, pl.ds(tp, 1), :] * keep, pltpu.roll(x, 1, axis=0))
            return x + (prev - x) * mu

        r, k, v = shifted(r_ref, mu_r), shifted(k_ref, mu_k), shifted(v_ref, mu_v)
        lo, glo = shifted(lo_ref, mu_lo), shifted(glo_ref, mu_g)
        wp = -(w0 + _mm(jnp.tanh(lo), wup_ref[...]))
        w = -(jnp.maximum(wp, 0.0) + jnp.log(1.0 + jnp.exp(-jnp.abs(wp)))) - 0.5
        a = jax.nn.sigmoid(a0 + _mm(lo, aup_ref[...]))
        kk = k * k_k
        kk = kk / jnp.maximum(jnp.sqrt(_head_sum(kk * kk, low)), 1e-12)
        k2 = k * (1.0 + (a - 1.0) * k_a)
        rs[pl.ds(t0, RB), :] = r
        ws[pl.ds(t0, RB), :] = -jnp.exp(w)
        ks[pl.ds(t0, RB), :] = k2
        vs[pl.ds(t0, RB), :] = v
        als[pl.ds(t0, RB), :] = -kk
        bes[pl.ds(t0, RB), :] = kk * a
        gs[pl.ds(t0, RB), :] = _mm(jax.nn.sigmoid(glo), gup_ref[...])
        o_ref[0, pl.ds(t0, RB), :] = _head_sum(r * k2 * r_k, low) * v
        return c

    lax.fori_loop(0, S // RB, prologue, 0)

    W2 = 2 * C
    row = lax.broadcasted_iota(jnp.int32, (W2, W2), 0)
    col = lax.broadcasted_iota(jnp.int32, (W2, W2), 1)
    t_idx, s_idx = row % C, col % C
    top, left = row < C, col < C
    same = top == left
    eye = jnp.where(row == col, 1.0, 0.0)
    tri = jnp.where(lax.broadcasted_iota(jnp.int32, (C, C), 1) <= lax.broadcasted_iota(jnp.int32, (C, C), 0), 1.0, 0.0)
    low_c = lax.broadcasted_iota(jnp.int32, (C, V7X_LANES), 1) < HEAD_DIM
    fold = lambda x: x[0:C] + x[C:W2]
    stack_heads = lambda x: jnp.concatenate([jnp.where(low_c, x, 0.0), jnp.where(low_c, 0.0, x)], axis=0)
    block_diag = lambda x: jnp.where(top, jnp.where(left, x, 0.0), jnp.where(left, 0.0, pltpu.roll(x, C, axis=1)))

    rows = lambda c: pl.ds(c * C if isinstance(c, int) else pl.multiple_of(c * C, C), C)

    def advance(c, H):
        ys[rows(c), :] += _mm(rqs[c], H, exact=True)
        return _mm(ms[c], H, exact=True) + ns[c]

    def transfers(i, lagged):
        each = lambda f, *xs: [f(*a) for a in zip(*xs)]
        cs = [i * RWKV_INTERLEAVE + u for u in range(RWKV_INTERLEAVE)]
        sls = [rows(c) for c in cs]
        state = [hs[...]] if lagged else None

        def lag(u):
            if lagged:
                state[0] = advance(cs[u] - RWKV_INTERLEAVE, state[0])

        r, lw, k2, v, al, be = ([ref[sl, :] for sl in sls] for ref in (rs, ws, ks, vs, als, bes))
        logp = each(lambda x: _mm(tri, x, exact=True), lw)
        lag(0)
        P = each(jnp.exp, logp)
        Pinv = each(lambda x: jnp.exp(-x), logp)
        At = each(lambda a_, lp, w_: a_ * jnp.exp(lp - w_), al, logp, lw)
        Rt, Bt, Kt = each(jnp.multiply, r, P), each(jnp.multiply, be, Pinv), each(jnp.multiply, k2, Pinv)
        PC = each(lambda p: p[C - 1:C, :], P)
        A_bd, R_bd = each(stack_heads, At), each(stack_heads, Rt)
        Yt = each(lambda b, k: jnp.concatenate([b, k], axis=0), Bt, Kt)
        A1 = each(lambda a, y: jnp.where(s_idx < t_idx, _mm(a, y, dims=_NT), 0.0), A_bd, Yt)
        A2 = each(lambda a, y: jnp.where(s_idx <= t_idx, _mm(a, y, dims=_NT), 0.0), R_bd, Yt)
        X, Arb = each(block_diag, A1), each(block_diag, A2)
        T = each(lambda x: eye + x, X)
        for it in range(5):
            X = each(lambda x: _mm(x, x), X)
            T = each(lambda t, x: t + _mm(t, x), T, X)
            if it % 2 == 0:
                lag(1 + it // 2)
        V0 = each(lambda x: jnp.concatenate([jnp.zeros_like(x), x], axis=0), v)
        TA = each(_mm, T, A_bd)
        AkV = each(lambda a, x: jnp.where(same, _mm(a, x), 0.0), A1, V0)
        U0 = each(_mm, T, AkV)
        AR = each(lambda a, t, u: _mm(a, jnp.concatenate([t, u], axis=1)), Arb, TA, U0)
        ArkV = each(lambda a, x: jnp.where(same, _mm(a, x), 0.0), A2, V0)
        Mx = each(lambda b, p, t: _mm((b * p).T, fold(t)), Bt, PC, TA)
        Nx = each(lambda b, k, p, u, x: _mm(jnp.concatenate([b * p, k * p], axis=0).T,
                                            jnp.concatenate([fold(u), x], axis=0)), Bt, Kt, PC, U0, v)
        for u in range(RWKV_INTERLEAVE):
            ys[sls[u], :] = fold(AR[u][:, W2:2 * W2] + ArkV[u])
            rqs[cs[u]] = Rt[u] + fold(AR[u][:, 0:W2])
            ms[cs[u]] = eye * PC[u] + jnp.where(same, Mx[u], 0.0)
            ns[cs[u]] = jnp.where(same, Nx[u], 0.0)
        if lagged:
            hs[...] = state[0]

    def pipelined(i, carry):
        transfers(i, True)
        return carry

    def drain(c, carry):
        hs[...] = advance(c, hs[...])
        return carry

    assert RWKV_INTERLEAVE == 4
    hs[...] = jnp.zeros((W2, W2), F32)
    transfers(0, False)
    lax.fori_loop(1, S // C // RWKV_INTERLEAVE, pipelined, 0)
    lax.fori_loop(S // C - RWKV_INTERLEAVE, S // C, drain, 0)

    def epilogue(i, c):
        sl = pl.ds(pl.multiple_of(i * RB, RB), RB)
        y = ys[sl, :]
        d = y - _head_sum(y, low) * (1.0 / HEAD_DIM)
        var = _head_sum(d * d, low) * (1.0 / HEAD_DIM)
        yn = d * lax.rsqrt(var + RWKV_GN_EPS) * ln_g + ln_b
        o_ref[0, sl, :] = (yn + o_ref[0, sl, :]) * gs[sl, :]
        return c

    lax.fori_loop(0, S // RB, epilogue, 0)


def rwkv7_mixer(z, shift_mu, w0, w_up, a0, a_up, g_up, k_k, k_a, r_k, ln_g, ln_b):
    B, S, _ = z.shape
    CW = RWKV_HEADS * HEAD_DIM
    npair = CW // V7X_LANES
    base = 3 * CW // V7X_LANES
    lora = w_up.shape[0] + a_up.shape[0]
    assert lora == V7X_LANES and g_up.shape[0] == V7X_LANES and S % RWKV_ROWS == 0
    pp = jnp.stack([shift_mu[0:CW], shift_mu[CW:2 * CW], shift_mu[2 * CW:3 * CW], w0, a0, k_k, k_a,
                    r_k.reshape(CW), ln_g, ln_b])
    pp = jnp.pad(pp, ((0, 16 - pp.shape[0]), (0, 0)))
    pl2 = jnp.pad(shift_mu[3 * CW:].reshape(2, V7X_LANES), ((0, 6), (0, 0)))
    wup = jnp.pad(w_up, ((0, a_up.shape[0]), (0, 0)))
    aup = jnp.pad(a_up, ((w_up.shape[0], 0), (0, 0)))
    tile = lambda off: pl.BlockSpec((1, S, V7X_LANES), lambda b, p: (b, 0, base + off * npair + p))
    fixed = lambda off: pl.BlockSpec((1, S, V7X_LANES), lambda b, p: (b, 0, base + 3 * npair + off))
    seq = pltpu.VMEM((S, V7X_LANES), F32)
    return pl.pallas_call(
        _rwkv_kernel,
        grid=(B, npair),
        in_specs=[
            tile(0), tile(1), tile(2), fixed(0), fixed(1),
            pl.BlockSpec((16, V7X_LANES), lambda b, p: (0, p)),
            pl.BlockSpec((8, V7X_LANES), lambda b, p: (0, 0)),
            pl.BlockSpec((V7X_LANES, V7X_LANES), lambda b, p: (0, p)),
            pl.BlockSpec((V7X_LANES, V7X_LANES), lambda b, p: (0, p)),
            pl.BlockSpec((V7X_LANES, V7X_LANES), lambda b, p: (0, p)),
        ],
        out_specs=pl.BlockSpec((1, S, V7X_LANES), lambda b, p: (b, 0, p)),
        out_shape=jax.ShapeDtypeStruct((B, S, CW), F32),
        scratch_shapes=[seq] * 8 + [
            pltpu.VMEM((V7X_LANES, V7X_LANES), F32),
            pltpu.VMEM((S // RWKV_CHUNK, RWKV_CHUNK, V7X_LANES), F32),
            pltpu.VMEM((S // RWKV_CHUNK, V7X_LANES, V7X_LANES), F32),
            pltpu.VMEM((S // RWKV_CHUNK, V7X_LANES, V7X_LANES), F32),
        ],
        compiler_params=_params("parallel", "parallel"),
        name="rwkv7_mixer",
    )(z, z, z, z, z, pp, pl2, wup, aup, g_up.astype(BF16))


def _ret_kernel(q_ref, k_ref, v_ref, g_ref, cos_ref, sin_ref, din_ref, dq_ref, dk_ref, dc_ref, o_ref, st_ref):
    S = q_ref.shape[1]
    C, DV = RET_CHUNK, RET_V_DIM
    lane = lax.broadcasted_iota(jnp.int32, (C, V7X_LANES), 1)
    first_half = (lane % RET_QK_DIM) < RET_QK_DIM // 2
    st_ref[...] = jnp.zeros_like(st_ref)

    def chunk(c, carry):
        sl = pl.ds(pl.multiple_of(c * C, C), C)
        cos, sin = cos_ref[sl, :], sin_ref[sl, :]

        def rot(z):
            swapped = jnp.where(first_half, pltpu.roll(z, V7X_LANES - RET_QK_DIM // 2, axis=1),
                                pltpu.roll(z, RET_QK_DIM // 2, axis=1))
            return z * cos + swapped * sin

        q = rot(q_ref[0, sl, :])
        k = rot(k_ref[0, sl, :]) * (RET_QK_DIM ** -0.5)
        for h in range(2):
            in_head = (lane >= h * RET_QK_DIM) & (lane < (h + 1) * RET_QK_DIM)
            qm, km = jnp.where(in_head, q, 0.0), jnp.where(in_head, k, 0.0)
            v = v_ref[0, sl, h * DV:(h + 1) * DV]
            st = st_ref[h]
            inner = _mm(qm, k, dims=_NT) * din_ref[h]
            o = _mm(inner, v) + _mm(qm, st) * dq_ref[h]
            st_ref[h] = _mm((km * dk_ref[h]).T, v) + dc_ref[h, 0:1, :] * st
            d = o - jnp.mean(o, axis=1, keepdims=True)
            on = d * lax.rsqrt(jnp.mean(d * d, axis=1, keepdims=True) + RET_GN_EPS)
            gate = g_ref[0, sl, h * DV:(h + 1) * DV]
            o_ref[0, sl, h * DV:(h + 1) * DV] = gate * jax.nn.sigmoid(gate) * on
        return carry

    lax.fori_loop(0, S // C, chunk, 0)


def retention_mixer(z):
    B, S, _ = z.shape
    H, C, DK, DV = RET_HEADS, RET_CHUNK, RET_QK_DIM, RET_V_DIM
    assert S % C == 0 and 2 * DK == V7X_LANES and DV == V7X_LANES
    npair = H // 2
    half = DK // 2
    inv = ROPE_BASE ** (-jnp.arange(half, dtype=F32) / half)
    ang = jnp.arange(S, dtype=F32)[:, None] * inv
    cos = jnp.tile(jnp.cos(ang), (1, 4))
    sin = jnp.tile(jnp.concatenate([-jnp.sin(ang), jnp.sin(ang)], axis=1), (1, 2))
    log_g = jnp.asarray(np.log(1.0 - 2.0 ** (-5.0 - np.arange(H))), F32)
    n = jnp.arange(C, dtype=F32)
    diff = n[:, None] - n[None, :]
    d_in = jnp.where(diff >= 0, jnp.exp(jnp.maximum(diff, 0.0) * log_g[:, None, None]), 0.0)
    lanes = lambda t: jnp.broadcast_to(t[..., None], t.shape + (V7X_LANES,))
    d_q = lanes(jnp.exp((n + 1.0) * log_g[:, None]))
    d_k = lanes(jnp.exp((C - 1.0 - n) * log_g[:, None]))
    d_c = lanes(jnp.broadcast_to(jnp.exp(C * log_g)[:, None], (H, 8)))
    qk_tiles = H * DK // V7X_LANES
    return pl.pallas_call(
        _ret_kernel,
        grid=(B, npair),
        in_specs=[
            pl.BlockSpec((1, S, V7X_LANES), lambda b, p: (b, 0, p)),
            pl.BlockSpec((1, S, V7X_LANES), lambda b, p: (b, 0, qk_tiles + p)),
            pl.BlockSpec((1, S, 2 * DV), lambda b, p: (b, 0, 2 * qk_tiles * V7X_LANES // (2 * DV) + p)),
            pl.BlockSpec((1, S, 2 * DV), lambda b, p: (b, 0, (2 * qk_tiles * V7X_LANES + H * DV) // (2 * DV) + p)),
            pl.BlockSpec((S, V7X_LANES), lambda b, p: (0, 0)),
            pl.BlockSpec((S, V7X_LANES), lambda b, p: (0, 0)),
            pl.BlockSpec((2, C, C), lambda b, p: (p, 0, 0)),
            pl.BlockSpec((2, C, V7X_LANES), lambda b, p: (p, 0, 0)),
            pl.BlockSpec((2, C, V7X_LANES), lambda b, p: (p, 0, 0)),
            pl.BlockSpec((2, 8, V7X_LANES), lambda b, p: (p, 0, 0)),
        ],
        out_specs=pl.BlockSpec((1, S, 2 * DV), lambda b, p: (b, 0, p)),
        out_shape=jax.ShapeDtypeStruct((B, S, H * DV), F32),
        scratch_shapes=[pltpu.VMEM((2, V7X_LANES, DV), F32)],
        compiler_params=_params("parallel", "parallel"),
        name="retention_mixer",
    )(z, z, z, z, cos, sin, d_in, d_q, d_k, d_c)


def _even_mixer(x, g_norm, w_in, shift_mu, w0, w_up, a0, a_up, g_up, k_k, k_a, r_k, ln_g, ln_b):
    B, S, D = x.shape
    z = norm_matmul(x.reshape(B * S, D), g_norm, w_in.astype(BF16)).reshape(B, S, -1)
    o_a = moba_attention(z)
    o_b = rwkv7_mixer(z, shift_mu, w0, w_up, a0, a_up, g_up, k_k, k_a, r_k, ln_g, ln_b)
    return o_a, o_b


def _odd_mixer(x, g_norm, w_in, pe_k, w1_k, w2_k, pe_v, w1_v, w2_v):
    B, S, D = x.shape
    perm, col = _odd_layout()
    w_p = jnp.take(jnp.pad(w_in, ((0, 0), (0, 1))), perm, axis=1).astype(BF16)
    z = norm_matmul(x.reshape(B * S, D), g_norm, w_p).reshape(B, S, -1)
    o_c = retention_mixer(z)
    cmp_kv, cmp_vk = nsa_compress(z, col["kc"], col["vc"], pe_k, w1_k, w2_k, pe_v, w1_v, w2_v)
    o_d = nsa_attention(z, cmp_kv, cmp_vk, col["nq"], col["slc"], col["win"], col["gate"])
    return o_c, o_d


def _odd_layout():
    G, Dh = NSA_KV_GROUPS, HEAD_DIM
    sizes = (RET_HEADS * RET_QK_DIM, RET_HEADS * RET_QK_DIM, RET_HEADS * RET_V_DIM, RET_HEADS * RET_V_DIM,
             NSA_HEADS * Dh) + (G * Dh,) * 6 + (3 * NSA_HEADS,)
    off = np.concatenate([[0], np.cumsum(sizes)])
    rq, rk, rv, rg, nq, kc, vc, ks, vs, kw, vw, ng = off[:-1]
    n_in = int(off[-1])
    pair = lambda a, b: np.concatenate([np.concatenate([a + g * Dh + np.arange(Dh), b + g * Dh + np.arange(Dh)])
                                        for g in range(G)])
    perm = np.concatenate([np.arange(ks), pair(ks, vs), pair(kw, vw), ng + np.arange(3 * NSA_HEADS)])
    n_pad = -(-len(perm) // (6 * V7X_MXU_DIM)) * 6 * V7X_MXU_DIM
    perm = np.concatenate([perm, np.full(n_pad - len(perm), n_in)]).astype(np.int32)
    col = {"nq": int(nq), "kc": int(kc), "vc": int(vc), "slc": int(ks), "win": int(ks) + 2 * G * Dh,
           "gate": int(ks) + 4 * G * Dh}
    return perm, col


def kernel(x, mix_norm, ffn_norm, even_w_in, even_shift_mu, even_w0, even_w_up, even_a0, even_a_up, even_g_up, even_k_k, even_k_a, even_r_k, even_ln_g, even_ln_b, even_w_out, odd_w_in, odd_cmp_pe_k, odd_cmp_w1_k, odd_cmp_w2_k, odd_cmp_pe_v, odd_cmp_w1_v, odd_cmp_w2_v, odd_w_out, ffn_w1, ffn_w3, ffn_w2, final_norm):
    B, S, D = x.shape
    depth = mix_norm.shape[0]
    for layer in range(depth):
        i = layer // 2
        if layer % 2 == 0:
            o1, o2 = _even_mixer(x, mix_norm[layer], even_w_in[i], even_shift_mu[i], even_w0[i], even_w_up[i],
                                 even_a0[i], even_a_up[i], even_g_up[i], even_k_k[i], even_k_a[i], even_r_k[i],
                                 even_ln_g[i], even_ln_b[i])
            w_out = even_w_out[i]
        else:
            o1, o2 = _odd_mixer(x, mix_norm[layer], odd_w_in[i], odd_cmp_pe_k[i], odd_cmp_w1_k[i], odd_cmp_w2_k[i],
                                odd_cmp_pe_v[i], odd_cmp_w1_v[i], odd_cmp_w2_v[i])
            w_out = odd_w_out[i]
        T = B * S
        x2 = out_proj_residual(o1.reshape(T, -1), o2.reshape(T, -1), w_out.astype(BF16), x.reshape(T, D))
        x2 = ffn_residual(x2, ffn_norm[layer], ffn_w1[layer].astype(BF16), ffn_w3[layer].astype(BF16),
                          ffn_w2[layer].astype(BF16), final_norm if layer == depth - 1 else None)
        x = x2.reshape(B, S, D)
    return x
```

```python
import functools

import jax
import jax.numpy as jnp
import numpy as np
from jax import lax
from jax.experimental import pallas as pl
from jax.experimental.pallas import tpu as pltpu

F32 = jnp.float32
BF16 = jnp.bfloat16

V7X_LANES = 128
V7X_MXU_DIM = 256
V7X_VMEM_BYTES = 64 * 1024 * 1024
VMEM_LIMIT = V7X_VMEM_BYTES * 7 // 8

NORM_EPS = 1e-6
HEAD_DIM = 64

MOBA_BLOCK = 256
MOBA_TOPK = 3
RWKV_HEADS = 16
RWKV_GN_EPS = 6.4e-4

RET_HEADS = 8
RET_QK_DIM = 64
RET_V_DIM = 128
RET_CHUNK = 128
RET_GN_EPS = 1e-6
ROPE_BASE = 10000.0
NSA_HEADS = 16
NSA_KV_GROUPS = 4
NSA_CMP_BLOCK = 32
NSA_CMP_STRIDE = 16
NSA_SLC_BLOCK = 64
NSA_SLC_TOPN = 16
NSA_WINDOW = 512


def _params(*semantics):
    return pltpu.CompilerParams(dimension_semantics=semantics, vmem_limit_bytes=VMEM_LIMIT)


def _rms(x, g):
    return x * lax.rsqrt(jnp.mean(x * x, axis=-1, keepdims=True) + NORM_EPS) * g


def _norm_matmul_kernel(x_ref, g_ref, w_ref, o_ref, h_ref):
    @pl.when(pl.program_id(1) == 0)
    def _():
        h_ref[...] = _rms(x_ref[...], g_ref[...]).astype(BF16)

    o_ref[...] = jnp.dot(h_ref[...], w_ref[...], preferred_element_type=F32)


def _proj_tile(n):
    assert n % V7X_MXU_DIM == 0
    k = n // V7X_MXU_DIM
    return V7X_MXU_DIM * max(d for d in range(1, 7) if k % d == 0)


def norm_matmul(x, g, w, *, tm=512):
    T, D = x.shape
    N = w.shape[1]
    tn = _proj_tile(N)
    assert T % tm == 0 and N % tn == 0
    return pl.pallas_call(
        _norm_matmul_kernel,
        grid=(T // tm, N // tn),
        in_specs=[
            pl.BlockSpec((tm, D), lambda i, j: (i, 0)),
            pl.BlockSpec((1, D), lambda i, j: (0, 0)),
            pl.BlockSpec((D, tn), lambda i, j: (0, j)),
        ],
        out_specs=pl.BlockSpec((tm, tn), lambda i, j: (i, j)),
        out_shape=jax.ShapeDtypeStruct((T, N), F32),
        scratch_shapes=[pltpu.VMEM((tm, D), BF16)],
        compiler_params=_params("parallel", "arbitrary"),
        name="norm_matmul",
    )(x, g.reshape(1, D), w)


def _out_proj_kernel(a_ref, b_ref, wa_ref, wb_ref, x_ref, o_ref):
    acc = jnp.dot(a_ref[...].astype(BF16), wa_ref[...], preferred_element_type=F32)
    acc += jnp.dot(b_ref[...].astype(BF16), wb_ref[...], preferred_element_type=F32)
    o_ref[...] = x_ref[...] + acc


def out_proj_residual(a, b, w, x, *, tm=512):
    T, D = x.shape
    Ka, Kb = a.shape[1], b.shape[1]
    assert T % tm == 0 and w.shape == (Ka + Kb, D)
    return pl.pallas_call(
        _out_proj_kernel,
        grid=(T // tm,),
        in_specs=[
            pl.BlockSpec((tm, Ka), lambda i: (i, 0)),
            pl.BlockSpec((tm, Kb), lambda i: (i, 0)),
            pl.BlockSpec((Ka, D), lambda i: (0, 0)),
            pl.BlockSpec((Kb, D), lambda i: (0, 0)),
            pl.BlockSpec((tm, D), lambda i: (i, 0)),
        ],
        out_specs=pl.BlockSpec((tm, D), lambda i: (i, 0)),
        out_shape=jax.ShapeDtypeStruct((T, D), F32),
        compiler_params=_params("parallel"),
        name="out_proj_residual",
    )(a, b, w[:Ka], w[Ka:], x)


def _ffn_kernel(x_ref, g_ref, w1_ref, w3_ref, w2_ref, gf_ref, o_ref, h_ref, acc_ref, *, final_norm):
    j = pl.program_id(1)

    @pl.when(j == 0)
    def _():
        h_ref[...] = _rms(x_ref[...], g_ref[...]).astype(BF16)
        acc_ref[...] = jnp.zeros_like(acc_ref)

    h = h_ref[...]
    a = jnp.dot(h, w1_ref[...], preferred_element_type=F32)
    b = jnp.dot(h, w3_ref[...], preferred_element_type=F32)
    act = (a * jax.nn.sigmoid(a) * b).astype(BF16)
    acc_ref[...] += jnp.dot(act, w2_ref[...], preferred_element_type=F32)

    @pl.when(j == pl.num_programs(1) - 1)
    def _():
        y = x_ref[...] + acc_ref[...]
        if final_norm:
            y = _rms(y, gf_ref[...])
        o_ref[...] = y


def ffn_residual(x, g, w1, w3, w2, g_final=None, *, tm=512, tf=512):
    T, D = x.shape
    Fh = w1.shape[1]
    assert T % tm == 0 and Fh % tf == 0
    final_norm = g_final is not None
    gf = (g_final if final_norm else g).reshape(1, D)
    return pl.pallas_call(
        functools.partial(_ffn_kernel, final_norm=final_norm),
        grid=(T // tm, Fh // tf),
        in_specs=[
            pl.BlockSpec((tm, D), lambda i, j: (i, 0)),
            pl.BlockSpec((1, D), lambda i, j: (0, 0)),
            pl.BlockSpec((D, tf), lambda i, j: (0, j)),
            pl.BlockSpec((D, tf), lambda i, j: (0, j)),
            pl.BlockSpec((tf, D), lambda i, j: (j, 0)),
            pl.BlockSpec((1, D), lambda i, j: (0, 0)),
        ],
        out_specs=pl.BlockSpec((tm, D), lambda i, j: (i, 0)),
        out_shape=jax.ShapeDtypeStruct((T, D), F32),
        scratch_shapes=[pltpu.VMEM((tm, D), BF16), pltpu.VMEM((tm, D), F32)],
        compiler_params=_params("parallel", "arbitrary"),
        name="ffn_residual",
    )(x, g.reshape(1, D), w1, w3, w2, gf)


NEG_BIG = -1e30
_NT = (((1,), (1,)), ((), ()))


def _flash_steps(qas, kas, vas, masks, m_prev, acc_prev):
    hs = range(len(qas))
    s = [lax.dot_general(qas[h], kas[h], _NT, preferred_element_type=F32) for h in hs]
    s = [s[h] if masks[h] is None else jnp.where(masks[h], s[h], NEG_BIG) for h in hs]
    m_new = [jnp.maximum(m_prev[h], jnp.max(s[h], axis=1, keepdims=True)) for h in hs]
    alpha = [jnp.exp2(m_prev[h] - m_new[h]) for h in hs]
    p = [jnp.exp2(s[h] - m_new[h]) for h in hs]
    pv = [jnp.dot(p[h].astype(BF16), vas[h], preferred_element_type=F32) for h in hs]
    return m_new, [alpha[h] * acc_prev[h] + pv[h] for h in hs]


def _augment_q(q_log2, in_head, keep_t, odd):
    nblk, tq = keep_t.shape
    bias_t = jnp.where(keep_t, 0.0, NEG_BIG)
    bias = jnp.concatenate([bias_t, jnp.zeros((V7X_LANES - nblk, tq), F32)], axis=0).T
    if not odd:
        bias = pltpu.roll(bias, HEAD_DIM, axis=1)
    return jnp.where(in_head, q_log2, bias).astype(BF16)


def _key_value_tiles(k, v, blk, lane):
    low = lane < HEAD_DIM
    hot_e = 0.0 if blk is None else jnp.where(lane - HEAD_DIM == blk, 1.0, 0.0)
    hot_o = 0.0 if blk is None else jnp.where(lane == blk, 1.0, 0.0)
    k_e, k_o = jnp.where(low, k, hot_e), jnp.where(low, hot_o, k)
    v_e, v_o = jnp.where(low, v, jnp.where(lane == HEAD_DIM, 1.0, 0.0)), jnp.where(low, jnp.where(lane == 0, 1.0, 0.0), v)
    return [t.astype(BF16) for t in (k_e, k_o, v_e, v_o)]


def _normalise(acc, lane, odd):
    return acc / jnp.sum(jnp.where(lane == (0 if odd else HEAD_DIM), acc, 0.0), axis=1, keepdims=True)


LOG2E = 1.4426950408889634


def _moba_kernel(q_ref, k_ref, v_ref, o_ref, ka_ref, va_ref, km_ref, acc_ref):
    L = MOBA_BLOCK
    S = k_ref.shape[1]
    nb = S // L
    qi = pl.program_id(2)
    lane = lax.broadcasted_iota(jnp.int32, (L, V7X_LANES), 1)

    @pl.when(qi == 0)
    def _():
        k = k_ref[0]
        lane_s = lax.broadcasted_iota(jnp.int32, (S, V7X_LANES), 1)
        blk = lax.broadcasted_iota(jnp.int32, (S, V7X_LANES), 0) // L
        ka_ref[0], ka_ref[1], va_ref[0], va_ref[1] = _key_value_tiles(k, v_ref[0], blk, lane_s)
        km_ref[...] = jnp.zeros_like(km_ref)
        km_ref[0:nb, :] = jnp.mean(k.reshape(nb, L, V7X_LANES), axis=1)

    q = q_ref[0] * (HEAD_DIM ** -0.5)
    row = lax.broadcasted_iota(jnp.int32, (L, L), 0)
    col = lax.broadcasted_iota(jnp.int32, (L, L), 1)
    causal = col <= row
    jrow = lax.broadcasted_iota(jnp.int32, (nb, L), 0)
    past = jrow < qi
    qas = []
    for h in range(2):
        in_head = (lane >= h * HEAD_DIM) & (lane < (h + 1) * HEAD_DIM)
        gate = lax.dot_general(km_ref[...], jnp.where(in_head, q, 0.0), _NT, precision=lax.Precision.HIGHEST,
                               preferred_element_type=F32)[0:nb]
        keep = jrow == qi
        for n in range(nb):
            g_n = gate[n:n + 1, :]
            beats = (gate > g_n) | ((gate == g_n) & (jrow < n))
            rank = jnp.sum(jnp.where(past & beats, 1.0, 0.0), axis=0, keepdims=True)
            keep = keep | ((jrow == n) & (rank < MOBA_TOPK) & past)
        qas.append(_augment_q(q * LOG2E, in_head, keep, odd=h == 1))

    tiles = lambda start: ([ka_ref[h, pl.ds(start, L), :] for h in range(2)],
                           [va_ref[h, pl.ds(start, L), :] for h in range(2)])

    m, acc = _flash_steps(qas, *tiles(pl.multiple_of(qi * L, L)), [causal] * 2,
                          [jnp.full((L, 1), NEG_BIG, F32)] * 2, [jnp.zeros((L, V7X_LANES), F32)] * 2)
    acc_ref[0], acc_ref[1] = acc

    def body(n, carry):
        m2, acc2 = _flash_steps(qas, *tiles(pl.multiple_of(n * L, L)), [None] * 2, list(carry),
                                [acc_ref[0], acc_ref[1]])
        acc_ref[0], acc_ref[1] = acc2
        return tuple(m2)

    lax.fori_loop(0, qi, body, tuple(m))
    o_ref[0] = jnp.where(lane < HEAD_DIM, _normalise(acc_ref[0], lane, False), _normalise(acc_ref[1], lane, True))


def moba_attention(z, *, n_heads=16):
    B, S, _ = z.shape
    L = MOBA_BLOCK
    assert S % L == 0 and n_heads % 2 == 0
    npair = n_heads // 2
    return pl.pallas_call(
        _moba_kernel,
        grid=(B, npair, S // L),
        in_specs=[
            pl.BlockSpec((1, L, V7X_LANES), lambda b, p, i: (b, i, p)),
            pl.BlockSpec((1, S, V7X_LANES), lambda b, p, i: (b, 0, npair + p)),
            pl.BlockSpec((1, S, V7X_LANES), lambda b, p, i: (b, 0, 2 * npair + p)),
        ],
        out_specs=pl.BlockSpec((1, L, V7X_LANES), lambda b, p, i: (b, i, p)),
        out_shape=jax.ShapeDtypeStruct((B, S, n_heads * HEAD_DIM), F32),
        scratch_shapes=[
            pltpu.VMEM((2, S, V7X_LANES), BF16),
            pltpu.VMEM((2, S, V7X_LANES), BF16),
            pltpu.VMEM((V7X_LANES, V7X_LANES), F32),
            pltpu.VMEM((2, L, V7X_LANES), F32),
        ],
        compiler_params=_params("parallel", "parallel", "arbitrary"),
        name="moba_attention",
    )(z, z, z)


NSA_TQ = 256
BIG = 3.0e38


def _gelu_tanh(x):
    return 0.5 * x * (1.0 + jnp.tanh(0.7978845608028654 * (x + 0.044715 * x * x * x)))


def _nsa_compress_kernel(xk0_ref, xk1_ref, xv0_ref, xv1_ref, pek_ref, pev_ref, w1k_ref, w1v_ref, w2k_ref, w2v_ref,
                         o1_ref, o2_ref):
    G, Lc, st = NSA_KV_GROUPS, NSA_CMP_BLOCK, NSA_CMP_STRIDE
    nrow = xk0_ref.shape[1] // st
    lane = lax.broadcasted_iota(jnp.int32, (nrow, G * HEAD_DIM), 1)

    def hidden(x_refs, pe_ref, w1_ref):
        acc = [jnp.zeros((G * nrow, V7X_LANES), F32) for _ in range(Lc // st)]
        for l in range(Lc):
            u, m = divmod(l, st)
            x = jnp.concatenate([r[0, pl.ds(m, nrow, stride=st), :] for r in x_refs], axis=1) + pe_ref[l:l + 1, :]
            xs = jnp.concatenate(
                [jnp.where((lane >= g * HEAD_DIM) & (lane < (g + 1) * HEAD_DIM), x, 0.0) for g in range(G)],
                axis=0).astype(BF16)
            acc[u] = acc[u] + jnp.dot(xs, w1_ref[l], preferred_element_type=F32)
        nxt = jnp.concatenate([pltpu.roll(acc[1][g * nrow:(g + 1) * nrow], nrow - 1, axis=0) for g in range(G)],
                              axis=0)
        return _gelu_tanh(acc[0] + nxt).astype(BF16)

    hk = hidden((xk0_ref, xk1_ref), pek_ref, w1k_ref)
    hv = hidden((xv0_ref, xv1_ref), pev_ref, w1v_ref)
    kc = jnp.dot(hk, w2k_ref[...], preferred_element_type=F32)
    vc = jnp.dot(hv, w2v_ref[...], preferred_element_type=F32)
    kv = kc + vc
    vk = pltpu.roll(kv, HEAD_DIM, axis=1)
    for g in range(G):
        o1_ref[0, :, g * V7X_LANES:(g + 1) * V7X_LANES] = kv[g * nrow:(g + 1) * nrow]
        o2_ref[0, :, g * V7X_LANES:(g + 1) * V7X_LANES] = vk[g * nrow:(g + 1) * nrow]


def nsa_compress(z, col_k, col_v, pe_k, w1_k, w2_k, pe_v, w1_v, w2_v):
    B, S, _ = z.shape
    G, Lc, st = NSA_KV_GROUPS, NSA_CMP_BLOCK, NSA_CMP_STRIDE
    GW = G * HEAD_DIM
    nrow = S // st
    hid = w1_k.shape[1]
    assert hid == V7X_LANES and col_k % GW == 0 and col_v % GW == 0
    tile_pe = lambda pe: jnp.tile(pe, (1, G))
    tile_w1 = lambda w: jnp.tile(w.reshape(Lc, 1, HEAD_DIM, hid), (1, G, 1, 1)).reshape(Lc, GW, hid).astype(BF16)
    w2k = jnp.pad(w2_k, ((0, 0), (0, HEAD_DIM))).astype(BF16)
    w2v = jnp.pad(w2_v, ((0, 0), (HEAD_DIM, 0))).astype(BF16)
    const = lambda shape: pl.BlockSpec(shape, lambda b: (0,) * len(shape))
    out = jax.ShapeDtypeStruct((B, nrow, G * V7X_LANES), F32)
    return pl.pallas_call(
        _nsa_compress_kernel,
        grid=(B,),
        in_specs=[
            pl.BlockSpec((1, S, V7X_LANES), lambda b: (b, 0, col_k // V7X_LANES)),
            pl.BlockSpec((1, S, V7X_LANES), lambda b: (b, 0, col_k // V7X_LANES + 1)),
            pl.BlockSpec((1, S, V7X_LANES), lambda b: (b, 0, col_v // V7X_LANES)),
            pl.BlockSpec((1, S, V7X_LANES), lambda b: (b, 0, col_v // V7X_LANES + 1)),
            const((Lc, GW)), const((Lc, GW)),
            const((Lc, GW, hid)), const((Lc, GW, hid)),
            const((hid, V7X_LANES)), const((hid, V7X_LANES)),
        ],
        out_specs=[pl.BlockSpec((1, nrow, G * V7X_LANES), lambda b: (b, 0, 0))] * 2,
        out_shape=[out, out],
        compiler_params=_params("parallel"),
        name="nsa_compress",
    )(z, z, z, z, tile_pe(pe_k), tile_pe(pe_v), tile_w1(w1_k), tile_w1(w1_v), w2k, w2v)


def _nsa_kernel(q_ref, c1_ref, c2_ref, s_ref, w_ref, g_ref, ovt_ref, o_ref, sk_ref, sv_ref, wk_ref, wv_ref, acc_ref):
    TQ = NSA_TQ
    S = s_ref.shape[1]
    R = NSA_HEADS // NSA_KV_GROUPS
    grp = pl.program_id(1)
    qi = pl.program_id(2)
    lane = lax.broadcasted_iota(jnp.int32, (TQ, V7X_LANES), 1)

    @pl.when(qi == 0)
    def _():
        lane_s = lax.broadcasted_iota(jnp.int32, (S, V7X_LANES), 1)
        blk = lax.broadcasted_iota(jnp.int32, (S, V7X_LANES), 0) // NSA_SLC_BLOCK
        kv = s_ref[0]
        vk = pltpu.roll(kv, HEAD_DIM, axis=1)
        sk_ref[0], sk_ref[1], sv_ref[0], sv_ref[1] = _key_value_tiles(
            jnp.where(lane_s < HEAD_DIM, kv, vk), jnp.where(lane_s < HEAD_DIM, vk, kv), blk, lane_s)
        kv = w_ref[0]
        vk = pltpu.roll(kv, HEAD_DIM, axis=1)
        wk_ref[0], wk_ref[1], wv_ref[0], wv_ref[1] = _key_value_tiles(
            jnp.where(lane_s < HEAD_DIM, kv, vk), jnp.where(lane_s < HEAD_DIM, vk, kv), None, lane_s)

    q0 = pl.multiple_of(qi * TQ, TQ)
    row = lax.broadcasted_iota(jnp.int32, (TQ, TQ), 0)
    col = lax.broadcasted_iota(jnp.int32, (TQ, TQ), 1)
    causal = col <= row
    qi_mat = jnp.zeros((TQ, TQ), jnp.int32) + qi
    t_abs = q0 + lax.broadcasted_iota(jnp.int32, (TQ, V7X_LANES), 0)
    even_lanes = lane < HEAD_DIM

    c_kv, c_vk = c1_ref[0], c2_ref[0]
    c_kv_b, c_vk_b = c_kv.astype(BF16), c_vk.astype(BF16)
    cmask = lane * NSA_CMP_STRIDE + (NSA_CMP_BLOCK - 1) <= t_abs
    tiles, o_cmp = [], []
    p_sum = jnp.zeros((TQ, V7X_LANES), F32)
    for r in range(R):
        tiles.append(q_ref[0, :, (r // 2) * V7X_LANES:(r // 2 + 1) * V7X_LANES] * (HEAD_DIM ** -0.5))
        qm = jnp.where(even_lanes if r % 2 == 0 else ~even_lanes, tiles[r], 0.0).astype(BF16)
        s = lax.dot_general(qm, c_kv_b if r % 2 == 0 else c_vk_b, _NT, preferred_element_type=F32)
        s = jnp.where(cmask, s, NEG_BIG)
        p = jnp.where(cmask, jnp.exp(s - jnp.max(s, axis=1, keepdims=True)), 0.0)
        den = jnp.sum(p, axis=1, keepdims=True)
        p = p / jnp.where(den > 0.0, den, 1.0)
        p_sum = p_sum + p
        o_cmp.append(jnp.dot(p.astype(BF16), c_vk_b if r % 2 == 0 else c_kv_b, preferred_element_type=F32))

    nblk = s_ref.shape[1] // NSA_SLC_BLOCK
    p_slc = lax.dot_general(ovt_ref[...], p_sum, _NT, precision=lax.Precision.HIGHEST,
                            preferred_element_type=F32)[0:nblk]
    jrow = lax.broadcasted_iota(jnp.int32, (nblk, TQ), 0)
    own = (q0 + lax.broadcasted_iota(jnp.int32, (nblk, TQ), 1)) // NSA_SLC_BLOCK
    score = jnp.where((jrow == own) | (jrow == 0), BIG, jnp.where(jrow > own, -BIG, p_slc))
    keep = jrow > nblk
    for j in range(nblk):
        s_j = score[j:j + 1, :]
        beats = (score > s_j) | ((score == s_j) & (jrow < j))
        rank = jnp.sum(jnp.where(beats, 1.0, 0.0), axis=0, keepdims=True)
        keep = keep | ((jrow == j) & (rank < NSA_SLC_TOPN) & (jrow <= own))
    qas = [_augment_q(tiles[r] * LOG2E, even_lanes if r % 2 == 0 else ~even_lanes, keep, odd=r % 2 == 1)
           for r in range(R)]

    neg = [jnp.full((TQ, 1), NEG_BIG, F32)] * R
    zacc = [jnp.zeros((TQ, V7X_LANES), F32)] * R

    def kv_blocks(k_ref, v_ref, start):
        return ([k_ref[r % 2, pl.ds(start, TQ), :] for r in range(R)],
                [v_ref[r % 2, pl.ds(start, TQ), :] for r in range(R)])

    m, acc = _flash_steps(qas, *kv_blocks(sk_ref, sv_ref, q0), [causal] * R, neg, zacc)
    for r in range(R):
        acc_ref[r] = acc[r]

    def body(kb, carry):
        m2, acc2 = _flash_steps(qas, *kv_blocks(sk_ref, sv_ref, pl.multiple_of(kb * TQ, TQ)), [None] * R,
                                list(carry), [acc_ref[r] for r in range(R)])
        for r in range(R):
            acc_ref[r] = acc2[r]
        return tuple(m2)

    lax.fori_loop(0, qi, body, tuple(m))

    m, acc = _flash_steps(qas, *kv_blocks(wk_ref, wv_ref, q0), [causal] * R, neg, zacc)
    m, acc = _flash_steps(qas, *kv_blocks(wk_ref, wv_ref, pl.multiple_of(jnp.maximum(qi - 1, 0) * TQ, TQ)),
                          [qi_mat >= 1] * R, m, acc)
    m, acc = _flash_steps(qas, *kv_blocks(wk_ref, wv_ref, pl.multiple_of(jnp.maximum(qi - 2, 0) * TQ, TQ)),
                          [(col > row) & (qi_mat >= 2)] * R, m, acc)

    gates = jax.nn.sigmoid(g_ref[0])
    outs = []
    for r in range(R):
        o_slc = _normalise(acc_ref[r], lane, r % 2 == 1)
        o_win = _normalise(acc[r], lane, r % 2 == 1)
        c0 = (grp * R + r) * 3
        gate = lambda c: jnp.sum(jnp.where(lane == c, gates, 0.0), axis=1, keepdims=True)
        outs.append(gate(c0) * o_cmp[r] + gate(c0 + 1) * o_slc + gate(c0 + 2) * o_win)
    for p2 in range(R // 2):
        o_ref[0, :, p2 * V7X_LANES:(p2 + 1) * V7X_LANES] = jnp.where(even_lanes, outs[2 * p2], outs[2 * p2 + 1])


def nsa_attention(z, cmp_kv, cmp_vk, col_q, col_slc, col_win, col_gate):
    B, S, _ = z.shape
    G, TQ = NSA_KV_GROUPS, NSA_TQ
    R = NSA_HEADS // G
    QW = R * HEAD_DIM
    ncmp = cmp_kv.shape[1]
    assert S % TQ == 0 and ncmp == V7X_LANES and S // NSA_SLC_BLOCK <= V7X_LANES and NSA_WINDOW == 2 * TQ
    assert col_q % QW == 0 and col_slc % V7X_LANES == 0 and col_win % V7X_LANES == 0 and col_gate % V7X_LANES == 0
    nc = (S - NSA_CMP_BLOCK) // NSA_CMP_STRIDE + 1
    c_start = np.arange(V7X_LANES) * NSA_CMP_STRIDE
    s_start = np.arange(V7X_LANES) * NSA_SLC_BLOCK
    overlap = ((c_start[:, None] <= s_start[None, :] + NSA_SLC_BLOCK - 1)
               & (c_start[:, None] + NSA_CMP_BLOCK - 1 >= s_start[None, :])
               & (np.arange(V7X_LANES)[:, None] < nc) & (np.arange(V7X_LANES)[None, :] < S // NSA_SLC_BLOCK))
    const = lambda shape: pl.BlockSpec(shape, lambda b, g, i: (0,) * len(shape))
    return pl.pallas_call(
        _nsa_kernel,
        grid=(B, G, S // TQ),
        in_specs=[
            pl.BlockSpec((1, TQ, QW), lambda b, g, i: (b, i, col_q // QW + g)),
            pl.BlockSpec((1, ncmp, V7X_LANES), lambda b, g, i: (b, 0, g)),
            pl.BlockSpec((1, ncmp, V7X_LANES), lambda b, g, i: (b, 0, g)),
            pl.BlockSpec((1, S, V7X_LANES), lambda b, g, i: (b, 0, col_slc // V7X_LANES + g)),
            pl.BlockSpec((1, S, V7X_LANES), lambda b, g, i: (b, 0, col_win // V7X_LANES + g)),
            pl.BlockSpec((1, TQ, V7X_LANES), lambda b, g, i: (b, i, col_gate // V7X_LANES)),
            const((V7X_LANES, V7X_LANES)),
        ],
        out_specs=pl.BlockSpec((1, TQ, QW), lambda b, g, i: (b, i, g)),
        out_shape=jax.ShapeDtypeStruct((B, S, NSA_HEADS * HEAD_DIM), F32),
        scratch_shapes=[pltpu.VMEM((2, S, V7X_LANES), BF16)] * 4 + [
            pltpu.VMEM((R, TQ, V7X_LANES), F32),
        ],
        compiler_params=_params("parallel", "parallel", "arbitrary"),
        name="nsa_attention",
    )(z, cmp_kv, cmp_vk, z, z, z, jnp.asarray(overlap.T, F32))


RWKV_CHUNK = 64
RWKV_ROWS = 256
RWKV_INTERLEAVE = 4


def _mm(a, b, exact=False, dims=None):
    dims = dims or (((1,), (0,)), ((), ()))
    if exact:
        return lax.dot_general(a, b, dims, precision=lax.Precision.HIGHEST, preferred_element_type=F32)
    return lax.dot_general(a.astype(BF16), b.astype(BF16), dims, preferred_element_type=F32)


def _head_sum(x, low):
    s0 = jnp.sum(jnp.where(low, x, 0.0), axis=1, keepdims=True)
    s1 = jnp.sum(jnp.where(low, 0.0, x), axis=1, keepdims=True)
    return jnp.where(low, s0, s1)


def _rwkv_kernel(r_ref, k_ref, v_ref, lo_ref, glo_ref, pp_ref, pl_ref, wup_ref, aup_ref, gup_ref, o_ref,
                 rs, ws, ks, vs, als, bes, gs, ys, hs, rqs, ms, ns):
    S = r_ref.shape[1]
    C, RB = RWKV_CHUNK, RWKV_ROWS
    pp = pp_ref[...]
    mu_r, mu_k, mu_v, w0, a0, k_k, k_a, r_k, ln_g, ln_b = [pp[i:i + 1, :] for i in range(10)]
    mu_lo, mu_g = pl_ref[0:1, :], pl_ref[1:2, :]
    low = lax.broadcasted_iota(jnp.int32, (RB, V7X_LANES), 1) < HEAD_DIM
    first = lax.broadcasted_iota(jnp.int32, (RB, V7X_LANES), 0) == 0

    def prologue(i, c):
        t0 = pl.multiple_of(i * RB, RB)
        tp = jnp.maximum(t0 - 1, 0)
        keep = jnp.where(i > 0, 1.0, 0.0)

        def shifted(ref, mu):
            x = ref[0, pl.ds(t0, RB), :]
            prev = jnp.where(first, ref[0, pl.ds(tp, 1), :] * keep, pltpu.roll(x, 1, axis=0))
            return x + (prev - x) * mu

        r, k, v = shifted(r_ref, mu_r), shifted(k_ref, mu_k), shifted(v_ref, mu_v)
        lo, glo = shifted(lo_ref, mu_lo), shifted(glo_ref, mu_g)
        wp = -(w0 + _mm(jnp.tanh(lo), wup_ref[...]))
        w = -(jnp.maximum(wp, 0.0) + jnp.log(1.0 + jnp.exp(-jnp.abs(wp)))) - 0.5
        a = jax.nn.sigmoid(a0 + _mm(lo, aup_ref[...]))
        kk = k * k_k
        kk = kk / jnp.maximum(jnp.sqrt(_head_sum(kk * kk, low)), 1e-12)
        k2 = k * (1.0 + (a - 1.0) * k_a)
        rs[pl.ds(t0, RB), :] = r
        ws[pl.ds(t0, RB), :] = -jnp.exp(w)
        ks[pl.ds(t0, RB), :] = k2
        vs[pl.ds(t0, RB), :] = v
        als[pl.ds(t0, RB), :] = -kk
        bes[pl.ds(t0, RB), :] = kk * a
        gs[pl.ds(t0, RB), :] = _mm(jax.nn.sigmoid(glo), gup_ref[...])
        o_ref[0, pl.ds(t0, RB), :] = _head_sum(r * k2 * r_k, low) * v
        return c

    lax.fori_loop(0, S // RB, prologue, 0)

    W2 = 2 * C
    row = lax.broadcasted_iota(jnp.int32, (W2, W2), 0)
    col = lax.broadcasted_iota(jnp.int32, (W2, W2), 1)
    t_idx, s_idx = row % C, col % C
    top, left = row < C, col < C
    same = top == left
    eye = jnp.where(row == col, 1.0, 0.0)
    tri = jnp.where(lax.broadcasted_iota(jnp.int32, (C, C), 1) <= lax.broadcasted_iota(jnp.int32, (C, C), 0), 1.0, 0.0)
    low_c = lax.broadcasted_iota(jnp.int32, (C, V7X_LANES), 1) < HEAD_DIM
    fold = lambda x: x[0:C] + x[C:W2]
    stack_heads = lambda x: jnp.concatenate([jnp.where(low_c, x, 0.0), jnp.where(low_c, 0.0, x)], axis=0)
    block_diag = lambda x: jnp.where(top, jnp.where(left, x, 0.0), jnp.where(left, 0.0, pltpu.roll(x, C, axis=1)))

    rows = lambda c: pl.ds(c * C if isinstance(c, int) else pl.multiple_of(c * C, C), C)

    def advance(c, H):
        ys[rows(c), :] += _mm(rqs[c], H, exact=True)
        return _mm(ms[c], H, exact=True) + ns[c]

    def transfers(i, lagged):
        each = lambda f, *xs: [f(*a) for a in zip(*xs)]
        cs = [i * RWKV_INTERLEAVE + u for u in range(RWKV_INTERLEAVE)]
        sls = [rows(c) for c in cs]
        state = [hs[...]] if lagged else None

        def lag(u):
            if lagged:
                state[0] = advance(cs[u] - RWKV_INTERLEAVE, state[0])

        r, lw, k2, v, al, be = ([ref[sl, :] for sl in sls] for ref in (rs, ws, ks, vs, als, bes))
        logp = each(lambda x: _mm(tri, x, exact=True), lw)
        lag(0)
        P = each(jnp.exp, logp)
        Pinv = each(lambda x: jnp.exp(-x), logp)
        At = each(lambda a_, lp, w_: a_ * jnp.exp(lp - w_), al, logp, lw)
        Rt, Bt, Kt = each(jnp.multiply, r, P), each(jnp.multiply, be, Pinv), each(jnp.multiply, k2, Pinv)
        PC = each(lambda p: p[C - 1:C, :], P)
        A_bd, R_bd = each(stack_heads, At), each(stack_heads, Rt)
        Yt = each(lambda b, k: jnp.concatenate([b, k], axis=0), Bt, Kt)
        A1 = each(lambda a, y: jnp.where(s_idx < t_idx, _mm(a, y, dims=_NT), 0.0), A_bd, Yt)
        A2 = each(lambda a, y: jnp.where(s_idx <= t_idx, _mm(a, y, dims=_NT), 0.0), R_bd, Yt)
        X, Arb = each(block_diag, A1), each(block_diag, A2)
        T = each(lambda x: eye + x, X)
        for it in range(5):
            X = each(lambda x: _mm(x, x), X)
            T = each(lambda t, x: t + _mm(t, x), T, X)
            if it % 2 == 0:
                lag(1 + it // 2)
        V0 = each(lambda x: jnp.concatenate([jnp.zeros_like(x), x], axis=0), v)
        TA = each(_mm, T, A_bd)
        AkV = each(lambda a, x: jnp.where(same, _mm(a, x), 0.0), A1, V0)
        U0 = each(_mm, T, AkV)
        AR = each(lambda a, t, u: _mm(a, jnp.concatenate([t, u], axis=1)), Arb, TA, U0)
        ArkV = each(lambda a, x: jnp.where(same, _mm(a, x), 0.0), A2, V0)
        Mx = each(lambda b, p, t: _mm((b * p).T, fold(t)), Bt, PC, TA)
        Nx = each(lambda b, k, p, u, x: _mm(jnp.concatenate([b * p, k * p], axis=0).T,
                                            jnp.concatenate([fold(u), x], axis=0)), Bt, Kt, PC, U0, v)
        for u in range(RWKV_INTERLEAVE):
            ys[sls[u], :] = fold(AR[u][:, W2:2 * W2] + ArkV[u])
            rqs[cs[u]] = Rt[u] + fold(AR[u][:, 0:W2])
            ms[cs[u]] = eye * PC[u] + jnp.where(same, Mx[u], 0.0)
            ns[cs[u]] = jnp.where(same, Nx[u], 0.0)
        if lagged:
            hs[...] = state[0]

    def pipelined(i, carry):
        transfers(i, True)
        return carry

    def drain(c, carry):
        hs[...] = advance(c, hs[...])
        return carry

    assert RWKV_INTERLEAVE == 4
    hs[...] = jnp.zeros((W2, W2), F32)
    transfers(0, False)
    lax.fori_loop(1, S // C // RWKV_INTERLEAVE, pipelined, 0)
    lax.fori_loop(S // C - RWKV_INTERLEAVE, S // C, drain, 0)

    def epilogue(i, c):
        sl = pl.ds(pl.multiple_of(i * RB, RB), RB)
        y = ys[sl, :]
        d = y - _head_sum(y, low) * (1.0 / HEAD_DIM)
        var = _head_sum(d * d, low) * (1.0 / HEAD_DIM)
        yn = d * lax.rsqrt(var + RWKV_GN_EPS) * ln_g + ln_b
        o_ref[0, sl, :] = (yn + o_ref[0, sl, :]) * gs[sl, :]
        return c

    lax.fori_loop(0, S // RB, epilogue, 0)


def rwkv7_mixer(z, shift_mu, w0, w_up, a0, a_up, g_up, k_k, k_a, r_k, ln_g, ln_b):
    B, S, _ = z.shape
    CW = RWKV_HEADS * HEAD_DIM
    npair = CW // V7X_LANES
    base = 3 * CW // V7X_LANES
    lora = w_up.shape[0] + a_up.shape[0]
    assert lora == V7X_LANES and g_up.shape[0] == V7X_LANES and S % RWKV_ROWS == 0
    pp = jnp.stack([shift_mu[0:CW], shift_mu[CW:2 * CW], shift_mu[2 * CW:3 * CW], w0, a0, k_k, k_a,
                    r_k.reshape(CW), ln_g, ln_b])
    pp = jnp.pad(pp, ((0, 16 - pp.shape[0]), (0, 0)))
    pl2 = jnp.pad(shift_mu[3 * CW:].reshape(2, V7X_LANES), ((0, 6), (0, 0)))
    wup = jnp.pad(w_up, ((0, a_up.shape[0]), (0, 0)))
    aup = jnp.pad(a_up, ((w_up.shape[0], 0), (0, 0)))
    tile = lambda off: pl.BlockSpec((1, S, V7X_LANES), lambda b, p: (b, 0, base + off * npair + p))
    fixed = lambda off: pl.BlockSpec((1, S, V7X_LANES), lambda b, p: (b, 0, base + 3 * npair + off))
    seq = pltpu.VMEM((S, V7X_LANES), F32)
    return pl.pallas_call(
        _rwkv_kernel,
        grid=(B, npair),
        in_specs=[
            tile(0), tile(1), tile(2), fixed(0), fixed(1),
            pl.BlockSpec((16, V7X_LANES), lambda b, p: (0, p)),
            pl.BlockSpec((8, V7X_LANES), lambda b, p: (0, 0)),
            pl.BlockSpec((V7X_LANES, V7X_LANES), lambda b, p: (0, p)),
            pl.BlockSpec((V7X_LANES, V7X_LANES), lambda b, p: (0, p)),
            pl.BlockSpec((V7X_LANES, V7X_LANES), lambda b, p: (0, p)),
        ],
        out_specs=pl.BlockSpec((1, S, V7X_LANES), lambda b, p: (b, 0, p)),
        out_shape=jax.ShapeDtypeStruct((B, S, CW), F32),
        scratch_shapes=[seq] * 8 + [
            pltpu.VMEM((V7X_LANES, V7X_LANES), F32),
            pltpu.VMEM((S // RWKV_CHUNK, RWKV_CHUNK, V7X_LANES), F32),
            pltpu.VMEM((S // RWKV_CHUNK, V7X_LANES, V7X_LANES), F32),
            pltpu.VMEM((S // RWKV_CHUNK, V7X_LANES, V7X_LANES), F32),
        ],
        compiler_params=_params("parallel", "parallel"),
        name="rwkv7_mixer",
    )(z, z, z, z, z, pp, pl2, wup, aup, g_up.astype(BF16))


def _ret_kernel(q_ref, k_ref, v_ref, g_ref, cos_ref, sin_ref, din_ref, dq_ref, dk_ref, dc_ref, o_ref, st_ref):
    S = q_ref.shape[1]
    C, DV = RET_CHUNK, RET_V_DIM
    lane = lax.broadcasted_iota(jnp.int32, (C, V7X_LANES), 1)
    first_half = (lane % RET_QK_DIM) < RET_QK_DIM // 2
    st_ref[...] = jnp.zeros_like(st_ref)

    def chunk(c, carry):
        sl = pl.ds(pl.multiple_of(c * C, C), C)
        cos, sin = cos_ref[sl, :], sin_ref[sl, :]

        def rot(z):
            swapped = jnp.where(first_half, pltpu.roll(z, V7X_LANES - RET_QK_DIM // 2, axis=1),
                                pltpu.roll(z, RET_QK_DIM // 2, axis=1))
            return z * cos + swapped * sin

        q = rot(q_ref[0, sl, :])
        k = rot(k_ref[0, sl, :]) * (RET_QK_DIM ** -0.5)
        for h in range(2):
            in_head = (lane >= h * RET_QK_DIM) & (lane < (h + 1) * RET_QK_DIM)
            qm, km = jnp.where(in_head, q, 0.0), jnp.where(in_head, k, 0.0)
            v = v_ref[0, sl, h * DV:(h + 1) * DV]
            st = st_ref[h]
            inner = _mm(qm, k, dims=_NT) * din_ref[h]
            o = _mm(inner, v) + _mm(qm, st) * dq_ref[h]
            st_ref[h] = _mm((km * dk_ref[h]).T, v) + dc_ref[h, 0:1, :] * st
            d = o - jnp.mean(o, axis=1, keepdims=True)
            on = d * lax.rsqrt(jnp.mean(d * d, axis=1, keepdims=True) + RET_GN_EPS)
            gate = g_ref[0, sl, h * DV:(h + 1) * DV]
            o_ref[0, sl, h * DV:(h + 1) * DV] = gate * jax.nn.sigmoid(gate) * on
        return carry

    lax.fori_loop(0, S // C, chunk, 0)


def retention_mixer(z):
    B, S, _ = z.shape
    H, C, DK, DV = RET_HEADS, RET_CHUNK, RET_QK_DIM, RET_V_DIM
    assert S % C == 0 and 2 * DK == V7X_LANES and DV == V7X_LANES
    npair = H // 2
    half = DK // 2
    inv = ROPE_BASE ** (-jnp.arange(half, dtype=F32) / half)
    ang = jnp.arange(S, dtype=F32)[:, None] * inv
    cos = jnp.tile(jnp.cos(ang), (1, 4))
    sin = jnp.tile(jnp.concatenate([-jnp.sin(ang), jnp.sin(ang)], axis=1), (1, 2))
    log_g = jnp.asarray(np.log(1.0 - 2.0 ** (-5.0 - np.arange(H))), F32)
    n = jnp.arange(C, dtype=F32)
    diff = n[:, None] - n[None, :]
    d_in = jnp.where(diff >= 0, jnp.exp(jnp.maximum(diff, 0.0) * log_g[:, None, None]), 0.0)
    lanes = lambda t: jnp.broadcast_to(t[..., None], t.shape + (V7X_LANES,))
    d_q = lanes(jnp.exp((n + 1.0) * log_g[:, None]))
    d_k = lanes(jnp.exp((C - 1.0 - n) * log_g[:, None]))
    d_c = lanes(jnp.broadcast_to(jnp.exp(C * log_g)[:, None], (H, 8)))
    qk_tiles = H * DK // V7X_LANES
    return pl.pallas_call(
        _ret_kernel,
        grid=(B, npair),
        in_specs=[
            pl.BlockSpec((1, S, V7X_LANES), lambda b, p: (b, 0, p)),
            pl.BlockSpec((1, S, V7X_LANES), lambda b, p: (b, 0, qk_tiles + p)),
            pl.BlockSpec((1, S, 2 * DV), lambda b, p: (b, 0, 2 * qk_tiles * V7X_LANES // (2 * DV) + p)),
            pl.BlockSpec((1, S, 2 * DV), lambda b, p: (b, 0, (2 * qk_tiles * V7X_LANES + H * DV) // (2 * DV) + p)),
            pl.BlockSpec((S, V7X_LANES), lambda b, p: (0, 0)),
            pl.BlockSpec((S, V7X_LANES), lambda b, p: (0, 0)),
            pl.BlockSpec((2, C, C), lambda b, p: (p, 0, 0)),
            pl.BlockSpec((2, C, V7X_LANES), lambda b, p: (p, 0, 0)),
            pl.BlockSpec((2, C, V7X_LANES), lambda b, p: (p, 0, 0)),
            pl.BlockSpec((2, 8, V7X_LANES), lambda b, p: (p, 0, 0)),
        ],
        out_specs=pl.BlockSpec((1, S, 2 * DV), lambda b, p: (b, 0, p)),
        out_shape=jax.ShapeDtypeStruct((B, S, H * DV), F32),
        scratch_shapes=[pltpu.VMEM((2, V7X_LANES, DV), F32)],
        compiler_params=_params("parallel", "parallel"),
        name="retention_mixer",
    )(z, z, z, z, cos, sin, d_in, d_q, d_k, d_c)


def _even_mixer(x, g_norm, w_in, shift_mu, w0, w_up, a0, a_up, g_up, k_k, k_a, r_k, ln_g, ln_b):
    B, S, D = x.shape
    z = norm_matmul(x.reshape(B * S, D), g_norm, w_in.astype(BF16)).reshape(B, S, -1)
    o_a = moba_attention(z)
    o_b = rwkv7_mixer(z, shift_mu, w0, w_up, a0, a_up, g_up, k_k, k_a, r_k, ln_g, ln_b)
    return o_a, o_b


def _odd_mixer(x, g_norm, w_in, pe_k, w1_k, w2_k, pe_v, w1_v, w2_v):
    B, S, D = x.shape
    perm, col = _odd_layout()
    w_p = jnp.take(jnp.pad(w_in, ((0, 0), (0, 1))), perm, axis=1).astype(BF16)
    z = norm_matmul(x.reshape(B * S, D), g_norm, w_p).reshape(B, S, -1)
    o_c = retention_mixer(z)
    cmp_kv, cmp_vk = nsa_compress(z, col["kc"], col["vc"], pe_k, w1_k, w2_k, pe_v, w1_v, w2_v)
    o_d = nsa_attention(z, cmp_kv, cmp_vk, col["nq"], col["slc"], col["win"], col["gate"])
    return o_c, o_d


def _odd_layout():
    G, Dh = NSA_KV_GROUPS, HEAD_DIM
    sizes = (RET_HEADS * RET_QK_DIM, RET_HEADS * RET_QK_DIM, RET_HEADS * RET_V_DIM, RET_HEADS * RET_V_DIM,
             NSA_HEADS * Dh) + (G * Dh,) * 6 + (3 * NSA_HEADS,)
    off = np.concatenate([[0], np.cumsum(sizes)])
    rq, rk, rv, rg, nq, kc, vc, ks, vs, kw, vw, ng = off[:-1]
    n_in = int(off[-1])
    pair = lambda a, b: np.concatenate([np.concatenate([a + g * Dh + np.arange(Dh), b + g * Dh + np.arange(Dh)])
                                        for g in range(G)])
    perm = np.concatenate([np.arange(ks), pair(ks, vs), pair(kw, vw), ng + np.arange(3 * NSA_HEADS)])
    n_pad = -(-len(perm) // (6 * V7X_MXU_DIM)) * 6 * V7X_MXU_DIM
    perm = np.concatenate([perm, np.full(n_pad - len(perm), n_in)]).astype(np.int32)
    col = {"nq": int(nq), "kc": int(kc), "vc": int(vc), "slc": int(ks), "win": int(ks) + 2 * G * Dh,
           "gate": int(ks) + 4 * G * Dh}
    return perm, col


def kernel(x, mix_norm, ffn_norm, even_w_in, even_shift_mu, even_w0, even_w_up, even_a0, even_a_up, even_g_up, even_k_k, even_k_a, even_r_k, even_ln_g, even_ln_b, even_w_out, odd_w_in, odd_cmp_pe_k, odd_cmp_w1_k, odd_cmp_w2_k, odd_cmp_pe_v, odd_cmp_w1_v, odd_cmp_w2_v, odd_w_out, ffn_w1, ffn_w3, ffn_w2, final_norm):
    B, S, D = x.shape
    depth = mix_norm.shape[0]
    for layer in range(depth):
        i = layer // 2
        if layer % 2 == 0:
            o1, o2 = _even_mixer(x, mix_norm[layer], even_w_in[i], even_shift_mu[i], even_w0[i], even_w_up[i],
                                 even_a0[i], even_a_up[i], even_g_up[i], even_k_k[i], even_k_a[i], even_r_k[i],
                                 even_ln_g[i], even_ln_b[i])
            w_out = even_w_out[i]
        else:
            o1, o2 = _odd_mixer(x, mix_norm[layer], odd_w_in[i], odd_cmp_pe_k[i], odd_cmp_w1_k[i], odd_cmp_w2_k[i],
                                odd_cmp_pe_v[i], odd_cmp_w1_v[i], odd_cmp_w2_v[i])
            w_out = odd_w_out[i]
        T = B * S
        x2 = out_proj_residual(o1.reshape(T, -1), o2.reshape(T, -1), w_out.astype(BF16), x.reshape(T, D))
        x2 = ffn_residual(x2, ffn_norm[layer], ffn_w1[layer].astype(BF16), ffn_w3[layer].astype(BF16),
                          ffn_w2[layer].astype(BF16), final_norm if layer == depth - 1 else None)
        x = x2.reshape(B, S, D)
    return x
```

```python
import functools

import jax
import jax.numpy as jnp
import numpy as np
from jax import lax
from jax.experimental import pallas as pl
from jax.experimental.pallas import tpu as pltpu

F32 = jnp.float32
BF16 = jnp.bfloat16

V7X_LANES = 128
V7X_MXU_DIM = 256
V7X_VMEM_BYTES = 64 * 1024 * 1024
VMEM_LIMIT = V7X_VMEM_BYTES * 7 // 8

NORM_EPS = 1e-6
HEAD_DIM = 64

MOBA_BLOCK = 256
MOBA_TOPK = 3
RWKV_HEADS = 16
RWKV_GN_EPS = 6.4e-4

RET_HEADS = 8
RET_QK_DIM = 64
RET_V_DIM = 128
RET_CHUNK = 128
RET_GN_EPS = 1e-6
ROPE_BASE = 10000.0
NSA_HEADS = 16
NSA_KV_GROUPS = 4
NSA_CMP_BLOCK = 32
NSA_CMP_STRIDE = 16
NSA_SLC_BLOCK = 64
NSA_SLC_TOPN = 16
NSA_WINDOW = 512


def _params(*semantics):
    return pltpu.CompilerParams(dimension_semantics=semantics, vmem_limit_bytes=VMEM_LIMIT)


def _rms(x, g):
    return x * lax.rsqrt(jnp.mean(x * x, axis=-1, keepdims=True) + NORM_EPS) * g


def _norm_matmul_kernel(x_ref, g_ref, w_ref, o_ref, h_ref):
    @pl.when(pl.program_id(1) == 0)
    def _():
        h_ref[...] = _rms(x_ref[...], g_ref[...]).astype(BF16)

    o_ref[...] = jnp.dot(h_ref[...], w_ref[...], preferred_element_type=F32)


def _proj_tile(n):
    assert n % V7X_MXU_DIM == 0
    k = n // V7X_MXU_DIM
    return V7X_MXU_DIM * max(d for d in range(1, 7) if k % d == 0)


def norm_matmul(x, g, w, *, tm=512):
    T, D = x.shape
    N = w.shape[1]
    tn = _proj_tile(N)
    assert T % tm == 0 and N % tn == 0
    return pl.pallas_call(
        _norm_matmul_kernel,
        grid=(T // tm, N // tn),
        in_specs=[
            pl.BlockSpec((tm, D), lambda i, j: (i, 0)),
            pl.BlockSpec((1, D), lambda i, j: (0, 0)),
            pl.BlockSpec((D, tn), lambda i, j: (0, j)),
        ],
        out_specs=pl.BlockSpec((tm, tn), lambda i, j: (i, j)),
        out_shape=jax.ShapeDtypeStruct((T, N), F32),
        scratch_shapes=[pltpu.VMEM((tm, D), BF16)],
        compiler_params=_params("parallel", "arbitrary"),
        name="norm_matmul",
    )(x, g.reshape(1, D), w)


def _out_proj_kernel(a_ref, b_ref, wa_ref, wb_ref, x_ref, o_ref):
    acc = jnp.dot(a_ref[...].astype(BF16), wa_ref[...], preferred_element_type=F32)
    acc += jnp.dot(b_ref[...].astype(BF16), wb_ref[...], preferred_element_type=F32)
    o_ref[...] = x_ref[...] + acc


def out_proj_residual(a, b, w, x, *, tm=512):
    T, D = x.shape
    Ka, Kb = a.shape[1], b.shape[1]
    assert T % tm == 0 and w.shape == (Ka + Kb, D)
    return pl.pallas_call(
        _out_proj_kernel,
        grid=(T // tm,),
        in_specs=[
            pl.BlockSpec((tm, Ka), lambda i: (i, 0)),
            pl.BlockSpec((tm, Kb), lambda i: (i, 0)),
            pl.BlockSpec((Ka, D), lambda i: (0, 0)),
            pl.BlockSpec((Kb, D), lambda i: (0, 0)),
            pl.BlockSpec((tm, D), lambda i: (i, 0)),
        ],
        out_specs=pl.BlockSpec((tm, D), lambda i: (i, 0)),
        out_shape=jax.ShapeDtypeStruct((T, D), F32),
        compiler_params=_params("parallel"),
        name="out_proj_residual",
    )(a, b, w[:Ka], w[Ka:], x)


def _ffn_kernel(x_ref, g_ref, w1_ref, w3_ref, w2_ref, gf_ref, o_ref, h_ref, acc_ref, *, final_norm):
    j = pl.program_id(1)

    @pl.when(j == 0)
    def _():
        h_ref[...] = _rms(x_ref[...], g_ref[...]).astype(BF16)
        acc_ref[...] = jnp.zeros_like(acc_ref)

    h = h_ref[...]
    a = jnp.dot(h, w1_ref[...], preferred_element_type=F32)
    b = jnp.dot(h, w3_ref[...], preferred_element_type=F32)
    act = (a * jax.nn.sigmoid(a) * b).astype(BF16)
    acc_ref[...] += jnp.dot(act, w2_ref[...], preferred_element_type=F32)

    @pl.when(j == pl.num_programs(1) - 1)
    def _():
        y = x_ref[...] + acc_ref[...]
        if final_norm:
            y = _rms(y, gf_ref[...])
        o_ref[...] = y


def ffn_residual(x, g, w1, w3, w2, g_final=None, *, tm=512, tf=512):
    T, D = x.shape
    Fh = w1.shape[1]
    assert T % tm == 0 and Fh % tf == 0
    final_norm = g_final is not None
    gf = (g_final if final_norm else g).reshape(1, D)
    return pl.pallas_call(
        functools.partial(_ffn_kernel, final_norm=final_norm),
        grid=(T // tm, Fh // tf),
        in_specs=[
            pl.BlockSpec((tm, D), lambda i, j: (i, 0)),
            pl.BlockSpec((1, D), lambda i, j: (0, 0)),
            pl.BlockSpec((D, tf), lambda i, j: (0, j)),
            pl.BlockSpec((D, tf), lambda i, j: (0, j)),
            pl.BlockSpec((tf, D), lambda i, j: (j, 0)),
            pl.BlockSpec((1, D), lambda i, j: (0, 0)),
        ],
        out_specs=pl.BlockSpec((tm, D), lambda i, j: (i, 0)),
        out_shape=jax.ShapeDtypeStruct((T, D), F32),
        scratch_shapes=[pltpu.VMEM((tm, D), BF16), pltpu.VMEM((tm, D), F32)],
        compiler_params=_params("parallel", "arbitrary"),
        name="ffn_residual",
    )(x, g.reshape(1, D), w1, w3, w2, gf)


NEG_BIG = -1e30
_NT = (((1,), (1,)), ((), ()))


def _flash_steps(qas, kas, vas, masks, m_prev, acc_prev):
    hs = range(len(qas))
    s = [lax.dot_general(qas[h], kas[h], _NT, preferred_element_type=F32) for h in hs]
    s = [s[h] if masks[h] is None else jnp.where(masks[h], s[h], NEG_BIG) for h in hs]
    m_new = [jnp.maximum(m_prev[h], jnp.max(s[h], axis=1, keepdims=True)) for h in hs]
    alpha = [jnp.exp2(m_prev[h] - m_new[h]) for h in hs]
    p = [jnp.exp2(s[h] - m_new[h]) for h in hs]
    pv = [jnp.dot(p[h].astype(BF16), vas[h], preferred_element_type=F32) for h in hs]
    return m_new, [alpha[h] * acc_prev[h] + pv[h] for h in hs]


def _augment_q(q_log2, in_head, keep_t, odd):
    nblk, tq = keep_t.shape
    bias_t = jnp.where(keep_t, 0.0, NEG_BIG)
    bias = jnp.concatenate([bias_t, jnp.zeros((V7X_LANES - nblk, tq), F32)], axis=0).T
    if not odd:
        bias = pltpu.roll(bias, HEAD_DIM, axis=1)
    return jnp.where(in_head, q_log2, bias).astype(BF16)


def _key_value_tiles(k, v, blk, lane):
    low = lane < HEAD_DIM
    hot_e = 0.0 if blk is None else jnp.where(lane - HEAD_DIM == blk, 1.0, 0.0)
    hot_o = 0.0 if blk is None else jnp.where(lane == blk, 1.0, 0.0)
    k_e, k_o = jnp.where(low, k, hot_e), jnp.where(low, hot_o, k)
    v_e, v_o = jnp.where(low, v, jnp.where(lane == HEAD_DIM, 1.0, 0.0)), jnp.where(low, jnp.where(lane == 0, 1.0, 0.0), v)
    return [t.astype(BF16) for t in (k_e, k_o, v_e, v_o)]


def _normalise(acc, lane, odd):
    return acc / jnp.sum(jnp.where(lane == (0 if odd else HEAD_DIM), acc, 0.0), axis=1, keepdims=True)


LOG2E = 1.4426950408889634


MOBA_HEADS_PER_STEP = 4


def _split_bf16(x):
    hi = x.astype(BF16)
    return hi, (x - hi.astype(F32)).astype(BF16)


def _moba_kernel(q_ref, k_ref, v_ref, o_ref, ka_ref, va_ref, km_ref, acc_ref):
    L = MOBA_BLOCK
    S = k_ref.shape[1]
    nb = S // L
    HP = MOBA_HEADS_PER_STEP
    qi = pl.program_id(2)
    lane = lax.broadcasted_iota(jnp.int32, (L, V7X_LANES), 1)
    lanes_of = lambda ref, pp: ref[0, :, pp * V7X_LANES:(pp + 1) * V7X_LANES]

    @pl.when(qi == 0)
    def _():
        lane_s = lax.broadcasted_iota(jnp.int32, (S, V7X_LANES), 1)
        blk = lax.broadcasted_iota(jnp.int32, (S, V7X_LANES), 0) // L
        for pp in range(HP // 2):
            k = lanes_of(k_ref, pp)
            (ka_ref[2 * pp], ka_ref[2 * pp + 1], va_ref[2 * pp], va_ref[2 * pp + 1]) = _key_value_tiles(
                k, lanes_of(v_ref, pp), blk, lane_s)
            km_ref[pp] = jnp.concatenate(_split_bf16(jnp.mean(k.reshape(nb, L, V7X_LANES), axis=1)), axis=0)

    row = lax.broadcasted_iota(jnp.int32, (L, L), 0)
    col = lax.broadcasted_iota(jnp.int32, (L, L), 1)
    causal = col <= row
    jrow = lax.broadcasted_iota(jnp.int32, (nb, L), 0)
    past = jrow < qi
    qas = []
    for pp in range(HP // 2):
        q = lanes_of(q_ref, pp) * (HEAD_DIM ** -0.5)
        q_hi, q_lo = _split_bf16(q)
        km = km_ref[pp]
        for e in range(2):
            in_head = (lane >= e * HEAD_DIM) & (lane < (e + 1) * HEAD_DIM)
            zero = jnp.zeros_like(q_hi)
            g1 = lax.dot_general(km, jnp.where(in_head, q_hi, zero), _NT, preferred_element_type=F32)
            g2 = lax.dot_general(km, jnp.where(in_head, q_lo, zero), _NT, preferred_element_type=F32)
            gate = g1[0:nb] + g1[nb:2 * nb] + g2[0:nb]
            keep = jrow == qi
            for n in range(nb):
                g_n = gate[n:n + 1, :]
                beats = (gate > g_n) | ((gate == g_n) & (jrow < n))
                rank = jnp.sum(jnp.where(past & beats, 1.0, 0.0), axis=0, keepdims=True)
                keep = keep | ((jrow == n) & (rank < MOBA_TOPK) & past)
            qas.append(_augment_q(q * LOG2E, in_head, keep, odd=e == 1))

    tiles = lambda start: ([ka_ref[h, pl.ds(start, L), :] for h in range(HP)],
                           [va_ref[h, pl.ds(start, L), :] for h in range(HP)])

    m, acc = _flash_steps(qas, *tiles(pl.multiple_of(qi * L, L)), [causal] * HP,
                          [jnp.full((L, 1), NEG_BIG, F32)] * HP, [jnp.zeros((L, V7X_LANES), F32)] * HP)
    for h in range(HP):
        acc_ref[h] = acc[h]

    def body(n, carry):
        m2, acc2 = _flash_steps(qas, *tiles(pl.multiple_of(n * L, L)), [None] * HP, list(carry),
                                [acc_ref[h] for h in range(HP)])
        for h in range(HP):
            acc_ref[h] = acc2[h]
        return tuple(m2)

    lax.fori_loop(0, qi, body, tuple(m))
    for pp in range(HP // 2):
        o_ref[0, :, pp * V7X_LANES:(pp + 1) * V7X_LANES] = jnp.where(
            lane < HEAD_DIM, _normalise(acc_ref[2 * pp], lane, False), _normalise(acc_ref[2 * pp + 1], lane, True))


def moba_attention(z, *, n_heads=16):
    B, S, _ = z.shape
    L, HP = MOBA_BLOCK, MOBA_HEADS_PER_STEP
    W = HP * HEAD_DIM
    nb = S // L
    assert S % L == 0 and n_heads % HP == 0 and W % V7X_LANES == 0 and 2 * nb <= 16
    ngrp = n_heads // HP
    return pl.pallas_call(
        _moba_kernel,
        grid=(B, ngrp, S // L),
        in_specs=[
            pl.BlockSpec((1, L, W), lambda b, p, i: (b, i, p)),
            pl.BlockSpec((1, S, W), lambda b, p, i: (b, 0, ngrp + p)),
            pl.BlockSpec((1, S, W), lambda b, p, i: (b, 0, 2 * ngrp + p)),
        ],
        out_specs=pl.BlockSpec((1, L, W), lambda b, p, i: (b, i, p)),
        out_shape=jax.ShapeDtypeStruct((B, S, n_heads * HEAD_DIM), F32),
        scratch_shapes=[
            pltpu.VMEM((HP, S, V7X_LANES), BF16),
            pltpu.VMEM((HP, S, V7X_LANES), BF16),
            pltpu.VMEM((HP // 2, 2 * nb, V7X_LANES), BF16),
            pltpu.VMEM((HP, L, V7X_LANES), F32),
        ],
        compiler_params=_params("parallel", "parallel", "arbitrary"),
        name="moba_attention",
    )(z, z, z)


NSA_TQ = 256
BIG = 3.0e38


def _gelu_tanh(x):
    return 0.5 * x * (1.0 + jnp.tanh(0.7978845608028654 * (x + 0.044715 * x * x * x)))


def _nsa_compress_kernel(xk0_ref, xk1_ref, xv0_ref, xv1_ref, pek_ref, pev_ref, w1k_ref, w1v_ref, w2k_ref, w2v_ref,
                         o1_ref, o2_ref):
    G, Lc, st = NSA_KV_GROUPS, NSA_CMP_BLOCK, NSA_CMP_STRIDE
    nrow = xk0_ref.shape[1] // st
    lane = lax.broadcasted_iota(jnp.int32, (nrow, G * HEAD_DIM), 1)

    def hidden(x_refs, pe_ref, w1_ref):
        acc = [jnp.zeros((G * nrow, V7X_LANES), F32) for _ in range(Lc // st)]
        for l in range(Lc):
            u, m = divmod(l, st)
            x = jnp.concatenate([r[0, pl.ds(m, nrow, stride=st), :] for r in x_refs], axis=1) + pe_ref[l:l + 1, :]
            xs = jnp.concatenate(
                [jnp.where((lane >= g * HEAD_DIM) & (lane < (g + 1) * HEAD_DIM), x, 0.0) for g in range(G)],
                axis=0).astype(BF16)
            acc[u] = acc[u] + jnp.dot(xs, w1_ref[l], preferred_element_type=F32)
        nxt = jnp.concatenate([pltpu.roll(acc[1][g * nrow:(g + 1) * nrow], nrow - 1, axis=0) for g in range(G)],
                              axis=0)
        return _gelu_tanh(acc[0] + nxt).astype(BF16)

    hk = hidden((xk0_ref, xk1_ref), pek_ref, w1k_ref)
    hv = hidden((xv0_ref, xv1_ref), pev_ref, w1v_ref)
    kc = jnp.dot(hk, w2k_ref[...], preferred_element_type=F32)
    vc = jnp.dot(hv, w2v_ref[...], preferred_element_type=F32)
    kv = kc + vc
    vk = pltpu.roll(kv, HEAD_DIM, axis=1)
    for g in range(G):
        o1_ref[0, :, g * V7X_LANES:(g + 1) * V7X_LANES] = kv[g * nrow:(g + 1) * nrow]
        o2_ref[0, :, g * V7X_LANES:(g + 1) * V7X_LANES] = vk[g * nrow:(g + 1) * nrow]


def nsa_compress(z, col_k, col_v, pe_k, w1_k, w2_k, pe_v, w1_v, w2_v):
    B, S, _ = z.shape
    G, Lc, st = NSA_KV_GROUPS, NSA_CMP_BLOCK, NSA_CMP_STRIDE
    GW = G * HEAD_DIM
    nrow = S // st
    hid = w1_k.shape[1]
    assert hid == V7X_LANES and col_k % GW == 0 and col_v % GW == 0
    tile_pe = lambda pe: jnp.tile(pe, (1, G))
    tile_w1 = lambda w: jnp.tile(w.reshape(Lc, 1, HEAD_DIM, hid), (1, G, 1, 1)).reshape(Lc, GW, hid).astype(BF16)
    w2k = jnp.pad(w2_k, ((0, 0), (0, HEAD_DIM))).astype(BF16)
    w2v = jnp.pad(w2_v, ((0, 0), (HEAD_DIM, 0))).astype(BF16)
    const = lambda shape: pl.BlockSpec(shape, lambda b: (0,) * len(shape))
    out = jax.ShapeDtypeStruct((B, nrow, G * V7X_LANES), F32)
    return pl.pallas_call(
        _nsa_compress_kernel,
        grid=(B,),
        in_specs=[
            pl.BlockSpec((1, S, V7X_LANES), lambda b: (b, 0, col_k // V7X_LANES)),
            pl.BlockSpec((1, S, V7X_LANES), lambda b: (b, 0, col_k // V7X_LANES + 1)),
            pl.BlockSpec((1, S, V7X_LANES), lambda b: (b, 0, col_v // V7X_LANES)),
            pl.BlockSpec((1, S, V7X_LANES), lambda b: (b, 0, col_v // V7X_LANES + 1)),
            const((Lc, GW)), const((Lc, GW)),
            const((Lc, GW, hid)), const((Lc, GW, hid)),
            const((hid, V7X_LANES)), const((hid, V7X_LANES)),
        ],
        out_specs=[pl.BlockSpec((1, nrow, G * V7X_LANES), lambda b: (b, 0, 0))] * 2,
        out_shape=[out, out],
        compiler_params=_params("parallel"),
        name="nsa_compress",
    )(z, z, z, z, tile_pe(pe_k), tile_pe(pe_v), tile_w1(w1_k), tile_w1(w1_v), w2k, w2v)


def _nsa_kernel(q_ref, c1_ref, c2_ref, s_ref, w_ref, g_ref, ovt_ref, o_ref, sk_ref, sv_ref, wk_ref, wv_ref, acc_ref):
    TQ = NSA_TQ
    S = s_ref.shape[1]
    R = NSA_HEADS // NSA_KV_GROUPS
    grp = pl.program_id(1)
    qi = pl.program_id(2)
    lane = lax.broadcasted_iota(jnp.int32, (TQ, V7X_LANES), 1)

    @pl.when(qi == 0)
    def _():
        lane_s = lax.broadcasted_iota(jnp.int32, (S, V7X_LANES), 1)
        blk = lax.broadcasted_iota(jnp.int32, (S, V7X_LANES), 0) // NSA_SLC_BLOCK
        kv = s_ref[0]
        vk = pltpu.roll(kv, HEAD_DIM, axis=1)
        sk_ref[0], sk_ref[1], sv_ref[0], sv_ref[1] = _key_value_tiles(
            jnp.where(lane_s < HEAD_DIM, kv, vk), jnp.where(lane_s < HEAD_DIM, vk, kv), blk, lane_s)
        kv = w_ref[0]
        vk = pltpu.roll(kv, HEAD_DIM, axis=1)
        wk_ref[0], wk_ref[1], wv_ref[0], wv_ref[1] = _key_value_tiles(
            jnp.where(lane_s < HEAD_DIM, kv, vk), jnp.where(lane_s < HEAD_DIM, vk, kv), None, lane_s)

    q0 = pl.multiple_of(qi * TQ, TQ)
    row = lax.broadcasted_iota(jnp.int32, (TQ, TQ), 0)
    col = lax.broadcasted_iota(jnp.int32, (TQ, TQ), 1)
    causal = col <= row
    qi_mat = jnp.zeros((TQ, TQ), jnp.int32) + qi
    t_abs = q0 + lax.broadcasted_iota(jnp.int32, (TQ, V7X_LANES), 0)
    even_lanes = lane < HEAD_DIM

    c_kv, c_vk = c1_ref[0], c2_ref[0]
    c_kv_b, c_vk_b = c_kv.astype(BF16), c_vk.astype(BF16)
    cmask = lane * NSA_CMP_STRIDE + (NSA_CMP_BLOCK - 1) <= t_abs
    tiles, o_cmp = [], []
    p_sum = jnp.zeros((TQ, V7X_LANES), F32)
    for r in range(R):
        tiles.append(q_ref[0, :, (r // 2) * V7X_LANES:(r // 2 + 1) * V7X_LANES] * (HEAD_DIM ** -0.5))
        qm = jnp.where(even_lanes if r % 2 == 0 else ~even_lanes, tiles[r], 0.0).astype(BF16)
        s = lax.dot_general(qm, c_kv_b if r % 2 == 0 else c_vk_b, _NT, preferred_element_type=F32)
        s = jnp.where(cmask, s, NEG_BIG)
        p = jnp.where(cmask, jnp.exp(s - jnp.max(s, axis=1, keepdims=True)), 0.0)
        den = jnp.sum(p, axis=1, keepdims=True)
        p = p / jnp.where(den > 0.0, den, 1.0)
        p_sum = p_sum + p
        o_cmp.append(jnp.dot(p.astype(BF16), c_vk_b if r % 2 == 0 else c_kv_b, preferred_element_type=F32))

    nblk = s_ref.shape[1] // NSA_SLC_BLOCK
    p_hi, p_lo = _split_bf16(p_sum)
    ovt = ovt_ref[...]
    p_slc = (lax.dot_general(ovt, p_hi, _NT, preferred_element_type=F32)
             + lax.dot_general(ovt, p_lo, _NT, preferred_element_type=F32))[0:nblk]
    jrow = lax.broadcasted_iota(jnp.int32, (nblk, TQ), 0)
    own = (q0 + lax.broadcasted_iota(jnp.int32, (nblk, TQ), 1)) // NSA_SLC_BLOCK
    score = jnp.where((jrow == own) | (jrow == 0), BIG, jnp.where(jrow > own, -BIG, p_slc))
    keep = jrow > nblk
    for j in range(nblk):
        s_j = score[j:j + 1, :]
        beats = (score > s_j) | ((score == s_j) & (jrow < j))
        rank = jnp.sum(jnp.where(beats, 1.0, 0.0), axis=0, keepdims=True)
        keep = keep | ((jrow == j) & (rank < NSA_SLC_TOPN) & (jrow <= own))
    qas = [_augment_q(tiles[r] * LOG2E, even_lanes if r % 2 == 0 else ~even_lanes, keep, odd=r % 2 == 1)
           for r in range(R)]

    neg = [jnp.full((TQ, 1), NEG_BIG, F32)] * R
    zacc = [jnp.zeros((TQ, V7X_LANES), F32)] * R

    def kv_blocks(k_ref, v_ref, start):
        return ([k_ref[r % 2, pl.ds(start, TQ), :] for r in range(R)],
                [v_ref[r % 2, pl.ds(start, TQ), :] for r in range(R)])

    m, acc = _flash_steps(qas, *kv_blocks(sk_ref, sv_ref, q0), [causal] * R, neg, zacc)
    for r in range(R):
        acc_ref[r] = acc[r]

    def body(kb, carry):
        m2, acc2 = _flash_steps(qas, *kv_blocks(sk_ref, sv_ref, pl.multiple_of(kb * TQ, TQ)), [None] * R,
                                list(carry), [acc_ref[r] for r in range(R)])
        for r in range(R):
            acc_ref[r] = acc2[r]
        return tuple(m2)

    lax.fori_loop(0, qi, body, tuple(m))

    m, acc = _flash_steps(qas, *kv_blocks(wk_ref, wv_ref, q0), [causal] * R, neg, zacc)
    m, acc = _flash_steps(qas, *kv_blocks(wk_ref, wv_ref, pl.multiple_of(jnp.maximum(qi - 1, 0) * TQ, TQ)),
                          [qi_mat >= 1] * R, m, acc)
    m, acc = _flash_steps(qas, *kv_blocks(wk_ref, wv_ref, pl.multiple_of(jnp.maximum(qi - 2, 0) * TQ, TQ)),
                          [(col > row) & (qi_mat >= 2)] * R, m, acc)

    gates = jax.nn.sigmoid(g_ref[0])
    outs = []
    for r in range(R):
        o_slc = _normalise(acc_ref[r], lane, r % 2 == 1)
        o_win = _normalise(acc[r], lane, r % 2 == 1)
        c0 = (grp * R + r) * 3
        gate = lambda c: jnp.sum(jnp.where(lane == c, gates, 0.0), axis=1, keepdims=True)
        outs.append(gate(c0) * o_cmp[r] + gate(c0 + 1) * o_slc + gate(c0 + 2) * o_win)
    for p2 in range(R // 2):
        o_ref[0, :, p2 * V7X_LANES:(p2 + 1) * V7X_LANES] = jnp.where(even_lanes, outs[2 * p2], outs[2 * p2 + 1])


def nsa_attention(z, cmp_kv, cmp_vk, col_q, col_slc, col_win, col_gate):
    B, S, _ = z.shape
    G, TQ = NSA_KV_GROUPS, NSA_TQ
    R = NSA_HEADS // G
    QW = R * HEAD_DIM
    ncmp = cmp_kv.shape[1]
    assert S % TQ == 0 and ncmp == V7X_LANES and S // NSA_SLC_BLOCK <= V7X_LANES and NSA_WINDOW == 2 * TQ
    assert col_q % QW == 0 and col_slc % V7X_LANES == 0 and col_win % V7X_LANES == 0 and col_gate % V7X_LANES == 0
    nc = (S - NSA_CMP_BLOCK) // NSA_CMP_STRIDE + 1
    c_start = np.arange(V7X_LANES) * NSA_CMP_STRIDE
    s_start = np.arange(V7X_LANES) * NSA_SLC_BLOCK
    overlap = ((c_start[:, None] <= s_start[None, :] + NSA_SLC_BLOCK - 1)
               & (c_start[:, None] + NSA_CMP_BLOCK - 1 >= s_start[None, :])
               & (np.arange(V7X_LANES)[:, None] < nc) & (np.arange(V7X_LANES)[None, :] < S // NSA_SLC_BLOCK))
    const = lambda shape: pl.BlockSpec(shape, lambda b, g, i: (0,) * len(shape))
    return pl.pallas_call(
        _nsa_kernel,
        grid=(B, G, S // TQ),
        in_specs=[
            pl.BlockSpec((1, TQ, QW), lambda b, g, i: (b, i, col_q // QW + g)),
            pl.BlockSpec((1, ncmp, V7X_LANES), lambda b, g, i: (b, 0, g)),
            pl.BlockSpec((1, ncmp, V7X_LANES), lambda b, g, i: (b, 0, g)),
            pl.BlockSpec((1, S, V7X_LANES), lambda b, g, i: (b, 0, col_slc // V7X_LANES + g)),
            pl.BlockSpec((1, S, V7X_LANES), lambda b, g, i: (b, 0, col_win // V7X_LANES + g)),
            pl.BlockSpec((1, TQ, V7X_LANES), lambda b, g, i: (b, i, col_gate // V7X_LANES)),
            const((V7X_LANES, V7X_LANES)),
        ],
        out_specs=pl.BlockSpec((1, TQ, QW), lambda b, g, i: (b, i, g)),
        out_shape=jax.ShapeDtypeStruct((B, S, NSA_HEADS * HEAD_DIM), F32),
        scratch_shapes=[pltpu.VMEM((2, S, V7X_LANES), BF16)] * 4 + [
            pltpu.VMEM((R, TQ, V7X_LANES), F32),
        ],
        compiler_params=_params("parallel", "parallel", "arbitrary"),
        name="nsa_attention",
    )(z, cmp_kv, cmp_vk, z, z, z, jnp.asarray(overlap.T, BF16))


RWKV_CHUNK = 64
RWKV_ROWS = 256
RWKV_INTERLEAVE = 4


def _mm(a, b, dims=None):
    dims = dims or (((1,), (0,)), ((), ()))
    return lax.dot_general(a.astype(BF16), b.astype(BF16), dims, preferred_element_type=F32)


def _mm3(a, b):
    (a_hi, a_lo), (b_hi, b_lo) = _split_bf16(a), _split_bf16(b)
    return _mm(a_hi, b_hi) + (_mm(a_hi, b_lo) + _mm(a_lo, b_hi))


def _mm_onehot(a01, b):
    hi = b.astype(BF16)
    mid, lo = _split_bf16(b - hi.astype(F32))
    return _mm(a01, hi) + (_mm(a01, mid) + _mm(a01, lo))


def _head_sum(x, low):
    s0 = jnp.sum(jnp.where(low, x, 0.0), axis=1, keepdims=True)
    s1 = jnp.sum(jnp.where(low, 0.0, x), axis=1, keepdims=True)
    return jnp.where(low, s0, s1)


def _rwkv_kernel(r_ref, k_ref, v_ref, lo_ref, glo_ref, pp_ref, pl_ref, wup_ref, aup_ref, gup_ref, o_ref,
                 rs, ws, ks, vs, als, bes, gs, ys, hs, rqs, ms, ns):
    S = r_ref.shape[1]
    C, RB = RWKV_CHUNK, RWKV_ROWS
    pp = pp_ref[...]
    mu_r, mu_k, mu_v, w0, a0, k_k, k_a, r_k, ln_g, ln_b = [pp[i:i + 1, :] for i in range(10)]
    mu_lo, mu_g = pl_ref[0:1, :], pl_ref[1:2, :]
    low = lax.broadcasted_iota(jnp.int32, (RB, V7X_LANES), 1) < HEAD_DIM
    first = lax.broadcasted_iota(jnp.int32, (RB, V7X_LANES), 0) == 0

    def prologue(i, c):
        t0 = pl.multiple_of(i * RB, RB)
        tp = jnp.maximum(t0 - 1, 0)
        keep = jnp.where(i > 0, 1.0, 0.0)

        def shifted(ref, mu):
            x = ref[0, pl.ds(t0, RB), :]
            prev = jnp.where(first, ref[0, pl.ds(tp, 1), :] * keep, pltpu.roll(x, 1, axis=0))
            return x + (prev - x) * mu

        r, k, v = shifted(r_ref, mu_r), shifted(k_ref, mu_k), shifted(v_ref, mu_v)
        lo, glo = shifted(lo_ref, mu_lo), shifted(glo_ref, mu_g)
        wp = -(w0 + _mm(jnp.tanh(lo), wup_ref[...]))
        w = -(jnp.maximum(wp, 0.0) + jnp.log(1.0 + jnp.exp(-jnp.abs(wp)))) - 0.5
        a = jax.nn.sigmoid(a0 + _mm(lo, aup_ref[...]))
        kk = k * k_k
        kk = kk / jnp.maximum(jnp.sqrt(_head_sum(kk * kk, low)), 1e-12)
        k2 = k * (1.0 + (a - 1.0) * k_a)
        rs[pl.ds(t0, RB), :] = r
        ws[pl.ds(t0, RB), :] = -jnp.exp(w)
        ks[pl.ds(t0, RB), :] = k2
        vs[pl.ds(t0, RB), :] = v
        als[pl.ds(t0, RB), :] = -kk
        bes[pl.ds(t0, RB), :] = kk * a
        gs[pl.ds(t0, RB), :] = _mm(jax.nn.sigmoid(glo), gup_ref[...])
        o_ref[0, pl.ds(t0, RB), :] = _head_sum(r * k2 * r_k, low) * v
        return c

    lax.fori_loop(0, S // RB, prologue, 0)

    W2 = 2 * C
    row = lax.broadcasted_iota(jnp.int32, (W2, W2), 0)
    col = lax.broadcasted_iota(jnp.int32, (W2, W2), 1)
    t_idx, s_idx = row % C, col % C
    top, left = row < C, col < C
    same = top == left
    eye = jnp.where(row == col, 1.0, 0.0)
    tri = jnp.where(lax.broadcasted_iota(jnp.int32, (C, C), 1) <= lax.broadcasted_iota(jnp.int32, (C, C), 0), 1.0, 0.0)
    low_c = lax.broadcasted_iota(jnp.int32, (C, V7X_LANES), 1) < HEAD_DIM
    fold = lambda x: x[0:C] + x[C:W2]
    stack_heads = lambda x: jnp.concatenate([jnp.where(low_c, x, 0.0), jnp.where(low_c, 0.0, x)], axis=0)
    block_diag = lambda x: jnp.where(top, jnp.where(left, x, 0.0), jnp.where(left, 0.0, pltpu.roll(x, C, axis=1)))

    rows = lambda c: pl.ds(c * C if isinstance(c, int) else pl.multiple_of(c * C, C), C)

    def advance(c, H):
        ys[rows(c), :] += _mm3(rqs[c], H)
        return _mm3(ms[c], H) + ns[c]

    def transfers(i, lagged):
        each = lambda f, *xs: [f(*a) for a in zip(*xs)]
        cs = [i * RWKV_INTERLEAVE + u for u in range(RWKV_INTERLEAVE)]
        sls = [rows(c) for c in cs]
        state = [hs[...]] if lagged else None

        def lag(u):
            if lagged:
                state[0] = advance(cs[u] - RWKV_INTERLEAVE, state[0])

        r, lw, k2, v, al, be = ([ref[sl, :] for sl in sls] for ref in (rs, ws, ks, vs, als, bes))
        logp = each(lambda x: _mm_onehot(tri, x), lw)
        lag(0)
        P = each(jnp.exp, logp)
        Pinv = each(lambda x: jnp.exp(-x), logp)
        At = each(lambda a_, lp, w_: a_ * jnp.exp(lp - w_), al, logp, lw)
        Rt, Bt, Kt = each(jnp.multiply, r, P), each(jnp.multiply, be, Pinv), each(jnp.multiply, k2, Pinv)
        PC = each(lambda p: p[C - 1:C, :], P)
        A_bd, R_bd = each(stack_heads, At), each(stack_heads, Rt)
        Yt = each(lambda b, k: jnp.concatenate([b, k], axis=0), Bt, Kt)
        A1 = each(lambda a, y: jnp.where(s_idx < t_idx, _mm(a, y, dims=_NT), 0.0), A_bd, Yt)
        A2 = each(lambda a, y: jnp.where(s_idx <= t_idx, _mm(a, y, dims=_NT), 0.0), R_bd, Yt)
        X, Arb = each(block_diag, A1), each(block_diag, A2)
        T = each(lambda x: eye + x, X)
        for it in range(5):
            X = each(lambda x: _mm(x, x), X)
            T = each(lambda t, x: t + _mm(t, x), T, X)
            if it % 2 == 0:
                lag(1 + it // 2)
        V0 = each(lambda x: jnp.concatenate([jnp.zeros_like(x), x], axis=0), v)
        TA = each(_mm, T, A_bd)
        AkV = each(lambda a, x: jnp.where(same, _mm(a, x), 0.0), A1, V0)
        U0 = each(_mm, T, AkV)
        AR = each(lambda a, t, u: _mm(a, jnp.concatenate([t, u], axis=1)), Arb, TA, U0)
        ArkV = each(lambda a, x: jnp.where(same, _mm(a, x), 0.0), A2, V0)
        Mx = each(lambda b, p, t: _mm((b * p).T, fold(t)), Bt, PC, TA)
        Nx = each(lambda b, k, p, u, x: _mm(jnp.concatenate([b * p, k * p], axis=0).T,
                                            jnp.concatenate([fold(u), x], axis=0)), Bt, Kt, PC, U0, v)
        for u in range(RWKV_INTERLEAVE):
            ys[sls[u], :] = fold(AR[u][:, W2:2 * W2] + ArkV[u])
            rqs[cs[u]] = Rt[u] + fold(AR[u][:, 0:W2])
            ms[cs[u]] = eye * PC[u] + jnp.where(same, Mx[u], 0.0)
            ns[cs[u]] = jnp.where(same, Nx[u], 0.0)
        if lagged:
            hs[...] = state[0]

    def pipelined(i, carry):
        transfers(i, True)
        return carry

    def drain(c, carry):
        hs[...] = advance(c, hs[...])
        return carry

    assert RWKV_INTERLEAVE == 4
    hs[...] = jnp.zeros((W2, W2), F32)
    transfers(0, False)
    lax.fori_loop(1, S // C // RWKV_INTERLEAVE, pipelined, 0)
    lax.fori_loop(S // C - RWKV_INTERLEAVE, S // C, drain, 0)

    def epilogue(i, c):
        sl = pl.ds(pl.multiple_of(i * RB, RB), RB)
        y = ys[sl, :]
        d = y - _head_sum(y, low) * (1.0 / HEAD_DIM)
        var = _head_sum(d * d, low) * (1.0 / HEAD_DIM)
        yn = d * lax.rsqrt(var + RWKV_GN_EPS) * ln_g + ln_b
        o_ref[0, sl, :] = (yn + o_ref[0, sl, :]) * gs[sl, :]
        return c

    lax.fori_loop(0, S // RB, epilogue, 0)


def rwkv7_mixer(z, shift_mu, w0, w_up, a0, a_up, g_up, k_k, k_a, r_k, ln_g, ln_b):
    B, S, _ = z.shape
    CW = RWKV_HEADS * HEAD_DIM
    npair = CW // V7X_LANES
    base = 3 * CW // V7X_LANES
    lora = w_up.shape[0] + a_up.shape[0]
    assert lora == V7X_LANES and g_up.shape[0] == V7X_LANES and S % RWKV_ROWS == 0
    pp = jnp.stack([shift_mu[0:CW], shift_mu[CW:2 * CW], shift_mu[2 * CW:3 * CW], w0, a0, k_k, k_a,
                    r_k.reshape(CW), ln_g, ln_b])
    pp = jnp.pad(pp, ((0, 16 - pp.shape[0]), (0, 0)))
    pl2 = jnp.pad(shift_mu[3 * CW:].reshape(2, V7X_LANES), ((0, 6), (0, 0)))
    wup = jnp.pad(w_up, ((0, a_up.shape[0]), (0, 0)))
    aup = jnp.pad(a_up, ((w_up.shape[0], 0), (0, 0)))
    tile = lambda off: pl.BlockSpec((1, S, V7X_LANES), lambda b, p: (b, 0, base + off * npair + p))
    fixed = lambda off: pl.BlockSpec((1, S, V7X_LANES), lambda b, p: (b, 0, base + 3 * npair + off))
    seq = pltpu.VMEM((S, V7X_LANES), F32)
    return pl.pallas_call(
        _rwkv_kernel,
        grid=(B, npair),
        in_specs=[
            tile(0), tile(1), tile(2), fixed(0), fixed(1),
            pl.BlockSpec((16, V7X_LANES), lambda b, p: (0, p)),
            pl.BlockSpec((8, V7X_LANES), lambda b, p: (0, 0)),
            pl.BlockSpec((V7X_LANES, V7X_LANES), lambda b, p: (0, p)),
            pl.BlockSpec((V7X_LANES, V7X_LANES), lambda b, p: (0, p)),
            pl.BlockSpec((V7X_LANES, V7X_LANES), lambda b, p: (0, p)),
        ],
        out_specs=pl.BlockSpec((1, S, V7X_LANES), lambda b, p: (b, 0, p)),
        out_shape=jax.ShapeDtypeStruct((B, S, CW), F32),
        scratch_shapes=[seq] * 8 + [
            pltpu.VMEM((V7X_LANES, V7X_LANES), F32),
            pltpu.VMEM((S // RWKV_CHUNK, RWKV_CHUNK, V7X_LANES), F32),
            pltpu.VMEM((S // RWKV_CHUNK, V7X_LANES, V7X_LANES), F32),
            pltpu.VMEM((S // RWKV_CHUNK, V7X_LANES, V7X_LANES), F32),
        ],
        compiler_params=_params("parallel", "parallel"),
        name="rwkv7_mixer",
    )(z, z, z, z, z, pp, pl2, wup, aup, g_up.astype(BF16))


def _ret_kernel(q_ref, k_ref, v_ref, g_ref, cos_ref, sin_ref, din_ref, dq_ref, dk_ref, dc_ref, o_ref, st_ref):
    S = q_ref.shape[1]
    C, DV = RET_CHUNK, RET_V_DIM
    lane = lax.broadcasted_iota(jnp.int32, (C, V7X_LANES), 1)
    first_half = (lane % RET_QK_DIM) < RET_QK_DIM // 2
    st_ref[...] = jnp.zeros_like(st_ref)

    def chunk(c, carry):
        sl = pl.ds(pl.multiple_of(c * C, C), C)
        cos, sin = cos_ref[sl, :], sin_ref[sl, :]

        def rot(z):
            swapped = jnp.where(first_half, pltpu.roll(z, V7X_LANES - RET_QK_DIM // 2, axis=1),
                                pltpu.roll(z, RET_QK_DIM // 2, axis=1))
            return z * cos + swapped * sin

        q = rot(q_ref[0, sl, :])
        k = rot(k_ref[0, sl, :]) * (RET_QK_DIM ** -0.5)
        for h in range(2):
            in_head = (lane >= h * RET_QK_DIM) & (lane < (h + 1) * RET_QK_DIM)
            qm, km = jnp.where(in_head, q, 0.0), jnp.where(in_head, k, 0.0)
            v = v_ref[0, sl, h * DV:(h + 1) * DV]
            st = st_ref[h]
            inner = _mm(qm, k, dims=_NT) * din_ref[h]
            o = _mm(inner, v) + _mm(qm, st) * dq_ref[h]
            st_ref[h] = _mm((km * dk_ref[h]).T, v) + dc_ref[h, 0:1, :] * st
            d = o - jnp.mean(o, axis=1, keepdims=True)
            on = d * lax.rsqrt(jnp.mean(d * d, axis=1, keepdims=True) + RET_GN_EPS)
            gate = g_ref[0, sl, h * DV:(h + 1) * DV]
            o_ref[0, sl, h * DV:(h + 1) * DV] = gate * jax.nn.sigmoid(gate) * on
        return carry

    lax.fori_loop(0, S // C, chunk, 0)


def retention_mixer(z):
    B, S, _ = z.shape
    H, C, DK, DV = RET_HEADS, RET_CHUNK, RET_QK_DIM, RET_V_DIM
    assert S % C == 0 and 2 * DK == V7X_LANES and DV == V7X_LANES
    npair = H // 2
    half = DK // 2
    inv = ROPE_BASE ** (-jnp.arange(half, dtype=F32) / half)
    ang = jnp.arange(S, dtype=F32)[:, None] * inv
    cos = jnp.tile(jnp.cos(ang), (1, 4))
    sin = jnp.tile(jnp.concatenate([-jnp.sin(ang), jnp.sin(ang)], axis=1), (1, 2))
    log_g = jnp.asarray(np.log(1.0 - 2.0 ** (-5.0 - np.arange(H))), F32)
    n = jnp.arange(C, dtype=F32)
    diff = n[:, None] - n[None, :]
    d_in = jnp.where(diff >= 0, jnp.exp(jnp.maximum(diff, 0.0) * log_g[:, None, None]), 0.0)
    lanes = lambda t: jnp.broadcast_to(t[..., None], t.shape + (V7X_LANES,))
    d_q = lanes(jnp.exp((n + 1.0) * log_g[:, None]))
    d_k = lanes(jnp.exp((C - 1.0 - n) * log_g[:, None]))
    d_c = lanes(jnp.broadcast_to(jnp.exp(C * log_g)[:, None], (H, 8)))
    qk_tiles = H * DK // V7X_LANES
    return pl.pallas_call(
        _ret_kernel,
        grid=(B, npair),
        in_specs=[
            pl.BlockSpec((1, S, V7X_LANES), lambda b, p: (b, 0, p)),
            pl.BlockSpec((1, S, V7X_LANES), lambda b, p: (b, 0, qk_tiles + p)),
            pl.BlockSpec((1, S, 2 * DV), lambda b, p: (b, 0, 2 * qk_tiles * V7X_LANES // (2 * DV) + p)),
            pl.BlockSpec((1, S, 2 * DV), lambda b, p: (b, 0, (2 * qk_tiles * V7X_LANES + H * DV) // (2 * DV) + p)),
            pl.BlockSpec((S, V7X_LANES), lambda b, p: (0, 0)),
            pl.BlockSpec((S, V7X_LANES), lambda b, p: (0, 0)),
            pl.BlockSpec((2, C, C), lambda b, p: (p, 0, 0)),
            pl.BlockSpec((2, C, V7X_LANES), lambda b, p: (p, 0, 0)),
            pl.BlockSpec((2, C, V7X_LANES), lambda b, p: (p, 0, 0)),
            pl.BlockSpec((2, 8, V7X_LANES), lambda b, p: (p, 0, 0)),
        ],
        out_specs=pl.BlockSpec((1, S, 2 * DV), lambda b, p: (b, 0, p)),
        out_shape=jax.ShapeDtypeStruct((B, S, H * DV), F32),
        scratch_shapes=[pltpu.VMEM((2, V7X_LANES, DV), F32)],
        compiler_params=_params("parallel", "parallel"),
        name="retention_mixer",
    )(z, z, z, z, cos, sin, d_in, d_q, d_k, d_c)


def _even_mixer(x, g_norm, w_in, shift_mu, w0, w_up, a0, a_up, g_up, k_k, k_a, r_k, ln_g, ln_b):
    B, S, D = x.shape
    z = norm_matmul(x.reshape(B * S, D), g_norm, w_in.astype(BF16)).reshape(B, S, -1)
    o_a = moba_attention(z)
    o_b = rwkv7_mixer(z, shift_mu, w0, w_up, a0, a_up, g_up, k_k, k_a, r_k, ln_g, ln_b)
    return o_a, o_b


def _odd_mixer(x, g_norm, w_in, pe_k, w1_k, w2_k, pe_v, w1_v, w2_v):
    B, S, D = x.shape
    perm, col = _odd_layout()
    w_p = jnp.take(jnp.pad(w_in, ((0, 0), (0, 1))), perm, axis=1).astype(BF16)
    z = norm_matmul(x.reshape(B * S, D), g_norm, w_p).reshape(B, S, -1)
    o_c = retention_mixer(z)
    cmp_kv, cmp_vk = nsa_compress(z, col["kc"], col["vc"], pe_k, w1_k, w2_k, pe_v, w1_v, w2_v)
    o_d = nsa_attention(z, cmp_kv, cmp_vk, col["nq"], col["slc"], col["win"], col["gate"])
    return o_c, o_d


def _odd_layout():
    G, Dh = NSA_KV_GROUPS, HEAD_DIM
    sizes = (RET_HEADS * RET_QK_DIM, RET_HEADS * RET_QK_DIM, RET_HEADS * RET_V_DIM, RET_HEADS * RET_V_DIM,
             NSA_HEADS * Dh) + (G * Dh,) * 6 + (3 * NSA_HEADS,)
    off = np.concatenate([[0], np.cumsum(sizes)])
    rq, rk, rv, rg, nq, kc, vc, ks, vs, kw, vw, ng = off[:-1]
    n_in = int(off[-1])
    pair = lambda a, b: np.concatenate([np.concatenate([a + g * Dh + np.arange(Dh), b + g * Dh + np.arange(Dh)])
                                        for g in range(G)])
    perm = np.concatenate([np.arange(ks), pair(ks, vs), pair(kw, vw), ng + np.arange(3 * NSA_HEADS)])
    n_pad = -(-len(perm) // (6 * V7X_MXU_DIM)) * 6 * V7X_MXU_DIM
    perm = np.concatenate([perm, np.full(n_pad - len(perm), n_in)]).astype(np.int32)
    col = {"nq": int(nq), "kc": int(kc), "vc": int(vc), "slc": int(ks), "win": int(ks) + 2 * G * Dh,
           "gate": int(ks) + 4 * G * Dh}
    return perm, col


def kernel(x, mix_norm, ffn_norm, even_w_in, even_shift_mu, even_w0, even_w_up, even_a0, even_a_up, even_g_up, even_k_k, even_k_a, even_r_k, even_ln_g, even_ln_b, even_w_out, odd_w_in, odd_cmp_pe_k, odd_cmp_w1_k, odd_cmp_w2_k, odd_cmp_pe_v, odd_cmp_w1_v, odd_cmp_w2_v, odd_w_out, ffn_w1, ffn_w3, ffn_w2, final_norm):
    B, S, D = x.shape
    depth = mix_norm.shape[0]
    for layer in range(depth):
        i = layer // 2
        if layer % 2 == 0:
            o1, o2 = _even_mixer(x, mix_norm[layer], even_w_in[i], even_shift_mu[i], even_w0[i], even_w_up[i],
                                 even_a0[i], even_a_up[i], even_g_up[i], even_k_k[i], even_k_a[i], even_r_k[i],
                                 even_ln_g[i], even_ln_b[i])
            w_out = even_w_out[i]
        else:
            o1, o2 = _odd_mixer(x, mix_norm[layer], odd_w_in[i], odd_cmp_pe_k[i], odd_cmp_w1_k[i], odd_cmp_w2_k[i],
                                odd_cmp_pe_v[i], odd_cmp_w1_v[i], odd_cmp_w2_v[i])
            w_out = odd_w_out[i]
        T = B * S
        x2 = out_proj_residual(o1.reshape(T, -1), o2.reshape(T, -1), w_out.astype(BF16), x.reshape(T, D))
        x2 = ffn_residual(x2, ffn_norm[layer], ffn_w1[layer].astype(BF16), ffn_w3[layer].astype(BF16),
                          ffn_w2[layer].astype(BF16), final_norm if layer == depth - 1 else None)
        x = x2.reshape(B, S, D)
    return x
```

```python
import functools

import jax
import jax.numpy as jnp
import numpy as np
from jax import lax
from jax.experimental import pallas as pl
from jax.experimental.pallas import tpu as pltpu

F32 = jnp.float32
BF16 = jnp.bfloat16

V7X_LANES = 128
V7X_MXU_DIM = 256
V7X_VMEM_BYTES = 64 * 1024 * 1024
VMEM_LIMIT = V7X_VMEM_BYTES * 7 // 8

NORM_EPS = 1e-6
HEAD_DIM = 64

MOBA_BLOCK = 256
MOBA_TOPK = 3
RWKV_HEADS = 16
RWKV_GN_EPS = 6.4e-4

RET_HEADS = 8
RET_QK_DIM = 64
RET_V_DIM = 128
RET_CHUNK = 128
RET_GN_EPS = 1e-6
ROPE_BASE = 10000.0
NSA_HEADS = 16
NSA_KV_GROUPS = 4
NSA_CMP_BLOCK = 32
NSA_CMP_STRIDE = 16
NSA_SLC_BLOCK = 64
NSA_SLC_TOPN = 16
NSA_WINDOW = 512


def _params(*semantics):
    return pltpu.CompilerParams(dimension_semantics=semantics, vmem_limit_bytes=VMEM_LIMIT)


def _rms(x, g):
    return x * lax.rsqrt(jnp.mean(x * x, axis=-1, keepdims=True) + NORM_EPS) * g


def _norm_matmul_kernel(x_ref, g_ref, w_ref, o_ref):
    x = x_ref[...]
    scale = lax.rsqrt(jnp.mean(x * x, axis=-1, keepdims=True) + NORM_EPS)
    o_ref[...] = jnp.dot((x * g_ref[...]).astype(BF16), w_ref[...], preferred_element_type=F32) * scale


def _proj_tile(n):
    assert n % V7X_MXU_DIM == 0
    k = n // V7X_MXU_DIM
    return V7X_MXU_DIM * max(d for d in range(1, 7) if k % d == 0)


def norm_matmul(x, g, w, *, tm=512):
    T, D = x.shape
    N = w.shape[1]
    tn = _proj_tile(N)
    assert T % tm == 0 and N % tn == 0
    return pl.pallas_call(
        _norm_matmul_kernel,
        grid=(N // tn, T // tm),
        in_specs=[
            pl.BlockSpec((tm, D), lambda j, i: (i, 0)),
            pl.BlockSpec((1, D), lambda j, i: (0, 0)),
            pl.BlockSpec((D, tn), lambda j, i: (0, j)),
        ],
        out_specs=pl.BlockSpec((tm, tn), lambda j, i: (i, j)),
        out_shape=jax.ShapeDtypeStruct((T, N), F32),
        compiler_params=_params("parallel", "parallel"),
        name="norm_matmul",
    )(x, g.reshape(1, D), w)


def _out_proj_kernel(a_ref, b_ref, wa_ref, wb_ref, x_ref, o_ref):
    acc = jnp.dot(a_ref[...].astype(BF16), wa_ref[...], preferred_element_type=F32)
    acc += jnp.dot(b_ref[...].astype(BF16), wb_ref[...], preferred_element_type=F32)
    o_ref[...] = x_ref[...] + acc


def out_proj_residual(a, b, w, x, *, tm=512):
    T, D = x.shape
    Ka, Kb = a.shape[1], b.shape[1]
    assert T % tm == 0 and w.shape == (Ka + Kb, D) and Ka == Kb
    return pl.pallas_call(
        _out_proj_kernel,
        grid=(T // tm,),
        in_specs=[
            pl.BlockSpec((tm, Ka), lambda i: (i, 0)),
            pl.BlockSpec((tm, Kb), lambda i: (i, 0)),
            pl.BlockSpec((Ka, D), lambda i: (0, 0)),
            pl.BlockSpec((Kb, D), lambda i: (1, 0)),
            pl.BlockSpec((tm, D), lambda i: (i, 0)),
        ],
        out_specs=pl.BlockSpec((tm, D), lambda i: (i, 0)),
        out_shape=jax.ShapeDtypeStruct((T, D), F32),
        compiler_params=_params("parallel"),
        name="out_proj_residual",
    )(a, b, w, w, x)


def _ffn_kernel(x_ref, g_ref, w1_ref, w3_ref, w2_ref, gf_ref, o_ref, h_ref, acc_ref, *, final_norm):
    j = pl.program_id(1)

    @pl.when(j == 0)
    def _():
        h_ref[...] = _rms(x_ref[...], g_ref[...]).astype(BF16)
        acc_ref[...] = jnp.zeros_like(acc_ref)

    h = h_ref[...]
    a = jnp.dot(h, w1_ref[...], preferred_element_type=F32)
    b = jnp.dot(h, w3_ref[...], preferred_element_type=F32)
    act = (a * jax.nn.sigmoid(a) * b).astype(BF16)
    acc_ref[...] += jnp.dot(act, w2_ref[...], preferred_element_type=F32)

    @pl.when(j == pl.num_programs(1) - 1)
    def _():
        y = x_ref[...] + acc_ref[...]
        if final_norm:
            y = _rms(y, gf_ref[...])
        o_ref[...] = y


def ffn_residual(x, g, w1, w3, w2, g_final=None, *, tm=512, tf=512):
    T, D = x.shape
    Fh = w1.shape[1]
    assert T % tm == 0 and Fh % tf == 0
    final_norm = g_final is not None
    gf = (g_final if final_norm else g).reshape(1, D)
    return pl.pallas_call(
        functools.partial(_ffn_kernel, final_norm=final_norm),
        grid=(T // tm, Fh // tf),
        in_specs=[
            pl.BlockSpec((tm, D), lambda i, j: (i, 0)),
            pl.BlockSpec((1, D), lambda i, j: (0, 0)),
            pl.BlockSpec((D, tf), lambda i, j: (0, j)),
            pl.BlockSpec((D, tf), lambda i, j: (0, j)),
            pl.BlockSpec((tf, D), lambda i, j: (j, 0)),
            pl.BlockSpec((1, D), lambda i, j: (0, 0)),
        ],
        out_specs=pl.BlockSpec((tm, D), lambda i, j: (i, 0)),
        out_shape=jax.ShapeDtypeStruct((T, D), F32),
        scratch_shapes=[pltpu.VMEM((tm, D), BF16), pltpu.VMEM((tm, D), F32)],
        compiler_params=_params("parallel", "arbitrary"),
        name="ffn_residual",
    )(x, g.reshape(1, D), w1, w3, w2, gf)


NEG_BIG = -1e30
_NT = (((1,), (1,)), ((), ()))


def _flash_steps(qas, kas, vas, masks, m_prev, acc_prev):
    hs = range(len(qas))
    s = [lax.dot_general(qas[h], kas[h], _NT, preferred_element_type=F32) for h in hs]
    s = [s[h] if masks[h] is None else jnp.where(masks[h], s[h], NEG_BIG) for h in hs]
    m_new = [jnp.maximum(m_prev[h], jnp.max(s[h], axis=1, keepdims=True)) for h in hs]
    alpha = [jnp.exp2(m_prev[h] - m_new[h]) for h in hs]
    p = [jnp.exp2(s[h] - m_new[h]) for h in hs]
    pv = [jnp.dot(p[h].astype(BF16), vas[h], preferred_element_type=F32) for h in hs]
    return m_new, [alpha[h] * acc_prev[h] + pv[h] for h in hs]


def _augment_q(q_log2, in_head, keep_t, odd):
    nblk, tq = keep_t.shape
    bias_t = jnp.where(keep_t, 0.0, NEG_BIG)
    bias = jnp.concatenate([bias_t, jnp.zeros((V7X_LANES - nblk, tq), F32)], axis=0).T
    if not odd:
        bias = pltpu.roll(bias, HEAD_DIM, axis=1)
    return jnp.where(in_head, q_log2, bias).astype(BF16)


def _key_value_tiles(k, v, blk, lane):
    low = lane < HEAD_DIM
    hot_e = 0.0 if blk is None else jnp.where(lane - HEAD_DIM == blk, 1.0, 0.0)
    hot_o = 0.0 if blk is None else jnp.where(lane == blk, 1.0, 0.0)
    k_e, k_o = jnp.where(low, k, hot_e), jnp.where(low, hot_o, k)
    v_e, v_o = jnp.where(low, v, jnp.where(lane == HEAD_DIM, 1.0, 0.0)), jnp.where(low, jnp.where(lane == 0, 1.0, 0.0), v)
    return [t.astype(BF16) for t in (k_e, k_o, v_e, v_o)]


def _normalise(acc, lane, odd):
    return acc / jnp.sum(jnp.where(lane == (0 if odd else HEAD_DIM), acc, 0.0), axis=1, keepdims=True)


LOG2E = 1.4426950408889634


MOBA_HEADS_PER_STEP = 4


def _split_bf16(x):
    hi = x.astype(BF16)
    return hi, (x - hi.astype(F32)).astype(BF16)


def _moba_kernel(q_ref, k_ref, v_ref, o_ref, ka_ref, va_ref, km_ref, acc_ref):
    L = MOBA_BLOCK
    S = k_ref.shape[1]
    nb = S // L
    HP = MOBA_HEADS_PER_STEP
    qi = pl.program_id(2)
    lane = lax.broadcasted_iota(jnp.int32, (L, V7X_LANES), 1)
    lanes_of = lambda ref, pp: ref[0, :, pp * V7X_LANES:(pp + 1) * V7X_LANES]

    @pl.when(qi == 0)
    def _():
        lane_s = lax.broadcasted_iota(jnp.int32, (S, V7X_LANES), 1)
        blk = lax.broadcasted_iota(jnp.int32, (S, V7X_LANES), 0) // L
        for pp in range(HP // 2):
            k = lanes_of(k_ref, pp)
            (ka_ref[2 * pp], ka_ref[2 * pp + 1], va_ref[2 * pp], va_ref[2 * pp + 1]) = _key_value_tiles(
                k, lanes_of(v_ref, pp), blk, lane_s)
            km_ref[pp] = jnp.concatenate(_split_bf16(jnp.mean(k.reshape(nb, L, V7X_LANES), axis=1)), axis=0)

    row = lax.broadcasted_iota(jnp.int32, (L, L), 0)
    col = lax.broadcasted_iota(jnp.int32, (L, L), 1)
    causal = col <= row
    jrow = lax.broadcasted_iota(jnp.int32, (nb, L), 0)
    past = jrow < qi
    qas = []
    for pp in range(HP // 2):
        q = lanes_of(q_ref, pp) * (HEAD_DIM ** -0.5)
        q_hi, q_lo = _split_bf16(q)
        km = km_ref[pp]
        for e in range(2):
            in_head = (lane >= e * HEAD_DIM) & (lane < (e + 1) * HEAD_DIM)
            zero = jnp.zeros_like(q_hi)
            g1 = lax.dot_general(km, jnp.where(in_head, q_hi, zero), _NT, preferred_element_type=F32)
            g2 = lax.dot_general(km, jnp.where(in_head, q_lo, zero), _NT, preferred_element_type=F32)
            gate = g1[0:nb] + g1[nb:2 * nb] + g2[0:nb]
            keep = jrow == qi
            for n in range(nb):
                g_n = gate[n:n + 1, :]
                beats = (gate > g_n) | ((gate == g_n) & (jrow < n))
                rank = jnp.sum(jnp.where(past & beats, 1.0, 0.0), axis=0, keepdims=True)
                keep = keep | ((jrow == n) & (rank < MOBA_TOPK) & past)
            qas.append(_augment_q(q * LOG2E, in_head, keep, odd=e == 1))

    tiles = lambda start: ([ka_ref[h, pl.ds(start, L), :] for h in range(HP)],
                           [va_ref[h, pl.ds(start, L), :] for h in range(HP)])

    m, acc = _flash_steps(qas, *tiles(pl.multiple_of(qi * L, L)), [causal] * HP,
                          [jnp.full((L, 1), NEG_BIG, F32)] * HP, [jnp.zeros((L, V7X_LANES), F32)] * HP)
    for h in range(HP):
        acc_ref[h] = acc[h]

    def body(n, carry):
        m2, acc2 = _flash_steps(qas, *tiles(pl.multiple_of(n * L, L)), [None] * HP, list(carry),
                                [acc_ref[h] for h in range(HP)])
        for h in range(HP):
            acc_ref[h] = acc2[h]
        return tuple(m2)

    lax.fori_loop(0, qi, body, tuple(m))
    for pp in range(HP // 2):
        o_ref[0, :, pp * V7X_LANES:(pp + 1) * V7X_LANES] = jnp.where(
            lane < HEAD_DIM, _normalise(acc_ref[2 * pp], lane, False), _normalise(acc_ref[2 * pp + 1], lane, True))


def moba_attention(z, *, n_heads=16):
    B, S, _ = z.shape
    L, HP = MOBA_BLOCK, MOBA_HEADS_PER_STEP
    W = HP * HEAD_DIM
    nb = S // L
    assert S % L == 0 and n_heads % HP == 0 and W % V7X_LANES == 0 and 2 * nb <= 16
    ngrp = n_heads // HP
    return pl.pallas_call(
        _moba_kernel,
        grid=(B, ngrp, S // L),
        in_specs=[
            pl.BlockSpec((1, L, W), lambda b, p, i: (b, i, p)),
            pl.BlockSpec((1, S, W), lambda b, p, i: (b, 0, ngrp + p)),
            pl.BlockSpec((1, S, W), lambda b, p, i: (b, 0, 2 * ngrp + p)),
        ],
        out_specs=pl.BlockSpec((1, L, W), lambda b, p, i: (b, i, p)),
        out_shape=jax.ShapeDtypeStruct((B, S, n_heads * HEAD_DIM), F32),
        scratch_shapes=[
            pltpu.VMEM((HP, S, V7X_LANES), BF16),
            pltpu.VMEM((HP, S, V7X_LANES), BF16),
            pltpu.VMEM((HP // 2, 2 * nb, V7X_LANES), BF16),
            pltpu.VMEM((HP, L, V7X_LANES), F32),
        ],
        compiler_params=_params("parallel", "parallel", "arbitrary"),
        name="moba_attention",
    )(z, z, z)


NSA_TQ = 256
BIG = 3.0e38


def _gelu_tanh(x):
    return 0.5 * x * (1.0 + jnp.tanh(0.7978845608028654 * (x + 0.044715 * x * x * x)))


def _nsa_compress_kernel(xk0_ref, xk1_ref, xv0_ref, xv1_ref, pek_ref, pev_ref, w1k_ref, w1v_ref, w2k_ref, w2v_ref,
                         o1_ref, o2_ref):
    G, Lc, st = NSA_KV_GROUPS, NSA_CMP_BLOCK, NSA_CMP_STRIDE
    nrow = xk0_ref.shape[1] // st
    lane = lax.broadcasted_iota(jnp.int32, (nrow, G * HEAD_DIM), 1)

    def hidden(x_refs, pe_ref, w1_ref):
        acc = [jnp.zeros((G * nrow, V7X_LANES), F32) for _ in range(Lc // st)]
        for l in range(Lc):
            u, m = divmod(l, st)
            x = jnp.concatenate([r[0, pl.ds(m, nrow, stride=st), :] for r in x_refs], axis=1) + pe_ref[l:l + 1, :]
            xs = jnp.concatenate(
                [jnp.where((lane >= g * HEAD_DIM) & (lane < (g + 1) * HEAD_DIM), x, 0.0) for g in range(G)],
                axis=0).astype(BF16)
            acc[u] = acc[u] + jnp.dot(xs, w1_ref[l], preferred_element_type=F32)
        nxt = jnp.concatenate([pltpu.roll(acc[1][g * nrow:(g + 1) * nrow], nrow - 1, axis=0) for g in range(G)],
                              axis=0)
        return _gelu_tanh(acc[0] + nxt).astype(BF16)

    hk = hidden((xk0_ref, xk1_ref), pek_ref, w1k_ref)
    hv = hidden((xv0_ref, xv1_ref), pev_ref, w1v_ref)
    kc = jnp.dot(hk, w2k_ref[...], preferred_element_type=F32)
    vc = jnp.dot(hv, w2v_ref[...], preferred_element_type=F32)
    kv = kc + vc
    vk = pltpu.roll(kv, HEAD_DIM, axis=1)
    for g in range(G):
        o1_ref[0, :, g * V7X_LANES:(g + 1) * V7X_LANES] = kv[g * nrow:(g + 1) * nrow]
        o2_ref[0, :, g * V7X_LANES:(g + 1) * V7X_LANES] = vk[g * nrow:(g + 1) * nrow]


def nsa_compress(z, col_k, col_v, pe_k, w1_k, w2_k, pe_v, w1_v, w2_v):
    B, S, _ = z.shape
    G, Lc, st = NSA_KV_GROUPS, NSA_CMP_BLOCK, NSA_CMP_STRIDE
    GW = G * HEAD_DIM
    nrow = S // st
    hid = w1_k.shape[1]
    assert hid == V7X_LANES and col_k % GW == 0 and col_v % GW == 0
    tile_pe = lambda pe: jnp.tile(pe, (1, G))
    tile_w1 = lambda w: jnp.tile(w.reshape(Lc, 1, HEAD_DIM, hid), (1, G, 1, 1)).reshape(Lc, GW, hid).astype(BF16)
    w2k = jnp.pad(w2_k, ((0, 0), (0, HEAD_DIM))).astype(BF16)
    w2v = jnp.pad(w2_v, ((0, 0), (HEAD_DIM, 0))).astype(BF16)
    const = lambda shape: pl.BlockSpec(shape, lambda b: (0,) * len(shape))
    out = jax.ShapeDtypeStruct((B, nrow, G * V7X_LANES), F32)
    return pl.pallas_call(
        _nsa_compress_kernel,
        grid=(B,),
        in_specs=[
            pl.BlockSpec((1, S, V7X_LANES), lambda b: (b, 0, col_k // V7X_LANES)),
            pl.BlockSpec((1, S, V7X_LANES), lambda b: (b, 0, col_k // V7X_LANES + 1)),
            pl.BlockSpec((1, S, V7X_LANES), lambda b: (b, 0, col_v // V7X_LANES)),
            pl.BlockSpec((1, S, V7X_LANES), lambda b: (b, 0, col_v // V7X_LANES + 1)),
            const((Lc, GW)), const((Lc, GW)),
            const((Lc, GW, hid)), const((Lc, GW, hid)),
            const((hid, V7X_LANES)), const((hid, V7X_LANES)),
        ],
        out_specs=[pl.BlockSpec((1, nrow, G * V7X_LANES), lambda b: (b, 0, 0))] * 2,
        out_shape=[out, out],
        compiler_params=_params("parallel"),
        name="nsa_compress",
    )(z, z, z, z, tile_pe(pe_k), tile_pe(pe_v), tile_w1(w1_k), tile_w1(w1_v), w2k, w2v)


def _nsa_kernel(q_ref, c1_ref, c2_ref, s_ref, w_ref, g_ref, ovt_ref, o_ref, sk_ref, sv_ref, wk_ref, wv_ref, acc_ref):
    TQ = NSA_TQ
    S = s_ref.shape[1]
    R = NSA_HEADS // NSA_KV_GROUPS
    grp = pl.program_id(1)
    qi = pl.program_id(2)
    lane = lax.broadcasted_iota(jnp.int32, (TQ, V7X_LANES), 1)

    @pl.when(qi == 0)
    def _():
        lane_s = lax.broadcasted_iota(jnp.int32, (S, V7X_LANES), 1)
        blk = lax.broadcasted_iota(jnp.int32, (S, V7X_LANES), 0) // NSA_SLC_BLOCK
        kv = s_ref[0]
        vk = pltpu.roll(kv, HEAD_DIM, axis=1)
        sk_ref[0], sk_ref[1], sv_ref[0], sv_ref[1] = _key_value_tiles(
            jnp.where(lane_s < HEAD_DIM, kv, vk), jnp.where(lane_s < HEAD_DIM, vk, kv), blk, lane_s)
        kv = w_ref[0]
        vk = pltpu.roll(kv, HEAD_DIM, axis=1)
        wk_ref[0], wk_ref[1], wv_ref[0], wv_ref[1] = _key_value_tiles(
            jnp.where(lane_s < HEAD_DIM, kv, vk), jnp.where(lane_s < HEAD_DIM, vk, kv), None, lane_s)

    q0 = pl.multiple_of(qi * TQ, TQ)
    row = lax.broadcasted_iota(jnp.int32, (TQ, TQ), 0)
    col = lax.broadcasted_iota(jnp.int32, (TQ, TQ), 1)
    causal = col <= row
    qi_mat = jnp.zeros((TQ, TQ), jnp.int32) + qi
    t_abs = q0 + lax.broadcasted_iota(jnp.int32, (TQ, V7X_LANES), 0)
    even_lanes = lane < HEAD_DIM

    c_kv, c_vk = c1_ref[0], c2_ref[0]
    c_kv_b, c_vk_b = c_kv.astype(BF16), c_vk.astype(BF16)
    cmask = lane * NSA_CMP_STRIDE + (NSA_CMP_BLOCK - 1) <= t_abs
    tiles, o_cmp = [], []
    p_sum = jnp.zeros((TQ, V7X_LANES), F32)
    for r in range(R):
        tiles.append(q_ref[0, :, (r // 2) * V7X_LANES:(r // 2 + 1) * V7X_LANES] * (HEAD_DIM ** -0.5))
        qm = jnp.where(even_lanes if r % 2 == 0 else ~even_lanes, tiles[r], 0.0).astype(BF16)
        s = lax.dot_general(qm, c_kv_b if r % 2 == 0 else c_vk_b, _NT, preferred_element_type=F32)
        s = jnp.where(cmask, s, NEG_BIG)
        p = jnp.where(cmask, jnp.exp(s - jnp.max(s, axis=1, keepdims=True)), 0.0)
        den = jnp.sum(p, axis=1, keepdims=True)
        p = p / jnp.where(den > 0.0, den, 1.0)
        p_sum = p_sum + p
        o_cmp.append(jnp.dot(p.astype(BF16), c_vk_b if r % 2 == 0 else c_kv_b, preferred_element_type=F32))

    nblk = s_ref.shape[1] // NSA_SLC_BLOCK
    p_hi, p_lo = _split_bf16(p_sum)
    ovt = ovt_ref[...]
    p_slc = (lax.dot_general(ovt, p_hi, _NT, preferred_element_type=F32)
             + lax.dot_general(ovt, p_lo, _NT, preferred_element_type=F32))[0:nblk]
    jrow = lax.broadcasted_iota(jnp.int32, (nblk, TQ), 0)
    own = (q0 + lax.broadcasted_iota(jnp.int32, (nblk, TQ), 1)) // NSA_SLC_BLOCK
    score = jnp.where((jrow == own) | (jrow == 0), BIG, jnp.where(jrow > own, -BIG, p_slc))
    keep = jrow > nblk
    for j in range(nblk):
        s_j = score[j:j + 1, :]
        beats = (score > s_j) | ((score == s_j) & (jrow < j))
        rank = jnp.sum(jnp.where(beats, 1.0, 0.0), axis=0, keepdims=True)
        keep = keep | ((jrow == j) & (rank < NSA_SLC_TOPN) & (jrow <= own))
    qas = [_augment_q(tiles[r] * LOG2E, even_lanes if r % 2 == 0 else ~even_lanes, keep, odd=r % 2 == 1)
           for r in range(R)]

    neg = [jnp.full((TQ, 1), NEG_BIG, F32)] * R
    zacc = [jnp.zeros((TQ, V7X_LANES), F32)] * R

    def kv_blocks(k_ref, v_ref, start):
        return ([k_ref[r % 2, pl.ds(start, TQ), :] for r in range(R)],
                [v_ref[r % 2, pl.ds(start, TQ), :] for r in range(R)])

    m, acc = _flash_steps(qas, *kv_blocks(sk_ref, sv_ref, q0), [causal] * R, neg, zacc)
    for r in range(R):
        acc_ref[r] = acc[r]

    def body(kb, carry):
        m2, acc2 = _flash_steps(qas, *kv_blocks(sk_ref, sv_ref, pl.multiple_of(kb * TQ, TQ)), [None] * R,
                                list(carry), [acc_ref[r] for r in range(R)])
        for r in range(R):
            acc_ref[r] = acc2[r]
        return tuple(m2)

    lax.fori_loop(0, qi, body, tuple(m))

    m, acc = _flash_steps(qas, *kv_blocks(wk_ref, wv_ref, q0), [causal] * R, neg, zacc)
    m, acc = _flash_steps(qas, *kv_blocks(wk_ref, wv_ref, pl.multiple_of(jnp.maximum(qi - 1, 0) * TQ, TQ)),
                          [qi_mat >= 1] * R, m, acc)
    m, acc = _flash_steps(qas, *kv_blocks(wk_ref, wv_ref, pl.multiple_of(jnp.maximum(qi - 2, 0) * TQ, TQ)),
                          [(col > row) & (qi_mat >= 2)] * R, m, acc)

    gates = jax.nn.sigmoid(g_ref[0])
    outs = []
    for r in range(R):
        o_slc = _normalise(acc_ref[r], lane, r % 2 == 1)
        o_win = _normalise(acc[r], lane, r % 2 == 1)
        c0 = (grp * R + r) * 3
        gate = lambda c: jnp.sum(jnp.where(lane == c, gates, 0.0), axis=1, keepdims=True)
        outs.append(gate(c0) * o_cmp[r] + gate(c0 + 1) * o_slc + gate(c0 + 2) * o_win)
    for p2 in range(R // 2):
        o_ref[0, :, p2 * V7X_LANES:(p2 + 1) * V7X_LANES] = jnp.where(even_lanes, outs[2 * p2], outs[2 * p2 + 1])


def nsa_attention(z, cmp_kv, cmp_vk, col_q, col_slc, col_win, col_gate):
    B, S, _ = z.shape
    G, TQ = NSA_KV_GROUPS, NSA_TQ
    R = NSA_HEADS // G
    QW = R * HEAD_DIM
    ncmp = cmp_kv.shape[1]
    assert S % TQ == 0 and ncmp == V7X_LANES and S // NSA_SLC_BLOCK <= V7X_LANES and NSA_WINDOW == 2 * TQ
    assert col_q % QW == 0 and col_slc % V7X_LANES == 0 and col_win % V7X_LANES == 0 and col_gate % V7X_LANES == 0
    nc = (S - NSA_CMP_BLOCK) // NSA_CMP_STRIDE + 1
    c_start = np.arange(V7X_LANES) * NSA_CMP_STRIDE
    s_start = np.arange(V7X_LANES) * NSA_SLC_BLOCK
    overlap = ((c_start[:, None] <= s_start[None, :] + NSA_SLC_BLOCK - 1)
               & (c_start[:, None] + NSA_CMP_BLOCK - 1 >= s_start[None, :])
               & (np.arange(V7X_LANES)[:, None] < nc) & (np.arange(V7X_LANES)[None, :] < S // NSA_SLC_BLOCK))
    const = lambda shape: pl.BlockSpec(shape, lambda b, g, i: (0,) * len(shape))
    return pl.pallas_call(
        _nsa_kernel,
        grid=(B, G, S // TQ),
        in_specs=[
            pl.BlockSpec((1, TQ, QW), lambda b, g, i: (b, i, col_q // QW + g)),
            pl.BlockSpec((1, ncmp, V7X_LANES), lambda b, g, i: (b, 0, g)),
            pl.BlockSpec((1, ncmp, V7X_LANES), lambda b, g, i: (b, 0, g)),
            pl.BlockSpec((1, S, V7X_LANES), lambda b, g, i: (b, 0, col_slc // V7X_LANES + g)),
            pl.BlockSpec((1, S, V7X_LANES), lambda b, g, i: (b, 0, col_win // V7X_LANES + g)),
            pl.BlockSpec((1, TQ, V7X_LANES), lambda b, g, i: (b, i, col_gate // V7X_LANES)),
            const((V7X_LANES, V7X_LANES)),
        ],
        out_specs=pl.BlockSpec((1, TQ, QW), lambda b, g, i: (b, i, g)),
        out_shape=jax.ShapeDtypeStruct((B, S, NSA_HEADS * HEAD_DIM), F32),
        scratch_shapes=[pltpu.VMEM((2, S, V7X_LANES), BF16)] * 4 + [
            pltpu.VMEM((R, TQ, V7X_LANES), F32),
        ],
        compiler_params=_params("parallel", "parallel", "arbitrary"),
        name="nsa_attention",
    )(z, cmp_kv, cmp_vk, z, z, z, jnp.asarray(overlap.T, BF16))


RWKV_CHUNK = 64
RWKV_ROWS = 256
RWKV_INTERLEAVE = 8


def _mm(a, b, dims=None):
    dims = dims or (((1,), (0,)), ((), ()))
    return lax.dot_general(a.astype(BF16), b.astype(BF16), dims, preferred_element_type=F32)


def _mm3(a, b):
    (a_hi, a_lo), (b_hi, b_lo) = _split_bf16(a), _split_bf16(b)
    return _mm(a_hi, b_hi) + (_mm(a_hi, b_lo) + _mm(a_lo, b_hi))


def _mm_onehot(a01, b):
    hi = b.astype(BF16)
    mid, lo = _split_bf16(b - hi.astype(F32))
    return _mm(a01, hi) + (_mm(a01, mid) + _mm(a01, lo))


def _head_sum(x, low):
    s0 = jnp.sum(jnp.where(low, x, 0.0), axis=1, keepdims=True)
    s1 = jnp.sum(jnp.where(low, 0.0, x), axis=1, keepdims=True)
    return jnp.where(low, s0, s1)


def _rwkv_kernel(r_ref, k_ref, v_ref, lo_ref, glo_ref, pp_ref, pl_ref, wup_ref, aup_ref, gup_ref, o_ref,
                 rs, ws, ks, vs, als, bes, gs, ys, hs, rqs, ms, ns):
    S = r_ref.shape[1]
    C, RB = RWKV_CHUNK, RWKV_ROWS
    pp = pp_ref[...]
    mu_r, mu_k, mu_v, w0, a0, k_k, k_a, r_k, ln_g, ln_b = [pp[i:i + 1, :] for i in range(10)]
    mu_lo, mu_g = pl_ref[0:1, :], pl_ref[1:2, :]
    low = lax.broadcasted_iota(jnp.int32, (RB, V7X_LANES), 1) < HEAD_DIM
    first = lax.broadcasted_iota(jnp.int32, (RB, V7X_LANES), 0) == 0

    def prologue(i, c):
        t0 = pl.multiple_of(i * RB, RB)
        tp = jnp.maximum(t0 - 1, 0)
        keep = jnp.where(i > 0, 1.0, 0.0)

        def shifted(ref, mu):
            x = ref[0, pl.ds(t0, RB), :]
            prev = jnp.where(first, ref[0, pl.ds(tp, 1), :] * keep, pltpu.roll(x, 1, axis=0))
            return x + (prev - x) * mu

        r, k, v = shifted(r_ref, mu_r), shifted(k_ref, mu_k), shifted(v_ref, mu_v)
        lo, glo = shifted(lo_ref, mu_lo), shifted(glo_ref, mu_g)
        wp = -(w0 + _mm(jnp.tanh(lo), wup_ref[...]))
        w = -(jnp.maximum(wp, 0.0) + jnp.log(1.0 + jnp.exp(-jnp.abs(wp)))) - 0.5
        a = jax.nn.sigmoid(a0 + _mm(lo, aup_ref[...]))
        kk = k * k_k
        kk = kk / jnp.maximum(jnp.sqrt(_head_sum(kk * kk, low)), 1e-12)
        k2 = k * (1.0 + (a - 1.0) * k_a)
        rs[pl.ds(t0, RB), :] = r
        ws[pl.ds(t0, RB), :] = -jnp.exp(w)
        ks[pl.ds(t0, RB), :] = k2
        vs[pl.ds(t0, RB), :] = v
        als[pl.ds(t0, RB), :] = -kk
        bes[pl.ds(t0, RB), :] = kk * a
        gs[pl.ds(t0, RB), :] = _mm(jax.nn.sigmoid(glo), gup_ref[...])
        o_ref[0, pl.ds(t0, RB), :] = _head_sum(r * k2 * r_k, low) * v
        return c

    lax.fori_loop(0, S // RB, prologue, 0)

    W2 = 2 * C
    row = lax.broadcasted_iota(jnp.int32, (W2, W2), 0)
    col = lax.broadcasted_iota(jnp.int32, (W2, W2), 1)
    t_idx, s_idx = row % C, col % C
    top, left = row < C, col < C
    same = top == left
    eye = jnp.where(row == col, 1.0, 0.0)
    tri = jnp.where(lax.broadcasted_iota(jnp.int32, (C, C), 1) <= lax.broadcasted_iota(jnp.int32, (C, C), 0), 1.0, 0.0)
    low_c = lax.broadcasted_iota(jnp.int32, (C, V7X_LANES), 1) < HEAD_DIM
    fold = lambda x: x[0:C] + x[C:W2]
    stack_heads = lambda x: jnp.concatenate([jnp.where(low_c, x, 0.0), jnp.where(low_c, 0.0, x)], axis=0)
    block_diag = lambda x: jnp.where(top, jnp.where(left, x, 0.0), jnp.where(left, 0.0, pltpu.roll(x, C, axis=1)))

    rows = lambda c: pl.ds(c * C if isinstance(c, int) else pl.multiple_of(c * C, C), C)

    def advance(c, H):
        ys[rows(c), :] += _mm3(rqs[c], H)
        return _mm3(ms[c], H) + ns[c]

    def transfers(i, lagged):
        each = lambda f, *xs: [f(*a) for a in zip(*xs)]
        cs = [i * RWKV_INTERLEAVE + u for u in range(RWKV_INTERLEAVE)]
        sls = [rows(c) for c in cs]
        state = [hs[...]] if lagged else None

        def lag(hook):
            if lagged:
                for u in range(hook * RWKV_INTERLEAVE // 8, (hook + 1) * RWKV_INTERLEAVE // 8):
                    state[0] = advance(cs[u] - RWKV_INTERLEAVE, state[0])

        r, lw, k2, v, al, be = ([ref[sl, :] for sl in sls] for ref in (rs, ws, ks, vs, als, bes))
        logp = each(lambda x: _mm_onehot(tri, x), lw)
        lag(0)
        P = each(jnp.exp, logp)
        Pinv = each(lambda x: jnp.exp(-x), logp)
        At = each(lambda a_, lp, w_: a_ * jnp.exp(lp - w_), al, logp, lw)
        Rt, Bt, Kt = each(jnp.multiply, r, P), each(jnp.multiply, be, Pinv), each(jnp.multiply, k2, Pinv)
        PC = each(lambda p: p[C - 1:C, :], P)
        A_bd, R_bd = each(stack_heads, At), each(stack_heads, Rt)
        Yt = each(lambda b, k: jnp.concatenate([b, k], axis=0), Bt, Kt)
        A1 = each(lambda a, y: jnp.where(s_idx < t_idx, _mm(a, y, dims=_NT), 0.0), A_bd, Yt)
        A2 = each(lambda a, y: jnp.where(s_idx <= t_idx, _mm(a, y, dims=_NT), 0.0), R_bd, Yt)
        X, Arb = each(block_diag, A1), each(block_diag, A2)
        T = each(lambda x: eye + x, X)
        for it in range(5):
            X = each(lambda x: _mm(x, x), X)
            T = each(lambda t, x: t + _mm(t, x), T, X)
            lag(1 + it)
        V0 = each(lambda x: jnp.concatenate([jnp.zeros_like(x), x], axis=0), v)
        TA = each(_mm, T, A_bd)
        AkV = each(lambda a, x: jnp.where(same, _mm(a, x), 0.0), A1, V0)
        lag(6)
        U0 = each(_mm, T, AkV)
        lag(7)
        AR = each(lambda a, t, u: _mm(a, jnp.concatenate([t, u], axis=1)), Arb, TA, U0)
        ArkV = each(lambda a, x: jnp.where(same, _mm(a, x), 0.0), A2, V0)
        Mx = each(lambda b, p, t: _mm((b * p).T, fold(t)), Bt, PC, TA)
        Nx = each(lambda b, k, p, u, x: _mm(jnp.concatenate([b * p, k * p], axis=0).T,
                                            jnp.concatenate([fold(u), x], axis=0)), Bt, Kt, PC, U0, v)
        for u in range(RWKV_INTERLEAVE):
            ys[sls[u], :] = fold(AR[u][:, W2:2 * W2] + ArkV[u])
            rqs[cs[u]] = Rt[u] + fold(AR[u][:, 0:W2])
            ms[cs[u]] = eye * PC[u] + jnp.where(same, Mx[u], 0.0)
            ns[cs[u]] = jnp.where(same, Nx[u], 0.0)
        if lagged:
            hs[...] = state[0]

    def pipelined(i, carry):
        transfers(i, True)
        return carry

    def drain(c, carry):
        hs[...] = advance(c, hs[...])
        return carry

    assert 8 % RWKV_INTERLEAVE == 0
    hs[...] = jnp.zeros((W2, W2), F32)
    transfers(0, False)
    lax.fori_loop(1, S // C // RWKV_INTERLEAVE, pipelined, 0)
    lax.fori_loop(S // C - RWKV_INTERLEAVE, S // C, drain, 0)

    def epilogue(i, c):
        sl = pl.ds(pl.multiple_of(i * RB, RB), RB)
        y = ys[sl, :]
        d = y - _head_sum(y, low) * (1.0 / HEAD_DIM)
        var = _head_sum(d * d, low) * (1.0 / HEAD_DIM)
        yn = d * lax.rsqrt(var + RWKV_GN_EPS) * ln_g + ln_b
        o_ref[0, sl, :] = (yn + o_ref[0, sl, :]) * gs[sl, :]
        return c

    lax.fori_loop(0, S // RB, epilogue, 0)


def rwkv7_mixer(z, shift_mu, w0, w_up, a0, a_up, g_up, k_k, k_a, r_k, ln_g, ln_b):
    B, S, _ = z.shape
    CW = RWKV_HEADS * HEAD_DIM
    npair = CW // V7X_LANES
    base = 3 * CW // V7X_LANES
    lora = w_up.shape[0] + a_up.shape[0]
    assert lora == V7X_LANES and g_up.shape[0] == V7X_LANES and S % RWKV_ROWS == 0
    pp = jnp.stack([shift_mu[0:CW], shift_mu[CW:2 * CW], shift_mu[2 * CW:3 * CW], w0, a0, k_k, k_a,
                    r_k.reshape(CW), ln_g, ln_b])
    pp = jnp.pad(pp, ((0, 16 - pp.shape[0]), (0, 0)))
    pl2 = jnp.pad(shift_mu[3 * CW:].reshape(2, V7X_LANES), ((0, 6), (0, 0)))
    wup = jnp.pad(w_up, ((0, a_up.shape[0]), (0, 0)))
    aup = jnp.pad(a_up, ((w_up.shape[0], 0), (0, 0)))
    tile = lambda off: pl.BlockSpec((1, S, V7X_LANES), lambda b, p: (b, 0, base + off * npair + p))
    fixed = lambda off: pl.BlockSpec((1, S, V7X_LANES), lambda b, p: (b, 0, base + 3 * npair + off))
    seq = pltpu.VMEM((S, V7X_LANES), F32)
    return pl.pallas_call(
        _rwkv_kernel,
        grid=(B, npair),
        in_specs=[
            tile(0), tile(1), tile(2), fixed(0), fixed(1),
            pl.BlockSpec((16, V7X_LANES), lambda b, p: (0, p)),
            pl.BlockSpec((8, V7X_LANES), lambda b, p: (0, 0)),
            pl.BlockSpec((V7X_LANES, V7X_LANES), lambda b, p: (0, p)),
            pl.BlockSpec((V7X_LANES, V7X_LANES), lambda b, p: (0, p)),
            pl.BlockSpec((V7X_LANES, V7X_LANES), lambda b, p: (0, p)),
        ],
        out_specs=pl.BlockSpec((1, S, V7X_LANES), lambda b, p: (b, 0, p)),
        out_shape=jax.ShapeDtypeStruct((B, S, CW), F32),
        scratch_shapes=[seq] * 8 + [
            pltpu.VMEM((V7X_LANES, V7X_LANES), F32),
            pltpu.VMEM((S // RWKV_CHUNK, RWKV_CHUNK, V7X_LANES), F32),
            pltpu.VMEM((S // RWKV_CHUNK, V7X_LANES, V7X_LANES), F32),
            pltpu.VMEM((S // RWKV_CHUNK, V7X_LANES, V7X_LANES), F32),
        ],
        compiler_params=_params("parallel", "parallel"),
        name="rwkv7_mixer",
    )(z, z, z, z, z, pp, pl2, wup, aup, g_up.astype(BF16))


RET_CHUNKS_PER_STEP = 2


def _ret_kernel(q_ref, k_ref, v_ref, g_ref, cos_ref, sin_ref, din_ref, dq_ref, dk_ref, dc_ref, o_ref, st_ref):
    S = q_ref.shape[1]
    C, DV = RET_CHUNK, RET_V_DIM
    lane = lax.broadcasted_iota(jnp.int32, (C, V7X_LANES), 1)
    first_half = (lane % RET_QK_DIM) < RET_QK_DIM // 2
    st_ref[...] = jnp.zeros_like(st_ref)

    in_head = [(lane >= h * RET_QK_DIM) & (lane < (h + 1) * RET_QK_DIM) for h in range(2)]
    NCH = RET_CHUNKS_PER_STEP
    units = [(u, h) for u in range(NCH) for h in range(2)]

    def step(i, carry):
        sls = [pl.ds(pl.multiple_of((i * NCH + u) * C, C), C) for u in range(NCH)]

        def rot(z, sl):
            swapped = jnp.where(first_half, pltpu.roll(z, V7X_LANES - RET_QK_DIM // 2, axis=1),
                                pltpu.roll(z, RET_QK_DIM // 2, axis=1))
            return z * cos_ref[sl, :] + swapped * sin_ref[sl, :]

        q = [rot(q_ref[0, sl, :], sl) for sl in sls]
        k = [rot(k_ref[0, sl, :], sl) * (RET_QK_DIM ** -0.5) for sl in sls]
        qm = [jnp.where(in_head[h], q[u], 0.0) for u, h in units]
        v = [v_ref[0, sls[u], h * DV:(h + 1) * DV] for u, h in units]
        inner = [_mm(qm[n], k[u], dims=_NT) * din_ref[h] for n, (u, h) in enumerate(units)]
        upd = [_mm((jnp.where(in_head[h], k[u], 0.0) * dk_ref[h]).T, v[n]) for n, (u, h) in enumerate(units)]
        local = [_mm(inner[n], v[n]) for n in range(len(units))]
        st = [st_ref[h] for h in range(2)]
        for n, (u, h) in enumerate(units):
            o = local[n] + _mm(qm[n], st[h]) * dq_ref[h]
            st[h] = upd[n] + dc_ref[h, 0:1, :] * st[h]
            d = o - jnp.mean(o, axis=1, keepdims=True)
            on = d * lax.rsqrt(jnp.mean(d * d, axis=1, keepdims=True) + RET_GN_EPS)
            gate = g_ref[0, sls[u], h * DV:(h + 1) * DV]
            o_ref[0, sls[u], h * DV:(h + 1) * DV] = gate * jax.nn.sigmoid(gate) * on
        st_ref[0], st_ref[1] = st
        return carry

    lax.fori_loop(0, S // C // NCH, step, 0)


def retention_mixer(z):
    B, S, _ = z.shape
    H, C, DK, DV = RET_HEADS, RET_CHUNK, RET_QK_DIM, RET_V_DIM
    assert S % C == 0 and 2 * DK == V7X_LANES and DV == V7X_LANES
    npair = H // 2
    half = DK // 2
    inv = ROPE_BASE ** (-jnp.arange(half, dtype=F32) / half)
    ang = jnp.arange(S, dtype=F32)[:, None] * inv
    cos = jnp.tile(jnp.cos(ang), (1, 4))
    sin = jnp.tile(jnp.concatenate([-jnp.sin(ang), jnp.sin(ang)], axis=1), (1, 2))
    log_g = jnp.asarray(np.log(1.0 - 2.0 ** (-5.0 - np.arange(H))), F32)
    n = jnp.arange(C, dtype=F32)
    diff = n[:, None] - n[None, :]
    d_in = jnp.where(diff >= 0, jnp.exp(jnp.maximum(diff, 0.0) * log_g[:, None, None]), 0.0)
    lanes = lambda t: jnp.broadcast_to(t[..., None], t.shape + (V7X_LANES,))
    d_q = lanes(jnp.exp((n + 1.0) * log_g[:, None]))
    d_k = lanes(jnp.exp((C - 1.0 - n) * log_g[:, None]))
    d_c = lanes(jnp.broadcast_to(jnp.exp(C * log_g)[:, None], (H, 8)))
    qk_tiles = H * DK // V7X_LANES
    return pl.pallas_call(
        _ret_kernel,
        grid=(B, npair),
        in_specs=[
            pl.BlockSpec((1, S, V7X_LANES), lambda b, p: (b, 0, p)),
            pl.BlockSpec((1, S, V7X_LANES), lambda b, p: (b, 0, qk_tiles + p)),
            pl.BlockSpec((1, S, 2 * DV), lambda b, p: (b, 0, 2 * qk_tiles * V7X_LANES // (2 * DV) + p)),
            pl.BlockSpec((1, S, 2 * DV), lambda b, p: (b, 0, (2 * qk_tiles * V7X_LANES + H * DV) // (2 * DV) + p)),
            pl.BlockSpec((S, V7X_LANES), lambda b, p: (0, 0)),
            pl.BlockSpec((S, V7X_LANES), lambda b, p: (0, 0)),
            pl.BlockSpec((2, C, C), lambda b, p: (p, 0, 0)),
            pl.BlockSpec((2, C, V7X_LANES), lambda b, p: (p, 0, 0)),
            pl.BlockSpec((2, C, V7X_LANES), lambda b, p: (p, 0, 0)),
            pl.BlockSpec((2, 8, V7X_LANES), lambda b, p: (p, 0, 0)),
        ],
        out_specs=pl.BlockSpec((1, S, 2 * DV), lambda b, p: (b, 0, p)),
        out_shape=jax.ShapeDtypeStruct((B, S, H * DV), F32),
        scratch_shapes=[pltpu.VMEM((2, V7X_LANES, DV), F32)],
        compiler_params=_params("parallel", "parallel"),
        name="retention_mixer",
    )(z, z, z, z, cos, sin, d_in, d_q, d_k, d_c)


def _even_mixer(x, g_norm, w_in, shift_mu, w0, w_up, a0, a_up, g_up, k_k, k_a, r_k, ln_g, ln_b):
    B, S, D = x.shape
    z = norm_matmul(x.reshape(B * S, D), g_norm, w_in.astype(BF16)).reshape(B, S, -1)
    o_a = moba_attention(z)
    o_b = rwkv7_mixer(z, shift_mu, w0, w_up, a0, a_up, g_up, k_k, k_a, r_k, ln_g, ln_b)
    return o_a, o_b


def _odd_mixer(x, g_norm, w_in, pe_k, w1_k, w2_k, pe_v, w1_v, w2_v):
    B, S, D = x.shape
    perm, col = _odd_layout()
    w_p = jnp.take(jnp.pad(w_in, ((0, 0), (0, 1))), perm, axis=1).astype(BF16)
    z = norm_matmul(x.reshape(B * S, D), g_norm, w_p).reshape(B, S, -1)
    o_c = retention_mixer(z)
    cmp_kv, cmp_vk = nsa_compress(z, col["kc"], col["vc"], pe_k, w1_k, w2_k, pe_v, w1_v, w2_v)
    o_d = nsa_attention(z, cmp_kv, cmp_vk, col["nq"], col["slc"], col["win"], col["gate"])
    return o_c, o_d


def _odd_layout():
    G, Dh = NSA_KV_GROUPS, HEAD_DIM
    sizes = (RET_HEADS * RET_QK_DIM, RET_HEADS * RET_QK_DIM, RET_HEADS * RET_V_DIM, RET_HEADS * RET_V_DIM,
             NSA_HEADS * Dh) + (G * Dh,) * 6 + (3 * NSA_HEADS,)
    off = np.concatenate([[0], np.cumsum(sizes)])
    rq, rk, rv, rg, nq, kc, vc, ks, vs, kw, vw, ng = off[:-1]
    n_in = int(off[-1])
    pair = lambda a, b: np.concatenate([np.concatenate([a + g * Dh + np.arange(Dh), b + g * Dh + np.arange(Dh)])
                                        for g in range(G)])
    perm = np.concatenate([np.arange(ks), pair(ks, vs), pair(kw, vw), ng + np.arange(3 * NSA_HEADS)])
    n_pad = -(-len(perm) // (6 * V7X_MXU_DIM)) * 6 * V7X_MXU_DIM
    perm = np.concatenate([perm, np.full(n_pad - len(perm), n_in)]).astype(np.int32)
    col = {"nq": int(nq), "kc": int(kc), "vc": int(vc), "slc": int(ks), "win": int(ks) + 2 * G * Dh,
           "gate": int(ks) + 4 * G * Dh}
    return perm, col


def kernel(x, mix_norm, ffn_norm, even_w_in, even_shift_mu, even_w0, even_w_up, even_a0, even_a_up, even_g_up, even_k_k, even_k_a, even_r_k, even_ln_g, even_ln_b, even_w_out, odd_w_in, odd_cmp_pe_k, odd_cmp_w1_k, odd_cmp_w2_k, odd_cmp_pe_v, odd_cmp_w1_v, odd_cmp_w2_v, odd_w_out, ffn_w1, ffn_w3, ffn_w2, final_norm):
    B, S, D = x.shape
    depth = mix_norm.shape[0]
    for layer in range(depth):
        i = layer // 2
        if layer % 2 == 0:
            o1, o2 = _even_mixer(x, mix_norm[layer], even_w_in[i], even_shift_mu[i], even_w0[i], even_w_up[i],
                                 even_a0[i], even_a_up[i], even_g_up[i], even_k_k[i], even_k_a[i], even_r_k[i],
                                 even_ln_g[i], even_ln_b[i])
            w_out = even_w_out[i]
        else:
            o1, o2 = _odd_mixer(x, mix_norm[layer], odd_w_in[i], odd_cmp_pe_k[i], odd_cmp_w1_k[i], odd_cmp_w2_k[i],
                                odd_cmp_pe_v[i], odd_cmp_w1_v[i], odd_cmp_w2_v[i])
            w_out = odd_w_out[i]
        T = B * S
        x2 = out_proj_residual(o1.reshape(T, -1), o2.reshape(T, -1), w_out.astype(BF16), x.reshape(T, D))
        x2 = ffn_residual(x2, ffn_norm[layer], ffn_w1[layer].astype(BF16), ffn_w3[layer].astype(BF16),
                          ffn_w2[layer].astype(BF16), final_norm if layer == depth - 1 else None)
        x = x2.reshape(B, S, D)
    return x
```

```python
import functools

import jax
import jax.numpy as jnp
import numpy as np
from jax import lax
from jax.experimental import pallas as pl
from jax.experimental.pallas import tpu as pltpu

F32 = jnp.float32
BF16 = jnp.bfloat16

V7X_LANES = 128
V7X_MXU_DIM = 256
V7X_VMEM_BYTES = 64 * 1024 * 1024
VMEM_LIMIT = V7X_VMEM_BYTES * 7 // 8

NORM_EPS = 1e-6
HEAD_DIM = 64

MOBA_BLOCK = 256
MOBA_TOPK = 3
RWKV_HEADS = 16
RWKV_GN_EPS = 6.4e-4

RET_HEADS = 8
RET_QK_DIM = 64
RET_V_DIM = 128
RET_CHUNK = 128
RET_GN_EPS = 1e-6
ROPE_BASE = 10000.0
NSA_HEADS = 16
NSA_KV_GROUPS = 4
NSA_CMP_BLOCK = 32
NSA_CMP_STRIDE = 16
NSA_SLC_BLOCK = 64
NSA_SLC_TOPN = 16
NSA_WINDOW = 512


def _params(*semantics):
    return pltpu.CompilerParams(dimension_semantics=semantics, vmem_limit_bytes=VMEM_LIMIT)


def _rms(x, g):
    return x * lax.rsqrt(jnp.mean(x * x, axis=-1, keepdims=True) + NORM_EPS) * g


def _norm_matmul_kernel(x_ref, g_ref, w_ref, o_ref):
    x = x_ref[...]
    scale = lax.rsqrt(jnp.mean(x * x, axis=-1, keepdims=True) + NORM_EPS)
    o_ref[...] = jnp.dot((x * g_ref[...]).astype(BF16), w_ref[...], preferred_element_type=F32) * scale


def _proj_tile(n):
    assert n % V7X_MXU_DIM == 0
    k = n // V7X_MXU_DIM
    return V7X_MXU_DIM * max(d for d in range(1, 7) if k % d == 0)


def norm_matmul(x, g, w, *, tm=512):
    T, D = x.shape
    N = w.shape[1]
    tn = _proj_tile(N)
    assert T % tm == 0 and N % tn == 0
    return pl.pallas_call(
        _norm_matmul_kernel,
        grid=(N // tn, T // tm),
        in_specs=[
            pl.BlockSpec((tm, D), lambda j, i: (i, 0)),
            pl.BlockSpec((1, D), lambda j, i: (0, 0)),
            pl.BlockSpec((D, tn), lambda j, i: (0, j)),
        ],
        out_specs=pl.BlockSpec((tm, tn), lambda j, i: (i, j)),
        out_shape=jax.ShapeDtypeStruct((T, N), F32),
        compiler_params=_params("parallel", "parallel"),
        name="norm_matmul",
    )(x, g.reshape(1, D), w)


def _out_proj_kernel(a_ref, b_ref, wa_ref, wb_ref, x_ref, o_ref):
    acc = jnp.dot(a_ref[...].astype(BF16), wa_ref[...], preferred_element_type=F32)
    acc += jnp.dot(b_ref[...].astype(BF16), wb_ref[...], preferred_element_type=F32)
    o_ref[...] = x_ref[...] + acc


def out_proj_residual(a, b, w, x, *, tm=512):
    T, D = x.shape
    Ka, Kb = a.shape[1], b.shape[1]
    assert T % tm == 0 and w.shape == (Ka + Kb, D) and Ka == Kb
    return pl.pallas_call(
        _out_proj_kernel,
        grid=(T // tm,),
        in_specs=[
            pl.BlockSpec((tm, Ka), lambda i: (i, 0)),
            pl.BlockSpec((tm, Kb), lambda i: (i, 0)),
            pl.BlockSpec((Ka, D), lambda i: (0, 0)),
            pl.BlockSpec((Kb, D), lambda i: (1, 0)),
            pl.BlockSpec((tm, D), lambda i: (i, 0)),
        ],
        out_specs=pl.BlockSpec((tm, D), lambda i: (i, 0)),
        out_shape=jax.ShapeDtypeStruct((T, D), F32),
        compiler_params=_params("parallel"),
        name="out_proj_residual",
    )(a, b, w, w, x)


def _ffn_kernel(x_ref, g_ref, w1_ref, w3_ref, w2_ref, gf_ref, o_ref, h_ref, acc_ref, *, final_norm):
    j = pl.program_id(1)

    @pl.when(j == 0)
    def _():
        h_ref[...] = _rms(x_ref[...], g_ref[...]).astype(BF16)
        acc_ref[...] = jnp.zeros_like(acc_ref)

    h = h_ref[...]
    a = jnp.dot(h, w1_ref[...], preferred_element_type=F32)
    b = jnp.dot(h, w3_ref[...], preferred_element_type=F32)
    act = (a * jax.nn.sigmoid(a) * b).astype(BF16)
    acc_ref[...] += jnp.dot(act, w2_ref[...], preferred_element_type=F32)

    @pl.when(j == pl.num_programs(1) - 1)
    def _():
        y = x_ref[...] + acc_ref[...]
        if final_norm:
            y = _rms(y, gf_ref[...])
        o_ref[...] = y


def ffn_residual(x, g, w1, w3, w2, g_final=None, *, tm=512, tf=512):
    T, D = x.shape
    Fh = w1.shape[1]
    assert T % tm == 0 and Fh % tf == 0
    final_norm = g_final is not None
    gf = (g_final if final_norm else g).reshape(1, D)
    return pl.pallas_call(
        functools.partial(_ffn_kernel, final_norm=final_norm),
        grid=(T // tm, Fh // tf),
        in_specs=[
            pl.BlockSpec((tm, D), lambda i, j: (i, 0)),
            pl.BlockSpec((1, D), lambda i, j: (0, 0)),
            pl.BlockSpec((D, tf), lambda i, j: (0, j)),
            pl.BlockSpec((D, tf), lambda i, j: (0, j)),
            pl.BlockSpec((tf, D), lambda i, j: (j, 0)),
            pl.BlockSpec((1, D), lambda i, j: (0, 0)),
        ],
        out_specs=pl.BlockSpec((tm, D), lambda i, j: (i, 0)),
        out_shape=jax.ShapeDtypeStruct((T, D), F32),
        scratch_shapes=[pltpu.VMEM((tm, D), BF16), pltpu.VMEM((tm, D), F32)],
        compiler_params=_params("parallel", "arbitrary"),
        name="ffn_residual",
    )(x, g.reshape(1, D), w1, w3, w2, gf)


NEG_BIG = -1e30
_NT = (((1,), (1,)), ((), ()))


def _flash_steps(qas, kas, vas, masks, m_prev, acc_prev):
    hs = range(len(qas))
    s = [lax.dot_general(qas[h], kas[h], _NT, preferred_element_type=F32) for h in hs]
    s = [s[h] if masks[h] is None else jnp.where(masks[h], s[h], NEG_BIG) for h in hs]
    m_new = [jnp.maximum(m_prev[h], jnp.max(s[h], axis=1, keepdims=True)) for h in hs]
    alpha = [jnp.exp2(m_prev[h] - m_new[h]) for h in hs]
    p = [jnp.exp2(s[h] - m_new[h]) for h in hs]
    pv = [jnp.dot(p[h].astype(BF16), vas[h], preferred_element_type=F32) for h in hs]
    return m_new, [alpha[h] * acc_prev[h] + pv[h] for h in hs]


def _augment_q(q_log2, in_head, keep_t, odd):
    nblk, tq = keep_t.shape
    bias_t = jnp.where(keep_t, 0.0, NEG_BIG)
    bias = jnp.concatenate([bias_t, jnp.zeros((V7X_LANES - nblk, tq), F32)], axis=0).T
    if not odd:
        bias = pltpu.roll(bias, HEAD_DIM, axis=1)
    return jnp.where(in_head, q_log2, bias).astype(BF16)


def _key_value_tiles(k, v, blk, lane):
    low = lane < HEAD_DIM
    hot_e = 0.0 if blk is None else jnp.where(lane - HEAD_DIM == blk, 1.0, 0.0)
    hot_o = 0.0 if blk is None else jnp.where(lane == blk, 1.0, 0.0)
    k_e, k_o = jnp.where(low, k, hot_e), jnp.where(low, hot_o, k)
    v_e, v_o = jnp.where(low, v, jnp.where(lane == HEAD_DIM, 1.0, 0.0)), jnp.where(low, jnp.where(lane == 0, 1.0, 0.0), v)
    return [t.astype(BF16) for t in (k_e, k_o, v_e, v_o)]


def _normalise(acc, lane, odd):
    return acc / jnp.sum(jnp.where(lane == (0 if odd else HEAD_DIM), acc, 0.0), axis=1, keepdims=True)


LOG2E = 1.4426950408889634


MOBA_HEADS_PER_STEP = 8


def _split_bf16(x):
    hi = x.astype(BF16)
    return hi, (x - hi.astype(F32)).astype(BF16)


def _moba_kernel(q_ref, k_ref, v_ref, o_ref, ka_ref, va_ref, km_ref, acc_ref):
    L = MOBA_BLOCK
    S = k_ref.shape[1]
    nb = S // L
    HP = MOBA_HEADS_PER_STEP
    qi = pl.program_id(2)
    lane = lax.broadcasted_iota(jnp.int32, (L, V7X_LANES), 1)
    lanes_of = lambda ref, pp: ref[0, :, pp * V7X_LANES:(pp + 1) * V7X_LANES]

    @pl.when(qi == 0)
    def _():
        lane_s = lax.broadcasted_iota(jnp.int32, (S, V7X_LANES), 1)
        blk = lax.broadcasted_iota(jnp.int32, (S, V7X_LANES), 0) // L
        for pp in range(HP // 2):
            k = lanes_of(k_ref, pp)
            (ka_ref[2 * pp], ka_ref[2 * pp + 1], va_ref[2 * pp], va_ref[2 * pp + 1]) = _key_value_tiles(
                k, lanes_of(v_ref, pp), blk, lane_s)
            km_ref[pp] = jnp.concatenate(_split_bf16(jnp.mean(k.reshape(nb, L, V7X_LANES), axis=1)), axis=0)

    row = lax.broadcasted_iota(jnp.int32, (L, L), 0)
    col = lax.broadcasted_iota(jnp.int32, (L, L), 1)
    causal = col <= row
    jrow = lax.broadcasted_iota(jnp.int32, (nb, L), 0)
    past = jrow < qi
    qas = []
    for pp in range(HP // 2):
        q = lanes_of(q_ref, pp) * (HEAD_DIM ** -0.5)
        q_hi, q_lo = _split_bf16(q)
        km = km_ref[pp]
        for e in range(2):
            in_head = (lane >= e * HEAD_DIM) & (lane < (e + 1) * HEAD_DIM)
            zero = jnp.zeros_like(q_hi)
            g1 = lax.dot_general(km, jnp.where(in_head, q_hi, zero), _NT, preferred_element_type=F32)
            g2 = lax.dot_general(km, jnp.where(in_head, q_lo, zero), _NT, preferred_element_type=F32)
            gate = g1[0:nb] + g1[nb:2 * nb] + g2[0:nb]
            keep = jrow == qi
            for n in range(nb):
                g_n = gate[n:n + 1, :]
                beats = (gate > g_n) | ((gate == g_n) & (jrow < n))
                rank = jnp.sum(jnp.where(past & beats, 1.0, 0.0), axis=0, keepdims=True)
                keep = keep | ((jrow == n) & (rank < MOBA_TOPK) & past)
            qas.append(_augment_q(q * LOG2E, in_head, keep, odd=e == 1))

    tiles = lambda start: ([ka_ref[h, pl.ds(start, L), :] for h in range(HP)],
                           [va_ref[h, pl.ds(start, L), :] for h in range(HP)])

    m, acc = _flash_steps(qas, *tiles(pl.multiple_of(qi * L, L)), [causal] * HP,
                          [jnp.full((L, 1), NEG_BIG, F32)] * HP, [jnp.zeros((L, V7X_LANES), F32)] * HP)
    for h in range(HP):
        acc_ref[h] = acc[h]

    def body(n, carry):
        m2, acc2 = _flash_steps(qas, *tiles(pl.multiple_of(n * L, L)), [None] * HP, list(carry),
                                [acc_ref[h] for h in range(HP)])
        for h in range(HP):
            acc_ref[h] = acc2[h]
        return tuple(m2)

    lax.fori_loop(0, qi, body, tuple(m))
    for pp in range(HP // 2):
        o_ref[0, :, pp * V7X_LANES:(pp + 1) * V7X_LANES] = jnp.where(
            lane < HEAD_DIM, _normalise(acc_ref[2 * pp], lane, False), _normalise(acc_ref[2 * pp + 1], lane, True))


def moba_attention(z, *, n_heads=16):
    B, S, _ = z.shape
    L, HP = MOBA_BLOCK, MOBA_HEADS_PER_STEP
    W = HP * HEAD_DIM
    nb = S // L
    assert S % L == 0 and n_heads % HP == 0 and W % V7X_LANES == 0 and 2 * nb <= 16
    ngrp = n_heads // HP
    return pl.pallas_call(
        _moba_kernel,
        grid=(B, ngrp, S // L),
        in_specs=[
            pl.BlockSpec((1, L, W), lambda b, p, i: (b, i, p)),
            pl.BlockSpec((1, S, W), lambda b, p, i: (b, 0, ngrp + p)),
            pl.BlockSpec((1, S, W), lambda b, p, i: (b, 0, 2 * ngrp + p)),
        ],
        out_specs=pl.BlockSpec((1, L, W), lambda b, p, i: (b, i, p)),
        out_shape=jax.ShapeDtypeStruct((B, S, n_heads * HEAD_DIM), F32),
        scratch_shapes=[
            pltpu.VMEM((HP, S, V7X_LANES), BF16),
            pltpu.VMEM((HP, S, V7X_LANES), BF16),
            pltpu.VMEM((HP // 2, 2 * nb, V7X_LANES), BF16),
            pltpu.VMEM((HP, L, V7X_LANES), F32),
        ],
        compiler_params=_params("parallel", "parallel", "arbitrary"),
        name="moba_attention",
    )(z, z, z)


NSA_TQ = 256
NSA_GROUPS_PER_STEP = 2
BIG = 3.0e38


def _gelu_tanh(x):
    return 0.5 * x * (1.0 + jnp.tanh(0.7978845608028654 * (x + 0.044715 * x * x * x)))


def _nsa_compress_kernel(xk0_ref, xk1_ref, xv0_ref, xv1_ref, pek_ref, pev_ref, w1k_ref, w1v_ref, w2k_ref, w2v_ref,
                         o1_ref, o2_ref):
    G, Lc, st = NSA_KV_GROUPS, NSA_CMP_BLOCK, NSA_CMP_STRIDE
    nrow = xk0_ref.shape[1] // st
    lane = lax.broadcasted_iota(jnp.int32, (nrow, G * HEAD_DIM), 1)

    def hidden(x_refs, pe_ref, w1_ref):
        acc = [jnp.zeros((G * nrow, V7X_LANES), F32) for _ in range(Lc // st)]
        for l in range(Lc):
            u, m = divmod(l, st)
            x = jnp.concatenate([r[0, pl.ds(m, nrow, stride=st), :] for r in x_refs], axis=1) + pe_ref[l:l + 1, :]
            xs = jnp.concatenate(
                [jnp.where((lane >= g * HEAD_DIM) & (lane < (g + 1) * HEAD_DIM), x, 0.0) for g in range(G)],
                axis=0).astype(BF16)
            acc[u] = acc[u] + jnp.dot(xs, w1_ref[l], preferred_element_type=F32)
        nxt = jnp.concatenate([pltpu.roll(acc[1][g * nrow:(g + 1) * nrow], nrow - 1, axis=0) for g in range(G)],
                              axis=0)
        return _gelu_tanh(acc[0] + nxt).astype(BF16)

    hk = hidden((xk0_ref, xk1_ref), pek_ref, w1k_ref)
    hv = hidden((xv0_ref, xv1_ref), pev_ref, w1v_ref)
    kc = jnp.dot(hk, w2k_ref[...], preferred_element_type=F32)
    vc = jnp.dot(hv, w2v_ref[...], preferred_element_type=F32)
    kv = kc + vc
    vk = pltpu.roll(kv, HEAD_DIM, axis=1)
    for g in range(G):
        o1_ref[0, :, g * V7X_LANES:(g + 1) * V7X_LANES] = kv[g * nrow:(g + 1) * nrow]
        o2_ref[0, :, g * V7X_LANES:(g + 1) * V7X_LANES] = vk[g * nrow:(g + 1) * nrow]


def nsa_compress(z, col_k, col_v, pe_k, w1_k, w2_k, pe_v, w1_v, w2_v):
    B, S, _ = z.shape
    G, Lc, st = NSA_KV_GROUPS, NSA_CMP_BLOCK, NSA_CMP_STRIDE
    GW = G * HEAD_DIM
    nrow = S // st
    hid = w1_k.shape[1]
    assert hid == V7X_LANES and col_k % GW == 0 and col_v % GW == 0
    tile_pe = lambda pe: jnp.tile(pe, (1, G))
    tile_w1 = lambda w: jnp.tile(w.reshape(Lc, 1, HEAD_DIM, hid), (1, G, 1, 1)).reshape(Lc, GW, hid).astype(BF16)
    w2k = jnp.pad(w2_k, ((0, 0), (0, HEAD_DIM))).astype(BF16)
    w2v = jnp.pad(w2_v, ((0, 0), (HEAD_DIM, 0))).astype(BF16)
    const = lambda shape: pl.BlockSpec(shape, lambda b: (0,) * len(shape))
    out = jax.ShapeDtypeStruct((B, nrow, G * V7X_LANES), F32)
    return pl.pallas_call(
        _nsa_compress_kernel,
        grid=(B,),
        in_specs=[
            pl.BlockSpec((1, S, V7X_LANES), lambda b: (b, 0, col_k // V7X_LANES)),
            pl.BlockSpec((1, S, V7X_LANES), lambda b: (b, 0, col_k // V7X_LANES + 1)),
            pl.BlockSpec((1, S, V7X_LANES), lambda b: (b, 0, col_v // V7X_LANES)),
            pl.BlockSpec((1, S, V7X_LANES), lambda b: (b, 0, col_v // V7X_LANES + 1)),
            const((Lc, GW)), const((Lc, GW)),
            const((Lc, GW, hid)), const((Lc, GW, hid)),
            const((hid, V7X_LANES)), const((hid, V7X_LANES)),
        ],
        out_specs=[pl.BlockSpec((1, nrow, G * V7X_LANES), lambda b: (b, 0, 0))] * 2,
        out_shape=[out, out],
        compiler_params=_params("parallel"),
        name="nsa_compress",
    )(z, z, z, z, tile_pe(pe_k), tile_pe(pe_v), tile_w1(w1_k), tile_w1(w1_v), w2k, w2v)


def _nsa_kernel(q_ref, c1_ref, c2_ref, s_ref, w_ref, g_ref, ovt_ref, o_ref, sk_ref, sv_ref, wk_ref, wv_ref, acc_ref):
    TQ = NSA_TQ
    S = s_ref.shape[1]
    R = NSA_HEADS // NSA_KV_GROUPS
    NG = NSA_GROUPS_PER_STEP
    NH = NG * R
    grp0 = pl.program_id(1) * NG
    qi = pl.program_id(2)
    lane = lax.broadcasted_iota(jnp.int32, (TQ, V7X_LANES), 1)
    tile_of = lambda ref, t: ref[0, :, t * V7X_LANES:(t + 1) * V7X_LANES]

    @pl.when(qi == 0)
    def _():
        lane_s = lax.broadcasted_iota(jnp.int32, (S, V7X_LANES), 1)
        blk = lax.broadcasted_iota(jnp.int32, (S, V7X_LANES), 0) // NSA_SLC_BLOCK
        for gg in range(NG):
            for src, k_ref, v_ref, hot in ((s_ref, sk_ref, sv_ref, blk), (w_ref, wk_ref, wv_ref, None)):
                kv = tile_of(src, gg)
                vk = pltpu.roll(kv, HEAD_DIM, axis=1)
                k_ref[2 * gg], k_ref[2 * gg + 1], v_ref[2 * gg], v_ref[2 * gg + 1] = _key_value_tiles(
                    jnp.where(lane_s < HEAD_DIM, kv, vk), jnp.where(lane_s < HEAD_DIM, vk, kv), hot, lane_s)

    q0 = pl.multiple_of(qi * TQ, TQ)
    row = lax.broadcasted_iota(jnp.int32, (TQ, TQ), 0)
    col = lax.broadcasted_iota(jnp.int32, (TQ, TQ), 1)
    causal = col <= row
    qi_mat = jnp.zeros((TQ, TQ), jnp.int32) + qi
    t_abs = q0 + lax.broadcasted_iota(jnp.int32, (TQ, V7X_LANES), 0)
    even_lanes = lane < HEAD_DIM

    cmask = lane * NSA_CMP_STRIDE + (NSA_CMP_BLOCK - 1) <= t_abs
    nblk = s_ref.shape[1] // NSA_SLC_BLOCK
    jrow = lax.broadcasted_iota(jnp.int32, (nblk, TQ), 0)
    own = (q0 + lax.broadcasted_iota(jnp.int32, (nblk, TQ), 1)) // NSA_SLC_BLOCK
    ovt = ovt_ref[...]
    head_lanes = [even_lanes if h % 2 == 0 else ~even_lanes for h in range(NH)]
    tiles = [tile_of(q_ref, h // 2) * (HEAD_DIM ** -0.5) for h in range(NH)]
    o_cmp, qas = [], []
    for gg in range(NG):
        c_kv_b, c_vk_b = tile_of(c1_ref, gg).astype(BF16), tile_of(c2_ref, gg).astype(BF16)
        p_sum = jnp.zeros((TQ, V7X_LANES), F32)
        for h in range(gg * R, (gg + 1) * R):
            qm = jnp.where(head_lanes[h], tiles[h], 0.0).astype(BF16)
            s = lax.dot_general(qm, c_kv_b if h % 2 == 0 else c_vk_b, _NT, preferred_element_type=F32)
            s = jnp.where(cmask, s, NEG_BIG)
            p = jnp.where(cmask, jnp.exp(s - jnp.max(s, axis=1, keepdims=True)), 0.0)
            den = jnp.sum(p, axis=1, keepdims=True)
            p = p / jnp.where(den > 0.0, den, 1.0)
            p_sum = p_sum + p
            o_cmp.append(jnp.dot(p.astype(BF16), c_vk_b if h % 2 == 0 else c_kv_b, preferred_element_type=F32))

        p_hi, p_lo = _split_bf16(p_sum)
        p_slc = (lax.dot_general(ovt, p_hi, _NT, preferred_element_type=F32)
                 + lax.dot_general(ovt, p_lo, _NT, preferred_element_type=F32))[0:nblk]
        score = jnp.where((jrow == own) | (jrow == 0), BIG, jnp.where(jrow > own, -BIG, p_slc))
        keep = jrow > nblk
        for j in range(nblk):
            s_j = score[j:j + 1, :]
            beats = (score > s_j) | ((score == s_j) & (jrow < j))
            rank = jnp.sum(jnp.where(beats, 1.0, 0.0), axis=0, keepdims=True)
            keep = keep | ((jrow == j) & (rank < NSA_SLC_TOPN) & (jrow <= own))
        qas += [_augment_q(tiles[h] * LOG2E, head_lanes[h], keep, odd=h % 2 == 1) for h in range(gg * R, (gg + 1) * R)]

    neg = [jnp.full((TQ, 1), NEG_BIG, F32)] * NH
    zacc = [jnp.zeros((TQ, V7X_LANES), F32)] * NH
    kv_index = [2 * (h // R) + h % 2 for h in range(NH)]

    def kv_blocks(k_ref, v_ref, start):
        return ([k_ref[kv_index[h], pl.ds(start, TQ), :] for h in range(NH)],
                [v_ref[kv_index[h], pl.ds(start, TQ), :] for h in range(NH)])

    m, acc = _flash_steps(qas, *kv_blocks(sk_ref, sv_ref, q0), [causal] * NH, neg, zacc)
    for h in range(NH):
        acc_ref[h] = acc[h]

    def body(kb, carry):
        m2, acc2 = _flash_steps(qas, *kv_blocks(sk_ref, sv_ref, pl.multiple_of(kb * TQ, TQ)), [None] * NH,
                                list(carry), [acc_ref[h] for h in range(NH)])
        for h in range(NH):
            acc_ref[h] = acc2[h]
        return tuple(m2)

    lax.fori_loop(0, qi, body, tuple(m))

    m, acc = _flash_steps(qas, *kv_blocks(wk_ref, wv_ref, q0), [causal] * NH, neg, zacc)
    m, acc = _flash_steps(qas, *kv_blocks(wk_ref, wv_ref, pl.multiple_of(jnp.maximum(qi - 1, 0) * TQ, TQ)),
                          [qi_mat >= 1] * NH, m, acc)
    m, acc = _flash_steps(qas, *kv_blocks(wk_ref, wv_ref, pl.multiple_of(jnp.maximum(qi - 2, 0) * TQ, TQ)),
                          [(col > row) & (qi_mat >= 2)] * NH, m, acc)

    gates = jax.nn.sigmoid(g_ref[0])
    outs = []
    for h in range(NH):
        o_slc = _normalise(acc_ref[h], lane, h % 2 == 1)
        o_win = _normalise(acc[h], lane, h % 2 == 1)
        c0 = (grp0 * R + h) * 3
        gate = lambda c: jnp.sum(jnp.where(lane == c, gates, 0.0), axis=1, keepdims=True)
        outs.append(gate(c0) * o_cmp[h] + gate(c0 + 1) * o_slc + gate(c0 + 2) * o_win)
    for p2 in range(NH // 2):
        o_ref[0, :, p2 * V7X_LANES:(p2 + 1) * V7X_LANES] = jnp.where(even_lanes, outs[2 * p2], outs[2 * p2 + 1])


def nsa_attention(z, cmp_kv, cmp_vk, col_q, col_slc, col_win, col_gate):
    B, S, _ = z.shape
    G, TQ, NG = NSA_KV_GROUPS, NSA_TQ, NSA_GROUPS_PER_STEP
    R = NSA_HEADS // G
    QW = NG * R * HEAD_DIM
    KW = NG * V7X_LANES
    ncmp = cmp_kv.shape[1]
    assert S % TQ == 0 and ncmp == V7X_LANES and S // NSA_SLC_BLOCK <= V7X_LANES and NSA_WINDOW == 2 * TQ
    assert G % NG == 0 and R % 2 == 0
    assert col_q % QW == 0 and col_slc % KW == 0 and col_win % KW == 0 and col_gate % V7X_LANES == 0
    nc = (S - NSA_CMP_BLOCK) // NSA_CMP_STRIDE + 1
    c_start = np.arange(V7X_LANES) * NSA_CMP_STRIDE
    s_start = np.arange(V7X_LANES) * NSA_SLC_BLOCK
    overlap = ((c_start[:, None] <= s_start[None, :] + NSA_SLC_BLOCK - 1)
               & (c_start[:, None] + NSA_CMP_BLOCK - 1 >= s_start[None, :])
               & (np.arange(V7X_LANES)[:, None] < nc) & (np.arange(V7X_LANES)[None, :] < S // NSA_SLC_BLOCK))
    const = lambda shape: pl.BlockSpec(shape, lambda b, g, i: (0,) * len(shape))
    return pl.pallas_call(
        _nsa_kernel,
        grid=(B, G // NG, S // TQ),
        in_specs=[
            pl.BlockSpec((1, TQ, QW), lambda b, g, i: (b, i, col_q // QW + g)),
            pl.BlockSpec((1, ncmp, KW), lambda b, g, i: (b, 0, g)),
            pl.BlockSpec((1, ncmp, KW), lambda b, g, i: (b, 0, g)),
            pl.BlockSpec((1, S, KW), lambda b, g, i: (b, 0, col_slc // KW + g)),
            pl.BlockSpec((1, S, KW), lambda b, g, i: (b, 0, col_win // KW + g)),
            pl.BlockSpec((1, TQ, V7X_LANES), lambda b, g, i: (b, i, col_gate // V7X_LANES)),
            const((V7X_LANES, V7X_LANES)),
        ],
        out_specs=pl.BlockSpec((1, TQ, QW), lambda b, g, i: (b, i, g)),
        out_shape=jax.ShapeDtypeStruct((B, S, NSA_HEADS * HEAD_DIM), F32),
        scratch_shapes=[pltpu.VMEM((2 * NG, S, V7X_LANES), BF16)] * 4 + [
            pltpu.VMEM((NG * R, TQ, V7X_LANES), F32),
        ],
        compiler_params=_params("parallel", "parallel", "arbitrary"),
        name="nsa_attention",
    )(z, cmp_kv, cmp_vk, z, z, z, jnp.asarray(overlap.T, BF16))


RWKV_CHUNK = 64
RWKV_ROWS = 256
RWKV_INTERLEAVE = 8


def _mm(a, b, dims=None):
    dims = dims or (((1,), (0,)), ((), ()))
    return lax.dot_general(a.astype(BF16), b.astype(BF16), dims, preferred_element_type=F32)


def _mm3(a, b):
    (a_hi, a_lo), (b_hi, b_lo) = _split_bf16(a), _split_bf16(b)
    return _mm(a_hi, b_hi) + (_mm(a_hi, b_lo) + _mm(a_lo, b_hi))


def _mm_onehot(a01, b):
    hi = b.astype(BF16)
    mid, lo = _split_bf16(b - hi.astype(F32))
    return _mm(a01, hi) + (_mm(a01, mid) + _mm(a01, lo))


def _head_sum(x, low):
    s0 = jnp.sum(jnp.where(low, x, 0.0), axis=1, keepdims=True)
    s1 = jnp.sum(jnp.where(low, 0.0, x), axis=1, keepdims=True)
    return jnp.where(low, s0, s1)


def _rwkv_kernel(r_ref, k_ref, v_ref, lo_ref, glo_ref, pp_ref, pl_ref, wup_ref, aup_ref, gup_ref, o_ref,
                 rs, ws, ks, vs, als, bes, gs, ys, hs, rqs, ms, ns):
    S = r_ref.shape[1]
    C, RB = RWKV_CHUNK, RWKV_ROWS
    pp = pp_ref[...]
    mu_r, mu_k, mu_v, w0, a0, k_k, k_a, r_k, ln_g, ln_b = [pp[i:i + 1, :] for i in range(10)]
    mu_lo, mu_g = pl_ref[0:1, :], pl_ref[1:2, :]
    low = lax.broadcasted_iota(jnp.int32, (RB, V7X_LANES), 1) < HEAD_DIM
    first = lax.broadcasted_iota(jnp.int32, (RB, V7X_LANES), 0) == 0

    def prologue(i, c):
        t0 = pl.multiple_of(i * RB, RB)
        tp = jnp.maximum(t0 - 1, 0)
        keep = jnp.where(i > 0, 1.0, 0.0)

        def shifted(ref, mu):
            x = ref[0, pl.ds(t0, RB), :]
            prev = jnp.where(first, ref[0, pl.ds(tp, 1), :] * keep, pltpu.roll(x, 1, axis=0))
            return x + (prev - x) * mu

        r, k, v = shifted(r_ref, mu_r), shifted(k_ref, mu_k), shifted(v_ref, mu_v)
        lo, glo = shifted(lo_ref, mu_lo), shifted(glo_ref, mu_g)
        wp = -(w0 + _mm(jnp.tanh(lo), wup_ref[...]))
        w = -(jnp.maximum(wp, 0.0) + jnp.log(1.0 + jnp.exp(-jnp.abs(wp)))) - 0.5
        a = jax.nn.sigmoid(a0 + _mm(lo, aup_ref[...]))
        kk = k * k_k
        kk = kk / jnp.maximum(jnp.sqrt(_head_sum(kk * kk, low)), 1e-12)
        k2 = k * (1.0 + (a - 1.0) * k_a)
        rs[pl.ds(t0, RB), :] = r
        ws[pl.ds(t0, RB), :] = -jnp.exp(w)
        ks[pl.ds(t0, RB), :] = k2
        vs[pl.ds(t0, RB), :] = v
        als[pl.ds(t0, RB), :] = -kk
        bes[pl.ds(t0, RB), :] = kk * a
        gs[pl.ds(t0, RB), :] = _mm(jax.nn.sigmoid(glo), gup_ref[...])
        o_ref[0, pl.ds(t0, RB), :] = _head_sum(r * k2 * r_k, low) * v
        return c

    lax.fori_loop(0, S // RB, prologue, 0)

    W2 = 2 * C
    row = lax.broadcasted_iota(jnp.int32, (W2, W2), 0)
    col = lax.broadcasted_iota(jnp.int32, (W2, W2), 1)
    t_idx, s_idx = row % C, col % C
    top, left = row < C, col < C
    same = top == left
    eye = jnp.where(row == col, 1.0, 0.0)
    tri = jnp.where(lax.broadcasted_iota(jnp.int32, (C, C), 1) <= lax.broadcasted_iota(jnp.int32, (C, C), 0), 1.0, 0.0)
    low_c = lax.broadcasted_iota(jnp.int32, (C, V7X_LANES), 1) < HEAD_DIM
    fold = lambda x: x[0:C] + x[C:W2]
    stack_heads = lambda x: jnp.concatenate([jnp.where(low_c, x, 0.0), jnp.where(low_c, 0.0, x)], axis=0)
    block_diag = lambda x: jnp.where(top, jnp.where(left, x, 0.0), jnp.where(left, 0.0, pltpu.roll(x, C, axis=1)))

    rows = lambda c: pl.ds(c * C if isinstance(c, int) else pl.multiple_of(c * C, C), C)

    def advance(c, H):
        ys[rows(c), :] += _mm3(rqs[c], H)
        return _mm3(ms[c], H) + ns[c]

    def transfers(i, lagged):
        each = lambda f, *xs: [f(*a) for a in zip(*xs)]
        cs = [i * RWKV_INTERLEAVE + u for u in range(RWKV_INTERLEAVE)]
        sls = [rows(c) for c in cs]
        state = [hs[...]] if lagged else None

        def lag(hook):
            if lagged:
                for u in range(hook * RWKV_INTERLEAVE // 8, (hook + 1) * RWKV_INTERLEAVE // 8):
                    state[0] = advance(cs[u] - RWKV_INTERLEAVE, state[0])

        r, lw, k2, v, al, be = ([ref[sl, :] for sl in sls] for ref in (rs, ws, ks, vs, als, bes))
        logp = each(lambda x: _mm_onehot(tri, x), lw)
        lag(0)
        P = each(jnp.exp, logp)
        Pinv = each(lambda x: jnp.exp(-x), logp)
        At = each(lambda a_, lp, w_: a_ * jnp.exp(lp - w_), al, logp, lw)
        Rt, Bt, Kt = each(jnp.multiply, r, P), each(jnp.multiply, be, Pinv), each(jnp.multiply, k2, Pinv)
        PC = each(lambda p: p[C - 1:C, :], P)
        A_bd, R_bd = each(stack_heads, At), each(stack_heads, Rt)
        Yt = each(lambda b, k: jnp.concatenate([b, k], axis=0), Bt, Kt)
        A1 = each(lambda a, y: jnp.where(s_idx < t_idx, _mm(a, y, dims=_NT), 0.0), A_bd, Yt)
        A2 = each(lambda a, y: jnp.where(s_idx <= t_idx, _mm(a, y, dims=_NT), 0.0), R_bd, Yt)
        X, Arb = each(block_diag, A1), each(block_diag, A2)
        T = each(lambda x: eye + x, X)
        for it in range(5):
            X = each(lambda x: _mm(x, x), X)
            T = each(lambda t, x: t + _mm(t, x), T, X)
            lag(1 + it)
        V0 = each(lambda x: jnp.concatenate([jnp.zeros_like(x), x], axis=0), v)
        TA = each(_mm, T, A_bd)
        AkV = each(lambda a, x: jnp.where(same, _mm(a, x), 0.0), A1, V0)
        lag(6)
        U0 = each(_mm, T, AkV)
        lag(7)
        AR = each(lambda a, t, u: _mm(a, jnp.concatenate([t, u], axis=1)), Arb, TA, U0)
        ArkV = each(lambda a, x: jnp.where(same, _mm(a, x), 0.0), A2, V0)
        Mx = each(lambda b, p, t: _mm((b * p).T, fold(t)), Bt, PC, TA)
        Nx = each(lambda b, k, p, u, x: _mm(jnp.concatenate([b * p, k * p], axis=0).T,
                                            jnp.concatenate([fold(u), x], axis=0)), Bt, Kt, PC, U0, v)
        for u in range(RWKV_INTERLEAVE):
            ys[sls[u], :] = fold(AR[u][:, W2:2 * W2] + ArkV[u])
            rqs[cs[u]] = Rt[u] + fold(AR[u][:, 0:W2])
            ms[cs[u]] = eye * PC[u] + jnp.where(same, Mx[u], 0.0)
            ns[cs[u]] = jnp.where(same, Nx[u], 0.0)
        if lagged:
            hs[...] = state[0]

    def pipelined(i, carry):
        transfers(i, True)
        return carry

    def drain(c, carry):
        hs[...] = advance(c, hs[...])
        return carry

    assert 8 % RWKV_INTERLEAVE == 0
    hs[...] = jnp.zeros((W2, W2), F32)
    transfers(0, False)
    lax.fori_loop(1, S // C // RWKV_INTERLEAVE, pipelined, 0)
    lax.fori_loop(S // C - RWKV_INTERLEAVE, S // C, drain, 0)

    def epilogue(i, c):
        sl = pl.ds(pl.multiple_of(i * RB, RB), RB)
        y = ys[sl, :]
        d = y - _head_sum(y, low) * (1.0 / HEAD_DIM)
        var = _head_sum(d * d, low) * (1.0 / HEAD_DIM)
        yn = d * lax.rsqrt(var + RWKV_GN_EPS) * ln_g + ln_b
        o_ref[0, sl, :] = (yn + o_ref[0, sl, :]) * gs[sl, :]
        return c

    lax.fori_loop(0, S // RB, epilogue, 0)


def rwkv7_mixer(z, shift_mu, w0, w_up, a0, a_up, g_up, k_k, k_a, r_k, ln_g, ln_b):
    B, S, _ = z.shape
    CW = RWKV_HEADS * HEAD_DIM
    npair = CW // V7X_LANES
    base = 3 * CW // V7X_LANES
    lora = w_up.shape[0] + a_up.shape[0]
    assert lora == V7X_LANES and g_up.shape[0] == V7X_LANES and S % RWKV_ROWS == 0
    pp = jnp.stack([shift_mu[0:CW], shift_mu[CW:2 * CW], shift_mu[2 * CW:3 * CW], w0, a0, k_k, k_a,
                    r_k.reshape(CW), ln_g, ln_b])
    pp = jnp.pad(pp, ((0, 16 - pp.shape[0]), (0, 0)))
    pl2 = jnp.pad(shift_mu[3 * CW:].reshape(2, V7X_LANES), ((0, 6), (0, 0)))
    wup = jnp.pad(w_up, ((0, a_up.shape[0]), (0, 0)))
    aup = jnp.pad(a_up, ((w_up.shape[0], 0), (0, 0)))
    tile = lambda off: pl.BlockSpec((1, S, V7X_LANES), lambda b, p: (b, 0, base + off * npair + p))
    fixed = lambda off: pl.BlockSpec((1, S, V7X_LANES), lambda b, p: (b, 0, base + 3 * npair + off))
    seq = pltpu.VMEM((S, V7X_LANES), F32)
    return pl.pallas_call(
        _rwkv_kernel,
        grid=(B, npair),
        in_specs=[
            tile(0), tile(1), tile(2), fixed(0), fixed(1),
            pl.BlockSpec((16, V7X_LANES), lambda b, p: (0, p)),
            pl.BlockSpec((8, V7X_LANES), lambda b, p: (0, 0)),
            pl.BlockSpec((V7X_LANES, V7X_LANES), lambda b, p: (0, p)),
            pl.BlockSpec((V7X_LANES, V7X_LANES), lambda b, p: (0, p)),
            pl.BlockSpec((V7X_LANES, V7X_LANES), lambda b, p: (0, p)),
        ],
        out_specs=pl.BlockSpec((1, S, V7X_LANES), lambda b, p: (b, 0, p)),
        out_shape=jax.ShapeDtypeStruct((B, S, CW), F32),
        scratch_shapes=[seq] * 8 + [
            pltpu.VMEM((V7X_LANES, V7X_LANES), F32),
            pltpu.VMEM((S // RWKV_CHUNK, RWKV_CHUNK, V7X_LANES), F32),
            pltpu.VMEM((S // RWKV_CHUNK, V7X_LANES, V7X_LANES), F32),
            pltpu.VMEM((S // RWKV_CHUNK, V7X_LANES, V7X_LANES), F32),
        ],
        compiler_params=_params("parallel", "parallel"),
        name="rwkv7_mixer",
    )(z, z, z, z, z, pp, pl2, wup, aup, g_up.astype(BF16))


RET_CHUNKS_PER_STEP = 2


def _ret_kernel(q_ref, k_ref, v_ref, g_ref, cos_ref, sin_ref, din_ref, dq_ref, dk_ref, dc_ref, o_ref, st_ref):
    S = q_ref.shape[1]
    C, DV = RET_CHUNK, RET_V_DIM
    lane = lax.broadcasted_iota(jnp.int32, (C, V7X_LANES), 1)
    first_half = (lane % RET_QK_DIM) < RET_QK_DIM // 2
    st_ref[...] = jnp.zeros_like(st_ref)

    in_head = [(lane >= h * RET_QK_DIM) & (lane < (h + 1) * RET_QK_DIM) for h in range(2)]
    NCH = RET_CHUNKS_PER_STEP
    units = [(u, h) for u in range(NCH) for h in range(2)]

    def step(i, carry):
        sls = [pl.ds(pl.multiple_of((i * NCH + u) * C, C), C) for u in range(NCH)]

        def rot(z, sl):
            swapped = jnp.where(first_half, pltpu.roll(z, V7X_LANES - RET_QK_DIM // 2, axis=1),
                                pltpu.roll(z, RET_QK_DIM // 2, axis=1))
            return z * cos_ref[sl, :] + swapped * sin_ref[sl, :]

        q = [rot(q_ref[0, sl, :], sl) for sl in sls]
        k = [rot(k_ref[0, sl, :], sl) * (RET_QK_DIM ** -0.5) for sl in sls]
        qm = [jnp.where(in_head[h], q[u], 0.0) for u, h in units]
        v = [v_ref[0, sls[u], h * DV:(h + 1) * DV] for u, h in units]
        inner = [_mm(qm[n], k[u], dims=_NT) * din_ref[h] for n, (u, h) in enumerate(units)]
        upd = [_mm((jnp.where(in_head[h], k[u], 0.0) * dk_ref[h]).T, v[n]) for n, (u, h) in enumerate(units)]
        local = [_mm(inner[n], v[n]) for n in range(len(units))]
        st = [st_ref[h] for h in range(2)]
        for n, (u, h) in enumerate(units):
            o = local[n] + _mm(qm[n], st[h]) * dq_ref[h]
            st[h] = upd[n] + dc_ref[h, 0:1, :] * st[h]
            d = o - jnp.mean(o, axis=1, keepdims=True)
            on = d * lax.rsqrt(jnp.mean(d * d, axis=1, keepdims=True) + RET_GN_EPS)
            gate = g_ref[0, sls[u], h * DV:(h + 1) * DV]
            o_ref[0, sls[u], h * DV:(h + 1) * DV] = gate * jax.nn.sigmoid(gate) * on
        st_ref[0], st_ref[1] = st
        return carry

    lax.fori_loop(0, S // C // NCH, step, 0)


def retention_mixer(z):
    B, S, _ = z.shape
    H, C, DK, DV = RET_HEADS, RET_CHUNK, RET_QK_DIM, RET_V_DIM
    assert S % C == 0 and 2 * DK == V7X_LANES and DV == V7X_LANES
    npair = H // 2
    half = DK // 2
    inv = ROPE_BASE ** (-jnp.arange(half, dtype=F32) / half)
    ang = jnp.arange(S, dtype=F32)[:, None] * inv
    cos = jnp.tile(jnp.cos(ang), (1, 4))
    sin = jnp.tile(jnp.concatenate([-jnp.sin(ang), jnp.sin(ang)], axis=1), (1, 2))
    log_g = jnp.asarray(np.log(1.0 - 2.0 ** (-5.0 - np.arange(H))), F32)
    n = jnp.arange(C, dtype=F32)
    diff = n[:, None] - n[None, :]
    d_in = jnp.where(diff >= 0, jnp.exp(jnp.maximum(diff, 0.0) * log_g[:, None, None]), 0.0)
    lanes = lambda t: jnp.broadcast_to(t[..., None], t.shape + (V7X_LANES,))
    d_q = lanes(jnp.exp((n + 1.0) * log_g[:, None]))
    d_k = lanes(jnp.exp((C - 1.0 - n) * log_g[:, None]))
    d_c = lanes(jnp.broadcast_to(jnp.exp(C * log_g)[:, None], (H, 8)))
    qk_tiles = H * DK // V7X_LANES
    return pl.pallas_call(
        _ret_kernel,
        grid=(B, npair),
        in_specs=[
            pl.BlockSpec((1, S, V7X_LANES), lambda b, p: (b, 0, p)),
            pl.BlockSpec((1, S, V7X_LANES), lambda b, p: (b, 0, qk_tiles + p)),
            pl.BlockSpec((1, S, 2 * DV), lambda b, p: (b, 0, 2 * qk_tiles * V7X_LANES // (2 * DV) + p)),
            pl.BlockSpec((1, S, 2 * DV), lambda b, p: (b, 0, (2 * qk_tiles * V7X_LANES + H * DV) // (2 * DV) + p)),
            pl.BlockSpec((S, V7X_LANES), lambda b, p: (0, 0)),
            pl.BlockSpec((S, V7X_LANES), lambda b, p: (0, 0)),
            pl.BlockSpec((2, C, C), lambda b, p: (p, 0, 0)),
            pl.BlockSpec((2, C, V7X_LANES), lambda b, p: (p, 0, 0)),
            pl.BlockSpec((2, C, V7X_LANES), lambda b, p: (p, 0, 0)),
            pl.BlockSpec((2, 8, V7X_LANES), lambda b, p: (p, 0, 0)),
        ],
        out_specs=pl.BlockSpec((1, S, 2 * DV), lambda b, p: (b, 0, p)),
        out_shape=jax.ShapeDtypeStruct((B, S, H * DV), F32),
        scratch_shapes=[pltpu.VMEM((2, V7X_LANES, DV), F32)],
        compiler_params=_params("parallel", "parallel"),
        name="retention_mixer",
    )(z, z, z, z, cos, sin, d_in, d_q, d_k, d_c)


def _even_mixer(x, g_norm, w_in, shift_mu, w0, w_up, a0, a_up, g_up, k_k, k_a, r_k, ln_g, ln_b):
    B, S, D = x.shape
    z = norm_matmul(x.reshape(B * S, D), g_norm, w_in.astype(BF16)).reshape(B, S, -1)
    o_a = moba_attention(z)
    o_b = rwkv7_mixer(z, shift_mu, w0, w_up, a0, a_up, g_up, k_k, k_a, r_k, ln_g, ln_b)
    return o_a, o_b


def _odd_mixer(x, g_norm, w_in, pe_k, w1_k, w2_k, pe_v, w1_v, w2_v):
    B, S, D = x.shape
    perm, col = _odd_layout()
    w_p = jnp.take(jnp.pad(w_in, ((0, 0), (0, 1))), perm, axis=1).astype(BF16)
    z = norm_matmul(x.reshape(B * S, D), g_norm, w_p).reshape(B, S, -1)
    o_c = retention_mixer(z)
    cmp_kv, cmp_vk = nsa_compress(z, col["kc"], col["vc"], pe_k, w1_k, w2_k, pe_v, w1_v, w2_v)
    o_d = nsa_attention(z, cmp_kv, cmp_vk, col["nq"], col["slc"], col["win"], col["gate"])
    return o_c, o_d


def _odd_layout():
    G, Dh = NSA_KV_GROUPS, HEAD_DIM
    sizes = (RET_HEADS * RET_QK_DIM, RET_HEADS * RET_QK_DIM, RET_HEADS * RET_V_DIM, RET_HEADS * RET_V_DIM,
             NSA_HEADS * Dh) + (G * Dh,) * 6 + (3 * NSA_HEADS,)
    off = np.concatenate([[0], np.cumsum(sizes)])
    rq, rk, rv, rg, nq, kc, vc, ks, vs, kw, vw, ng = off[:-1]
    n_in = int(off[-1])
    pair = lambda a, b: np.concatenate([np.concatenate([a + g * Dh + np.arange(Dh), b + g * Dh + np.arange(Dh)])
                                        for g in range(G)])
    perm = np.concatenate([np.arange(ks), pair(ks, vs), pair(kw, vw), ng + np.arange(3 * NSA_HEADS)])
    n_pad = -(-len(perm) // (6 * V7X_MXU_DIM)) * 6 * V7X_MXU_DIM
    perm = np.concatenate([perm, np.full(n_pad - len(perm), n_in)]).astype(np.int32)
    col = {"nq": int(nq), "kc": int(kc), "vc": int(vc), "slc": int(ks), "win": int(ks) + 2 * G * Dh,
           "gate": int(ks) + 4 * G * Dh}
    return perm, col


def kernel(x, mix_norm, ffn_norm, even_w_in, even_shift_mu, even_w0, even_w_up, even_a0, even_a_up, even_g_up, even_k_k, even_k_a, even_r_k, even_ln_g, even_ln_b, even_w_out, odd_w_in, odd_cmp_pe_k, odd_cmp_w1_k, odd_cmp_w2_k, odd_cmp_pe_v, odd_cmp_w1_v, odd_cmp_w2_v, odd_w_out, ffn_w1, ffn_w3, ffn_w2, final_norm):
    B, S, D = x.shape
    depth = mix_norm.shape[0]
    for layer in range(depth):
        i = layer // 2
        if layer % 2 == 0:
            o1, o2 = _even_mixer(x, mix_norm[layer], even_w_in[i], even_shift_mu[i], even_w0[i], even_w_up[i],
                                 even_a0[i], even_a_up[i], even_g_up[i], even_k_k[i], even_k_a[i], even_r_k[i],
                                 even_ln_g[i], even_ln_b[i])
            w_out = even_w_out[i]
        else:
            o1, o2 = _odd_mixer(x, mix_norm[layer], odd_w_in[i], odd_cmp_pe_k[i], odd_cmp_w1_k[i], odd_cmp_w2_k[i],
                                odd_cmp_pe_v[i], odd_cmp_w1_v[i], odd_cmp_w2_v[i])
            w_out = odd_w_out[i]
        T = B * S
        x2 = out_proj_residual(o1.reshape(T, -1), o2.reshape(T, -1), w_out.astype(BF16), x.reshape(T, D))
        x2 = ffn_residual(x2, ffn_norm[layer], ffn_w1[layer].astype(BF16), ffn_w3[layer].astype(BF16),
                          ffn_w2[layer].astype(BF16), final_norm if layer == depth - 1 else None)
        x = x2.reshape(B, S, D)
    return x
```

```python
import functools

import jax
import jax.numpy as jnp
import numpy as np
from jax import lax
from jax.experimental import pallas as pl
from jax.experimental.pallas import tpu as pltpu

F32 = jnp.float32
BF16 = jnp.bfloat16

V7X_LANES = 128
V7X_MXU_DIM = 256
V7X_VMEM_BYTES = 64 * 1024 * 1024
VMEM_LIMIT = V7X_VMEM_BYTES * 7 // 8

NORM_EPS = 1e-6
HEAD_DIM = 64

MOBA_BLOCK = 256
MOBA_TOPK = 3
RWKV_HEADS = 16
RWKV_GN_EPS = 6.4e-4

RET_HEADS = 8
RET_QK_DIM = 64
RET_V_DIM = 128
RET_CHUNK = 128
RET_GN_EPS = 1e-6
ROPE_BASE = 10000.0
NSA_HEADS = 16
NSA_KV_GROUPS = 4
NSA_CMP_BLOCK = 32
NSA_CMP_STRIDE = 16
NSA_SLC_BLOCK = 64
NSA_SLC_TOPN = 16
NSA_WINDOW = 512


def _params(*semantics):
    return pltpu.CompilerParams(dimension_semantics=semantics, vmem_limit_bytes=VMEM_LIMIT)


def _rms(x, g):
    return x * lax.rsqrt(jnp.mean(x * x, axis=-1, keepdims=True) + NORM_EPS) * g


def _norm_matmul_kernel(x_ref, g_ref, w_ref, o_ref):
    x = x_ref[...]
    scale = lax.rsqrt(jnp.mean(x * x, axis=-1, keepdims=True) + NORM_EPS)
    o_ref[...] = jnp.dot((x * g_ref[...]).astype(BF16), w_ref[...], preferred_element_type=F32) * scale


def _proj_tile(n):
    assert n % V7X_MXU_DIM == 0
    k = n // V7X_MXU_DIM
    return V7X_MXU_DIM * max(d for d in range(1, 7) if k % d == 0)


def norm_matmul(x, g, w, *, tm=512):
    T, D = x.shape
    N = w.shape[1]
    tn = _proj_tile(N)
    assert T % tm == 0 and N % tn == 0
    return pl.pallas_call(
        _norm_matmul_kernel,
        grid=(N // tn, T // tm),
        in_specs=[
            pl.BlockSpec((tm, D), lambda j, i: (i, 0)),
            pl.BlockSpec((1, D), lambda j, i: (0, 0)),
            pl.BlockSpec((D, tn), lambda j, i: (0, j)),
        ],
        out_specs=pl.BlockSpec((tm, tn), lambda j, i: (i, j)),
        out_shape=jax.ShapeDtypeStruct((T, N), F32),
        compiler_params=_params("parallel", "parallel"),
        name="norm_matmul",
    )(x, g.reshape(1, D), w)


def _mix_ffn_kernel(a_ref, b_ref, wa_ref, wb_ref, x_ref, g_ref, w1_ref, w3_ref, w2_ref, gf_ref, o_ref,
                    h_ref, acc_ref, *, final_norm):
    j = pl.program_id(1)

    @pl.when(j == 0)
    def _():
        x2 = (x_ref[...] + jnp.dot(a_ref[...], wa_ref[...], preferred_element_type=F32)
              + jnp.dot(b_ref[...], wb_ref[...], preferred_element_type=F32))
        o_ref[...] = x2
        h_ref[...] = _rms(x2, g_ref[...]).astype(BF16)
        acc_ref[...] = jnp.zeros_like(acc_ref)

    h = h_ref[...]
    a = jnp.dot(h, w1_ref[...], preferred_element_type=F32)
    b = jnp.dot(h, w3_ref[...], preferred_element_type=F32)
    act = (a * jax.nn.sigmoid(a) * b).astype(BF16)
    acc_ref[...] += jnp.dot(act, w2_ref[...], preferred_element_type=F32)

    @pl.when(j == pl.num_programs(1) - 1)
    def _():
        y = o_ref[...] + acc_ref[...]
        if final_norm:
            y = _rms(y, gf_ref[...])
        o_ref[...] = y


def mix_ffn_residual(a, b, w_out, x, g, w1, w3, w2, g_final=None, *, tm=512, tf=512):
    T, D = x.shape
    K = a.shape[1]
    Fh = w1.shape[1]
    assert T % tm == 0 and Fh % tf == 0 and b.shape == a.shape and w_out.shape == (2 * K, D)
    final_norm = g_final is not None
    gf = (g_final if final_norm else g).reshape(1, D)
    once = pl.Buffered(1)
    return pl.pallas_call(
        functools.partial(_mix_ffn_kernel, final_norm=final_norm),
        grid=(T // tm, Fh // tf),
        in_specs=[
            pl.BlockSpec((tm, K), lambda i, j: (i, 0)),
            pl.BlockSpec((tm, K), lambda i, j: (i, 0)),
            pl.BlockSpec((K, D), lambda i, j: (0, 0), pipeline_mode=once),
            pl.BlockSpec((K, D), lambda i, j: (1, 0), pipeline_mode=once),
            pl.BlockSpec((tm, D), lambda i, j: (i, 0)),
            pl.BlockSpec((1, D), lambda i, j: (0, 0), pipeline_mode=once),
            pl.BlockSpec((D, tf), lambda i, j: (0, j)),
            pl.BlockSpec((D, tf), lambda i, j: (0, j)),
            pl.BlockSpec((tf, D), lambda i, j: (j, 0)),
            pl.BlockSpec((1, D), lambda i, j: (0, 0), pipeline_mode=once),
        ],
        out_specs=pl.BlockSpec((tm, D), lambda i, j: (i, 0)),
        out_shape=jax.ShapeDtypeStruct((T, D), F32),
        scratch_shapes=[pltpu.VMEM((tm, D), BF16), pltpu.VMEM((tm, D), F32)],
        compiler_params=_params("parallel", "arbitrary"),
        name="mix_ffn_residual",
    )(a, b, w_out, w_out, x, g.reshape(1, D), w1, w3, w2, gf)


NEG_BIG = -1e30
_NT = (((1,), (1,)), ((), ()))


def _flash_steps(qas, kas, vas, masks, m_prev, acc_prev):
    hs = range(len(qas))
    s = [lax.dot_general(qas[h], kas[h], _NT, preferred_element_type=F32) for h in hs]
    s = [s[h] if masks[h] is None else jnp.where(masks[h], s[h], NEG_BIG) for h in hs]
    m_new = [jnp.maximum(m_prev[h], jnp.max(s[h], axis=1, keepdims=True)) for h in hs]
    alpha = [jnp.exp2(m_prev[h] - m_new[h]) for h in hs]
    p = [jnp.exp2(s[h] - m_new[h]) for h in hs]
    pv = [jnp.dot(p[h].astype(BF16), vas[h], preferred_element_type=F32) for h in hs]
    return m_new, [alpha[h] * acc_prev[h] + pv[h] for h in hs]


def _augment_q(q_log2, in_head, keep_t, odd):
    nblk, tq = keep_t.shape
    bias_t = jnp.where(keep_t, 0.0, NEG_BIG)
    bias = jnp.concatenate([bias_t, jnp.zeros((V7X_LANES - nblk, tq), F32)], axis=0).T
    if not odd:
        bias = pltpu.roll(bias, HEAD_DIM, axis=1)
    return jnp.where(in_head, q_log2, bias).astype(BF16)


def _key_value_tiles(k, v, blk, lane):
    low = lane < HEAD_DIM
    hot_e = 0.0 if blk is None else jnp.where(lane - HEAD_DIM == blk, 1.0, 0.0)
    hot_o = 0.0 if blk is None else jnp.where(lane == blk, 1.0, 0.0)
    k_e, k_o = jnp.where(low, k, hot_e), jnp.where(low, hot_o, k)
    v_e, v_o = jnp.where(low, v, jnp.where(lane == HEAD_DIM, 1.0, 0.0)), jnp.where(low, jnp.where(lane == 0, 1.0, 0.0), v)
    return [t.astype(BF16) for t in (k_e, k_o, v_e, v_o)]


def _normalise(acc, lane, odd):
    return acc / jnp.sum(jnp.where(lane == (0 if odd else HEAD_DIM), acc, 0.0), axis=1, keepdims=True)


LOG2E = 1.4426950408889634


MOBA_HEADS_PER_STEP = 8


def _split_bf16(x):
    hi = x.astype(BF16)
    return hi, (x - hi.astype(F32)).astype(BF16)


def _moba_kernel(q_ref, k_ref, v_ref, o_ref, ka_ref, va_ref, km_ref, acc_ref):
    L = MOBA_BLOCK
    S = k_ref.shape[1]
    nb = S // L
    HP = MOBA_HEADS_PER_STEP
    qi = pl.program_id(2)
    lane = lax.broadcasted_iota(jnp.int32, (L, V7X_LANES), 1)
    lanes_of = lambda ref, pp: ref[0, :, pp * V7X_LANES:(pp + 1) * V7X_LANES]

    @pl.when(qi == 0)
    def _():
        lane_s = lax.broadcasted_iota(jnp.int32, (S, V7X_LANES), 1)
        blk = lax.broadcasted_iota(jnp.int32, (S, V7X_LANES), 0) // L
        for pp in range(HP // 2):
            k = lanes_of(k_ref, pp)
            (ka_ref[2 * pp], ka_ref[2 * pp + 1], va_ref[2 * pp], va_ref[2 * pp + 1]) = _key_value_tiles(
                k, lanes_of(v_ref, pp), blk, lane_s)
            km_ref[pp] = jnp.concatenate(_split_bf16(jnp.mean(k.reshape(nb, L, V7X_LANES), axis=1)), axis=0)

    row = lax.broadcasted_iota(jnp.int32, (L, L), 0)
    col = lax.broadcasted_iota(jnp.int32, (L, L), 1)
    causal = col <= row
    jrow = lax.broadcasted_iota(jnp.int32, (nb, L), 0)
    past = jrow < qi
    qas = []
    for pp in range(HP // 2):
        q = lanes_of(q_ref, pp) * (HEAD_DIM ** -0.5)
        q_hi, q_lo = _split_bf16(q)
        km = km_ref[pp]
        for e in range(2):
            in_head = (lane >= e * HEAD_DIM) & (lane < (e + 1) * HEAD_DIM)
            zero = jnp.zeros_like(q_hi)
            g1 = lax.dot_general(km, jnp.where(in_head, q_hi, zero), _NT, preferred_element_type=F32)
            g2 = lax.dot_general(km, jnp.where(in_head, q_lo, zero), _NT, preferred_element_type=F32)
            gate = g1[0:nb] + g1[nb:2 * nb] + g2[0:nb]
            keep = jrow == qi
            for n in range(nb):
                g_n = gate[n:n + 1, :]
                beats = (gate > g_n) | ((gate == g_n) & (jrow < n))
                rank = jnp.sum(jnp.where(past & beats, 1.0, 0.0), axis=0, keepdims=True)
                keep = keep | ((jrow == n) & (rank < MOBA_TOPK) & past)
            qas.append(_augment_q(q * LOG2E, in_head, keep, odd=e == 1))

    tiles = lambda start: ([ka_ref[h, pl.ds(start, L), :] for h in range(HP)],
                           [va_ref[h, pl.ds(start, L), :] for h in range(HP)])

    m, acc = _flash_steps(qas, *tiles(pl.multiple_of(qi * L, L)), [causal] * HP,
                          [jnp.full((L, 1), NEG_BIG, F32)] * HP, [jnp.zeros((L, V7X_LANES), F32)] * HP)
    for h in range(HP):
        acc_ref[h] = acc[h]

    def body(n, carry):
        m2, acc2 = _flash_steps(qas, *tiles(pl.multiple_of(n * L, L)), [None] * HP, list(carry),
                                [acc_ref[h] for h in range(HP)])
        for h in range(HP):
            acc_ref[h] = acc2[h]
        return tuple(m2)

    lax.fori_loop(0, qi, body, tuple(m))
    for pp in range(HP // 2):
        o_ref[0, :, pp * V7X_LANES:(pp + 1) * V7X_LANES] = jnp.where(
            lane < HEAD_DIM, _normalise(acc_ref[2 * pp], lane, False), _normalise(acc_ref[2 * pp + 1], lane, True)
        ).astype(o_ref.dtype)


def moba_attention(z, *, n_heads=16):
    B, S, _ = z.shape
    L, HP = MOBA_BLOCK, MOBA_HEADS_PER_STEP
    W = HP * HEAD_DIM
    nb = S // L
    assert S % L == 0 and n_heads % HP == 0 and W % V7X_LANES == 0 and 2 * nb <= 16
    ngrp = n_heads // HP
    return pl.pallas_call(
        _moba_kernel,
        grid=(B, ngrp, S // L),
        in_specs=[
            pl.BlockSpec((1, L, W), lambda b, p, i: (b, i, p)),
            pl.BlockSpec((1, S, W), lambda b, p, i: (b, 0, ngrp + p)),
            pl.BlockSpec((1, S, W), lambda b, p, i: (b, 0, 2 * ngrp + p)),
        ],
        out_specs=pl.BlockSpec((1, L, W), lambda b, p, i: (b, i, p)),
        out_shape=jax.ShapeDtypeStruct((B, S, n_heads * HEAD_DIM), BF16),
        scratch_shapes=[
            pltpu.VMEM((HP, S, V7X_LANES), BF16),
            pltpu.VMEM((HP, S, V7X_LANES), BF16),
            pltpu.VMEM((HP // 2, 2 * nb, V7X_LANES), BF16),
            pltpu.VMEM((HP, L, V7X_LANES), F32),
        ],
        compiler_params=_params("parallel", "parallel", "arbitrary"),
        name="moba_attention",
    )(z, z, z)


NSA_TQ = 256
NSA_GROUPS_PER_STEP = 2
BIG = 3.0e38


def _gelu_tanh(x):
    return 0.5 * x * (1.0 + jnp.tanh(0.7978845608028654 * (x + 0.044715 * x * x * x)))


def _nsa_compress_kernel(xk0_ref, xk1_ref, xv0_ref, xv1_ref, pek_ref, pev_ref, w1k_ref, w1v_ref, w2k_ref, w2v_ref,
                         o1_ref, o2_ref):
    G, Lc, st = NSA_KV_GROUPS, NSA_CMP_BLOCK, NSA_CMP_STRIDE
    nrow = xk0_ref.shape[1] // st
    lane = lax.broadcasted_iota(jnp.int32, (nrow, G * HEAD_DIM), 1)

    def hidden(x_refs, pe_ref, w1_ref):
        acc = [jnp.zeros((G * nrow, V7X_LANES), F32) for _ in range(Lc // st)]
        for l in range(Lc):
            u, m = divmod(l, st)
            x = jnp.concatenate([r[0, pl.ds(m, nrow, stride=st), :] for r in x_refs], axis=1) + pe_ref[l:l + 1, :]
            xs = jnp.concatenate(
                [jnp.where((lane >= g * HEAD_DIM) & (lane < (g + 1) * HEAD_DIM), x, 0.0) for g in range(G)],
                axis=0).astype(BF16)
            acc[u] = acc[u] + jnp.dot(xs, w1_ref[l], preferred_element_type=F32)
        nxt = jnp.concatenate([pltpu.roll(acc[1][g * nrow:(g + 1) * nrow], nrow - 1, axis=0) for g in range(G)],
                              axis=0)
        return _gelu_tanh(acc[0] + nxt).astype(BF16)

    hk = hidden((xk0_ref, xk1_ref), pek_ref, w1k_ref)
    hv = hidden((xv0_ref, xv1_ref), pev_ref, w1v_ref)
    kc = jnp.dot(hk, w2k_ref[...], preferred_element_type=F32)
    vc = jnp.dot(hv, w2v_ref[...], preferred_element_type=F32)
    kv = kc + vc
    vk = pltpu.roll(kv, HEAD_DIM, axis=1)
    for g in range(G):
        o1_ref[0, :, g * V7X_LANES:(g + 1) * V7X_LANES] = kv[g * nrow:(g + 1) * nrow]
        o2_ref[0, :, g * V7X_LANES:(g + 1) * V7X_LANES] = vk[g * nrow:(g + 1) * nrow]


def nsa_compress(z, col_k, col_v, pe_k, w1_k, w2_k, pe_v, w1_v, w2_v):
    B, S, _ = z.shape
    G, Lc, st = NSA_KV_GROUPS, NSA_CMP_BLOCK, NSA_CMP_STRIDE
    GW = G * HEAD_DIM
    nrow = S // st
    hid = w1_k.shape[1]
    assert hid == V7X_LANES and col_k % GW == 0 and col_v % GW == 0
    tile_pe = lambda pe: jnp.tile(pe, (1, G))
    tile_w1 = lambda w: jnp.tile(w.reshape(Lc, 1, HEAD_DIM, hid), (1, G, 1, 1)).reshape(Lc, GW, hid).astype(BF16)
    w2k = jnp.pad(w2_k, ((0, 0), (0, HEAD_DIM))).astype(BF16)
    w2v = jnp.pad(w2_v, ((0, 0), (HEAD_DIM, 0))).astype(BF16)
    const = lambda shape: pl.BlockSpec(shape, lambda b: (0,) * len(shape))
    out = jax.ShapeDtypeStruct((B, nrow, G * V7X_LANES), F32)
    return pl.pallas_call(
        _nsa_compress_kernel,
        grid=(B,),
        in_specs=[
            pl.BlockSpec((1, S, V7X_LANES), lambda b: (b, 0, col_k // V7X_LANES)),
            pl.BlockSpec((1, S, V7X_LANES), lambda b: (b, 0, col_k // V7X_LANES + 1)),
            pl.BlockSpec((1, S, V7X_LANES), lambda b: (b, 0, col_v // V7X_LANES)),
            pl.BlockSpec((1, S, V7X_LANES), lambda b: (b, 0, col_v // V7X_LANES + 1)),
            const((Lc, GW)), const((Lc, GW)),
            const((Lc, GW, hid)), const((Lc, GW, hid)),
            const((hid, V7X_LANES)), const((hid, V7X_LANES)),
        ],
        out_specs=[pl.BlockSpec((1, nrow, G * V7X_LANES), lambda b: (b, 0, 0))] * 2,
        out_shape=[out, out],
        compiler_params=_params("parallel"),
        name="nsa_compress",
    )(z, z, z, z, tile_pe(pe_k), tile_pe(pe_v), tile_w1(w1_k), tile_w1(w1_v), w2k, w2v)


def _nsa_kernel(q_ref, c1_ref, c2_ref, s_ref, w_ref, g_ref, ovt_ref, o_ref, sk_ref, sv_ref, wk_ref, wv_ref, acc_ref):
    TQ = NSA_TQ
    S = s_ref.shape[1]
    R = NSA_HEADS // NSA_KV_GROUPS
    NG = NSA_GROUPS_PER_STEP
    NH = NG * R
    grp0 = pl.program_id(1) * NG
    qi = pl.program_id(2)
    lane = lax.broadcasted_iota(jnp.int32, (TQ, V7X_LANES), 1)
    tile_of = lambda ref, t: ref[0, :, t * V7X_LANES:(t + 1) * V7X_LANES]

    @pl.when(qi == 0)
    def _():
        lane_s = lax.broadcasted_iota(jnp.int32, (S, V7X_LANES), 1)
        blk = lax.broadcasted_iota(jnp.int32, (S, V7X_LANES), 0) // NSA_SLC_BLOCK
        for gg in range(NG):
            for src, k_ref, v_ref, hot in ((s_ref, sk_ref, sv_ref, blk), (w_ref, wk_ref, wv_ref, None)):
                kv = tile_of(src, gg)
                vk = pltpu.roll(kv, HEAD_DIM, axis=1)
                k_ref[2 * gg], k_ref[2 * gg + 1], v_ref[2 * gg], v_ref[2 * gg + 1] = _key_value_tiles(
                    jnp.where(lane_s < HEAD_DIM, kv, vk), jnp.where(lane_s < HEAD_DIM, vk, kv), hot, lane_s)

    q0 = pl.multiple_of(qi * TQ, TQ)
    row = lax.broadcasted_iota(jnp.int32, (TQ, TQ), 0)
    col = lax.broadcasted_iota(jnp.int32, (TQ, TQ), 1)
    causal = col <= row
    qi_mat = jnp.zeros((TQ, TQ), jnp.int32) + qi
    t_abs = q0 + lax.broadcasted_iota(jnp.int32, (TQ, V7X_LANES), 0)
    even_lanes = lane < HEAD_DIM

    cmask = lane * NSA_CMP_STRIDE + (NSA_CMP_BLOCK - 1) <= t_abs
    nblk = s_ref.shape[1] // NSA_SLC_BLOCK
    jrow = lax.broadcasted_iota(jnp.int32, (nblk, TQ), 0)
    own = (q0 + lax.broadcasted_iota(jnp.int32, (nblk, TQ), 1)) // NSA_SLC_BLOCK
    ovt = ovt_ref[...]
    head_lanes = [even_lanes if h % 2 == 0 else ~even_lanes for h in range(NH)]
    tiles = [tile_of(q_ref, h // 2) * (HEAD_DIM ** -0.5) for h in range(NH)]
    o_cmp, qas = [], []
    for gg in range(NG):
        c_kv_b, c_vk_b = tile_of(c1_ref, gg).astype(BF16), tile_of(c2_ref, gg).astype(BF16)
        p_sum = jnp.zeros((TQ, V7X_LANES), F32)
        for h in range(gg * R, (gg + 1) * R):
            qm = jnp.where(head_lanes[h], tiles[h], 0.0).astype(BF16)
            s = lax.dot_general(qm, c_kv_b if h % 2 == 0 else c_vk_b, _NT, preferred_element_type=F32)
            s = jnp.where(cmask, s, NEG_BIG)
            p = jnp.where(cmask, jnp.exp(s - jnp.max(s, axis=1, keepdims=True)), 0.0)
            den = jnp.sum(p, axis=1, keepdims=True)
            p = p / jnp.where(den > 0.0, den, 1.0)
            p_sum = p_sum + p
            o_cmp.append(jnp.dot(p.astype(BF16), c_vk_b if h % 2 == 0 else c_kv_b, preferred_element_type=F32))

        p_hi, p_lo = _split_bf16(p_sum)
        p_slc = (lax.dot_general(ovt, p_hi, _NT, preferred_element_type=F32)
                 + lax.dot_general(ovt, p_lo, _NT, preferred_element_type=F32))[0:nblk]
        score = jnp.where((jrow == own) | (jrow == 0), BIG, jnp.where(jrow > own, -BIG, p_slc))
        keep = jrow > nblk
        for j in range(nblk):
            s_j = score[j:j + 1, :]
            beats = (score > s_j) | ((score == s_j) & (jrow < j))
            rank = jnp.sum(jnp.where(beats, 1.0, 0.0), axis=0, keepdims=True)
            keep = keep | ((jrow == j) & (rank < NSA_SLC_TOPN) & (jrow <= own))
        qas += [_augment_q(tiles[h] * LOG2E, head_lanes[h], keep, odd=h % 2 == 1) for h in range(gg * R, (gg + 1) * R)]

    neg = [jnp.full((TQ, 1), NEG_BIG, F32)] * NH
    zacc = [jnp.zeros((TQ, V7X_LANES), F32)] * NH
    kv_index = [2 * (h // R) + h % 2 for h in range(NH)]

    def kv_blocks(k_ref, v_ref, start):
        return ([k_ref[kv_index[h], pl.ds(start, TQ), :] for h in range(NH)],
                [v_ref[kv_index[h], pl.ds(start, TQ), :] for h in range(NH)])

    m, acc = _flash_steps(qas, *kv_blocks(sk_ref, sv_ref, q0), [causal] * NH, neg, zacc)
    for h in range(NH):
        acc_ref[h] = acc[h]

    def body(kb, carry):
        m2, acc2 = _flash_steps(qas, *kv_blocks(sk_ref, sv_ref, pl.multiple_of(kb * TQ, TQ)), [None] * NH,
                                list(carry), [acc_ref[h] for h in range(NH)])
        for h in range(NH):
            acc_ref[h] = acc2[h]
        return tuple(m2)

    lax.fori_loop(0, qi, body, tuple(m))

    m, acc = _flash_steps(qas, *kv_blocks(wk_ref, wv_ref, q0), [causal] * NH, neg, zacc)
    m, acc = _flash_steps(qas, *kv_blocks(wk_ref, wv_ref, pl.multiple_of(jnp.maximum(qi - 1, 0) * TQ, TQ)),
                          [qi_mat >= 1] * NH, m, acc)
    m, acc = _flash_steps(qas, *kv_blocks(wk_ref, wv_ref, pl.multiple_of(jnp.maximum(qi - 2, 0) * TQ, TQ)),
                          [(col > row) & (qi_mat >= 2)] * NH, m, acc)

    gates = jax.nn.sigmoid(g_ref[0])
    outs = []
    for h in range(NH):
        o_slc = _normalise(acc_ref[h], lane, h % 2 == 1)
        o_win = _normalise(acc[h], lane, h % 2 == 1)
        c0 = (grp0 * R + h) * 3
        gate = lambda c: jnp.sum(jnp.where(lane == c, gates, 0.0), axis=1, keepdims=True)
        outs.append(gate(c0) * o_cmp[h] + gate(c0 + 1) * o_slc + gate(c0 + 2) * o_win)
    for p2 in range(NH // 2):
        o_ref[0, :, p2 * V7X_LANES:(p2 + 1) * V7X_LANES] = jnp.where(
            even_lanes, outs[2 * p2], outs[2 * p2 + 1]).astype(o_ref.dtype)


def nsa_attention(z, cmp_kv, cmp_vk, col_q, col_slc, col_win, col_gate):
    B, S, _ = z.shape
    G, TQ, NG = NSA_KV_GROUPS, NSA_TQ, NSA_GROUPS_PER_STEP
    R = NSA_HEADS // G
    QW = NG * R * HEAD_DIM
    KW = NG * V7X_LANES
    ncmp = cmp_kv.shape[1]
    assert S % TQ == 0 and ncmp == V7X_LANES and S // NSA_SLC_BLOCK <= V7X_LANES and NSA_WINDOW == 2 * TQ
    assert G % NG == 0 and R % 2 == 0
    assert col_q % QW == 0 and col_slc % KW == 0 and col_win % KW == 0 and col_gate % V7X_LANES == 0
    nc = (S - NSA_CMP_BLOCK) // NSA_CMP_STRIDE + 1
    c_start = np.arange(V7X_LANES) * NSA_CMP_STRIDE
    s_start = np.arange(V7X_LANES) * NSA_SLC_BLOCK
    overlap = ((c_start[:, None] <= s_start[None, :] + NSA_SLC_BLOCK - 1)
               & (c_start[:, None] + NSA_CMP_BLOCK - 1 >= s_start[None, :])
               & (np.arange(V7X_LANES)[:, None] < nc) & (np.arange(V7X_LANES)[None, :] < S // NSA_SLC_BLOCK))
    const = lambda shape: pl.BlockSpec(shape, lambda b, g, i: (0,) * len(shape))
    return pl.pallas_call(
        _nsa_kernel,
        grid=(B, G // NG, S // TQ),
        in_specs=[
            pl.BlockSpec((1, TQ, QW), lambda b, g, i: (b, i, col_q // QW + g)),
            pl.BlockSpec((1, ncmp, KW), lambda b, g, i: (b, 0, g)),
            pl.BlockSpec((1, ncmp, KW), lambda b, g, i: (b, 0, g)),
            pl.BlockSpec((1, S, KW), lambda b, g, i: (b, 0, col_slc // KW + g)),
            pl.BlockSpec((1, S, KW), lambda b, g, i: (b, 0, col_win // KW + g)),
            pl.BlockSpec((1, TQ, V7X_LANES), lambda b, g, i: (b, i, col_gate // V7X_LANES)),
            const((V7X_LANES, V7X_LANES)),
        ],
        out_specs=pl.BlockSpec((1, TQ, QW), lambda b, g, i: (b, i, g)),
        out_shape=jax.ShapeDtypeStruct((B, S, NSA_HEADS * HEAD_DIM), BF16),
        scratch_shapes=[pltpu.VMEM((2 * NG, S, V7X_LANES), BF16)] * 4 + [
            pltpu.VMEM((NG * R, TQ, V7X_LANES), F32),
        ],
        compiler_params=_params("parallel", "parallel", "arbitrary"),
        name="nsa_attention",
    )(z, cmp_kv, cmp_vk, z, z, z, jnp.asarray(overlap.T, BF16))


RWKV_CHUNK = 64
RWKV_ROWS = 256
RWKV_INTERLEAVE = 8


def _mm(a, b, dims=None):
    dims = dims or (((1,), (0,)), ((), ()))
    return lax.dot_general(a.astype(BF16), b.astype(BF16), dims, preferred_element_type=F32)


def _mm3(a, b):
    (a_hi, a_lo), (b_hi, b_lo) = _split_bf16(a), _split_bf16(b)
    return _mm(a_hi, b_hi) + (_mm(a_hi, b_lo) + _mm(a_lo, b_hi))


def _mm_onehot(a01, b):
    hi = b.astype(BF16)
    mid, lo = _split_bf16(b - hi.astype(F32))
    return _mm(a01, hi) + (_mm(a01, mid) + _mm(a01, lo))


def _head_sum(x, low):
    s0 = jnp.sum(jnp.where(low, x, 0.0), axis=1, keepdims=True)
    s1 = jnp.sum(jnp.where(low, 0.0, x), axis=1, keepdims=True)
    return jnp.where(low, s0, s1)


def _rwkv_kernel(r_ref, k_ref, v_ref, lo_ref, glo_ref, pp_ref, pl_ref, wup_ref, aup_ref, gup_ref, o_ref,
                 rs, ws, ks, vs, als, bes, gs, ys, bon, hs, rqs, ms, ns):
    S = r_ref.shape[1]
    C, RB = RWKV_CHUNK, RWKV_ROWS
    pp = pp_ref[...]
    mu_r, mu_k, mu_v, w0, a0, k_k, k_a, r_k, ln_g, ln_b = [pp[i:i + 1, :] for i in range(10)]
    mu_lo, mu_g = pl_ref[0:1, :], pl_ref[1:2, :]
    low = lax.broadcasted_iota(jnp.int32, (RB, V7X_LANES), 1) < HEAD_DIM
    first = lax.broadcasted_iota(jnp.int32, (RB, V7X_LANES), 0) == 0

    def prologue(i, c):
        t0 = pl.multiple_of(i * RB, RB)
        tp = jnp.maximum(t0 - 1, 0)
        keep = jnp.where(i > 0, 1.0, 0.0)

        def shifted(ref, mu):
            x = ref[0, pl.ds(t0, RB), :]
            prev = jnp.where(first, ref[0, pl.ds(tp, 1), :] * keep, pltpu.roll(x, 1, axis=0))
            return x + (prev - x) * mu

        r, k, v = shifted(r_ref, mu_r), shifted(k_ref, mu_k), shifted(v_ref, mu_v)
        lo, glo = shifted(lo_ref, mu_lo), shifted(glo_ref, mu_g)
        wp = -(w0 + _mm(jnp.tanh(lo), wup_ref[...]))
        w = -(jnp.maximum(wp, 0.0) + jnp.log(1.0 + jnp.exp(-jnp.abs(wp)))) - 0.5
        a = jax.nn.sigmoid(a0 + _mm(lo, aup_ref[...]))
        kk = k * k_k
        kk = kk / jnp.maximum(jnp.sqrt(_head_sum(kk * kk, low)), 1e-12)
        k2 = k * (1.0 + (a - 1.0) * k_a)
        rs[pl.ds(t0, RB), :] = r
        ws[pl.ds(t0, RB), :] = -jnp.exp(w)
        ks[pl.ds(t0, RB), :] = k2
        vs[pl.ds(t0, RB), :] = v
        als[pl.ds(t0, RB), :] = -kk
        bes[pl.ds(t0, RB), :] = kk * a
        gs[pl.ds(t0, RB), :] = _mm(jax.nn.sigmoid(glo), gup_ref[...])
        bon[pl.ds(t0, RB), :] = _head_sum(r * k2 * r_k, low) * v
        return c

    lax.fori_loop(0, S // RB, prologue, 0)

    W2 = 2 * C
    row = lax.broadcasted_iota(jnp.int32, (W2, W2), 0)
    col = lax.broadcasted_iota(jnp.int32, (W2, W2), 1)
    t_idx, s_idx = row % C, col % C
    top, left = row < C, col < C
    same = top == left
    eye = jnp.where(row == col, 1.0, 0.0)
    tri = jnp.where(lax.broadcasted_iota(jnp.int32, (C, C), 1) <= lax.broadcasted_iota(jnp.int32, (C, C), 0), 1.0, 0.0)
    low_c = lax.broadcasted_iota(jnp.int32, (C, V7X_LANES), 1) < HEAD_DIM
    fold = lambda x: x[0:C] + x[C:W2]
    stack_heads = lambda x: jnp.concatenate([jnp.where(low_c, x, 0.0), jnp.where(low_c, 0.0, x)], axis=0)
    block_diag = lambda x: jnp.where(top, jnp.where(left, x, 0.0), jnp.where(left, 0.0, pltpu.roll(x, C, axis=1)))

    rows = lambda c: pl.ds(c * C if isinstance(c, int) else pl.multiple_of(c * C, C), C)

    def advance(c, H):
        ys[rows(c), :] += _mm3(rqs[c], H)
        return _mm3(ms[c], H) + ns[c]

    def transfers(i, lagged):
        each = lambda f, *xs: [f(*a) for a in zip(*xs)]
        cs = [i * RWKV_INTERLEAVE + u for u in range(RWKV_INTERLEAVE)]
        sls = [rows(c) for c in cs]
        state = [hs[...]] if lagged else None

        def lag(hook):
            if lagged:
                for u in range(hook * RWKV_INTERLEAVE // 8, (hook + 1) * RWKV_INTERLEAVE // 8):
                    state[0] = advance(cs[u] - RWKV_INTERLEAVE, state[0])

        r, lw, k2, v, al, be = ([ref[sl, :] for sl in sls] for ref in (rs, ws, ks, vs, als, bes))
        logp = each(lambda x: _mm_onehot(tri, x), lw)
        lag(0)
        P = each(jnp.exp, logp)
        Pinv = each(lambda x: jnp.exp(-x), logp)
        At = each(lambda a_, lp, w_: a_ * jnp.exp(lp - w_), al, logp, lw)
        Rt, Bt, Kt = each(jnp.multiply, r, P), each(jnp.multiply, be, Pinv), each(jnp.multiply, k2, Pinv)
        PC = each(lambda p: p[C - 1:C, :], P)
        A_bd, R_bd = each(stack_heads, At), each(stack_heads, Rt)
        Yt = each(lambda b, k: jnp.concatenate([b, k], axis=0), Bt, Kt)
        A1 = each(lambda a, y: jnp.where(s_idx < t_idx, _mm(a, y, dims=_NT), 0.0), A_bd, Yt)
        A2 = each(lambda a, y: jnp.where(s_idx <= t_idx, _mm(a, y, dims=_NT), 0.0), R_bd, Yt)
        X, Arb = each(block_diag, A1), each(block_diag, A2)
        T = each(lambda x: eye + x, X)
        for it in range(5):
            X = each(lambda x: _mm(x, x), X)
            T = each(lambda t, x: t + _mm(t, x), T, X)
            lag(1 + it)
        V0 = each(lambda x: jnp.concatenate([jnp.zeros_like(x), x], axis=0), v)
        TA = each(_mm, T, A_bd)
        AkV = each(lambda a, x: jnp.where(same, _mm(a, x), 0.0), A1, V0)
        lag(6)
        U0 = each(_mm, T, AkV)
        lag(7)
        AR = each(lambda a, t, u: _mm(a, jnp.concatenate([t, u], axis=1)), Arb, TA, U0)
        ArkV = each(lambda a, x: jnp.where(same, _mm(a, x), 0.0), A2, V0)
        Mx = each(lambda b, p, t: _mm((b * p).T, fold(t)), Bt, PC, TA)
        Nx = each(lambda b, k, p, u, x: _mm(jnp.concatenate([b * p, k * p], axis=0).T,
                                            jnp.concatenate([fold(u), x], axis=0)), Bt, Kt, PC, U0, v)
        for u in range(RWKV_INTERLEAVE):
            ys[sls[u], :] = fold(AR[u][:, W2:2 * W2] + ArkV[u])
            rqs[cs[u]] = Rt[u] + fold(AR[u][:, 0:W2])
            ms[cs[u]] = eye * PC[u] + jnp.where(same, Mx[u], 0.0)
            ns[cs[u]] = jnp.where(same, Nx[u], 0.0)
        if lagged:
            hs[...] = state[0]

    def pipelined(i, carry):
        transfers(i, True)
        return carry

    def drain(c, carry):
        hs[...] = advance(c, hs[...])
        return carry

    assert 8 % RWKV_INTERLEAVE == 0
    hs[...] = jnp.zeros((W2, W2), F32)
    transfers(0, False)
    lax.fori_loop(1, S // C // RWKV_INTERLEAVE, pipelined, 0)
    lax.fori_loop(S // C - RWKV_INTERLEAVE, S // C, drain, 0)

    def epilogue(i, c):
        sl = pl.ds(pl.multiple_of(i * RB, RB), RB)
        y = ys[sl, :]
        d = y - _head_sum(y, low) * (1.0 / HEAD_DIM)
        var = _head_sum(d * d, low) * (1.0 / HEAD_DIM)
        yn = d * lax.rsqrt(var + RWKV_GN_EPS) * ln_g + ln_b
        o_ref[0, sl, :] = ((yn + bon[sl, :]) * gs[sl, :]).astype(o_ref.dtype)
        return c

    lax.fori_loop(0, S // RB, epilogue, 0)


def rwkv7_mixer(z, shift_mu, w0, w_up, a0, a_up, g_up, k_k, k_a, r_k, ln_g, ln_b):
    B, S, _ = z.shape
    CW = RWKV_HEADS * HEAD_DIM
    npair = CW // V7X_LANES
    base = 3 * CW // V7X_LANES
    lora = w_up.shape[0] + a_up.shape[0]
    assert lora == V7X_LANES and g_up.shape[0] == V7X_LANES and S % RWKV_ROWS == 0
    pp = jnp.stack([shift_mu[0:CW], shift_mu[CW:2 * CW], shift_mu[2 * CW:3 * CW], w0, a0, k_k, k_a,
                    r_k.reshape(CW), ln_g, ln_b])
    pp = jnp.pad(pp, ((0, 16 - pp.shape[0]), (0, 0)))
    pl2 = jnp.pad(shift_mu[3 * CW:].reshape(2, V7X_LANES), ((0, 6), (0, 0)))
    wup = jnp.pad(w_up, ((0, a_up.shape[0]), (0, 0)))
    aup = jnp.pad(a_up, ((w_up.shape[0], 0), (0, 0)))
    tile = lambda off: pl.BlockSpec((1, S, V7X_LANES), lambda b, p: (b, 0, base + off * npair + p))
    fixed = lambda off: pl.BlockSpec((1, S, V7X_LANES), lambda b, p: (b, 0, base + 3 * npair + off))
    seq = pltpu.VMEM((S, V7X_LANES), F32)
    return pl.pallas_call(
        _rwkv_kernel,
        grid=(B, npair),
        in_specs=[
            tile(0), tile(1), tile(2), fixed(0), fixed(1),
            pl.BlockSpec((16, V7X_LANES), lambda b, p: (0, p)),
            pl.BlockSpec((8, V7X_LANES), lambda b, p: (0, 0)),
            pl.BlockSpec((V7X_LANES, V7X_LANES), lambda b, p: (0, p)),
            pl.BlockSpec((V7X_LANES, V7X_LANES), lambda b, p: (0, p)),
            pl.BlockSpec((V7X_LANES, V7X_LANES), lambda b, p: (0, p)),
        ],
        out_specs=pl.BlockSpec((1, S, V7X_LANES), lambda b, p: (b, 0, p)),
        out_shape=jax.ShapeDtypeStruct((B, S, CW), BF16),
        scratch_shapes=[seq] * 9 + [
            pltpu.VMEM((V7X_LANES, V7X_LANES), F32),
            pltpu.VMEM((S // RWKV_CHUNK, RWKV_CHUNK, V7X_LANES), F32),
            pltpu.VMEM((S // RWKV_CHUNK, V7X_LANES, V7X_LANES), F32),
            pltpu.VMEM((S // RWKV_CHUNK, V7X_LANES, V7X_LANES), F32),
        ],
        compiler_params=_params("parallel", "parallel"),
        name="rwkv7_mixer",
    )(z, z, z, z, z, pp, pl2, wup, aup, g_up.astype(BF16))


RET_CHUNKS_PER_STEP = 2


def _ret_kernel(q_ref, k_ref, v_ref, g_ref, cos_ref, sin_ref, din_ref, dq_ref, dk_ref, dc_ref, o_ref, st_ref):
    S = q_ref.shape[1]
    C, DV = RET_CHUNK, RET_V_DIM
    lane = lax.broadcasted_iota(jnp.int32, (C, V7X_LANES), 1)
    first_half = (lane % RET_QK_DIM) < RET_QK_DIM // 2
    st_ref[...] = jnp.zeros_like(st_ref)

    in_head = [(lane >= h * RET_QK_DIM) & (lane < (h + 1) * RET_QK_DIM) for h in range(2)]
    NCH = RET_CHUNKS_PER_STEP
    units = [(u, h) for u in range(NCH) for h in range(2)]

    def step(i, carry):
        sls = [pl.ds(pl.multiple_of((i * NCH + u) * C, C), C) for u in range(NCH)]

        def rot(z, sl):
            swapped = jnp.where(first_half, pltpu.roll(z, V7X_LANES - RET_QK_DIM // 2, axis=1),
                                pltpu.roll(z, RET_QK_DIM // 2, axis=1))
            return z * cos_ref[sl, :] + swapped * sin_ref[sl, :]

        q = [rot(q_ref[0, sl, :], sl) for sl in sls]
        k = [rot(k_ref[0, sl, :], sl) * (RET_QK_DIM ** -0.5) for sl in sls]
        qm = [jnp.where(in_head[h], q[u], 0.0) for u, h in units]
        v = [v_ref[0, sls[u], h * DV:(h + 1) * DV] for u, h in units]
        inner = [_mm(qm[n], k[u], dims=_NT) * din_ref[h] for n, (u, h) in enumerate(units)]
        upd = [_mm((jnp.where(in_head[h], k[u], 0.0) * dk_ref[h]).T, v[n]) for n, (u, h) in enumerate(units)]
        local = [_mm(inner[n], v[n]) for n in range(len(units))]
        st = [st_ref[h] for h in range(2)]
        for n, (u, h) in enumerate(units):
            o = local[n] + _mm(qm[n], st[h]) * dq_ref[h]
            st[h] = upd[n] + dc_ref[h, 0:1, :] * st[h]
            d = o - jnp.mean(o, axis=1, keepdims=True)
            on = d * lax.rsqrt(jnp.mean(d * d, axis=1, keepdims=True) + RET_GN_EPS)
            gate = g_ref[0, sls[u], h * DV:(h + 1) * DV]
            o_ref[0, sls[u], h * DV:(h + 1) * DV] = (gate * jax.nn.sigmoid(gate) * on).astype(o_ref.dtype)
        st_ref[0], st_ref[1] = st
        return carry

    lax.fori_loop(0, S // C // NCH, step, 0)


def retention_mixer(z):
    B, S, _ = z.shape
    H, C, DK, DV = RET_HEADS, RET_CHUNK, RET_QK_DIM, RET_V_DIM
    assert S % C == 0 and 2 * DK == V7X_LANES and DV == V7X_LANES
    npair = H // 2
    half = DK // 2
    inv = ROPE_BASE ** (-jnp.arange(half, dtype=F32) / half)
    ang = jnp.arange(S, dtype=F32)[:, None] * inv
    cos = jnp.tile(jnp.cos(ang), (1, 4))
    sin = jnp.tile(jnp.concatenate([-jnp.sin(ang), jnp.sin(ang)], axis=1), (1, 2))
    log_g = jnp.asarray(np.log(1.0 - 2.0 ** (-5.0 - np.arange(H))), F32)
    n = jnp.arange(C, dtype=F32)
    diff = n[:, None] - n[None, :]
    d_in = jnp.where(diff >= 0, jnp.exp(jnp.maximum(diff, 0.0) * log_g[:, None, None]), 0.0)
    lanes = lambda t: jnp.broadcast_to(t[..., None], t.shape + (V7X_LANES,))
    d_q = lanes(jnp.exp((n + 1.0) * log_g[:, None]))
    d_k = lanes(jnp.exp((C - 1.0 - n) * log_g[:, None]))
    d_c = lanes(jnp.broadcast_to(jnp.exp(C * log_g)[:, None], (H, 8)))
    qk_tiles = H * DK // V7X_LANES
    return pl.pallas_call(
        _ret_kernel,
        grid=(B, npair),
        in_specs=[
            pl.BlockSpec((1, S, V7X_LANES), lambda b, p: (b, 0, p)),
            pl.BlockSpec((1, S, V7X_LANES), lambda b, p: (b, 0, qk_tiles + p)),
            pl.BlockSpec((1, S, 2 * DV), lambda b, p: (b, 0, 2 * qk_tiles * V7X_LANES // (2 * DV) + p)),
            pl.BlockSpec((1, S, 2 * DV), lambda b, p: (b, 0, (2 * qk_tiles * V7X_LANES + H * DV) // (2 * DV) + p)),
            pl.BlockSpec((S, V7X_LANES), lambda b, p: (0, 0)),
            pl.BlockSpec((S, V7X_LANES), lambda b, p: (0, 0)),
            pl.BlockSpec((2, C, C), lambda b, p: (p, 0, 0)),
            pl.BlockSpec((2, C, V7X_LANES), lambda b, p: (p, 0, 0)),
            pl.BlockSpec((2, C, V7X_LANES), lambda b, p: (p, 0, 0)),
            pl.BlockSpec((2, 8, V7X_LANES), lambda b, p: (p, 0, 0)),
        ],
        out_specs=pl.BlockSpec((1, S, 2 * DV), lambda b, p: (b, 0, p)),
        out_shape=jax.ShapeDtypeStruct((B, S, H * DV), BF16),
        scratch_shapes=[pltpu.VMEM((2, V7X_LANES, DV), F32)],
        compiler_params=_params("parallel", "parallel"),
        name="retention_mixer",
    )(z, z, z, z, cos, sin, d_in, d_q, d_k, d_c)


def _even_mixer(x, g_norm, w_in, shift_mu, w0, w_up, a0, a_up, g_up, k_k, k_a, r_k, ln_g, ln_b):
    B, S, D = x.shape
    z = norm_matmul(x.reshape(B * S, D), g_norm, w_in.astype(BF16)).reshape(B, S, -1)
    o_a = moba_attention(z)
    o_b = rwkv7_mixer(z, shift_mu, w0, w_up, a0, a_up, g_up, k_k, k_a, r_k, ln_g, ln_b)
    return o_a, o_b


def _odd_mixer(x, g_norm, w_in, pe_k, w1_k, w2_k, pe_v, w1_v, w2_v):
    B, S, D = x.shape
    perm, col = _odd_layout()
    w_p = jnp.take(jnp.pad(w_in, ((0, 0), (0, 1))), perm, axis=1).astype(BF16)
    z = norm_matmul(x.reshape(B * S, D), g_norm, w_p).reshape(B, S, -1)
    o_c = retention_mixer(z)
    cmp_kv, cmp_vk = nsa_compress(z, col["kc"], col["vc"], pe_k, w1_k, w2_k, pe_v, w1_v, w2_v)
    o_d = nsa_attention(z, cmp_kv, cmp_vk, col["nq"], col["slc"], col["win"], col["gate"])
    return o_c, o_d


def _odd_layout():
    G, Dh = NSA_KV_GROUPS, HEAD_DIM
    sizes = (RET_HEADS * RET_QK_DIM, RET_HEADS * RET_QK_DIM, RET_HEADS * RET_V_DIM, RET_HEADS * RET_V_DIM,
             NSA_HEADS * Dh) + (G * Dh,) * 6 + (3 * NSA_HEADS,)
    off = np.concatenate([[0], np.cumsum(sizes)])
    rq, rk, rv, rg, nq, kc, vc, ks, vs, kw, vw, ng = off[:-1]
    n_in = int(off[-1])
    pair = lambda a, b: np.concatenate([np.concatenate([a + g * Dh + np.arange(Dh), b + g * Dh + np.arange(Dh)])
                                        for g in range(G)])
    perm = np.concatenate([np.arange(ks), pair(ks, vs), pair(kw, vw), ng + np.arange(3 * NSA_HEADS)])
    n_pad = -(-len(perm) // (6 * V7X_MXU_DIM)) * 6 * V7X_MXU_DIM
    perm = np.concatenate([perm, np.full(n_pad - len(perm), n_in)]).astype(np.int32)
    col = {"nq": int(nq), "kc": int(kc), "vc": int(vc), "slc": int(ks), "win": int(ks) + 2 * G * Dh,
           "gate": int(ks) + 4 * G * Dh}
    return perm, col


def kernel(x, mix_norm, ffn_norm, even_w_in, even_shift_mu, even_w0, even_w_up, even_a0, even_a_up, even_g_up, even_k_k, even_k_a, even_r_k, even_ln_g, even_ln_b, even_w_out, odd_w_in, odd_cmp_pe_k, odd_cmp_w1_k, odd_cmp_w2_k, odd_cmp_pe_v, odd_cmp_w1_v, odd_cmp_w2_v, odd_w_out, ffn_w1, ffn_w3, ffn_w2, final_norm):
    B, S, D = x.shape
    depth = mix_norm.shape[0]
    for layer in range(depth):
        i = layer // 2
        if layer % 2 == 0:
            o1, o2 = _even_mixer(x, mix_norm[layer], even_w_in[i], even_shift_mu[i], even_w0[i], even_w_up[i],
                                 even_a0[i], even_a_up[i], even_g_up[i], even_k_k[i], even_k_a[i], even_r_k[i],
                                 even_ln_g[i], even_ln_b[i])
            w_out = even_w_out[i]
        else:
            o1, o2 = _odd_mixer(x, mix_norm[layer], odd_w_in[i], odd_cmp_pe_k[i], odd_cmp_w1_k[i], odd_cmp_w2_k[i],
                                odd_cmp_pe_v[i], odd_cmp_w1_v[i], odd_cmp_w2_v[i])
            w_out = odd_w_out[i]
        T = B * S
        x2 = mix_ffn_residual(o1.reshape(T, -1), o2.reshape(T, -1), w_out.astype(BF16), x.reshape(T, D),
                              ffn_norm[layer], ffn_w1[layer].astype(BF16), ffn_w3[layer].astype(BF16),
                              ffn_w2[layer].astype(BF16), final_norm if layer == depth - 1 else None)
        x = x2.reshape(B, S, D)
    return x
```

```python
import functools

import jax
import jax.numpy as jnp
import numpy as np
from jax import lax
from jax.experimental import pallas as pl
from jax.experimental.pallas import tpu as pltpu

F32 = jnp.float32
BF16 = jnp.bfloat16

V7X_LANES = 128
V7X_MXU_DIM = 256
V7X_VMEM_BYTES = 64 * 1024 * 1024
VMEM_LIMIT = V7X_VMEM_BYTES * 7 // 8

NORM_EPS = 1e-6
HEAD_DIM = 64

MOBA_BLOCK = 256
MOBA_TOPK = 3
RWKV_HEADS = 16
RWKV_GN_EPS = 6.4e-4

RET_HEADS = 8
RET_QK_DIM = 64
RET_V_DIM = 128
RET_CHUNK = 128
RET_GN_EPS = 1e-6
ROPE_BASE = 10000.0
NSA_HEADS = 16
NSA_KV_GROUPS = 4
NSA_CMP_BLOCK = 32
NSA_CMP_STRIDE = 16
NSA_SLC_BLOCK = 64
NSA_SLC_TOPN = 16
NSA_WINDOW = 512


def _params(*semantics):
    return pltpu.CompilerParams(dimension_semantics=semantics, vmem_limit_bytes=VMEM_LIMIT)


def _rms(x, g):
    return x * lax.rsqrt(jnp.mean(x * x, axis=-1, keepdims=True) + NORM_EPS) * g


def _norm_matmul_kernel(x_ref, g_ref, w_ref, o_ref):
    x = x_ref[...]
    scale = lax.rsqrt(jnp.mean(x * x, axis=-1, keepdims=True) + NORM_EPS)
    o_ref[...] = jnp.dot((x * g_ref[...]).astype(BF16), w_ref[...], preferred_element_type=F32) * scale


def _proj_tile(n):
    assert n % V7X_MXU_DIM == 0
    k = n // V7X_MXU_DIM
    return V7X_MXU_DIM * max(d for d in range(1, 7) if k % d == 0)


def norm_matmul(x, g, w, *, tm=512):
    T, D = x.shape
    N = w.shape[1]
    tn = _proj_tile(N)
    assert T % tm == 0 and N % tn == 0
    return pl.pallas_call(
        _norm_matmul_kernel,
        grid=(N // tn, T // tm),
        in_specs=[
            pl.BlockSpec((tm, D), lambda j, i: (i, 0)),
            pl.BlockSpec((1, D), lambda j, i: (0, 0)),
            pl.BlockSpec((D, tn), lambda j, i: (0, j)),
        ],
        out_specs=pl.BlockSpec((tm, tn), lambda j, i: (i, j)),
        out_shape=jax.ShapeDtypeStruct((T, N), F32),
        compiler_params=_params("parallel", "parallel"),
        name="norm_matmul",
    )(x, g.reshape(1, D), w)


def _mix_ffn_kernel(a_ref, b_ref, wa_ref, wb_ref, x_ref, g_ref, w1_ref, w3_ref, w2_ref, gf_ref, o_ref,
                    h_ref, acc_ref, *, final_norm):
    j = pl.program_id(1)

    @pl.when(j == 0)
    def _():
        x2 = (x_ref[...] + jnp.dot(a_ref[...], wa_ref[...], preferred_element_type=F32)
              + jnp.dot(b_ref[...], wb_ref[...], preferred_element_type=F32))
        o_ref[...] = x2
        h_ref[...] = _rms(x2, g_ref[...]).astype(BF16)
        acc_ref[...] = jnp.zeros_like(acc_ref)

    h = h_ref[...]
    a = jnp.dot(h, w1_ref[...], preferred_element_type=F32)
    b = jnp.dot(h, w3_ref[...], preferred_element_type=F32)
    act = (a * jax.nn.sigmoid(a) * b).astype(BF16)
    acc_ref[...] += jnp.dot(act, w2_ref[...], preferred_element_type=F32)

    @pl.when(j == pl.num_programs(1) - 1)
    def _():
        y = o_ref[...] + acc_ref[...]
        if final_norm:
            y = _rms(y, gf_ref[...])
        o_ref[...] = y


def mix_ffn_residual(a, b, w_out, x, g, w1, w3, w2, layer, g_final=None, *, tm=512, tf=512):
    T, D = x.shape
    K = a.shape[1]
    Fh = w1.shape[2]
    assert T % tm == 0 and Fh % tf == 0 and b.shape == a.shape and w_out.shape == (2 * K, D)
    final_norm = g_final is not None
    gf = (g_final if final_norm else g).reshape(1, D)
    once = pl.Buffered(1)
    return pl.pallas_call(
        functools.partial(_mix_ffn_kernel, final_norm=final_norm),
        grid=(T // tm, Fh // tf),
        in_specs=[
            pl.BlockSpec((tm, K), lambda i, j: (i, 0)),
            pl.BlockSpec((tm, K), lambda i, j: (i, 0)),
            pl.BlockSpec((K, D), lambda i, j: (0, 0), pipeline_mode=once),
            pl.BlockSpec((K, D), lambda i, j: (1, 0), pipeline_mode=once),
            pl.BlockSpec((tm, D), lambda i, j: (i, 0)),
            pl.BlockSpec((1, D), lambda i, j: (0, 0), pipeline_mode=once),
            pl.BlockSpec((None, D, tf), lambda i, j: (layer, 0, j)),
            pl.BlockSpec((None, D, tf), lambda i, j: (layer, 0, j)),
            pl.BlockSpec((None, tf, D), lambda i, j: (layer, j, 0)),
            pl.BlockSpec((1, D), lambda i, j: (0, 0), pipeline_mode=once),
        ],
        out_specs=pl.BlockSpec((tm, D), lambda i, j: (i, 0)),
        out_shape=jax.ShapeDtypeStruct((T, D), F32),
        scratch_shapes=[pltpu.VMEM((tm, D), BF16), pltpu.VMEM((tm, D), F32)],
        compiler_params=_params("parallel", "arbitrary"),
        name="mix_ffn_residual",
    )(a, b, w_out, w_out, x, g.reshape(1, D), w1, w3, w2, gf)


NEG_BIG = -1e30
_NT = (((1,), (1,)), ((), ()))


def _flash_steps(qas, kas, vas, masks, m_prev, acc_prev):
    hs = range(len(qas))
    s = [lax.dot_general(qas[h], kas[h], _NT, preferred_element_type=F32) for h in hs]
    s = [s[h] if masks[h] is None else jnp.where(masks[h], s[h], NEG_BIG) for h in hs]
    m_new = [jnp.maximum(m_prev[h], jnp.max(s[h], axis=1, keepdims=True)) for h in hs]
    alpha = [jnp.exp2(m_prev[h] - m_new[h]) for h in hs]
    p = [jnp.exp2(s[h] - m_new[h]) for h in hs]
    pv = [jnp.dot(p[h].astype(BF16), vas[h], preferred_element_type=F32) for h in hs]
    return m_new, [alpha[h] * acc_prev[h] + pv[h] for h in hs]


def _augment_q(q_log2, in_head, keep_t, odd):
    nblk, tq = keep_t.shape
    bias_t = jnp.where(keep_t, 0.0, NEG_BIG)
    bias = jnp.concatenate([bias_t, jnp.zeros((V7X_LANES - nblk, tq), F32)], axis=0).T
    if not odd:
        bias = pltpu.roll(bias, HEAD_DIM, axis=1)
    return jnp.where(in_head, q_log2, bias).astype(BF16)


def _key_value_tiles(k, v, blk, lane):
    low = lane < HEAD_DIM
    hot_e = 0.0 if blk is None else jnp.where(lane - HEAD_DIM == blk, 1.0, 0.0)
    hot_o = 0.0 if blk is None else jnp.where(lane == blk, 1.0, 0.0)
    k_e, k_o = jnp.where(low, k, hot_e), jnp.where(low, hot_o, k)
    v_e, v_o = jnp.where(low, v, jnp.where(lane == HEAD_DIM, 1.0, 0.0)), jnp.where(low, jnp.where(lane == 0, 1.0, 0.0), v)
    return [t.astype(BF16) for t in (k_e, k_o, v_e, v_o)]


def _normalise(acc, lane, odd):
    return acc / jnp.sum(jnp.where(lane == (0 if odd else HEAD_DIM), acc, 0.0), axis=1, keepdims=True)


LOG2E = 1.4426950408889634


MOBA_HEADS_PER_STEP = 8


def _split_bf16(x):
    hi = x.astype(BF16)
    return hi, (x - hi.astype(F32)).astype(BF16)


def _moba_kernel(q_ref, k_ref, v_ref, o_ref, ka_ref, va_ref, km_ref, acc_ref):
    L = MOBA_BLOCK
    S = k_ref.shape[1]
    nb = S // L
    HP = MOBA_HEADS_PER_STEP
    qi = pl.program_id(2)
    lane = lax.broadcasted_iota(jnp.int32, (L, V7X_LANES), 1)
    lanes_of = lambda ref, pp: ref[0, :, pp * V7X_LANES:(pp + 1) * V7X_LANES]

    @pl.when(qi == 0)
    def _():
        lane_s = lax.broadcasted_iota(jnp.int32, (S, V7X_LANES), 1)
        blk = lax.broadcasted_iota(jnp.int32, (S, V7X_LANES), 0) // L
        for pp in range(HP // 2):
            k = lanes_of(k_ref, pp)
            (ka_ref[2 * pp], ka_ref[2 * pp + 1], va_ref[2 * pp], va_ref[2 * pp + 1]) = _key_value_tiles(
                k, lanes_of(v_ref, pp), blk, lane_s)
            km_ref[pp] = jnp.concatenate(_split_bf16(jnp.mean(k.reshape(nb, L, V7X_LANES), axis=1)), axis=0)

    row = lax.broadcasted_iota(jnp.int32, (L, L), 0)
    col = lax.broadcasted_iota(jnp.int32, (L, L), 1)
    causal = col <= row
    jrow = lax.broadcasted_iota(jnp.int32, (nb, L), 0)
    past = jrow < qi
    qas = []
    for pp in range(HP // 2):
        q = lanes_of(q_ref, pp) * (HEAD_DIM ** -0.5)
        q_hi, q_lo = _split_bf16(q)
        km = km_ref[pp]
        for e in range(2):
            in_head = (lane >= e * HEAD_DIM) & (lane < (e + 1) * HEAD_DIM)
            zero = jnp.zeros_like(q_hi)
            g1 = lax.dot_general(km, jnp.where(in_head, q_hi, zero), _NT, preferred_element_type=F32)
            g2 = lax.dot_general(km, jnp.where(in_head, q_lo, zero), _NT, preferred_element_type=F32)
            gate = g1[0:nb] + g1[nb:2 * nb] + g2[0:nb]
            keep = jrow == qi
            for n in range(nb):
                g_n = gate[n:n + 1, :]
                beats = (gate > g_n) | ((gate == g_n) & (jrow < n))
                rank = jnp.sum(jnp.where(past & beats, 1.0, 0.0), axis=0, keepdims=True)
                keep = keep | ((jrow == n) & (rank < MOBA_TOPK) & past)
            qas.append(_augment_q(q * LOG2E, in_head, keep, odd=e == 1))

    tiles = lambda start: ([ka_ref[h, pl.ds(start, L), :] for h in range(HP)],
                           [va_ref[h, pl.ds(start, L), :] for h in range(HP)])

    m, acc = _flash_steps(qas, *tiles(pl.multiple_of(qi * L, L)), [causal] * HP,
                          [jnp.full((L, 1), NEG_BIG, F32)] * HP, [jnp.zeros((L, V7X_LANES), F32)] * HP)
    for h in range(HP):
        acc_ref[h] = acc[h]

    def body(n, carry):
        m2, acc2 = _flash_steps(qas, *tiles(pl.multiple_of(n * L, L)), [None] * HP, list(carry),
                                [acc_ref[h] for h in range(HP)])
        for h in range(HP):
            acc_ref[h] = acc2[h]
        return tuple(m2)

    lax.fori_loop(0, qi, body, tuple(m))
    for pp in range(HP // 2):
        o_ref[0, :, pp * V7X_LANES:(pp + 1) * V7X_LANES] = jnp.where(
            lane < HEAD_DIM, _normalise(acc_ref[2 * pp], lane, False), _normalise(acc_ref[2 * pp + 1], lane, True)
        ).astype(o_ref.dtype)


def moba_attention(z, *, n_heads=16):
    B, S, _ = z.shape
    L, HP = MOBA_BLOCK, MOBA_HEADS_PER_STEP
    W = HP * HEAD_DIM
    nb = S // L
    assert S % L == 0 and n_heads % HP == 0 and W % V7X_LANES == 0 and 2 * nb <= 16
    ngrp = n_heads // HP
    return pl.pallas_call(
        _moba_kernel,
        grid=(B, ngrp, S // L),
        in_specs=[
            pl.BlockSpec((1, L, W), lambda b, p, i: (b, i, p)),
            pl.BlockSpec((1, S, W), lambda b, p, i: (b, 0, ngrp + p)),
            pl.BlockSpec((1, S, W), lambda b, p, i: (b, 0, 2 * ngrp + p)),
        ],
        out_specs=pl.BlockSpec((1, L, W), lambda b, p, i: (b, i, p)),
        out_shape=jax.ShapeDtypeStruct((B, S, n_heads * HEAD_DIM), BF16),
        scratch_shapes=[
            pltpu.VMEM((HP, S, V7X_LANES), BF16),
            pltpu.VMEM((HP, S, V7X_LANES), BF16),
            pltpu.VMEM((HP // 2, 2 * nb, V7X_LANES), BF16),
            pltpu.VMEM((HP, L, V7X_LANES), F32),
        ],
        compiler_params=_params("parallel", "parallel", "arbitrary"),
        name="moba_attention",
    )(z, z, z)


NSA_TQ = 256
NSA_GROUPS_PER_STEP = 2
BIG = 3.0e38


def _gelu_tanh(x):
    return 0.5 * x * (1.0 + jnp.tanh(0.7978845608028654 * (x + 0.044715 * x * x * x)))


def _nsa_compress_kernel(xk0_ref, xk1_ref, xv0_ref, xv1_ref, pek_ref, pev_ref, w1k_ref, w1v_ref, w2k_ref, w2v_ref,
                         o1_ref, o2_ref):
    G, Lc, st = NSA_KV_GROUPS, NSA_CMP_BLOCK, NSA_CMP_STRIDE
    nrow = xk0_ref.shape[1] // st
    lane = lax.broadcasted_iota(jnp.int32, (nrow, G * HEAD_DIM), 1)

    def hidden(x_refs, pe_ref, w1_ref):
        acc = [jnp.zeros((G * nrow, V7X_LANES), F32) for _ in range(Lc // st)]
        for l in range(Lc):
            u, m = divmod(l, st)
            x = jnp.concatenate([r[0, pl.ds(m, nrow, stride=st), :] for r in x_refs], axis=1) + pe_ref[l:l + 1, :]
            xs = jnp.concatenate(
                [jnp.where((lane >= g * HEAD_DIM) & (lane < (g + 1) * HEAD_DIM), x, 0.0) for g in range(G)],
                axis=0).astype(BF16)
            acc[u] = acc[u] + jnp.dot(xs, w1_ref[l], preferred_element_type=F32)
        nxt = jnp.concatenate([pltpu.roll(acc[1][g * nrow:(g + 1) * nrow], nrow - 1, axis=0) for g in range(G)],
                              axis=0)
        return _gelu_tanh(acc[0] + nxt).astype(BF16)

    hk = hidden((xk0_ref, xk1_ref), pek_ref, w1k_ref)
    hv = hidden((xv0_ref, xv1_ref), pev_ref, w1v_ref)
    kc = jnp.dot(hk, w2k_ref[...], preferred_element_type=F32)
    vc = jnp.dot(hv, w2v_ref[...], preferred_element_type=F32)
    kv = kc + vc
    vk = pltpu.roll(kv, HEAD_DIM, axis=1)
    for g in range(G):
        o1_ref[0, :, g * V7X_LANES:(g + 1) * V7X_LANES] = kv[g * nrow:(g + 1) * nrow]
        o2_ref[0, :, g * V7X_LANES:(g + 1) * V7X_LANES] = vk[g * nrow:(g + 1) * nrow]


def nsa_compress(z, col_k, col_v, pe_k, w1_k, w2_k, pe_v, w1_v, w2_v):
    B, S, _ = z.shape
    G, Lc, st = NSA_KV_GROUPS, NSA_CMP_BLOCK, NSA_CMP_STRIDE
    GW = G * HEAD_DIM
    nrow = S // st
    hid = w1_k.shape[1]
    assert hid == V7X_LANES and col_k % GW == 0 and col_v % GW == 0
    tile_pe = lambda pe: jnp.tile(pe, (1, G))
    tile_w1 = lambda w: jnp.tile(w.reshape(Lc, 1, HEAD_DIM, hid), (1, G, 1, 1)).reshape(Lc, GW, hid).astype(BF16)
    w2k = jnp.pad(w2_k, ((0, 0), (0, HEAD_DIM))).astype(BF16)
    w2v = jnp.pad(w2_v, ((0, 0), (HEAD_DIM, 0))).astype(BF16)
    const = lambda shape: pl.BlockSpec(shape, lambda b: (0,) * len(shape))
    out = jax.ShapeDtypeStruct((B, nrow, G * V7X_LANES), F32)
    return pl.pallas_call(
        _nsa_compress_kernel,
        grid=(B,),
        in_specs=[
            pl.BlockSpec((1, S, V7X_LANES), lambda b: (b, 0, col_k // V7X_LANES)),
            pl.BlockSpec((1, S, V7X_LANES), lambda b: (b, 0, col_k // V7X_LANES + 1)),
            pl.BlockSpec((1, S, V7X_LANES), lambda b: (b, 0, col_v // V7X_LANES)),
            pl.BlockSpec((1, S, V7X_LANES), lambda b: (b, 0, col_v // V7X_LANES + 1)),
            const((Lc, GW)), const((Lc, GW)),
            const((Lc, GW, hid)), const((Lc, GW, hid)),
            const((hid, V7X_LANES)), const((hid, V7X_LANES)),
        ],
        out_specs=[pl.BlockSpec((1, nrow, G * V7X_LANES), lambda b: (b, 0, 0))] * 2,
        out_shape=[out, out],
        compiler_params=_params("parallel"),
        name="nsa_compress",
    )(z, z, z, z, tile_pe(pe_k), tile_pe(pe_v), tile_w1(w1_k), tile_w1(w1_v), w2k, w2v)


def _nsa_kernel(q_ref, c1_ref, c2_ref, s_ref, w_ref, g_ref, ovt_ref, o_ref, sk_ref, sv_ref, wk_ref, wv_ref, acc_ref):
    TQ = NSA_TQ
    S = s_ref.shape[1]
    R = NSA_HEADS // NSA_KV_GROUPS
    NG = NSA_GROUPS_PER_STEP
    NH = NG * R
    grp0 = pl.program_id(1) * NG
    qi = pl.program_id(2)
    lane = lax.broadcasted_iota(jnp.int32, (TQ, V7X_LANES), 1)
    tile_of = lambda ref, t: ref[0, :, t * V7X_LANES:(t + 1) * V7X_LANES]

    @pl.when(qi == 0)
    def _():
        lane_s = lax.broadcasted_iota(jnp.int32, (S, V7X_LANES), 1)
        blk = lax.broadcasted_iota(jnp.int32, (S, V7X_LANES), 0) // NSA_SLC_BLOCK
        for gg in range(NG):
            for src, k_ref, v_ref, hot in ((s_ref, sk_ref, sv_ref, blk), (w_ref, wk_ref, wv_ref, None)):
                kv = tile_of(src, gg)
                vk = pltpu.roll(kv, HEAD_DIM, axis=1)
                k_ref[2 * gg], k_ref[2 * gg + 1], v_ref[2 * gg], v_ref[2 * gg + 1] = _key_value_tiles(
                    jnp.where(lane_s < HEAD_DIM, kv, vk), jnp.where(lane_s < HEAD_DIM, vk, kv), hot, lane_s)

    q0 = pl.multiple_of(qi * TQ, TQ)
    row = lax.broadcasted_iota(jnp.int32, (TQ, TQ), 0)
    col = lax.broadcasted_iota(jnp.int32, (TQ, TQ), 1)
    causal = col <= row
    t_abs = q0 + lax.broadcasted_iota(jnp.int32, (TQ, V7X_LANES), 0)
    even_lanes = lane < HEAD_DIM

    cmask = lane * NSA_CMP_STRIDE + (NSA_CMP_BLOCK - 1) <= t_abs
    nblk = s_ref.shape[1] // NSA_SLC_BLOCK
    jrow = lax.broadcasted_iota(jnp.int32, (nblk, TQ), 0)
    own = (q0 + lax.broadcasted_iota(jnp.int32, (nblk, TQ), 1)) // NSA_SLC_BLOCK
    ovt = ovt_ref[...]
    head_lanes = [even_lanes if h % 2 == 0 else ~even_lanes for h in range(NH)]
    tiles = [tile_of(q_ref, h // 2) * (HEAD_DIM ** -0.5) for h in range(NH)]
    o_cmp, qas = [], []
    for gg in range(NG):
        c_kv_b, c_vk_b = tile_of(c1_ref, gg).astype(BF16), tile_of(c2_ref, gg).astype(BF16)
        p_sum = jnp.zeros((TQ, V7X_LANES), F32)
        for h in range(gg * R, (gg + 1) * R):
            qm = jnp.where(head_lanes[h], tiles[h], 0.0).astype(BF16)
            s = lax.dot_general(qm, c_kv_b if h % 2 == 0 else c_vk_b, _NT, preferred_element_type=F32)
            s = jnp.where(cmask, s, NEG_BIG)
            p = jnp.where(cmask, jnp.exp(s - jnp.max(s, axis=1, keepdims=True)), 0.0)
            den = jnp.sum(p, axis=1, keepdims=True)
            p = p / jnp.where(den > 0.0, den, 1.0)
            p_sum = p_sum + p
            o_cmp.append(jnp.dot(p.astype(BF16), c_vk_b if h % 2 == 0 else c_kv_b, preferred_element_type=F32))

        p_hi, p_lo = _split_bf16(p_sum)
        p_slc = (lax.dot_general(ovt, p_hi, _NT, preferred_element_type=F32)
                 + lax.dot_general(ovt, p_lo, _NT, preferred_element_type=F32))[0:nblk]
        score = jnp.where((jrow == own) | (jrow == 0), BIG, jnp.where(jrow > own, -BIG, p_slc))
        keep = jrow > nblk
        for j in range(nblk):
            s_j = score[j:j + 1, :]
            beats = (score > s_j) | ((score == s_j) & (jrow < j))
            rank = jnp.sum(jnp.where(beats, 1.0, 0.0), axis=0, keepdims=True)
            keep = keep | ((jrow == j) & (rank < NSA_SLC_TOPN) & (jrow <= own))
        qas += [_augment_q(tiles[h] * LOG2E, head_lanes[h], keep, odd=h % 2 == 1) for h in range(gg * R, (gg + 1) * R)]

    neg = [jnp.full((TQ, 1), NEG_BIG, F32)] * NH
    zacc = [jnp.zeros((TQ, V7X_LANES), F32)] * NH
    kv_index = [2 * (h // R) + h % 2 for h in range(NH)]

    def kv_blocks(k_ref, v_ref, start):
        return ([k_ref[kv_index[h], pl.ds(start, TQ), :] for h in range(NH)],
                [v_ref[kv_index[h], pl.ds(start, TQ), :] for h in range(NH)])

    m, acc = _flash_steps(qas, *kv_blocks(sk_ref, sv_ref, q0), [causal] * NH, neg, zacc)
    for h in range(NH):
        acc_ref[h] = acc[h]

    def body(kb, carry):
        m2, acc2 = _flash_steps(qas, *kv_blocks(sk_ref, sv_ref, pl.multiple_of(kb * TQ, TQ)), [None] * NH,
                                list(carry), [acc_ref[h] for h in range(NH)])
        for h in range(NH):
            acc_ref[h] = acc2[h]
        return tuple(m2)

    lax.fori_loop(0, qi, body, tuple(m))

    WK = NSA_WINDOW + TQ
    w0 = pl.multiple_of(jnp.maximum(qi - NSA_WINDOW // TQ, 0) * TQ, TQ)
    key_pos = w0 + lax.broadcasted_iota(jnp.int32, (TQ, WK), 1)
    t_win = q0 + lax.broadcasted_iota(jnp.int32, (TQ, WK), 0)
    in_window = (key_pos <= t_win) & (key_pos > t_win - NSA_WINDOW)
    _, acc = _flash_steps(qas, [wk_ref[kv_index[h], pl.ds(w0, WK), :] for h in range(NH)],
                          [wv_ref[kv_index[h], pl.ds(w0, WK), :] for h in range(NH)], [in_window] * NH, neg, zacc)

    gates = jax.nn.sigmoid(g_ref[0])
    outs = []
    for h in range(NH):
        o_slc = _normalise(acc_ref[h], lane, h % 2 == 1)
        o_win = _normalise(acc[h], lane, h % 2 == 1)
        c0 = (grp0 * R + h) * 3
        gate = lambda c: jnp.sum(jnp.where(lane == c, gates, 0.0), axis=1, keepdims=True)
        outs.append(gate(c0) * o_cmp[h] + gate(c0 + 1) * o_slc + gate(c0 + 2) * o_win)
    for p2 in range(NH // 2):
        o_ref[0, :, p2 * V7X_LANES:(p2 + 1) * V7X_LANES] = jnp.where(
            even_lanes, outs[2 * p2], outs[2 * p2 + 1]).astype(o_ref.dtype)


def nsa_attention(z, cmp_kv, cmp_vk, col_q, col_slc, col_win, col_gate):
    B, S, _ = z.shape
    G, TQ, NG = NSA_KV_GROUPS, NSA_TQ, NSA_GROUPS_PER_STEP
    R = NSA_HEADS // G
    QW = NG * R * HEAD_DIM
    KW = NG * V7X_LANES
    ncmp = cmp_kv.shape[1]
    assert S % TQ == 0 and ncmp == V7X_LANES and S // NSA_SLC_BLOCK <= V7X_LANES
    assert NSA_WINDOW % TQ == 0 and S >= NSA_WINDOW + TQ
    assert G % NG == 0 and R % 2 == 0
    assert col_q % QW == 0 and col_slc % KW == 0 and col_win % KW == 0 and col_gate % V7X_LANES == 0
    nc = (S - NSA_CMP_BLOCK) // NSA_CMP_STRIDE + 1
    c_start = np.arange(V7X_LANES) * NSA_CMP_STRIDE
    s_start = np.arange(V7X_LANES) * NSA_SLC_BLOCK
    overlap = ((c_start[:, None] <= s_start[None, :] + NSA_SLC_BLOCK - 1)
               & (c_start[:, None] + NSA_CMP_BLOCK - 1 >= s_start[None, :])
               & (np.arange(V7X_LANES)[:, None] < nc) & (np.arange(V7X_LANES)[None, :] < S // NSA_SLC_BLOCK))
    const = lambda shape: pl.BlockSpec(shape, lambda b, g, i: (0,) * len(shape))
    return pl.pallas_call(
        _nsa_kernel,
        grid=(B, G // NG, S // TQ),
        in_specs=[
            pl.BlockSpec((1, TQ, QW), lambda b, g, i: (b, i, col_q // QW + g)),
            pl.BlockSpec((1, ncmp, KW), lambda b, g, i: (b, 0, g)),
            pl.BlockSpec((1, ncmp, KW), lambda b, g, i: (b, 0, g)),
            pl.BlockSpec((1, S, KW), lambda b, g, i: (b, 0, col_slc // KW + g)),
            pl.BlockSpec((1, S, KW), lambda b, g, i: (b, 0, col_win // KW + g)),
            pl.BlockSpec((1, TQ, V7X_LANES), lambda b, g, i: (b, i, col_gate // V7X_LANES)),
            const((V7X_LANES, V7X_LANES)),
        ],
        out_specs=pl.BlockSpec((1, TQ, QW), lambda b, g, i: (b, i, g)),
        out_shape=jax.ShapeDtypeStruct((B, S, NSA_HEADS * HEAD_DIM), BF16),
        scratch_shapes=[pltpu.VMEM((2 * NG, S, V7X_LANES), BF16)] * 4 + [
            pltpu.VMEM((NG * R, TQ, V7X_LANES), F32),
        ],
        compiler_params=_params("parallel", "parallel", "arbitrary"),
        name="nsa_attention",
    )(z, cmp_kv, cmp_vk, z, z, z, jnp.asarray(overlap.T, BF16))


RWKV_CHUNK = 64
RWKV_ROWS = 256
RWKV_INTERLEAVE = 8


def _mm(a, b, dims=None):
    dims = dims or (((1,), (0,)), ((), ()))
    return lax.dot_general(a.astype(BF16), b.astype(BF16), dims, preferred_element_type=F32)


def _mm3(a, b):
    (a_hi, a_lo), (b_hi, b_lo) = _split_bf16(a), _split_bf16(b)
    return _mm(a_hi, b_hi) + (_mm(a_hi, b_lo) + _mm(a_lo, b_hi))


def _mm_onehot(a01, b):
    hi = b.astype(BF16)
    mid, lo = _split_bf16(b - hi.astype(F32))
    return _mm(a01, hi) + (_mm(a01, mid) + _mm(a01, lo))


def _head_sum(x, low):
    s0 = jnp.sum(jnp.where(low, x, 0.0), axis=1, keepdims=True)
    s1 = jnp.sum(jnp.where(low, 0.0, x), axis=1, keepdims=True)
    return jnp.where(low, s0, s1)


def _rwkv_kernel(r_ref, k_ref, v_ref, lo_ref, glo_ref, pp_ref, pl_ref, wup_ref, aup_ref, gup_ref, o_ref,
                 rs, ws, ks, vs, als, bes, gs, ys, bon, hs, rqs, ms, ns):
    S = r_ref.shape[1]
    C, RB = RWKV_CHUNK, RWKV_ROWS
    pp = pp_ref[...]
    mu_r, mu_k, mu_v, w0, a0, k_k, k_a, r_k, ln_g, ln_b = [pp[i:i + 1, :] for i in range(10)]
    mu_lo, mu_g = pl_ref[0:1, :], pl_ref[1:2, :]
    heads = lax.broadcasted_iota(jnp.int32, (RB, V7X_LANES), 1) < HEAD_DIM
    first = lax.broadcasted_iota(jnp.int32, (RB, V7X_LANES), 0) == 0

    def prologue(i, c):
        t0 = pl.multiple_of(i * RB, RB)
        tp = jnp.maximum(t0 - 1, 0)
        keep = jnp.where(i > 0, 1.0, 0.0)

        def shifted(ref, mu):
            x = ref[0, pl.ds(t0, RB), :]
            prev = jnp.where(first, ref[0, pl.ds(tp, 1), :] * keep, pltpu.roll(x, 1, axis=0))
            return x + (prev - x) * mu

        r, k, v = shifted(r_ref, mu_r), shifted(k_ref, mu_k), shifted(v_ref, mu_v)
        lo, glo = shifted(lo_ref, mu_lo), shifted(glo_ref, mu_g)
        wp = -(w0 + _mm(jnp.tanh(lo), wup_ref[...]))
        w = -(jnp.maximum(wp, 0.0) + jnp.log(1.0 + jnp.exp(-jnp.abs(wp)))) - 0.5
        a = jax.nn.sigmoid(a0 + _mm(lo, aup_ref[...]))
        kk = k * k_k
        kk = kk * lax.rsqrt(jnp.maximum(_head_sum(kk * kk, heads), 1e-24))
        k2 = k * (1.0 + (a - 1.0) * k_a)
        rs[pl.ds(t0, RB), :] = r
        ws[pl.ds(t0, RB), :] = -jnp.exp(w)
        ks[pl.ds(t0, RB), :] = k2
        vs[pl.ds(t0, RB), :] = v
        als[pl.ds(t0, RB), :] = -kk
        bes[pl.ds(t0, RB), :] = kk * a
        gs[pl.ds(t0, RB), :] = _mm(jax.nn.sigmoid(glo), gup_ref[...])
        bon[pl.ds(t0, RB), :] = _head_sum(r * k2 * r_k, heads) * v
        return c

    lax.fori_loop(0, S // RB, prologue, 0)

    W2 = 2 * C
    row = lax.broadcasted_iota(jnp.int32, (W2, W2), 0)
    col = lax.broadcasted_iota(jnp.int32, (W2, W2), 1)
    t_idx, s_idx = row % C, col % C
    top, left = row < C, col < C
    same = top == left
    eye = jnp.where(row == col, 1.0, 0.0)
    tri = jnp.where(lax.broadcasted_iota(jnp.int32, (C, C), 1) <= lax.broadcasted_iota(jnp.int32, (C, C), 0), 1.0, 0.0)
    low_c = lax.broadcasted_iota(jnp.int32, (C, V7X_LANES), 1) < HEAD_DIM
    fold = lambda x: x[0:C] + x[C:W2]
    stack_heads = lambda x: jnp.concatenate([jnp.where(low_c, x, 0.0), jnp.where(low_c, 0.0, x)], axis=0)
    block_diag = lambda x: jnp.where(top, jnp.where(left, x, 0.0), jnp.where(left, 0.0, pltpu.roll(x, C, axis=1)))

    rows = lambda c: pl.ds(c * C if isinstance(c, int) else pl.multiple_of(c * C, C), C)

    def advance(c, H):
        ys[rows(c), :] += _mm3(rqs[c], H)
        return _mm3(ms[c], H) + ns[c]

    def transfers(i, lagged):
        each = lambda f, *xs: [f(*a) for a in zip(*xs)]
        cs = [i * RWKV_INTERLEAVE + u for u in range(RWKV_INTERLEAVE)]
        sls = [rows(c) for c in cs]
        state = [hs[...]] if lagged else None

        def lag(hook):
            if lagged:
                for u in range(hook * RWKV_INTERLEAVE // 8, (hook + 1) * RWKV_INTERLEAVE // 8):
                    state[0] = advance(cs[u] - RWKV_INTERLEAVE, state[0])

        r, lw, k2, v, al, be = ([ref[sl, :] for sl in sls] for ref in (rs, ws, ks, vs, als, bes))
        logp = each(lambda x: _mm_onehot(tri, x), lw)
        lag(0)
        P = each(jnp.exp, logp)
        Pinv = each(lambda x: jnp.exp(-x), logp)
        At = each(lambda a_, lp, w_: a_ * jnp.exp(lp - w_), al, logp, lw)
        Rt, Bt, Kt = each(jnp.multiply, r, P), each(jnp.multiply, be, Pinv), each(jnp.multiply, k2, Pinv)
        PC = each(lambda p: p[C - 1:C, :], P)
        A_bd, R_bd = each(stack_heads, At), each(stack_heads, Rt)
        Yt = each(lambda b, k: jnp.concatenate([b, k], axis=0), Bt, Kt)
        A1 = each(lambda a, y: jnp.where(s_idx < t_idx, _mm(a, y, dims=_NT), 0.0), A_bd, Yt)
        A2 = each(lambda a, y: jnp.where(s_idx <= t_idx, _mm(a, y, dims=_NT), 0.0), R_bd, Yt)
        X, Arb = each(block_diag, A1), each(block_diag, A2)
        T = each(lambda x: eye + x, X)
        for it in range(5):
            X = each(lambda x: _mm(x, x), X)
            T = each(lambda t, x: t + _mm(t, x), T, X)
            lag(1 + it)
        V0 = each(lambda x: jnp.concatenate([jnp.zeros_like(x), x], axis=0), v)
        TA = each(_mm, T, A_bd)
        AkV = each(lambda a, x: jnp.where(same, _mm(a, x), 0.0), A1, V0)
        lag(6)
        U0 = each(_mm, T, AkV)
        lag(7)
        AR = each(lambda a, t, u: _mm(a, jnp.concatenate([t, u], axis=1)), Arb, TA, U0)
        ArkV = each(lambda a, x: jnp.where(same, _mm(a, x), 0.0), A2, V0)
        Mx = each(lambda b, p, t: _mm((b * p).T, fold(t)), Bt, PC, TA)
        Nx = each(lambda b, k, p, u, x: _mm(jnp.concatenate([b * p, k * p], axis=0).T,
                                            jnp.concatenate([fold(u), x], axis=0)), Bt, Kt, PC, U0, v)
        for u in range(RWKV_INTERLEAVE):
            ys[sls[u], :] = fold(AR[u][:, W2:2 * W2] + ArkV[u])
            rqs[cs[u]] = Rt[u] + fold(AR[u][:, 0:W2])
            ms[cs[u]] = eye * PC[u] + jnp.where(same, Mx[u], 0.0)
            ns[cs[u]] = jnp.where(same, Nx[u], 0.0)
        if lagged:
            hs[...] = state[0]

    def pipelined(i, carry):
        transfers(i, True)
        return carry

    def drain(c, carry):
        hs[...] = advance(c, hs[...])
        return carry

    assert 8 % RWKV_INTERLEAVE == 0
    hs[...] = jnp.zeros((W2, W2), F32)
    transfers(0, False)
    lax.fori_loop(1, S // C // RWKV_INTERLEAVE, pipelined, 0)
    lax.fori_loop(S // C - RWKV_INTERLEAVE, S // C, drain, 0)

    def epilogue(i, c):
        sl = pl.ds(pl.multiple_of(i * RB, RB), RB)
        y = ys[sl, :]
        d = y - _head_sum(y, heads) * (1.0 / HEAD_DIM)
        var = _head_sum(d * d, heads) * (1.0 / HEAD_DIM)
        yn = d * lax.rsqrt(var + RWKV_GN_EPS) * ln_g + ln_b
        o_ref[0, sl, :] = ((yn + bon[sl, :]) * gs[sl, :]).astype(o_ref.dtype)
        return c

    lax.fori_loop(0, S // RB, epilogue, 0)


def rwkv7_mixer(z, shift_mu, w0, w_up, a0, a_up, g_up, k_k, k_a, r_k, ln_g, ln_b):
    B, S, _ = z.shape
    CW = RWKV_HEADS * HEAD_DIM
    npair = CW // V7X_LANES
    base = 3 * CW // V7X_LANES
    lora = w_up.shape[0] + a_up.shape[0]
    assert lora == V7X_LANES and g_up.shape[0] == V7X_LANES and S % RWKV_ROWS == 0
    pp = jnp.stack([shift_mu[0:CW], shift_mu[CW:2 * CW], shift_mu[2 * CW:3 * CW], w0, a0, k_k, k_a,
                    r_k.reshape(CW), ln_g, ln_b])
    pp = jnp.pad(pp, ((0, 16 - pp.shape[0]), (0, 0)))
    pl2 = jnp.pad(shift_mu[3 * CW:].reshape(2, V7X_LANES), ((0, 6), (0, 0)))
    wup = jnp.pad(w_up, ((0, a_up.shape[0]), (0, 0)))
    aup = jnp.pad(a_up, ((w_up.shape[0], 0), (0, 0)))
    tile = lambda off: pl.BlockSpec((1, S, V7X_LANES), lambda b, p: (b, 0, base + off * npair + p))
    fixed = lambda off: pl.BlockSpec((1, S, V7X_LANES), lambda b, p: (b, 0, base + 3 * npair + off))
    seq = pltpu.VMEM((S, V7X_LANES), F32)
    return pl.pallas_call(
        _rwkv_kernel,
        grid=(B, npair),
        in_specs=[
            tile(0), tile(1), tile(2), fixed(0), fixed(1),
            pl.BlockSpec((16, V7X_LANES), lambda b, p: (0, p)),
            pl.BlockSpec((8, V7X_LANES), lambda b, p: (0, 0)),
            pl.BlockSpec((V7X_LANES, V7X_LANES), lambda b, p: (0, p)),
            pl.BlockSpec((V7X_LANES, V7X_LANES), lambda b, p: (0, p)),
            pl.BlockSpec((V7X_LANES, V7X_LANES), lambda b, p: (0, p)),
        ],
        out_specs=pl.BlockSpec((1, S, V7X_LANES), lambda b, p: (b, 0, p)),
        out_shape=jax.ShapeDtypeStruct((B, S, CW), BF16),
        scratch_shapes=[seq] * 9 + [
            pltpu.VMEM((V7X_LANES, V7X_LANES), F32),
            pltpu.VMEM((S // RWKV_CHUNK, RWKV_CHUNK, V7X_LANES), F32),
            pltpu.VMEM((S // RWKV_CHUNK, V7X_LANES, V7X_LANES), F32),
            pltpu.VMEM((S // RWKV_CHUNK, V7X_LANES, V7X_LANES), F32),
        ],
        compiler_params=_params("parallel", "parallel"),
        name="rwkv7_mixer",
    )(z, z, z, z, z, pp, pl2, wup, aup, g_up.astype(BF16))


RET_CHUNKS_PER_STEP = 2


def _ret_kernel(q_ref, k_ref, v_ref, g_ref, cos_ref, sin_ref, din_ref, dq_ref, dk_ref, dc_ref, o_ref, st_ref):
    S = q_ref.shape[1]
    C, DV = RET_CHUNK, RET_V_DIM
    lane = lax.broadcasted_iota(jnp.int32, (C, V7X_LANES), 1)
    first_half = (lane % RET_QK_DIM) < RET_QK_DIM // 2
    st_ref[...] = jnp.zeros_like(st_ref)

    in_head = [(lane >= h * RET_QK_DIM) & (lane < (h + 1) * RET_QK_DIM) for h in range(2)]
    NCH = RET_CHUNKS_PER_STEP
    units = [(u, h) for u in range(NCH) for h in range(2)]

    def step(i, carry):
        sls = [pl.ds(pl.multiple_of((i * NCH + u) * C, C), C) for u in range(NCH)]

        def rot(z, sl):
            swapped = jnp.where(first_half, pltpu.roll(z, V7X_LANES - RET_QK_DIM // 2, axis=1),
                                pltpu.roll(z, RET_QK_DIM // 2, axis=1))
            return z * cos_ref[sl, :] + swapped * sin_ref[sl, :]

        q = [rot(q_ref[0, sl, :], sl) for sl in sls]
        k = [rot(k_ref[0, sl, :], sl) * (RET_QK_DIM ** -0.5) for sl in sls]
        qm = [jnp.where(in_head[h], q[u], 0.0) for u, h in units]
        v = [v_ref[0, sls[u], h * DV:(h + 1) * DV] for u, h in units]
        inner = [_mm(qm[n], k[u], dims=_NT) * din_ref[h] for n, (u, h) in enumerate(units)]
        upd = [_mm((jnp.where(in_head[h], k[u], 0.0) * dk_ref[h]).T, v[n]) for n, (u, h) in enumerate(units)]
        local = [_mm(inner[n], v[n]) for n in range(len(units))]
        st = [st_ref[h] for h in range(2)]
        for n, (u, h) in enumerate(units):
            o = local[n] + _mm(qm[n], st[h]) * dq_ref[h]
            st[h] = upd[n] + dc_ref[h, 0:1, :] * st[h]
            d = o - jnp.mean(o, axis=1, keepdims=True)
            on = d * lax.rsqrt(jnp.mean(d * d, axis=1, keepdims=True) + RET_GN_EPS)
            gate = g_ref[0, sls[u], h * DV:(h + 1) * DV]
            o_ref[0, sls[u], h * DV:(h + 1) * DV] = (gate * jax.nn.sigmoid(gate) * on).astype(o_ref.dtype)
        st_ref[0], st_ref[1] = st
        return carry

    lax.fori_loop(0, S // C // NCH, step, 0)


def retention_mixer(z):
    B, S, _ = z.shape
    H, C, DK, DV = RET_HEADS, RET_CHUNK, RET_QK_DIM, RET_V_DIM
    assert S % C == 0 and 2 * DK == V7X_LANES and DV == V7X_LANES
    npair = H // 2
    half = DK // 2
    inv = ROPE_BASE ** (-jnp.arange(half, dtype=F32) / half)
    ang = jnp.arange(S, dtype=F32)[:, None] * inv
    cos = jnp.tile(jnp.cos(ang), (1, 4))
    sin = jnp.tile(jnp.concatenate([-jnp.sin(ang), jnp.sin(ang)], axis=1), (1, 2))
    log_g = jnp.asarray(np.log(1.0 - 2.0 ** (-5.0 - np.arange(H))), F32)
    n = jnp.arange(C, dtype=F32)
    diff = n[:, None] - n[None, :]
    d_in = jnp.where(diff >= 0, jnp.exp(jnp.maximum(diff, 0.0) * log_g[:, None, None]), 0.0)
    lanes = lambda t: jnp.broadcast_to(t[..., None], t.shape + (V7X_LANES,))
    d_q = lanes(jnp.exp((n + 1.0) * log_g[:, None]))
    d_k = lanes(jnp.exp((C - 1.0 - n) * log_g[:, None]))
    d_c = lanes(jnp.broadcast_to(jnp.exp(C * log_g)[:, None], (H, 8)))
    qk_tiles = H * DK // V7X_LANES
    return pl.pallas_call(
        _ret_kernel,
        grid=(B, npair),
        in_specs=[
            pl.BlockSpec((1, S, V7X_LANES), lambda b, p: (b, 0, p)),
            pl.BlockSpec((1, S, V7X_LANES), lambda b, p: (b, 0, qk_tiles + p)),
            pl.BlockSpec((1, S, 2 * DV), lambda b, p: (b, 0, 2 * qk_tiles * V7X_LANES // (2 * DV) + p)),
            pl.BlockSpec((1, S, 2 * DV), lambda b, p: (b, 0, (2 * qk_tiles * V7X_LANES + H * DV) // (2 * DV) + p)),
            pl.BlockSpec((S, V7X_LANES), lambda b, p: (0, 0)),
            pl.BlockSpec((S, V7X_LANES), lambda b, p: (0, 0)),
            pl.BlockSpec((2, C, C), lambda b, p: (p, 0, 0)),
            pl.BlockSpec((2, C, V7X_LANES), lambda b, p: (p, 0, 0)),
            pl.BlockSpec((2, C, V7X_LANES), lambda b, p: (p, 0, 0)),
            pl.BlockSpec((2, 8, V7X_LANES), lambda b, p: (p, 0, 0)),
        ],
        out_specs=pl.BlockSpec((1, S, 2 * DV), lambda b, p: (b, 0, p)),
        out_shape=jax.ShapeDtypeStruct((B, S, H * DV), BF16),
        scratch_shapes=[pltpu.VMEM((2, V7X_LANES, DV), F32)],
        compiler_params=_params("parallel", "parallel"),
        name="retention_mixer",
    )(z, z, z, z, cos, sin, d_in, d_q, d_k, d_c)


def _even_mixer(x, g_norm, w_in, shift_mu, w0, w_up, a0, a_up, g_up, k_k, k_a, r_k, ln_g, ln_b):
    B, S, D = x.shape
    z = norm_matmul(x.reshape(B * S, D), g_norm, w_in.astype(BF16)).reshape(B, S, -1)
    o_a = moba_attention(z)
    o_b = rwkv7_mixer(z, shift_mu, w0, w_up, a0, a_up, g_up, k_k, k_a, r_k, ln_g, ln_b)
    return o_a, o_b


def _odd_mixer(x, g_norm, w_in, pe_k, w1_k, w2_k, pe_v, w1_v, w2_v):
    B, S, D = x.shape
    perm, col = _odd_layout()
    w_p = jnp.take(jnp.pad(w_in, ((0, 0), (0, 1))), perm, axis=1).astype(BF16)
    z = norm_matmul(x.reshape(B * S, D), g_norm, w_p).reshape(B, S, -1)
    o_c = retention_mixer(z)
    cmp_kv, cmp_vk = nsa_compress(z, col["kc"], col["vc"], pe_k, w1_k, w2_k, pe_v, w1_v, w2_v)
    o_d = nsa_attention(z, cmp_kv, cmp_vk, col["nq"], col["slc"], col["win"], col["gate"])
    return o_c, o_d


def _odd_layout():
    G, Dh = NSA_KV_GROUPS, HEAD_DIM
    sizes = (RET_HEADS * RET_QK_DIM, RET_HEADS * RET_QK_DIM, RET_HEADS * RET_V_DIM, RET_HEADS * RET_V_DIM,
             NSA_HEADS * Dh) + (G * Dh,) * 6 + (3 * NSA_HEADS,)
    off = np.concatenate([[0], np.cumsum(sizes)])
    rq, rk, rv, rg, nq, kc, vc, ks, vs, kw, vw, ng = off[:-1]
    n_in = int(off[-1])
    pair = lambda a, b: np.concatenate([np.concatenate([a + g * Dh + np.arange(Dh), b + g * Dh + np.arange(Dh)])
                                        for g in range(G)])
    perm = np.concatenate([np.arange(ks), pair(ks, vs), pair(kw, vw), ng + np.arange(3 * NSA_HEADS)])
    n_pad = -(-len(perm) // (6 * V7X_MXU_DIM)) * 6 * V7X_MXU_DIM
    perm = np.concatenate([perm, np.full(n_pad - len(perm), n_in)]).astype(np.int32)
    col = {"nq": int(nq), "kc": int(kc), "vc": int(vc), "slc": int(ks), "win": int(ks) + 2 * G * Dh,
           "gate": int(ks) + 4 * G * Dh}
    return perm, col


def kernel(x, mix_norm, ffn_norm, even_w_in, even_shift_mu, even_w0, even_w_up, even_a0, even_a_up, even_g_up, even_k_k, even_k_a, even_r_k, even_ln_g, even_ln_b, even_w_out, odd_w_in, odd_cmp_pe_k, odd_cmp_w1_k, odd_cmp_w2_k, odd_cmp_pe_v, odd_cmp_w1_v, odd_cmp_w2_v, odd_w_out, ffn_w1, ffn_w3, ffn_w2, final_norm):
    B, S, D = x.shape
    depth = mix_norm.shape[0]
    w1, w3, w2 = ffn_w1.astype(BF16), ffn_w3.astype(BF16), ffn_w2.astype(BF16)
    for layer in range(depth):
        i = layer // 2
        if layer % 2 == 0:
            o1, o2 = _even_mixer(x, mix_norm[layer], even_w_in[i], even_shift_mu[i], even_w0[i], even_w_up[i],
                                 even_a0[i], even_a_up[i], even_g_up[i], even_k_k[i], even_k_a[i], even_r_k[i],
                                 even_ln_g[i], even_ln_b[i])
            w_out = even_w_out[i]
        else:
            o1, o2 = _odd_mixer(x, mix_norm[layer], odd_w_in[i], odd_cmp_pe_k[i], odd_cmp_w1_k[i], odd_cmp_w2_k[i],
                                odd_cmp_pe_v[i], odd_cmp_w1_v[i], odd_cmp_w2_v[i])
            w_out = odd_w_out[i]
        T = B * S
        x2 = mix_ffn_residual(o1.reshape(T, -1), o2.reshape(T, -1), w_out.astype(BF16), x.reshape(T, D),
                              ffn_norm[layer], w1, w3, w2, layer, final_norm if layer == depth - 1 else None)
        x = x2.reshape(B, S, D)
    return x
```

```python
import functools

import jax
import jax.numpy as jnp
import numpy as np
from jax import lax
from jax.experimental import pallas as pl
from jax.experimental.pallas import tpu as pltpu

F32 = jnp.float32
BF16 = jnp.bfloat16

V7X_LANES = 128
V7X_MXU_DIM = 256
V7X_VMEM_BYTES = 64 * 1024 * 1024
VMEM_LIMIT = V7X_VMEM_BYTES * 7 // 8

NORM_EPS = 1e-6
HEAD_DIM = 64

MOBA_BLOCK = 256
MOBA_TOPK = 3
RWKV_HEADS = 16
RWKV_GN_EPS = 6.4e-4

RET_HEADS = 8
RET_QK_DIM = 64
RET_V_DIM = 128
RET_CHUNK = 128
RET_GN_EPS = 1e-6
ROPE_BASE = 10000.0
NSA_HEADS = 16
NSA_KV_GROUPS = 4
NSA_CMP_BLOCK = 32
NSA_CMP_STRIDE = 16
NSA_SLC_BLOCK = 64
NSA_SLC_TOPN = 16
NSA_WINDOW = 512


def _params(*semantics):
    return pltpu.CompilerParams(dimension_semantics=semantics, vmem_limit_bytes=VMEM_LIMIT)


def _rms(x, g):
    return x * lax.rsqrt(jnp.mean(x * x, axis=-1, keepdims=True) + NORM_EPS) * g


def _norm_matmul_kernel(x_ref, g_ref, w_ref, o_ref):
    x = x_ref[...]
    scale = lax.rsqrt(jnp.mean(x * x, axis=-1, keepdims=True) + NORM_EPS)
    o_ref[...] = jnp.dot((x * g_ref[...]).astype(BF16), w_ref[...], preferred_element_type=F32) * scale


def _proj_tile(n):
    assert n % V7X_MXU_DIM == 0
    k = n // V7X_MXU_DIM
    return V7X_MXU_DIM * max(d for d in range(1, 7) if k % d == 0)


def norm_matmul(x, g, w, *, tm=512):
    T, D = x.shape
    N = w.shape[1]
    tn = _proj_tile(N)
    assert T % tm == 0 and N % tn == 0
    return pl.pallas_call(
        _norm_matmul_kernel,
        grid=(N // tn, T // tm),
        in_specs=[
            pl.BlockSpec((tm, D), lambda j, i: (i, 0)),
            pl.BlockSpec((1, D), lambda j, i: (0, 0)),
            pl.BlockSpec((D, tn), lambda j, i: (0, j)),
        ],
        out_specs=pl.BlockSpec((tm, tn), lambda j, i: (i, j)),
        out_shape=jax.ShapeDtypeStruct((T, N), F32),
        compiler_params=_params("parallel", "parallel"),
        name="norm_matmul",
    )(x, g.reshape(1, D), w)


def _mix_ffn_kernel(a_ref, b_ref, wa_ref, wb_ref, x_ref, g_ref, w1_ref, w3_ref, w2_ref, gf_ref, o_ref,
                    h_ref, acc_ref, *, final_norm):
    j = pl.program_id(1)

    @pl.when(j == 0)
    def _():
        x2 = (x_ref[...] + jnp.dot(a_ref[...], wa_ref[...], preferred_element_type=F32)
              + jnp.dot(b_ref[...], wb_ref[...], preferred_element_type=F32))
        o_ref[...] = x2
        h_ref[...] = _rms(x2, g_ref[...]).astype(BF16)
        acc_ref[...] = jnp.zeros_like(acc_ref)

    h = h_ref[...]
    a = jnp.dot(h, w1_ref[...], preferred_element_type=F32)
    b = jnp.dot(h, w3_ref[...], preferred_element_type=F32)
    act = (a * jax.nn.sigmoid(a) * b).astype(BF16)
    acc_ref[...] += jnp.dot(act, w2_ref[...], preferred_element_type=F32)

    @pl.when(j == pl.num_programs(1) - 1)
    def _():
        y = o_ref[...] + acc_ref[...]
        if final_norm:
            y = _rms(y, gf_ref[...])
        o_ref[...] = y


def mix_ffn_residual(a, b, w_out, x, g, w1, w3, w2, layer, g_final=None, *, tm=512, tf=512):
    T, D = x.shape
    K = a.shape[1]
    Fh = w1.shape[2]
    assert T % tm == 0 and Fh % tf == 0 and b.shape == a.shape and w_out.shape == (2 * K, D)
    final_norm = g_final is not None
    gf = (g_final if final_norm else g).reshape(1, D)
    once = pl.Buffered(1)
    return pl.pallas_call(
        functools.partial(_mix_ffn_kernel, final_norm=final_norm),
        grid=(T // tm, Fh // tf),
        in_specs=[
            pl.BlockSpec((tm, K), lambda i, j: (i, 0)),
            pl.BlockSpec((tm, K), lambda i, j: (i, 0)),
            pl.BlockSpec((K, D), lambda i, j: (0, 0), pipeline_mode=once),
            pl.BlockSpec((K, D), lambda i, j: (1, 0), pipeline_mode=once),
            pl.BlockSpec((tm, D), lambda i, j: (i, 0)),
            pl.BlockSpec((1, D), lambda i, j: (0, 0), pipeline_mode=once),
            pl.BlockSpec((None, D, tf), lambda i, j: (layer, 0, j)),
            pl.BlockSpec((None, D, tf), lambda i, j: (layer, 0, j)),
            pl.BlockSpec((None, tf, D), lambda i, j: (layer, j, 0)),
            pl.BlockSpec((1, D), lambda i, j: (0, 0), pipeline_mode=once),
        ],
        out_specs=pl.BlockSpec((tm, D), lambda i, j: (i, 0)),
        out_shape=jax.ShapeDtypeStruct((T, D), F32),
        scratch_shapes=[pltpu.VMEM((tm, D), BF16), pltpu.VMEM((tm, D), F32)],
        compiler_params=_params("parallel", "arbitrary"),
        name="mix_ffn_residual",
    )(a, b, w_out, w_out, x, g.reshape(1, D), w1, w3, w2, gf)


NEG_BIG = -1e30
_NT = (((1,), (1,)), ((), ()))


def _flash_steps(qas, kas, vas, masks, m_prev, acc_prev):
    hs = range(len(qas))
    s = [lax.dot_general(qas[h], kas[h], _NT, preferred_element_type=F32) for h in hs]
    s = [s[h] if masks[h] is None else jnp.where(masks[h], s[h], NEG_BIG) for h in hs]
    m_new = [jnp.maximum(m_prev[h], jnp.max(s[h], axis=1, keepdims=True)) for h in hs]
    alpha = [jnp.exp2(m_prev[h] - m_new[h]) for h in hs]
    p = [jnp.exp2(s[h] - m_new[h]) for h in hs]
    pv = [jnp.dot(p[h].astype(BF16), vas[h], preferred_element_type=F32) for h in hs]
    return m_new, [alpha[h] * acc_prev[h] + pv[h] for h in hs]


def _augment_q(q_log2, in_head, keep_t, odd):
    nblk, tq = keep_t.shape
    bias_t = jnp.where(keep_t, 0.0, NEG_BIG)
    bias = jnp.concatenate([bias_t, jnp.zeros((V7X_LANES - nblk, tq), F32)], axis=0).T
    if not odd:
        bias = pltpu.roll(bias, HEAD_DIM, axis=1)
    return jnp.where(in_head, q_log2, bias).astype(BF16)


def _key_value_tiles(k, v, blk, lane):
    low = lane < HEAD_DIM
    hot_e = 0.0 if blk is None else jnp.where(lane - HEAD_DIM == blk, 1.0, 0.0)
    hot_o = 0.0 if blk is None else jnp.where(lane == blk, 1.0, 0.0)
    k_e, k_o = jnp.where(low, k, hot_e), jnp.where(low, hot_o, k)
    v_e, v_o = jnp.where(low, v, jnp.where(lane == HEAD_DIM, 1.0, 0.0)), jnp.where(low, jnp.where(lane == 0, 1.0, 0.0), v)
    return [t.astype(BF16) for t in (k_e, k_o, v_e, v_o)]


def _normalise(acc, lane, odd):
    return acc / jnp.sum(jnp.where(lane == (0 if odd else HEAD_DIM), acc, 0.0), axis=1, keepdims=True)


LOG2E = 1.4426950408889634


MOBA_HEADS_PER_STEP = 8


def _split_bf16(x):
    hi = x.astype(BF16)
    return hi, (x - hi.astype(F32)).astype(BF16)


def _moba_kernel(q_ref, k_ref, v_ref, o_ref, ka_ref, va_ref, km_ref, acc_ref):
    L = MOBA_BLOCK
    S = k_ref.shape[1]
    nb = S // L
    HP = MOBA_HEADS_PER_STEP
    qi = pl.program_id(2)
    lane = lax.broadcasted_iota(jnp.int32, (L, V7X_LANES), 1)
    lanes_of = lambda ref, pp: ref[0, :, pp * V7X_LANES:(pp + 1) * V7X_LANES]

    @pl.when(qi == 0)
    def _():
        lane_s = lax.broadcasted_iota(jnp.int32, (S, V7X_LANES), 1)
        blk = lax.broadcasted_iota(jnp.int32, (S, V7X_LANES), 0) // L
        for pp in range(HP // 2):
            k = lanes_of(k_ref, pp)
            (ka_ref[2 * pp], ka_ref[2 * pp + 1], va_ref[2 * pp], va_ref[2 * pp + 1]) = _key_value_tiles(
                k, lanes_of(v_ref, pp), blk, lane_s)
            km_ref[pp] = jnp.concatenate(_split_bf16(jnp.mean(k.reshape(nb, L, V7X_LANES), axis=1)), axis=0)

    row = lax.broadcasted_iota(jnp.int32, (L, L), 0)
    col = lax.broadcasted_iota(jnp.int32, (L, L), 1)
    causal = col <= row
    jrow = lax.broadcasted_iota(jnp.int32, (nb, L), 0)
    past = jrow < qi
    qas = []
    for pp in range(HP // 2):
        q = lanes_of(q_ref, pp) * (HEAD_DIM ** -0.5)
        q_hi, q_lo = _split_bf16(q)
        km = km_ref[pp]
        for e in range(2):
            in_head = (lane >= e * HEAD_DIM) & (lane < (e + 1) * HEAD_DIM)
            zero = jnp.zeros_like(q_hi)
            g1 = lax.dot_general(km, jnp.where(in_head, q_hi, zero), _NT, preferred_element_type=F32)
            g2 = lax.dot_general(km, jnp.where(in_head, q_lo, zero), _NT, preferred_element_type=F32)
            gate = g1[0:nb] + g1[nb:2 * nb] + g2[0:nb]
            keep = jrow == qi
            for n in range(nb):
                g_n = gate[n:n + 1, :]
                beats = (gate > g_n) | ((gate == g_n) & (jrow < n))
                rank = jnp.sum(jnp.where(past & beats, 1.0, 0.0), axis=0, keepdims=True)
                keep = keep | ((jrow == n) & (rank < MOBA_TOPK) & past)
            qas.append(_augment_q(q * LOG2E, in_head, keep, odd=e == 1))

    tiles = lambda start: ([ka_ref[h, pl.ds(start, L), :] for h in range(HP)],
                           [va_ref[h, pl.ds(start, L), :] for h in range(HP)])

    m, acc = _flash_steps(qas, *tiles(pl.multiple_of(qi * L, L)), [causal] * HP,
                          [jnp.full((L, 1), NEG_BIG, F32)] * HP, [jnp.zeros((L, V7X_LANES), F32)] * HP)
    for h in range(HP):
        acc_ref[h] = acc[h]

    def body(n, carry):
        m2, acc2 = _flash_steps(qas, *tiles(pl.multiple_of(n * L, L)), [None] * HP, list(carry),
                                [acc_ref[h] for h in range(HP)])
        for h in range(HP):
            acc_ref[h] = acc2[h]
        return tuple(m2)

    lax.fori_loop(0, qi, body, tuple(m))
    for pp in range(HP // 2):
        o_ref[0, :, pp * V7X_LANES:(pp + 1) * V7X_LANES] = jnp.where(
            lane < HEAD_DIM, _normalise(acc_ref[2 * pp], lane, False), _normalise(acc_ref[2 * pp + 1], lane, True)
        ).astype(o_ref.dtype)


def moba_attention(z, *, n_heads=16):
    B, S, _ = z.shape
    L, HP = MOBA_BLOCK, MOBA_HEADS_PER_STEP
    W = HP * HEAD_DIM
    nb = S // L
    assert S % L == 0 and n_heads % HP == 0 and W % V7X_LANES == 0 and 2 * nb <= 16
    ngrp = n_heads // HP
    return pl.pallas_call(
        _moba_kernel,
        grid=(B, ngrp, S // L),
        in_specs=[
            pl.BlockSpec((1, L, W), lambda b, p, i: (b, i, p)),
            pl.BlockSpec((1, S, W), lambda b, p, i: (b, 0, ngrp + p)),
            pl.BlockSpec((1, S, W), lambda b, p, i: (b, 0, 2 * ngrp + p)),
        ],
        out_specs=pl.BlockSpec((1, L, W), lambda b, p, i: (b, i, p)),
        out_shape=jax.ShapeDtypeStruct((B, S, n_heads * HEAD_DIM), BF16),
        scratch_shapes=[
            pltpu.VMEM((HP, S, V7X_LANES), BF16),
            pltpu.VMEM((HP, S, V7X_LANES), BF16),
            pltpu.VMEM((HP // 2, 2 * nb, V7X_LANES), BF16),
            pltpu.VMEM((HP, L, V7X_LANES), F32),
        ],
        compiler_params=_params("parallel", "parallel", "arbitrary"),
        name="moba_attention",
    )(z, z, z)


NSA_TQ = 256
NSA_GROUPS_PER_STEP = 2
BIG = 3.0e38


def _gelu_tanh(x):
    return 0.5 * x * (1.0 + jnp.tanh(0.7978845608028654 * (x + 0.044715 * x * x * x)))


def _nsa_compress_kernel(xk0_ref, xk1_ref, xv0_ref, xv1_ref, pek_ref, pev_ref, w1k_ref, w1v_ref, w2k_ref, w2v_ref,
                         o1_ref, o2_ref):
    G, Lc, st = NSA_KV_GROUPS, NSA_CMP_BLOCK, NSA_CMP_STRIDE
    nrow = xk0_ref.shape[1] // st
    lane = lax.broadcasted_iota(jnp.int32, (nrow, G * HEAD_DIM), 1)

    def hidden(x_refs, pe_ref, w1_ref):
        acc = [jnp.zeros((G * nrow, V7X_LANES), F32) for _ in range(Lc // st)]
        for l in range(Lc):
            u, m = divmod(l, st)
            x = jnp.concatenate([r[0, pl.ds(m, nrow, stride=st), :] for r in x_refs], axis=1) + pe_ref[l:l + 1, :]
            xs = jnp.concatenate(
                [jnp.where((lane >= g * HEAD_DIM) & (lane < (g + 1) * HEAD_DIM), x, 0.0) for g in range(G)],
                axis=0).astype(BF16)
            acc[u] = acc[u] + jnp.dot(xs, w1_ref[l], preferred_element_type=F32)
        nxt = jnp.concatenate([pltpu.roll(acc[1][g * nrow:(g + 1) * nrow], nrow - 1, axis=0) for g in range(G)],
                              axis=0)
        return _gelu_tanh(acc[0] + nxt).astype(BF16)

    hk = hidden((xk0_ref, xk1_ref), pek_ref, w1k_ref)
    hv = hidden((xv0_ref, xv1_ref), pev_ref, w1v_ref)
    kc = jnp.dot(hk, w2k_ref[...], preferred_element_type=F32)
    vc = jnp.dot(hv, w2v_ref[...], preferred_element_type=F32)
    kv = kc + vc
    vk = pltpu.roll(kv, HEAD_DIM, axis=1)
    for g in range(G):
        o1_ref[0, :, g * V7X_LANES:(g + 1) * V7X_LANES] = kv[g * nrow:(g + 1) * nrow]
        o2_ref[0, :, g * V7X_LANES:(g + 1) * V7X_LANES] = vk[g * nrow:(g + 1) * nrow]


def nsa_compress(z, col_k, col_v, pe_k, w1_k, w2_k, pe_v, w1_v, w2_v):
    B, S, _ = z.shape
    G, Lc, st = NSA_KV_GROUPS, NSA_CMP_BLOCK, NSA_CMP_STRIDE
    GW = G * HEAD_DIM
    nrow = S // st
    hid = w1_k.shape[1]
    assert hid == V7X_LANES and col_k % GW == 0 and col_v % GW == 0
    tile_pe = lambda pe: jnp.tile(pe, (1, G))
    tile_w1 = lambda w: jnp.tile(w.reshape(Lc, 1, HEAD_DIM, hid), (1, G, 1, 1)).reshape(Lc, GW, hid).astype(BF16)
    w2k = jnp.pad(w2_k, ((0, 0), (0, HEAD_DIM))).astype(BF16)
    w2v = jnp.pad(w2_v, ((0, 0), (HEAD_DIM, 0))).astype(BF16)
    const = lambda shape: pl.BlockSpec(shape, lambda b: (0,) * len(shape))
    out = jax.ShapeDtypeStruct((B, nrow, G * V7X_LANES), F32)
    return pl.pallas_call(
        _nsa_compress_kernel,
        grid=(B,),
        in_specs=[
            pl.BlockSpec((1, S, V7X_LANES), lambda b: (b, 0, col_k // V7X_LANES)),
            pl.BlockSpec((1, S, V7X_LANES), lambda b: (b, 0, col_k // V7X_LANES + 1)),
            pl.BlockSpec((1, S, V7X_LANES), lambda b: (b, 0, col_v // V7X_LANES)),
            pl.BlockSpec((1, S, V7X_LANES), lambda b: (b, 0, col_v // V7X_LANES + 1)),
            const((Lc, GW)), const((Lc, GW)),
            const((Lc, GW, hid)), const((Lc, GW, hid)),
            const((hid, V7X_LANES)), const((hid, V7X_LANES)),
        ],
        out_specs=[pl.BlockSpec((1, nrow, G * V7X_LANES), lambda b: (b, 0, 0))] * 2,
        out_shape=[out, out],
        compiler_params=_params("parallel"),
        name="nsa_compress",
    )(z, z, z, z, tile_pe(pe_k), tile_pe(pe_v), tile_w1(w1_k), tile_w1(w1_v), w2k, w2v)


def _nsa_kernel(q_ref, c1_ref, c2_ref, s_ref, w_ref, g_ref, ovt_ref, o_ref, sk_ref, sv_ref, wk_ref, wv_ref, acc_ref):
    TQ = NSA_TQ
    S = s_ref.shape[1]
    R = NSA_HEADS // NSA_KV_GROUPS
    NG = NSA_GROUPS_PER_STEP
    NH = NG * R
    grp0 = pl.program_id(1) * NG
    qi = pl.program_id(2)
    lane = lax.broadcasted_iota(jnp.int32, (TQ, V7X_LANES), 1)
    tile_of = lambda ref, t: ref[0, :, t * V7X_LANES:(t + 1) * V7X_LANES]

    @pl.when(qi == 0)
    def _():
        lane_s = lax.broadcasted_iota(jnp.int32, (S, V7X_LANES), 1)
        blk = lax.broadcasted_iota(jnp.int32, (S, V7X_LANES), 0) // NSA_SLC_BLOCK
        for gg in range(NG):
            for src, k_ref, v_ref, hot in ((s_ref, sk_ref, sv_ref, blk), (w_ref, wk_ref, wv_ref, None)):
                kv = tile_of(src, gg)
                vk = pltpu.roll(kv, HEAD_DIM, axis=1)
                k_ref[2 * gg], k_ref[2 * gg + 1], v_ref[2 * gg], v_ref[2 * gg + 1] = _key_value_tiles(
                    jnp.where(lane_s < HEAD_DIM, kv, vk), jnp.where(lane_s < HEAD_DIM, vk, kv), hot, lane_s)

    q0 = pl.multiple_of(qi * TQ, TQ)
    row = lax.broadcasted_iota(jnp.int32, (TQ, TQ), 0)
    col = lax.broadcasted_iota(jnp.int32, (TQ, TQ), 1)
    causal = col <= row
    t_abs = q0 + lax.broadcasted_iota(jnp.int32, (TQ, V7X_LANES), 0)
    even_lanes = lane < HEAD_DIM

    cmask = lane * NSA_CMP_STRIDE + (NSA_CMP_BLOCK - 1) <= t_abs
    nblk = s_ref.shape[1] // NSA_SLC_BLOCK
    jrow = lax.broadcasted_iota(jnp.int32, (nblk, TQ), 0)
    own = (q0 + lax.broadcasted_iota(jnp.int32, (nblk, TQ), 1)) // NSA_SLC_BLOCK
    ovt = ovt_ref[...]
    head_lanes = [even_lanes if h % 2 == 0 else ~even_lanes for h in range(NH)]
    tiles = [tile_of(q_ref, h // 2) * (HEAD_DIM ** -0.5) for h in range(NH)]
    o_cmp, qas = [], []
    for gg in range(NG):
        c_kv_b, c_vk_b = tile_of(c1_ref, gg).astype(BF16), tile_of(c2_ref, gg).astype(BF16)
        p_sum = jnp.zeros((TQ, V7X_LANES), F32)
        for h in range(gg * R, (gg + 1) * R):
            qm = jnp.where(head_lanes[h], tiles[h], 0.0).astype(BF16)
            s = lax.dot_general(qm, c_kv_b if h % 2 == 0 else c_vk_b, _NT, preferred_element_type=F32)
            s = jnp.where(cmask, s, NEG_BIG)
            p = jnp.where(cmask, jnp.exp(s - jnp.max(s, axis=1, keepdims=True)), 0.0)
            den = jnp.sum(p, axis=1, keepdims=True)
            p = p / jnp.where(den > 0.0, den, 1.0)
            p_sum = p_sum + p
            o_cmp.append(jnp.dot(p.astype(BF16), c_vk_b if h % 2 == 0 else c_kv_b, preferred_element_type=F32))

        p_hi, p_lo = _split_bf16(p_sum)
        p_slc = (lax.dot_general(ovt, p_hi, _NT, preferred_element_type=F32)
                 + lax.dot_general(ovt, p_lo, _NT, preferred_element_type=F32))[0:nblk]
        score = jnp.where((jrow == own) | (jrow == 0), BIG, jnp.where(jrow > own, -BIG, p_slc))
        keep = jrow > nblk
        for j in range(nblk):
            s_j = score[j:j + 1, :]
            beats = (score > s_j) | ((score == s_j) & (jrow < j))
            rank = jnp.sum(jnp.where(beats, 1.0, 0.0), axis=0, keepdims=True)
            keep = keep | ((jrow == j) & (rank < NSA_SLC_TOPN) & (jrow <= own))
        qas += [_augment_q(tiles[h] * LOG2E, head_lanes[h], keep, odd=h % 2 == 1) for h in range(gg * R, (gg + 1) * R)]

    neg = [jnp.full((TQ, 1), NEG_BIG, F32)] * NH
    zacc = [jnp.zeros((TQ, V7X_LANES), F32)] * NH
    kv_index = [2 * (h // R) + h % 2 for h in range(NH)]

    def kv_blocks(k_ref, v_ref, start):
        return ([k_ref[kv_index[h], pl.ds(start, TQ), :] for h in range(NH)],
                [v_ref[kv_index[h], pl.ds(start, TQ), :] for h in range(NH)])

    m, acc = _flash_steps(qas, *kv_blocks(sk_ref, sv_ref, q0), [causal] * NH, neg, zacc)
    for h in range(NH):
        acc_ref[h] = acc[h]

    def body(kb, carry):
        m2, acc2 = _flash_steps(qas, *kv_blocks(sk_ref, sv_ref, pl.multiple_of(kb * TQ, TQ)), [None] * NH,
                                list(carry), [acc_ref[h] for h in range(NH)])
        for h in range(NH):
            acc_ref[h] = acc2[h]
        return tuple(m2)

    lax.fori_loop(0, qi, body, tuple(m))

    WK = NSA_WINDOW + TQ
    w0 = pl.multiple_of(jnp.maximum(qi - NSA_WINDOW // TQ, 0) * TQ, TQ)
    key_pos = w0 + lax.broadcasted_iota(jnp.int32, (TQ, WK), 1)
    t_win = q0 + lax.broadcasted_iota(jnp.int32, (TQ, WK), 0)
    in_window = (key_pos <= t_win) & (key_pos > t_win - NSA_WINDOW)
    _, acc = _flash_steps(qas, [wk_ref[kv_index[h], pl.ds(w0, WK), :] for h in range(NH)],
                          [wv_ref[kv_index[h], pl.ds(w0, WK), :] for h in range(NH)], [in_window] * NH, neg, zacc)

    gates = jax.nn.sigmoid(g_ref[0])
    outs = []
    for h in range(NH):
        o_slc = _normalise(acc_ref[h], lane, h % 2 == 1)
        o_win = _normalise(acc[h], lane, h % 2 == 1)
        c0 = (grp0 * R + h) * 3
        gate = lambda c: jnp.sum(jnp.where(lane == c, gates, 0.0), axis=1, keepdims=True)
        outs.append(gate(c0) * o_cmp[h] + gate(c0 + 1) * o_slc + gate(c0 + 2) * o_win)
    for p2 in range(NH // 2):
        o_ref[0, :, p2 * V7X_LANES:(p2 + 1) * V7X_LANES] = jnp.where(
            even_lanes, outs[2 * p2], outs[2 * p2 + 1]).astype(o_ref.dtype)


def nsa_attention(z, cmp_kv, cmp_vk, col_q, col_slc, col_win, col_gate):
    B, S, _ = z.shape
    G, TQ, NG = NSA_KV_GROUPS, NSA_TQ, NSA_GROUPS_PER_STEP
    R = NSA_HEADS // G
    QW = NG * R * HEAD_DIM
    KW = NG * V7X_LANES
    ncmp = cmp_kv.shape[1]
    assert S % TQ == 0 and ncmp == V7X_LANES and S // NSA_SLC_BLOCK <= V7X_LANES
    assert NSA_WINDOW % TQ == 0 and S >= NSA_WINDOW + TQ
    assert G % NG == 0 and R % 2 == 0
    assert col_q % QW == 0 and col_slc % KW == 0 and col_win % KW == 0 and col_gate % V7X_LANES == 0
    nc = (S - NSA_CMP_BLOCK) // NSA_CMP_STRIDE + 1
    c_start = np.arange(V7X_LANES) * NSA_CMP_STRIDE
    s_start = np.arange(V7X_LANES) * NSA_SLC_BLOCK
    overlap = ((c_start[:, None] <= s_start[None, :] + NSA_SLC_BLOCK - 1)
               & (c_start[:, None] + NSA_CMP_BLOCK - 1 >= s_start[None, :])
               & (np.arange(V7X_LANES)[:, None] < nc) & (np.arange(V7X_LANES)[None, :] < S // NSA_SLC_BLOCK))
    const = lambda shape: pl.BlockSpec(shape, lambda b, g, i: (0,) * len(shape))
    return pl.pallas_call(
        _nsa_kernel,
        grid=(B, G // NG, S // TQ),
        in_specs=[
            pl.BlockSpec((1, TQ, QW), lambda b, g, i: (b, i, col_q // QW + g)),
            pl.BlockSpec((1, ncmp, KW), lambda b, g, i: (b, 0, g)),
            pl.BlockSpec((1, ncmp, KW), lambda b, g, i: (b, 0, g)),
            pl.BlockSpec((1, S, KW), lambda b, g, i: (b, 0, col_slc // KW + g)),
            pl.BlockSpec((1, S, KW), lambda b, g, i: (b, 0, col_win // KW + g)),
            pl.BlockSpec((1, TQ, V7X_LANES), lambda b, g, i: (b, i, col_gate // V7X_LANES)),
            const((V7X_LANES, V7X_LANES)),
        ],
        out_specs=pl.BlockSpec((1, TQ, QW), lambda b, g, i: (b, i, g)),
        out_shape=jax.ShapeDtypeStruct((B, S, NSA_HEADS * HEAD_DIM), BF16),
        scratch_shapes=[pltpu.VMEM((2 * NG, S, V7X_LANES), BF16)] * 4 + [
            pltpu.VMEM((NG * R, TQ, V7X_LANES), F32),
        ],
        compiler_params=_params("parallel", "parallel", "arbitrary"),
        name="nsa_attention",
    )(z, cmp_kv, cmp_vk, z, z, z, jnp.asarray(overlap.T, BF16))


RWKV_CHUNK = 64
RWKV_ROWS = 512
RWKV_INTERLEAVE = 8


def _mm(a, b, dims=None):
    dims = dims or (((1,), (0,)), ((), ()))
    return lax.dot_general(a.astype(BF16), b.astype(BF16), dims, preferred_element_type=F32)


def _mm3(a, b):
    (a_hi, a_lo), (b_hi, b_lo) = _split_bf16(a), _split_bf16(b)
    return _mm(a_hi, b_hi) + (_mm(a_hi, b_lo) + _mm(a_lo, b_hi))


def _mm_onehot(a01, b):
    hi = b.astype(BF16)
    mid, lo = _split_bf16(b - hi.astype(F32))
    return _mm(a01, hi) + (_mm(a01, mid) + _mm(a01, lo))


def _head_sum(x, low):
    s0 = jnp.sum(jnp.where(low, x, 0.0), axis=1, keepdims=True)
    s1 = jnp.sum(jnp.where(low, 0.0, x), axis=1, keepdims=True)
    return jnp.where(low, s0, s1)


def _rwkv_kernel(r_ref, k_ref, v_ref, lo_ref, glo_ref, pp_ref, pl_ref, wup_ref, aup_ref, gup_ref, o_ref,
                 rs, ws, ks, vs, als, bes, gs, ys, bon, hs, rqs, ms, ns):
    S = r_ref.shape[1]
    C, RB = RWKV_CHUNK, RWKV_ROWS
    pp = pp_ref[...]
    mu_r, mu_k, mu_v, w0, a0, k_k, k_a, r_k, ln_g, ln_b = [pp[i:i + 1, :] for i in range(10)]
    mu_lo, mu_g = pl_ref[0:1, :], pl_ref[1:2, :]
    heads = lax.broadcasted_iota(jnp.int32, (RB, V7X_LANES), 1) < HEAD_DIM
    first = lax.broadcasted_iota(jnp.int32, (RB, V7X_LANES), 0) == 0

    def prologue(i, c):
        t0 = pl.multiple_of(i * RB, RB)
        tp = jnp.maximum(t0 - 1, 0)
        keep = jnp.where(i > 0, 1.0, 0.0)

        def shifted(ref, mu):
            x = ref[0, pl.ds(t0, RB), :]
            prev = jnp.where(first, ref[0, pl.ds(tp, 1), :] * keep, pltpu.roll(x, 1, axis=0))
            return x + (prev - x) * mu

        r, k, v = shifted(r_ref, mu_r), shifted(k_ref, mu_k), shifted(v_ref, mu_v)
        lo, glo = shifted(lo_ref, mu_lo), shifted(glo_ref, mu_g)
        wp = -(w0 + _mm(jnp.tanh(lo), wup_ref[...]))
        w = -(jnp.maximum(wp, 0.0) + jnp.log(1.0 + jnp.exp(-jnp.abs(wp)))) - 0.5
        a = jax.nn.sigmoid(a0 + _mm(lo, aup_ref[...]))
        kk = k * k_k
        kk = kk * lax.rsqrt(jnp.maximum(_head_sum(kk * kk, heads), 1e-24))
        k2 = k * (1.0 + (a - 1.0) * k_a)
        rs[pl.ds(t0, RB), :] = r
        ws[pl.ds(t0, RB), :] = -jnp.exp(w)
        ks[pl.ds(t0, RB), :] = k2
        vs[pl.ds(t0, RB), :] = v
        als[pl.ds(t0, RB), :] = -kk
        bes[pl.ds(t0, RB), :] = kk * a
        gs[pl.ds(t0, RB), :] = _mm(jax.nn.sigmoid(glo), gup_ref[...])
        bon[pl.ds(t0, RB), :] = _head_sum(r * k2 * r_k, heads) * v
        return c

    lax.fori_loop(0, S // RB, prologue, 0)

    W2 = 2 * C
    row = lax.broadcasted_iota(jnp.int32, (W2, W2), 0)
    col = lax.broadcasted_iota(jnp.int32, (W2, W2), 1)
    t_idx, s_idx = row % C, col % C
    top, left = row < C, col < C
    same = top == left
    eye = jnp.where(row == col, 1.0, 0.0)
    tri = jnp.where(lax.broadcasted_iota(jnp.int32, (C, C), 1) <= lax.broadcasted_iota(jnp.int32, (C, C), 0), 1.0, 0.0)
    low_c = lax.broadcasted_iota(jnp.int32, (C, V7X_LANES), 1) < HEAD_DIM
    fold = lambda x: x[0:C] + x[C:W2]
    stack_heads = lambda x: jnp.concatenate([jnp.where(low_c, x, 0.0), jnp.where(low_c, 0.0, x)], axis=0)
    block_diag = lambda x: jnp.where(top, jnp.where(left, x, 0.0), jnp.where(left, 0.0, pltpu.roll(x, C, axis=1)))

    rows = lambda c: pl.ds(c * C if isinstance(c, int) else pl.multiple_of(c * C, C), C)

    def advance(c, H):
        ys[rows(c), :] += _mm3(rqs[c], H)
        return _mm3(ms[c], H) + ns[c]

    def transfers(i, lagged):
        each = lambda f, *xs: [f(*a) for a in zip(*xs)]
        cs = [i * RWKV_INTERLEAVE + u for u in range(RWKV_INTERLEAVE)]
        sls = [rows(c) for c in cs]
        state = [hs[...]] if lagged else None

        def lag(hook):
            if lagged:
                for u in range(hook * RWKV_INTERLEAVE // 8, (hook + 1) * RWKV_INTERLEAVE // 8):
                    state[0] = advance(cs[u] - RWKV_INTERLEAVE, state[0])

        r, lw, k2, v, al, be = ([ref[sl, :] for sl in sls] for ref in (rs, ws, ks, vs, als, bes))
        logp = each(lambda x: _mm_onehot(tri, x), lw)
        lag(0)
        P = each(jnp.exp, logp)
        Pinv = each(lambda x: jnp.exp(-x), logp)
        At = each(lambda a_, lp, w_: a_ * jnp.exp(lp - w_), al, logp, lw)
        Rt, Bt, Kt = each(jnp.multiply, r, P), each(jnp.multiply, be, Pinv), each(jnp.multiply, k2, Pinv)
        PC = each(lambda p: p[C - 1:C, :], P)
        A_bd, R_bd = each(stack_heads, At), each(stack_heads, Rt)
        Yt = each(lambda b, k: jnp.concatenate([b, k], axis=0), Bt, Kt)
        A1 = each(lambda a, y: jnp.where(s_idx < t_idx, _mm(a, y, dims=_NT), 0.0), A_bd, Yt)
        A2 = each(lambda a, y: jnp.where(s_idx <= t_idx, _mm(a, y, dims=_NT), 0.0), R_bd, Yt)
        X, Arb = each(block_diag, A1), each(block_diag, A2)
        T = each(lambda x: eye + x, X)
        for it in range(5):
            X = each(lambda x: _mm(x, x), X)
            T = each(lambda t, x: t + _mm(t, x), T, X)
            lag(1 + it)
        V0 = each(lambda x: jnp.concatenate([jnp.zeros_like(x), x], axis=0), v)
        TA = each(_mm, T, A_bd)
        AkV = each(lambda a, x: jnp.where(same, _mm(a, x), 0.0), A1, V0)
        lag(6)
        U0 = each(_mm, T, AkV)
        lag(7)
        AR = each(lambda a, t, u: _mm(a, jnp.concatenate([t, u], axis=1)), Arb, TA, U0)
        ArkV = each(lambda a, x: jnp.where(same, _mm(a, x), 0.0), A2, V0)
        Mx = each(lambda b, p, t: _mm((b * p).T, fold(t)), Bt, PC, TA)
        Nx = each(lambda b, k, p, u, x: _mm(jnp.concatenate([b * p, k * p], axis=0).T,
                                            jnp.concatenate([fold(u), x], axis=0)), Bt, Kt, PC, U0, v)
        for u in range(RWKV_INTERLEAVE):
            ys[sls[u], :] = fold(AR[u][:, W2:2 * W2] + ArkV[u])
            rqs[cs[u]] = Rt[u] + fold(AR[u][:, 0:W2])
            ms[cs[u]] = eye * PC[u] + jnp.where(same, Mx[u], 0.0)
            ns[cs[u]] = jnp.where(same, Nx[u], 0.0)
        if lagged:
            hs[...] = state[0]

    def pipelined(i, carry):
        transfers(i, True)
        return carry

    def drain(c, carry):
        hs[...] = advance(c, hs[...])
        return carry

    assert 8 % RWKV_INTERLEAVE == 0
    hs[...] = jnp.zeros((W2, W2), F32)
    transfers(0, False)
    lax.fori_loop(1, S // C // RWKV_INTERLEAVE, pipelined, 0)
    lax.fori_loop(S // C - RWKV_INTERLEAVE, S // C, drain, 0)

    def epilogue(i, c):
        sl = pl.ds(pl.multiple_of(i * RB, RB), RB)
        y = ys[sl, :]
        d = y - _head_sum(y, heads) * (1.0 / HEAD_DIM)
        var = _head_sum(d * d, heads) * (1.0 / HEAD_DIM)
        yn = d * lax.rsqrt(var + RWKV_GN_EPS) * ln_g + ln_b
        o_ref[0, sl, :] = ((yn + bon[sl, :]) * gs[sl, :]).astype(o_ref.dtype)
        return c

    lax.fori_loop(0, S // RB, epilogue, 0)


def rwkv7_mixer(z, shift_mu, w0, w_up, a0, a_up, g_up, k_k, k_a, r_k, ln_g, ln_b):
    B, S, _ = z.shape
    CW = RWKV_HEADS * HEAD_DIM
    npair = CW // V7X_LANES
    base = 3 * CW // V7X_LANES
    lora = w_up.shape[0] + a_up.shape[0]
    assert lora == V7X_LANES and g_up.shape[0] == V7X_LANES and S % RWKV_ROWS == 0
    pp = jnp.stack([shift_mu[0:CW], shift_mu[CW:2 * CW], shift_mu[2 * CW:3 * CW], w0, a0, k_k, k_a,
                    r_k.reshape(CW), ln_g, ln_b])
    pp = jnp.pad(pp, ((0, 16 - pp.shape[0]), (0, 0)))
    pl2 = jnp.pad(shift_mu[3 * CW:].reshape(2, V7X_LANES), ((0, 6), (0, 0)))
    wup = jnp.pad(w_up, ((0, a_up.shape[0]), (0, 0)))
    aup = jnp.pad(a_up, ((w_up.shape[0], 0), (0, 0)))
    tile = lambda off: pl.BlockSpec((1, S, V7X_LANES), lambda b, p: (b, 0, base + off * npair + p))
    fixed = lambda off: pl.BlockSpec((1, S, V7X_LANES), lambda b, p: (b, 0, base + 3 * npair + off))
    seq = pltpu.VMEM((S, V7X_LANES), F32)
    return pl.pallas_call(
        _rwkv_kernel,
        grid=(B, npair),
        in_specs=[
            tile(0), tile(1), tile(2), fixed(0), fixed(1),
            pl.BlockSpec((16, V7X_LANES), lambda b, p: (0, p)),
            pl.BlockSpec((8, V7X_LANES), lambda b, p: (0, 0)),
            pl.BlockSpec((V7X_LANES, V7X_LANES), lambda b, p: (0, p)),
            pl.BlockSpec((V7X_LANES, V7X_LANES), lambda b, p: (0, p)),
            pl.BlockSpec((V7X_LANES, V7X_LANES), lambda b, p: (0, p)),
        ],
        out_specs=pl.BlockSpec((1, S, V7X_LANES), lambda b, p: (b, 0, p)),
        out_shape=jax.ShapeDtypeStruct((B, S, CW), BF16),
        scratch_shapes=[seq] * 9 + [
            pltpu.VMEM((V7X_LANES, V7X_LANES), F32),
            pltpu.VMEM((S // RWKV_CHUNK, RWKV_CHUNK, V7X_LANES), F32),
            pltpu.VMEM((S // RWKV_CHUNK, V7X_LANES, V7X_LANES), F32),
            pltpu.VMEM((S // RWKV_CHUNK, V7X_LANES, V7X_LANES), F32),
        ],
        compiler_params=_params("parallel", "parallel"),
        name="rwkv7_mixer",
    )(z, z, z, z, z, pp, pl2, wup, aup, g_up.astype(BF16))


RET_CHUNKS_PER_STEP = 4


def _ret_kernel(q_ref, k_ref, v_ref, g_ref, cos_ref, sin_ref, din_ref, dq_ref, dk_ref, dc_ref, o_ref, st_ref):
    S = q_ref.shape[1]
    C, DV = RET_CHUNK, RET_V_DIM
    lane = lax.broadcasted_iota(jnp.int32, (C, V7X_LANES), 1)
    first_half = (lane % RET_QK_DIM) < RET_QK_DIM // 2
    st_ref[...] = jnp.zeros_like(st_ref)

    in_head = [(lane >= h * RET_QK_DIM) & (lane < (h + 1) * RET_QK_DIM) for h in range(2)]
    NCH = RET_CHUNKS_PER_STEP
    units = [(u, h) for u in range(NCH) for h in range(2)]

    def step(i, carry):
        sls = [pl.ds(pl.multiple_of((i * NCH + u) * C, C), C) for u in range(NCH)]

        def rot(z, sl):
            swapped = jnp.where(first_half, pltpu.roll(z, V7X_LANES - RET_QK_DIM // 2, axis=1),
                                pltpu.roll(z, RET_QK_DIM // 2, axis=1))
            return z * cos_ref[sl, :] + swapped * sin_ref[sl, :]

        q = [rot(q_ref[0, sl, :], sl) for sl in sls]
        k = [rot(k_ref[0, sl, :], sl) * (RET_QK_DIM ** -0.5) for sl in sls]
        qm = [jnp.where(in_head[h], q[u], 0.0) for u, h in units]
        v = [v_ref[0, sls[u], h * DV:(h + 1) * DV] for u, h in units]
        inner = [_mm(qm[n], k[u], dims=_NT) * din_ref[h] for n, (u, h) in enumerate(units)]
        upd = [_mm((jnp.where(in_head[h], k[u], 0.0) * dk_ref[h]).T, v[n]) for n, (u, h) in enumerate(units)]
        local = [_mm(inner[n], v[n]) for n in range(len(units))]
        st = [st_ref[h] for h in range(2)]
        for n, (u, h) in enumerate(units):
            o = local[n] + _mm(qm[n], st[h]) * dq_ref[h]
            st[h] = upd[n] + dc_ref[h, 0:1, :] * st[h]
            d = o - jnp.mean(o, axis=1, keepdims=True)
            on = d * lax.rsqrt(jnp.mean(d * d, axis=1, keepdims=True) + RET_GN_EPS)
            gate = g_ref[0, sls[u], h * DV:(h + 1) * DV]
            o_ref[0, sls[u], h * DV:(h + 1) * DV] = (gate * jax.nn.sigmoid(gate) * on).astype(o_ref.dtype)
        st_ref[0], st_ref[1] = st
        return carry

    lax.fori_loop(0, S // C // NCH, step, 0)


def retention_mixer(z):
    B, S, _ = z.shape
    H, C, DK, DV = RET_HEADS, RET_CHUNK, RET_QK_DIM, RET_V_DIM
    assert S % C == 0 and 2 * DK == V7X_LANES and DV == V7X_LANES
    npair = H // 2
    half = DK // 2
    inv = ROPE_BASE ** (-jnp.arange(half, dtype=F32) / half)
    ang = jnp.arange(S, dtype=F32)[:, None] * inv
    cos = jnp.tile(jnp.cos(ang), (1, 4))
    sin = jnp.tile(jnp.concatenate([-jnp.sin(ang), jnp.sin(ang)], axis=1), (1, 2))
    log_g = jnp.asarray(np.log(1.0 - 2.0 ** (-5.0 - np.arange(H))), F32)
    n = jnp.arange(C, dtype=F32)
    diff = n[:, None] - n[None, :]
    d_in = jnp.where(diff >= 0, jnp.exp(jnp.maximum(diff, 0.0) * log_g[:, None, None]), 0.0)
    lanes = lambda t: jnp.broadcast_to(t[..., None], t.shape + (V7X_LANES,))
    d_q = lanes(jnp.exp((n + 1.0) * log_g[:, None]))
    d_k = lanes(jnp.exp((C - 1.0 - n) * log_g[:, None]))
    d_c = lanes(jnp.broadcast_to(jnp.exp(C * log_g)[:, None], (H, 8)))
    qk_tiles = H * DK // V7X_LANES
    return pl.pallas_call(
        _ret_kernel,
        grid=(B, npair),
        in_specs=[
            pl.BlockSpec((1, S, V7X_LANES), lambda b, p: (b, 0, p)),
            pl.BlockSpec((1, S, V7X_LANES), lambda b, p: (b, 0, qk_tiles + p)),
            pl.BlockSpec((1, S, 2 * DV), lambda b, p: (b, 0, 2 * qk_tiles * V7X_LANES // (2 * DV) + p)),
            pl.BlockSpec((1, S, 2 * DV), lambda b, p: (b, 0, (2 * qk_tiles * V7X_LANES + H * DV) // (2 * DV) + p)),
            pl.BlockSpec((S, V7X_LANES), lambda b, p: (0, 0)),
            pl.BlockSpec((S, V7X_LANES), lambda b, p: (0, 0)),
            pl.BlockSpec((2, C, C), lambda b, p: (p, 0, 0)),
            pl.BlockSpec((2, C, V7X_LANES), lambda b, p: (p, 0, 0)),
            pl.BlockSpec((2, C, V7X_LANES), lambda b, p: (p, 0, 0)),
            pl.BlockSpec((2, 8, V7X_LANES), lambda b, p: (p, 0, 0)),
        ],
        out_specs=pl.BlockSpec((1, S, 2 * DV), lambda b, p: (b, 0, p)),
        out_shape=jax.ShapeDtypeStruct((B, S, H * DV), BF16),
        scratch_shapes=[pltpu.VMEM((2, V7X_LANES, DV), F32)],
        compiler_params=_params("parallel", "parallel"),
        name="retention_mixer",
    )(z, z, z, z, cos, sin, d_in, d_q, d_k, d_c)


def _even_mixer(x, g_norm, w_in, shift_mu, w0, w_up, a0, a_up, g_up, k_k, k_a, r_k, ln_g, ln_b):
    B, S, D = x.shape
    z = norm_matmul(x.reshape(B * S, D), g_norm, w_in.astype(BF16)).reshape(B, S, -1)
    o_a = moba_attention(z)
    o_b = rwkv7_mixer(z, shift_mu, w0, w_up, a0, a_up, g_up, k_k, k_a, r_k, ln_g, ln_b)
    return o_a, o_b


def _odd_mixer(x, g_norm, w_in, pe_k, w1_k, w2_k, pe_v, w1_v, w2_v):
    B, S, D = x.shape
    perm, col = _odd_layout()
    w_p = jnp.take(jnp.pad(w_in, ((0, 0), (0, 1))), perm, axis=1).astype(BF16)
    z = norm_matmul(x.reshape(B * S, D), g_norm, w_p).reshape(B, S, -1)
    o_c = retention_mixer(z)
    cmp_kv, cmp_vk = nsa_compress(z, col["kc"], col["vc"], pe_k, w1_k, w2_k, pe_v, w1_v, w2_v)
    o_d = nsa_attention(z, cmp_kv, cmp_vk, col["nq"], col["slc"], col["win"], col["gate"])
    return o_c, o_d


def _odd_layout():
    G, Dh = NSA_KV_GROUPS, HEAD_DIM
    sizes = (RET_HEADS * RET_QK_DIM, RET_HEADS * RET_QK_DIM, RET_HEADS * RET_V_DIM, RET_HEADS * RET_V_DIM,
             NSA_HEADS * Dh) + (G * Dh,) * 6 + (3 * NSA_HEADS,)
    off = np.concatenate([[0], np.cumsum(sizes)])
    rq, rk, rv, rg, nq, kc, vc, ks, vs, kw, vw, ng = off[:-1]
    n_in = int(off[-1])
    pair = lambda a, b: np.concatenate([np.concatenate([a + g * Dh + np.arange(Dh), b + g * Dh + np.arange(Dh)])
                                        for g in range(G)])
    perm = np.concatenate([np.arange(ks), pair(ks, vs), pair(kw, vw), ng + np.arange(3 * NSA_HEADS)])
    n_pad = -(-len(perm) // (6 * V7X_MXU_DIM)) * 6 * V7X_MXU_DIM
    perm = np.concatenate([perm, np.full(n_pad - len(perm), n_in)]).astype(np.int32)
    col = {"nq": int(nq), "kc": int(kc), "vc": int(vc), "slc": int(ks), "win": int(ks) + 2 * G * Dh,
           "gate": int(ks) + 4 * G * Dh}
    return perm, col


def kernel(x, mix_norm, ffn_norm, even_w_in, even_shift_mu, even_w0, even_w_up, even_a0, even_a_up, even_g_up, even_k_k, even_k_a, even_r_k, even_ln_g, even_ln_b, even_w_out, odd_w_in, odd_cmp_pe_k, odd_cmp_w1_k, odd_cmp_w2_k, odd_cmp_pe_v, odd_cmp_w1_v, odd_cmp_w2_v, odd_w_out, ffn_w1, ffn_w3, ffn_w2, final_norm):
    B, S, D = x.shape
    depth = mix_norm.shape[0]
    w1, w3, w2 = ffn_w1.astype(BF16), ffn_w3.astype(BF16), ffn_w2.astype(BF16)
    for layer in range(depth):
        i = layer // 2
        if layer % 2 == 0:
            o1, o2 = _even_mixer(x, mix_norm[layer], even_w_in[i], even_shift_mu[i], even_w0[i], even_w_up[i],
                                 even_a0[i], even_a_up[i], even_g_up[i], even_k_k[i], even_k_a[i], even_r_k[i],
                                 even_ln_g[i], even_ln_b[i])
            w_out = even_w_out[i]
        else:
            o1, o2 = _odd_mixer(x, mix_norm[layer], odd_w_in[i], odd_cmp_pe_k[i], odd_cmp_w1_k[i], odd_cmp_w2_k[i],
                                odd_cmp_pe_v[i], odd_cmp_w1_v[i], odd_cmp_w2_v[i])
            w_out = odd_w_out[i]
        T = B * S
        x2 = mix_ffn_residual(o1.reshape(T, -1), o2.reshape(T, -1), w_out.astype(BF16), x.reshape(T, D),
                              ffn_norm[layer], w1, w3, w2, layer, final_norm if layer == depth - 1 else None)
        x = x2.reshape(B, S, D)
    return x
```

```python
import functools

import jax
import jax.numpy as jnp
import numpy as np
from jax import lax
from jax.experimental import pallas as pl
from jax.experimental.pallas import tpu as pltpu

F32 = jnp.float32
BF16 = jnp.bfloat16

V7X_LANES = 128
V7X_MXU_DIM = 256
V7X_VMEM_BYTES = 64 * 1024 * 1024
VMEM_LIMIT = V7X_VMEM_BYTES * 7 // 8

NORM_EPS = 1e-6
HEAD_DIM = 64

MOBA_BLOCK = 256
MOBA_TOPK = 3
RWKV_HEADS = 16
RWKV_GN_EPS = 6.4e-4

RET_HEADS = 8
RET_QK_DIM = 64
RET_V_DIM = 128
RET_CHUNK = 128
RET_GN_EPS = 1e-6
ROPE_BASE = 10000.0
NSA_HEADS = 16
NSA_KV_GROUPS = 4
NSA_CMP_BLOCK = 32
NSA_CMP_STRIDE = 16
NSA_SLC_BLOCK = 64
NSA_SLC_TOPN = 16
NSA_WINDOW = 512


def _params(*semantics):
    return pltpu.CompilerParams(dimension_semantics=semantics, vmem_limit_bytes=VMEM_LIMIT)


def _rms(x, g):
    return x * lax.rsqrt(jnp.mean(x * x, axis=-1, keepdims=True) + NORM_EPS) * g


def _norm_matmul_kernel(x_ref, g_ref, w_ref, o_ref):
    x = x_ref[...]
    scale = lax.rsqrt(jnp.mean(x * x, axis=-1, keepdims=True) + NORM_EPS)
    o_ref[...] = jnp.dot((x * g_ref[...]).astype(BF16), w_ref[...], preferred_element_type=F32) * scale


def _proj_tile(n):
    assert n % V7X_MXU_DIM == 0
    k = n // V7X_MXU_DIM
    return V7X_MXU_DIM * max(d for d in range(1, 7) if k % d == 0)


def norm_matmul(x, g, w, *, tm=512):
    T, D = x.shape
    N = w.shape[1]
    tn = _proj_tile(N)
    assert T % tm == 0 and N % tn == 0
    return pl.pallas_call(
        _norm_matmul_kernel,
        grid=(N // tn, T // tm),
        in_specs=[
            pl.BlockSpec((tm, D), lambda j, i: (i, 0)),
            pl.BlockSpec((1, D), lambda j, i: (0, 0)),
            pl.BlockSpec((D, tn), lambda j, i: (0, j)),
        ],
        out_specs=pl.BlockSpec((tm, tn), lambda j, i: (i, j)),
        out_shape=jax.ShapeDtypeStruct((T, N), F32),
        compiler_params=_params("parallel", "parallel"),
        name="norm_matmul",
    )(x, g.reshape(1, D), w)


def _mix_ffn_kernel(a_ref, b_ref, wa_ref, wb_ref, x_ref, g_ref, w1_ref, w3_ref, w2_ref, gf_ref, o_ref,
                    h_ref, acc_ref, *, final_norm):
    j = pl.program_id(1)

    @pl.when(j == 0)
    def _():
        x2 = (x_ref[...] + jnp.dot(a_ref[...], wa_ref[...], preferred_element_type=F32)
              + jnp.dot(b_ref[...], wb_ref[...], preferred_element_type=F32))
        o_ref[...] = x2
        h_ref[...] = _rms(x2, g_ref[...]).astype(BF16)
        acc_ref[...] = jnp.zeros_like(acc_ref)

    h = h_ref[...]
    a = jnp.dot(h, w1_ref[...], preferred_element_type=F32)
    b = jnp.dot(h, w3_ref[...], preferred_element_type=F32)
    act = (a * jax.nn.sigmoid(a) * b).astype(BF16)
    acc_ref[...] += jnp.dot(act, w2_ref[...], preferred_element_type=F32)

    @pl.when(j == pl.num_programs(1) - 1)
    def _():
        y = o_ref[...] + acc_ref[...]
        if final_norm:
            y = _rms(y, gf_ref[...])
        o_ref[...] = y


def mix_ffn_residual(a, b, w_out, x, g, w1, w3, w2, layer, g_final=None, *, tm=512, tf=512):
    T, D = x.shape
    K = a.shape[1]
    Fh = w1.shape[2]
    assert T % tm == 0 and Fh % tf == 0 and b.shape == a.shape and w_out.shape == (2 * K, D)
    final_norm = g_final is not None
    gf = (g_final if final_norm else g).reshape(1, D)
    once = pl.Buffered(1)
    return pl.pallas_call(
        functools.partial(_mix_ffn_kernel, final_norm=final_norm),
        grid=(T // tm, Fh // tf),
        in_specs=[
            pl.BlockSpec((tm, K), lambda i, j: (i, 0)),
            pl.BlockSpec((tm, K), lambda i, j: (i, 0)),
            pl.BlockSpec((K, D), lambda i, j: (0, 0), pipeline_mode=once),
            pl.BlockSpec((K, D), lambda i, j: (1, 0), pipeline_mode=once),
            pl.BlockSpec((tm, D), lambda i, j: (i, 0)),
            pl.BlockSpec((1, D), lambda i, j: (0, 0), pipeline_mode=once),
            pl.BlockSpec((None, D, tf), lambda i, j: (layer, 0, j)),
            pl.BlockSpec((None, D, tf), lambda i, j: (layer, 0, j)),
            pl.BlockSpec((None, tf, D), lambda i, j: (layer, j, 0)),
            pl.BlockSpec((1, D), lambda i, j: (0, 0), pipeline_mode=once),
        ],
        out_specs=pl.BlockSpec((tm, D), lambda i, j: (i, 0)),
        out_shape=jax.ShapeDtypeStruct((T, D), F32),
        scratch_shapes=[pltpu.VMEM((tm, D), BF16), pltpu.VMEM((tm, D), F32)],
        compiler_params=_params("parallel", "arbitrary"),
        name="mix_ffn_residual",
    )(a, b, w_out, w_out, x, g.reshape(1, D), w1, w3, w2, gf)


NEG_BIG = -1e30
_NT = (((1,), (1,)), ((), ()))


def _flash_steps(qas, kas, vas, masks, m_prev, acc_prev):
    hs = range(len(qas))
    s = [lax.dot_general(qas[h], kas[h], _NT, preferred_element_type=F32) for h in hs]
    s = [s[h] if masks[h] is None else jnp.where(masks[h], s[h], NEG_BIG) for h in hs]
    m_new = [jnp.maximum(m_prev[h], jnp.max(s[h], axis=1, keepdims=True)) for h in hs]
    alpha = [jnp.exp2(m_prev[h] - m_new[h]) for h in hs]
    p = [jnp.exp2(s[h] - m_new[h]) for h in hs]
    pv = [jnp.dot(p[h].astype(BF16), vas[h], preferred_element_type=F32) for h in hs]
    return m_new, [alpha[h] * acc_prev[h] + pv[h] for h in hs]


def _augment_q(q_log2, in_head, keep_t, odd):
    nblk, tq = keep_t.shape
    bias_t = jnp.where(keep_t, 0.0, NEG_BIG)
    bias = jnp.concatenate([bias_t, jnp.zeros((V7X_LANES - nblk, tq), F32)], axis=0).T
    if not odd:
        bias = pltpu.roll(bias, HEAD_DIM, axis=1)
    return jnp.where(in_head, q_log2, bias).astype(BF16)


def _key_value_tiles(k, v, blk, lane):
    low = lane < HEAD_DIM
    hot_e = 0.0 if blk is None else jnp.where(lane - HEAD_DIM == blk, 1.0, 0.0)
    hot_o = 0.0 if blk is None else jnp.where(lane == blk, 1.0, 0.0)
    k_e, k_o = jnp.where(low, k, hot_e), jnp.where(low, hot_o, k)
    v_e, v_o = jnp.where(low, v, jnp.where(lane == HEAD_DIM, 1.0, 0.0)), jnp.where(low, jnp.where(lane == 0, 1.0, 0.0), v)
    return [t.astype(BF16) for t in (k_e, k_o, v_e, v_o)]


def _normalise(acc, lane, odd):
    return acc / jnp.sum(jnp.where(lane == (0 if odd else HEAD_DIM), acc, 0.0), axis=1, keepdims=True)


LOG2E = 1.4426950408889634


MOBA_HEADS_PER_STEP = 8


def _split_bf16(x):
    hi = x.astype(BF16)
    return hi, (x - hi.astype(F32)).astype(BF16)


def _moba_kernel(q_ref, k_ref, v_ref, o_ref, ka_ref, va_ref, km_ref, acc_ref):
    L = MOBA_BLOCK
    S = k_ref.shape[1]
    nb = S // L
    HP = MOBA_HEADS_PER_STEP
    qi = pl.program_id(2)
    lane = lax.broadcasted_iota(jnp.int32, (L, V7X_LANES), 1)
    lanes_of = lambda ref, pp: ref[0, :, pp * V7X_LANES:(pp + 1) * V7X_LANES]

    @pl.when(qi == 0)
    def _():
        lane_s = lax.broadcasted_iota(jnp.int32, (S, V7X_LANES), 1)
        blk = lax.broadcasted_iota(jnp.int32, (S, V7X_LANES), 0) // L
        for pp in range(HP // 2):
            k = lanes_of(k_ref, pp)
            (ka_ref[2 * pp], ka_ref[2 * pp + 1], va_ref[2 * pp], va_ref[2 * pp + 1]) = _key_value_tiles(
                k, lanes_of(v_ref, pp), blk, lane_s)
            km_ref[pp] = jnp.concatenate(_split_bf16(jnp.mean(k.reshape(nb, L, V7X_LANES), axis=1)), axis=0)

    row = lax.broadcasted_iota(jnp.int32, (L, L), 0)
    col = lax.broadcasted_iota(jnp.int32, (L, L), 1)
    causal = col <= row
    jrow = lax.broadcasted_iota(jnp.int32, (nb, L), 0)
    past = jrow < qi
    qas = []
    for pp in range(HP // 2):
        q = lanes_of(q_ref, pp) * (HEAD_DIM ** -0.5)
        q_hi, q_lo = _split_bf16(q)
        km = km_ref[pp]
        for e in range(2):
            in_head = (lane >= e * HEAD_DIM) & (lane < (e + 1) * HEAD_DIM)
            zero = jnp.zeros_like(q_hi)
            g1 = lax.dot_general(km, jnp.where(in_head, q_hi, zero), _NT, preferred_element_type=F32)
            g2 = lax.dot_general(km, jnp.where(in_head, q_lo, zero), _NT, preferred_element_type=F32)
            gate = g1[0:nb] + g1[nb:2 * nb] + g2[0:nb]
            keep = jrow == qi
            for n in range(nb):
                g_n = gate[n:n + 1, :]
                beats = (gate > g_n) | ((gate == g_n) & (jrow < n))
                rank = jnp.sum(jnp.where(past & beats, 1.0, 0.0), axis=0, keepdims=True)
                keep = keep | ((jrow == n) & (rank < MOBA_TOPK) & past)
            qas.append(_augment_q(q * LOG2E, in_head, keep, odd=e == 1))

    tiles = lambda start: ([ka_ref[h, pl.ds(start, L), :] for h in range(HP)],
                           [va_ref[h, pl.ds(start, L), :] for h in range(HP)])

    m, acc = _flash_steps(qas, *tiles(pl.multiple_of(qi * L, L)), [causal] * HP,
                          [jnp.full((L, 1), NEG_BIG, F32)] * HP, [jnp.zeros((L, V7X_LANES), F32)] * HP)
    for h in range(HP):
        acc_ref[h] = acc[h]

    def body(n, carry):
        m2, acc2 = _flash_steps(qas, *tiles(pl.multiple_of(n * L, L)), [None] * HP, list(carry),
                                [acc_ref[h] for h in range(HP)])
        for h in range(HP):
            acc_ref[h] = acc2[h]
        return tuple(m2)

    lax.fori_loop(0, qi, body, tuple(m))
    for pp in range(HP // 2):
        o_ref[0, :, pp * V7X_LANES:(pp + 1) * V7X_LANES] = jnp.where(
            lane < HEAD_DIM, _normalise(acc_ref[2 * pp], lane, False), _normalise(acc_ref[2 * pp + 1], lane, True)
        ).astype(o_ref.dtype)


def moba_attention(z, *, n_heads=16):
    B, S, _ = z.shape
    L, HP = MOBA_BLOCK, MOBA_HEADS_PER_STEP
    W = HP * HEAD_DIM
    nb = S // L
    assert S % L == 0 and n_heads % HP == 0 and W % V7X_LANES == 0 and 2 * nb <= 16
    ngrp = n_heads // HP
    return pl.pallas_call(
        _moba_kernel,
        grid=(B, ngrp, S // L),
        in_specs=[
            pl.BlockSpec((1, L, W), lambda b, p, i: (b, i, p)),
            pl.BlockSpec((1, S, W), lambda b, p, i: (b, 0, ngrp + p)),
            pl.BlockSpec((1, S, W), lambda b, p, i: (b, 0, 2 * ngrp + p)),
        ],
        out_specs=pl.BlockSpec((1, L, W), lambda b, p, i: (b, i, p)),
        out_shape=jax.ShapeDtypeStruct((B, S, n_heads * HEAD_DIM), BF16),
        scratch_shapes=[
            pltpu.VMEM((HP, S, V7X_LANES), BF16),
            pltpu.VMEM((HP, S, V7X_LANES), BF16),
            pltpu.VMEM((HP // 2, 2 * nb, V7X_LANES), BF16),
            pltpu.VMEM((HP, L, V7X_LANES), F32),
        ],
        compiler_params=_params("parallel", "parallel", "arbitrary"),
        name="moba_attention",
    )(z, z, z)


NSA_TQ = 256
NSA_GROUPS_PER_STEP = 2
BIG = 3.0e38


def _gelu_tanh(x):
    return 0.5 * x * (1.0 + jnp.tanh(0.7978845608028654 * (x + 0.044715 * x * x * x)))


def _nsa_compress_kernel(xk0_ref, xk1_ref, xv0_ref, xv1_ref, pek_ref, pev_ref, w1k_ref, w1v_ref, w2k_ref, w2v_ref,
                         o1_ref, o2_ref):
    G, Lc, st = NSA_KV_GROUPS, NSA_CMP_BLOCK, NSA_CMP_STRIDE
    nrow = xk0_ref.shape[1] // st
    lane = lax.broadcasted_iota(jnp.int32, (nrow, G * HEAD_DIM), 1)

    def hidden(x_refs, pe_ref, w1_ref):
        acc = [jnp.zeros((G * nrow, V7X_LANES), F32) for _ in range(Lc // st)]
        for l in range(Lc):
            u, m = divmod(l, st)
            x = jnp.concatenate([r[0, pl.ds(m, nrow, stride=st), :] for r in x_refs], axis=1) + pe_ref[l:l + 1, :]
            xs = jnp.concatenate(
                [jnp.where((lane >= g * HEAD_DIM) & (lane < (g + 1) * HEAD_DIM), x, 0.0) for g in range(G)],
                axis=0).astype(BF16)
            acc[u] = acc[u] + jnp.dot(xs, w1_ref[l], preferred_element_type=F32)
        nxt = jnp.concatenate([pltpu.roll(acc[1][g * nrow:(g + 1) * nrow], nrow - 1, axis=0) for g in range(G)],
                              axis=0)
        return _gelu_tanh(acc[0] + nxt).astype(BF16)

    hk = hidden((xk0_ref, xk1_ref), pek_ref, w1k_ref)
    hv = hidden((xv0_ref, xv1_ref), pev_ref, w1v_ref)
    kc = jnp.dot(hk, w2k_ref[...], preferred_element_type=F32)
    vc = jnp.dot(hv, w2v_ref[...], preferred_element_type=F32)
    kv = kc + vc
    vk = pltpu.roll(kv, HEAD_DIM, axis=1)
    for g in range(G):
        o1_ref[0, :, g * V7X_LANES:(g + 1) * V7X_LANES] = kv[g * nrow:(g + 1) * nrow]
        o2_ref[0, :, g * V7X_LANES:(g + 1) * V7X_LANES] = vk[g * nrow:(g + 1) * nrow]


def nsa_compress(z, col_k, col_v, pe_k, w1_k, w2_k, pe_v, w1_v, w2_v):
    B, S, _ = z.shape
    G, Lc, st = NSA_KV_GROUPS, NSA_CMP_BLOCK, NSA_CMP_STRIDE
    GW = G * HEAD_DIM
    nrow = S // st
    hid = w1_k.shape[1]
    assert hid == V7X_LANES and col_k % GW == 0 and col_v % GW == 0
    tile_pe = lambda pe: jnp.tile(pe, (1, G))
    tile_w1 = lambda w: jnp.tile(w.reshape(Lc, 1, HEAD_DIM, hid), (1, G, 1, 1)).reshape(Lc, GW, hid).astype(BF16)
    w2k = jnp.pad(w2_k, ((0, 0), (0, HEAD_DIM))).astype(BF16)
    w2v = jnp.pad(w2_v, ((0, 0), (HEAD_DIM, 0))).astype(BF16)
    const = lambda shape: pl.BlockSpec(shape, lambda b: (0,) * len(shape))
    out = jax.ShapeDtypeStruct((B, nrow, G * V7X_LANES), F32)
    return pl.pallas_call(
        _nsa_compress_kernel,
        grid=(B,),
        in_specs=[
            pl.BlockSpec((1, S, V7X_LANES), lambda b: (b, 0, col_k // V7X_LANES)),
            pl.BlockSpec((1, S, V7X_LANES), lambda b: (b, 0, col_k // V7X_LANES + 1)),
            pl.BlockSpec((1, S, V7X_LANES), lambda b: (b, 0, col_v // V7X_LANES)),
            pl.BlockSpec((1, S, V7X_LANES), lambda b: (b, 0, col_v // V7X_LANES + 1)),
            const((Lc, GW)), const((Lc, GW)),
            const((Lc, GW, hid)), const((Lc, GW, hid)),
            const((hid, V7X_LANES)), const((hid, V7X_LANES)),
        ],
        out_specs=[pl.BlockSpec((1, nrow, G * V7X_LANES), lambda b: (b, 0, 0))] * 2,
        out_shape=[out, out],
        compiler_params=_params("parallel"),
        name="nsa_compress",
    )(z, z, z, z, tile_pe(pe_k), tile_pe(pe_v), tile_w1(w1_k), tile_w1(w1_v), w2k, w2v)


def _nsa_kernel(q_ref, c1_ref, c2_ref, s_ref, w_ref, g_ref, ovt_ref, o_ref, sk_ref, sv_ref, wk_ref, wv_ref, acc_ref):
    TQ = NSA_TQ
    S = s_ref.shape[1]
    R = NSA_HEADS // NSA_KV_GROUPS
    NG = NSA_GROUPS_PER_STEP
    NH = NG * R
    grp0 = pl.program_id(1) * NG
    qi = pl.program_id(2)
    lane = lax.broadcasted_iota(jnp.int32, (TQ, V7X_LANES), 1)
    tile_of = lambda ref, t: ref[0, :, t * V7X_LANES:(t + 1) * V7X_LANES]

    @pl.when(qi == 0)
    def _():
        lane_s = lax.broadcasted_iota(jnp.int32, (S, V7X_LANES), 1)
        blk = lax.broadcasted_iota(jnp.int32, (S, V7X_LANES), 0) // NSA_SLC_BLOCK
        for gg in range(NG):
            for src, k_ref, v_ref, hot in ((s_ref, sk_ref, sv_ref, blk), (w_ref, wk_ref, wv_ref, None)):
                kv = tile_of(src, gg)
                vk = pltpu.roll(kv, HEAD_DIM, axis=1)
                k_ref[2 * gg], k_ref[2 * gg + 1], v_ref[2 * gg], v_ref[2 * gg + 1] = _key_value_tiles(
                    jnp.where(lane_s < HEAD_DIM, kv, vk), jnp.where(lane_s < HEAD_DIM, vk, kv), hot, lane_s)

    q0 = pl.multiple_of(qi * TQ, TQ)
    row = lax.broadcasted_iota(jnp.int32, (TQ, TQ), 0)
    col = lax.broadcasted_iota(jnp.int32, (TQ, TQ), 1)
    causal = col <= row
    t_abs = q0 + lax.broadcasted_iota(jnp.int32, (TQ, V7X_LANES), 0)
    even_lanes = lane < HEAD_DIM

    cmask = lane * NSA_CMP_STRIDE + (NSA_CMP_BLOCK - 1) <= t_abs
    nblk = s_ref.shape[1] // NSA_SLC_BLOCK
    jrow = lax.broadcasted_iota(jnp.int32, (nblk, TQ), 0)
    own = (q0 + lax.broadcasted_iota(jnp.int32, (nblk, TQ), 1)) // NSA_SLC_BLOCK
    ovt = ovt_ref[...]
    head_lanes = [even_lanes if h % 2 == 0 else ~even_lanes for h in range(NH)]
    tiles = [tile_of(q_ref, h // 2) * (HEAD_DIM ** -0.5) for h in range(NH)]
    o_cmp, qas = [], []
    for gg in range(NG):
        c_kv_b, c_vk_b = tile_of(c1_ref, gg).astype(BF16), tile_of(c2_ref, gg).astype(BF16)
        p_sum = jnp.zeros((TQ, V7X_LANES), F32)
        for h in range(gg * R, (gg + 1) * R):
            qm = jnp.where(head_lanes[h], tiles[h], 0.0).astype(BF16)
            s = lax.dot_general(qm, c_kv_b if h % 2 == 0 else c_vk_b, _NT, preferred_element_type=F32)
            s = jnp.where(cmask, s, NEG_BIG)
            p = jnp.where(cmask, jnp.exp(s - jnp.max(s, axis=1, keepdims=True)), 0.0)
            den = jnp.sum(p, axis=1, keepdims=True)
            p = p / jnp.where(den > 0.0, den, 1.0)
            p_sum = p_sum + p
            o_cmp.append(jnp.dot(p.astype(BF16), c_vk_b if h % 2 == 0 else c_kv_b, preferred_element_type=F32))

        p_hi, p_lo = _split_bf16(p_sum)
        p_slc = (lax.dot_general(ovt, p_hi, _NT, preferred_element_type=F32)
                 + lax.dot_general(ovt, p_lo, _NT, preferred_element_type=F32))[0:nblk]
        score = jnp.where((jrow == own) | (jrow == 0), BIG, jnp.where(jrow > own, -BIG, p_slc))
        keep = jrow > nblk
        for j in range(nblk):
            s_j = score[j:j + 1, :]
            beats = (score > s_j) | ((score == s_j) & (jrow < j))
            rank = jnp.sum(jnp.where(beats, 1.0, 0.0), axis=0, keepdims=True)
            keep = keep | ((jrow == j) & (rank < NSA_SLC_TOPN) & (jrow <= own))
        qas += [_augment_q(tiles[h] * LOG2E, head_lanes[h], keep, odd=h % 2 == 1) for h in range(gg * R, (gg + 1) * R)]

    neg = [jnp.full((TQ, 1), NEG_BIG, F32)] * NH
    zacc = [jnp.zeros((TQ, V7X_LANES), F32)] * NH
    kv_index = [2 * (h // R) + h % 2 for h in range(NH)]

    def kv_blocks(k_ref, v_ref, start):
        return ([k_ref[kv_index[h], pl.ds(start, TQ), :] for h in range(NH)],
                [v_ref[kv_index[h], pl.ds(start, TQ), :] for h in range(NH)])

    m, acc = _flash_steps(qas, *kv_blocks(sk_ref, sv_ref, q0), [causal] * NH, neg, zacc)
    for h in range(NH):
        acc_ref[h] = acc[h]

    def body(kb, carry):
        m2, acc2 = _flash_steps(qas, *kv_blocks(sk_ref, sv_ref, pl.multiple_of(kb * TQ, TQ)), [None] * NH,
                                list(carry), [acc_ref[h] for h in range(NH)])
        for h in range(NH):
            acc_ref[h] = acc2[h]
        return tuple(m2)

    lax.fori_loop(0, qi, body, tuple(m))

    WK = NSA_WINDOW + TQ
    w0 = pl.multiple_of(jnp.maximum(qi - NSA_WINDOW // TQ, 0) * TQ, TQ)
    key_pos = w0 + lax.broadcasted_iota(jnp.int32, (TQ, WK), 1)
    t_win = q0 + lax.broadcasted_iota(jnp.int32, (TQ, WK), 0)
    in_window = (key_pos <= t_win) & (key_pos > t_win - NSA_WINDOW)
    _, acc = _flash_steps(qas, [wk_ref[kv_index[h], pl.ds(w0, WK), :] for h in range(NH)],
                          [wv_ref[kv_index[h], pl.ds(w0, WK), :] for h in range(NH)], [in_window] * NH, neg, zacc)

    gates = jax.nn.sigmoid(g_ref[0])
    outs = []
    for h in range(NH):
        o_slc = _normalise(acc_ref[h], lane, h % 2 == 1)
        o_win = _normalise(acc[h], lane, h % 2 == 1)
        c0 = (grp0 * R + h) * 3
        gate = lambda c: jnp.sum(jnp.where(lane == c, gates, 0.0), axis=1, keepdims=True)
        outs.append(gate(c0) * o_cmp[h] + gate(c0 + 1) * o_slc + gate(c0 + 2) * o_win)
    for p2 in range(NH // 2):
        o_ref[0, :, p2 * V7X_LANES:(p2 + 1) * V7X_LANES] = jnp.where(
            even_lanes, outs[2 * p2], outs[2 * p2 + 1]).astype(o_ref.dtype)


def nsa_attention(z, cmp_kv, cmp_vk, col_q, col_slc, col_win, col_gate):
    B, S, _ = z.shape
    G, TQ, NG = NSA_KV_GROUPS, NSA_TQ, NSA_GROUPS_PER_STEP
    R = NSA_HEADS // G
    QW = NG * R * HEAD_DIM
    KW = NG * V7X_LANES
    ncmp = cmp_kv.shape[1]
    assert S % TQ == 0 and ncmp == V7X_LANES and S // NSA_SLC_BLOCK <= V7X_LANES
    assert NSA_WINDOW % TQ == 0 and S >= NSA_WINDOW + TQ
    assert G % NG == 0 and R % 2 == 0
    assert col_q % QW == 0 and col_slc % KW == 0 and col_win % KW == 0 and col_gate % V7X_LANES == 0
    nc = (S - NSA_CMP_BLOCK) // NSA_CMP_STRIDE + 1
    c_start = np.arange(V7X_LANES) * NSA_CMP_STRIDE
    s_start = np.arange(V7X_LANES) * NSA_SLC_BLOCK
    overlap = ((c_start[:, None] <= s_start[None, :] + NSA_SLC_BLOCK - 1)
               & (c_start[:, None] + NSA_CMP_BLOCK - 1 >= s_start[None, :])
               & (np.arange(V7X_LANES)[:, None] < nc) & (np.arange(V7X_LANES)[None, :] < S // NSA_SLC_BLOCK))
    const = lambda shape: pl.BlockSpec(shape, lambda b, g, i: (0,) * len(shape))
    return pl.pallas_call(
        _nsa_kernel,
        grid=(B, G // NG, S // TQ),
        in_specs=[
            pl.BlockSpec((1, TQ, QW), lambda b, g, i: (b, i, col_q // QW + g)),
            pl.BlockSpec((1, ncmp, KW), lambda b, g, i: (b, 0, g)),
            pl.BlockSpec((1, ncmp, KW), lambda b, g, i: (b, 0, g)),
            pl.BlockSpec((1, S, KW), lambda b, g, i: (b, 0, col_slc // KW + g)),
            pl.BlockSpec((1, S, KW), lambda b, g, i: (b, 0, col_win // KW + g)),
            pl.BlockSpec((1, TQ, V7X_LANES), lambda b, g, i: (b, i, col_gate // V7X_LANES)),
            const((V7X_LANES, V7X_LANES)),
        ],
        out_specs=pl.BlockSpec((1, TQ, QW), lambda b, g, i: (b, i, g)),
        out_shape=jax.ShapeDtypeStruct((B, S, NSA_HEADS * HEAD_DIM), BF16),
        scratch_shapes=[pltpu.VMEM((2 * NG, S, V7X_LANES), BF16)] * 4 + [
            pltpu.VMEM((NG * R, TQ, V7X_LANES), F32),
        ],
        compiler_params=_params("parallel", "parallel", "arbitrary"),
        name="nsa_attention",
    )(z, cmp_kv, cmp_vk, z, z, z, jnp.asarray(overlap.T, BF16))


RWKV_CHUNK = 64
RWKV_ROWS = 512
RWKV_INTERLEAVE = 8


def _mm(a, b, dims=None):
    dims = dims or (((1,), (0,)), ((), ()))
    return lax.dot_general(a.astype(BF16), b.astype(BF16), dims, preferred_element_type=F32)


def _mm3(a, b):
    (a_hi, a_lo), (b_hi, b_lo) = _split_bf16(a), _split_bf16(b)
    return _mm(a_hi, b_hi) + (_mm(a_hi, b_lo) + _mm(a_lo, b_hi))


def _mm_onehot(a01, b):
    hi = b.astype(BF16)
    mid, lo = _split_bf16(b - hi.astype(F32))
    return _mm(a01, hi) + (_mm(a01, mid) + _mm(a01, lo))


def _head_sum(x, low):
    s0 = jnp.sum(jnp.where(low, x, 0.0), axis=1, keepdims=True)
    s1 = jnp.sum(jnp.where(low, 0.0, x), axis=1, keepdims=True)
    return jnp.where(low, s0, s1)


def _rwkv_kernel(r_ref, k_ref, v_ref, lo_ref, glo_ref, pp_ref, pl_ref, wup_ref, aup_ref, gup_ref, o_ref,
                 rs, ws, ks, vs, als, bes, gs, ys, bon, hs, rqs, ms, ns):
    S = r_ref.shape[1]
    C, RB = RWKV_CHUNK, RWKV_ROWS
    pp = pp_ref[...]
    mu_r, mu_k, mu_v, w0, a0, k_k, k_a, r_k, ln_g, ln_b = [pp[i:i + 1, :] for i in range(10)]
    mu_lo, mu_g = pl_ref[0:1, :], pl_ref[1:2, :]
    heads = lax.broadcasted_iota(jnp.int32, (RB, V7X_LANES), 1) < HEAD_DIM
    first = lax.broadcasted_iota(jnp.int32, (RB, V7X_LANES), 0) == 0

    def prologue(i, c):
        t0 = pl.multiple_of(i * RB, RB)
        tp = jnp.maximum(t0 - 1, 0)
        keep = jnp.where(i > 0, 1.0, 0.0)

        def shifted(ref, mu):
            x = ref[0, pl.ds(t0, RB), :]
            prev = jnp.where(first, ref[0, pl.ds(tp, 1), :] * keep, pltpu.roll(x, 1, axis=0))
            return x + (prev - x) * mu

        r, k, v = shifted(r_ref, mu_r), shifted(k_ref, mu_k), shifted(v_ref, mu_v)
        lo, glo = shifted(lo_ref, mu_lo), shifted(glo_ref, mu_g)
        wp = -(w0 + _mm(jnp.tanh(lo), wup_ref[...]))
        w = -(jnp.maximum(wp, 0.0) + jnp.log(1.0 + jnp.exp(-jnp.abs(wp)))) - 0.5
        a = jax.nn.sigmoid(a0 + _mm(lo, aup_ref[...]))
        kk = k * k_k
        kk = kk * lax.rsqrt(jnp.maximum(_head_sum(kk * kk, heads), 1e-24))
        k2 = k * (1.0 + (a - 1.0) * k_a)
        rs[pl.ds(t0, RB), :] = r
        ws[pl.ds(t0, RB), :] = -jnp.exp(w)
        ks[pl.ds(t0, RB), :] = k2
        vs[pl.ds(t0, RB), :] = v
        als[pl.ds(t0, RB), :] = -kk
        bes[pl.ds(t0, RB), :] = kk * a
        gs[pl.ds(t0, RB), :] = _mm(jax.nn.sigmoid(glo), gup_ref[...])
        bon[pl.ds(t0, RB), :] = _head_sum(r * k2 * r_k, heads) * v
        return c

    lax.fori_loop(0, S // RB, prologue, 0)

    W2 = 2 * C
    row = lax.broadcasted_iota(jnp.int32, (W2, W2), 0)
    col = lax.broadcasted_iota(jnp.int32, (W2, W2), 1)
    t_idx, s_idx = row % C, col % C
    top, left = row < C, col < C
    same = top == left
    eye = jnp.where(row == col, 1.0, 0.0)
    tri = jnp.where(lax.broadcasted_iota(jnp.int32, (C, C), 1) <= lax.broadcasted_iota(jnp.int32, (C, C), 0), 1.0, 0.0)
    low_c = lax.broadcasted_iota(jnp.int32, (C, V7X_LANES), 1) < HEAD_DIM
    fold = lambda x: x[0:C] + x[C:W2]
    stack_heads = lambda x: jnp.concatenate([jnp.where(low_c, x, 0.0), jnp.where(low_c, 0.0, x)], axis=0)
    block_diag = lambda x: jnp.where(top, jnp.where(left, x, 0.0), jnp.where(left, 0.0, pltpu.roll(x, C, axis=1)))

    rows = lambda c: pl.ds(c * C if isinstance(c, int) else pl.multiple_of(c * C, C), C)

    def advance(c, H):
        ys[rows(c), :] += _mm3(rqs[c], H)
        return _mm3(ms[c], H) + ns[c]

    def transfers(i, lagged):
        each = lambda f, *xs: [f(*a) for a in zip(*xs)]
        cs = [i * RWKV_INTERLEAVE + u for u in range(RWKV_INTERLEAVE)]
        sls = [rows(c) for c in cs]
        state = [hs[...]] if lagged else None

        def lag(hook):
            if lagged:
                for u in range(hook * RWKV_INTERLEAVE // 8, (hook + 1) * RWKV_INTERLEAVE // 8):
                    state[0] = advance(cs[u] - RWKV_INTERLEAVE, state[0])

        r, lw, k2, v, al, be = ([ref[sl, :] for sl in sls] for ref in (rs, ws, ks, vs, als, bes))
        logp = each(lambda x: _mm_onehot(tri, x), lw)
        lag(0)
        P = each(jnp.exp, logp)
        Pinv = each(lambda x: jnp.exp(-x), logp)
        At = each(lambda a_, lp, w_: a_ * jnp.exp(lp - w_), al, logp, lw)
        Rt, Bt, Kt = each(jnp.multiply, r, P), each(jnp.multiply, be, Pinv), each(jnp.multiply, k2, Pinv)
        PC = each(lambda p: p[C - 1:C, :], P)
        A_bd, R_bd = each(stack_heads, At), each(stack_heads, Rt)
        Yt = each(lambda b, k: jnp.concatenate([b, k], axis=0), Bt, Kt)
        A1 = each(lambda a, y: jnp.where(s_idx < t_idx, _mm(a, y, dims=_NT), 0.0), A_bd, Yt)
        A2 = each(lambda a, y: jnp.where(s_idx <= t_idx, _mm(a, y, dims=_NT), 0.0), R_bd, Yt)
        X, Arb = each(block_diag, A1), each(block_diag, A2)
        T = each(lambda x: eye + x, X)
        for it in range(5):
            X = each(lambda x: _mm(x, x), X)
            T = each(lambda t, x: t + _mm(t, x), T, X)
            lag(1 + it)
        V0 = each(lambda x: jnp.concatenate([jnp.zeros_like(x), x], axis=0), v)
        TA = each(_mm, T, A_bd)
        AkV = each(lambda a, x: jnp.where(same, _mm(a, x), 0.0), A1, V0)
        lag(6)
        U0 = each(_mm, T, AkV)
        lag(7)
        AR = each(lambda a, t, u: _mm(a, jnp.concatenate([t, u], axis=1)), Arb, TA, U0)
        ArkV = each(lambda a, x: jnp.where(same, _mm(a, x), 0.0), A2, V0)
        Mx = each(lambda b, p, t: _mm((b * p).T, fold(t)), Bt, PC, TA)
        Nx = each(lambda b, k, p, u, x: _mm(jnp.concatenate([b * p, k * p], axis=0).T,
                                            jnp.concatenate([fold(u), x], axis=0)), Bt, Kt, PC, U0, v)
        for u in range(RWKV_INTERLEAVE):
            ys[sls[u], :] = fold(AR[u][:, W2:2 * W2] + ArkV[u])
            rqs[cs[u]] = Rt[u] + fold(AR[u][:, 0:W2])
            ms[cs[u]] = eye * PC[u] + jnp.where(same, Mx[u], 0.0)
            ns[cs[u]] = jnp.where(same, Nx[u], 0.0)
        if lagged:
            hs[...] = state[0]

    def pipelined(i, carry):
        transfers(i, True)
        return carry

    def drain(c, carry):
        hs[...] = advance(c, hs[...])
        return carry

    assert 8 % RWKV_INTERLEAVE == 0
    hs[...] = jnp.zeros((W2, W2), F32)
    transfers(0, False)
    lax.fori_loop(1, S // C // RWKV_INTERLEAVE, pipelined, 0)
    lax.fori_loop(S // C - RWKV_INTERLEAVE, S // C, drain, 0)

    def epilogue(i, c):
        sl = pl.ds(pl.multiple_of(i * RB, RB), RB)
        y = ys[sl, :]
        d = y - _head_sum(y, heads) * (1.0 / HEAD_DIM)
        var = _head_sum(d * d, heads) * (1.0 / HEAD_DIM)
        yn = d * lax.rsqrt(var + RWKV_GN_EPS) * ln_g + ln_b
        o_ref[0, sl, :] = ((yn + bon[sl, :]) * gs[sl, :]).astype(o_ref.dtype)
        return c

    lax.fori_loop(0, S // RB, epilogue, 0)


def rwkv7_mixer(z, shift_mu, w0, w_up, a0, a_up, g_up, k_k, k_a, r_k, ln_g, ln_b):
    B, S, _ = z.shape
    CW = RWKV_HEADS * HEAD_DIM
    npair = CW // V7X_LANES
    base = 3 * CW // V7X_LANES
    lora = w_up.shape[0] + a_up.shape[0]
    assert lora == V7X_LANES and g_up.shape[0] == V7X_LANES and S % RWKV_ROWS == 0
    pp = jnp.stack([shift_mu[0:CW], shift_mu[CW:2 * CW], shift_mu[2 * CW:3 * CW], w0, a0, k_k, k_a,
                    r_k.reshape(CW), ln_g, ln_b])
    pp = jnp.pad(pp, ((0, 16 - pp.shape[0]), (0, 0)))
    pl2 = jnp.pad(shift_mu[3 * CW:].reshape(2, V7X_LANES), ((0, 6), (0, 0)))
    wup = jnp.pad(w_up, ((0, a_up.shape[0]), (0, 0)))
    aup = jnp.pad(a_up, ((w_up.shape[0], 0), (0, 0)))
    tile = lambda off: pl.BlockSpec((1, S, V7X_LANES), lambda b, p: (b, 0, base + off * npair + p))
    fixed = lambda off: pl.BlockSpec((1, S, V7X_LANES), lambda b, p: (b, 0, base + 3 * npair + off))
    seq = pltpu.VMEM((S, V7X_LANES), F32)
    return pl.pallas_call(
        _rwkv_kernel,
        grid=(B, npair),
        in_specs=[
            tile(0), tile(1), tile(2), fixed(0), fixed(1),
            pl.BlockSpec((16, V7X_LANES), lambda b, p: (0, p)),
            pl.BlockSpec((8, V7X_LANES), lambda b, p: (0, 0)),
            pl.BlockSpec((V7X_LANES, V7X_LANES), lambda b, p: (0, p)),
            pl.BlockSpec((V7X_LANES, V7X_LANES), lambda b, p: (0, p)),
            pl.BlockSpec((V7X_LANES, V7X_LANES), lambda b, p: (0, p)),
        ],
        out_specs=pl.BlockSpec((1, S, V7X_LANES), lambda b, p: (b, 0, p)),
        out_shape=jax.ShapeDtypeStruct((B, S, CW), BF16),
        scratch_shapes=[seq] * 9 + [
            pltpu.VMEM((V7X_LANES, V7X_LANES), F32),
            pltpu.VMEM((S // RWKV_CHUNK, RWKV_CHUNK, V7X_LANES), F32),
            pltpu.VMEM((S // RWKV_CHUNK, V7X_LANES, V7X_LANES), F32),
            pltpu.VMEM((S // RWKV_CHUNK, V7X_LANES, V7X_LANES), F32),
        ],
        compiler_params=_params("parallel", "parallel"),
        name="rwkv7_mixer",
    )(z, z, z, z, z, pp, pl2, wup, aup, g_up.astype(BF16))


RET_CHUNKS_PER_STEP = 4


def _ret_kernel(q_ref, k_ref, v_ref, g_ref, cos_ref, sin_ref, din_ref, dq_ref, dk_ref, dc_ref, o_ref, st_ref):
    S = q_ref.shape[1]
    C, DV = RET_CHUNK, RET_V_DIM
    lane = lax.broadcasted_iota(jnp.int32, (C, V7X_LANES), 1)
    first_half = (lane % RET_QK_DIM) < RET_QK_DIM // 2
    st_ref[...] = jnp.zeros_like(st_ref)

    in_head = [(lane >= h * RET_QK_DIM) & (lane < (h + 1) * RET_QK_DIM) for h in range(2)]
    NCH = RET_CHUNKS_PER_STEP
    units = [(u, h) for u in range(NCH) for h in range(2)]

    def step(i, carry):
        sls = [pl.ds(pl.multiple_of((i * NCH + u) * C, C), C) for u in range(NCH)]

        def rot(z, sl):
            swapped = jnp.where(first_half, pltpu.roll(z, V7X_LANES - RET_QK_DIM // 2, axis=1),
                                pltpu.roll(z, RET_QK_DIM // 2, axis=1))
            return z * cos_ref[sl, :] + swapped * sin_ref[sl, :]

        q = [rot(q_ref[0, sl, :], sl) for sl in sls]
        k = [rot(k_ref[0, sl, :], sl) * (RET_QK_DIM ** -0.5) for sl in sls]
        qm = [jnp.where(in_head[h], q[u], 0.0) for u, h in units]
        v = [v_ref[0, sls[u], h * DV:(h + 1) * DV] for u, h in units]
        inner = [_mm(qm[n], k[u], dims=_NT) * din_ref[h] for n, (u, h) in enumerate(units)]
        upd = [_mm((jnp.where(in_head[h], k[u], 0.0) * dk_ref[h]).T, v[n]) for n, (u, h) in enumerate(units)]
        local = [_mm(inner[n], v[n]) for n in range(len(units))]
        st = [st_ref[h] for h in range(2)]
        for n, (u, h) in enumerate(units):
            o = local[n] + _mm(qm[n], st[h]) * dq_ref[h]
            st[h] = upd[n] + dc_ref[h, 0:1, :] * st[h]
            d = o - jnp.mean(o, axis=1, keepdims=True)
            on = d * lax.rsqrt(jnp.mean(d * d, axis=1, keepdims=True) + RET_GN_EPS)
            gate = g_ref[0, sls[u], h * DV:(h + 1) * DV]
            o_ref[0, sls[u], h * DV:(h + 1) * DV] = (gate * jax.nn.sigmoid(gate) * on).astype(o_ref.dtype)
        st_ref[0], st_ref[1] = st
        return carry

    lax.fori_loop(0, S // C // NCH, step, 0)


def retention_mixer(z):
    B, S, _ = z.shape
    H, C, DK, DV = RET_HEADS, RET_CHUNK, RET_QK_DIM, RET_V_DIM
    assert S % C == 0 and 2 * DK == V7X_LANES and DV == V7X_LANES
    npair = H // 2
    half = DK // 2
    inv = ROPE_BASE ** (-jnp.arange(half, dtype=F32) / half)
    ang = jnp.arange(S, dtype=F32)[:, None] * inv
    cos = jnp.tile(jnp.cos(ang), (1, 4))
    sin = jnp.tile(jnp.concatenate([-jnp.sin(ang), jnp.sin(ang)], axis=1), (1, 2))
    log_g = jnp.asarray(np.log(1.0 - 2.0 ** (-5.0 - np.arange(H))), F32)
    n = jnp.arange(C, dtype=F32)
    diff = n[:, None] - n[None, :]
    d_in = jnp.where(diff >= 0, jnp.exp(jnp.maximum(diff, 0.0) * log_g[:, None, None]), 0.0)
    lanes = lambda t: jnp.broadcast_to(t[..., None], t.shape + (V7X_LANES,))
    d_q = lanes(jnp.exp((n + 1.0) * log_g[:, None]))
    d_k = lanes(jnp.exp((C - 1.0 - n) * log_g[:, None]))
    d_c = lanes(jnp.broadcast_to(jnp.exp(C * log_g)[:, None], (H, 8)))
    qk_tiles = H * DK // V7X_LANES
    return pl.pallas_call(
        _ret_kernel,
        grid=(B, npair),
        in_specs=[
            pl.BlockSpec((1, S, V7X_LANES), lambda b, p: (b, 0, p)),
            pl.BlockSpec((1, S, V7X_LANES), lambda b, p: (b, 0, qk_tiles + p)),
            pl.BlockSpec((1, S, 2 * DV), lambda b, p: (b, 0, 2 * qk_tiles * V7X_LANES // (2 * DV) + p)),
            pl.BlockSpec((1, S, 2 * DV), lambda b, p: (b, 0, (2 * qk_tiles * V7X_LANES + H * DV) // (2 * DV) + p)),
            pl.BlockSpec((S, V7X_LANES), lambda b, p: (0, 0)),
            pl.BlockSpec((S, V7X_LANES), lambda b, p: (0, 0)),
            pl.BlockSpec((2, C, C), lambda b, p: (p, 0, 0)),
            pl.BlockSpec((2, C, V7X_LANES), lambda b, p: (p, 0, 0)),
            pl.BlockSpec((2, C, V7X_LANES), lambda b, p: (p, 0, 0)),
            pl.BlockSpec((2, 8, V7X_LANES), lambda b, p: (p, 0, 0)),
        ],
        out_specs=pl.BlockSpec((1, S, 2 * DV), lambda b, p: (b, 0, p)),
        out_shape=jax.ShapeDtypeStruct((B, S, H * DV), BF16),
        scratch_shapes=[pltpu.VMEM((2, V7X_LANES, DV), F32)],
        compiler_params=_params("parallel", "parallel"),
        name="retention_mixer",
    )(z, z, z, z, cos, sin, d_in, d_q, d_k, d_c)


def _even_mixer(x, g_norm, w_in, shift_mu, w0, w_up, a0, a_up, g_up, k_k, k_a, r_k, ln_g, ln_b):
    B, S, D = x.shape
    z = norm_matmul(x.reshape(B * S, D), g_norm, w_in.astype(BF16)).reshape(B, S, -1)
    o_a = moba_attention(z)
    o_b = rwkv7_mixer(z, shift_mu, w0, w_up, a0, a_up, g_up, k_k, k_a, r_k, ln_g, ln_b)
    return o_a, o_b


def _odd_mixer(x, g_norm, w_in, pe_k, w1_k, w2_k, pe_v, w1_v, w2_v):
    B, S, D = x.shape
    perm, col = _odd_layout()
    n_in = w_in.shape[1]
    real = perm[perm < n_in]
    cuts = [0] + [k for k in range(1, len(real)) if real[k] != real[k - 1] + 1] + [len(real)]
    w_b = w_in.astype(BF16)
    w_p = jnp.concatenate([w_b[:, int(real[a]):int(real[b - 1]) + 1] for a, b in zip(cuts[:-1], cuts[1:])]
                          + [jnp.zeros((w_in.shape[0], len(perm) - len(real)), BF16)], axis=1)
    z = norm_matmul(x.reshape(B * S, D), g_norm, w_p).reshape(B, S, -1)
    o_c = retention_mixer(z)
    cmp_kv, cmp_vk = nsa_compress(z, col["kc"], col["vc"], pe_k, w1_k, w2_k, pe_v, w1_v, w2_v)
    o_d = nsa_attention(z, cmp_kv, cmp_vk, col["nq"], col["slc"], col["win"], col["gate"])
    return o_c, o_d


def _odd_layout():
    G, Dh = NSA_KV_GROUPS, HEAD_DIM
    sizes = (RET_HEADS * RET_QK_DIM, RET_HEADS * RET_QK_DIM, RET_HEADS * RET_V_DIM, RET_HEADS * RET_V_DIM,
             NSA_HEADS * Dh) + (G * Dh,) * 6 + (3 * NSA_HEADS,)
    off = np.concatenate([[0], np.cumsum(sizes)])
    rq, rk, rv, rg, nq, kc, vc, ks, vs, kw, vw, ng = off[:-1]
    n_in = int(off[-1])
    pair = lambda a, b: np.concatenate([np.concatenate([a + g * Dh + np.arange(Dh), b + g * Dh + np.arange(Dh)])
                                        for g in range(G)])
    perm = np.concatenate([np.arange(ks), pair(ks, vs), pair(kw, vw), ng + np.arange(3 * NSA_HEADS)])
    n_pad = -(-len(perm) // (6 * V7X_MXU_DIM)) * 6 * V7X_MXU_DIM
    perm = np.concatenate([perm, np.full(n_pad - len(perm), n_in)]).astype(np.int32)
    col = {"nq": int(nq), "kc": int(kc), "vc": int(vc), "slc": int(ks), "win": int(ks) + 2 * G * Dh,
           "gate": int(ks) + 4 * G * Dh}
    return perm, col


def kernel(x, mix_norm, ffn_norm, even_w_in, even_shift_mu, even_w0, even_w_up, even_a0, even_a_up, even_g_up, even_k_k, even_k_a, even_r_k, even_ln_g, even_ln_b, even_w_out, odd_w_in, odd_cmp_pe_k, odd_cmp_w1_k, odd_cmp_w2_k, odd_cmp_pe_v, odd_cmp_w1_v, odd_cmp_w2_v, odd_w_out, ffn_w1, ffn_w3, ffn_w2, final_norm):
    B, S, D = x.shape
    depth = mix_norm.shape[0]
    w1, w3, w2 = ffn_w1.astype(BF16), ffn_w3.astype(BF16), ffn_w2.astype(BF16)
    for layer in range(depth):
        i = layer // 2
        if layer % 2 == 0:
            o1, o2 = _even_mixer(x, mix_norm[layer], even_w_in[i], even_shift_mu[i], even_w0[i], even_w_up[i],
                                 even_a0[i], even_a_up[i], even_g_up[i], even_k_k[i], even_k_a[i], even_r_k[i],
                                 even_ln_g[i], even_ln_b[i])
            w_out = even_w_out[i]
        else:
            o1, o2 = _odd_mixer(x, mix_norm[layer], odd_w_in[i], odd_cmp_pe_k[i], odd_cmp_w1_k[i], odd_cmp_w2_k[i],
                                odd_cmp_pe_v[i], odd_cmp_w1_v[i], odd_cmp_w2_v[i])
            w_out = odd_w_out[i]
        T = B * S
        x2 = mix_ffn_residual(o1.reshape(T, -1), o2.reshape(T, -1), w_out.astype(BF16), x.reshape(T, D),
                              ffn_norm[layer], w1, w3, w2, layer, final_norm if layer == depth - 1 else None)
        x = x2.reshape(B, S, D)
    return x
```

```python
import functools

import jax
import jax.numpy as jnp
import numpy as np
from jax import lax
from jax.experimental import pallas as pl
from jax.experimental.pallas import tpu as pltpu

F32 = jnp.float32
BF16 = jnp.bfloat16

V7X_LANES = 128
V7X_MXU_DIM = 256
V7X_VMEM_BYTES = 64 * 1024 * 1024
VMEM_LIMIT = V7X_VMEM_BYTES * 7 // 8

NORM_EPS = 1e-6
HEAD_DIM = 64

MOBA_BLOCK = 256
MOBA_TOPK = 3
RWKV_HEADS = 16
RWKV_GN_EPS = 6.4e-4

RET_HEADS = 8
RET_QK_DIM = 64
RET_V_DIM = 128
RET_CHUNK = 128
RET_GN_EPS = 1e-6
ROPE_BASE = 10000.0
NSA_HEADS = 16
NSA_KV_GROUPS = 4
NSA_CMP_BLOCK = 32
NSA_CMP_STRIDE = 16
NSA_SLC_BLOCK = 64
NSA_SLC_TOPN = 16
NSA_WINDOW = 512


def _params(*semantics):
    return pltpu.CompilerParams(dimension_semantics=semantics, vmem_limit_bytes=VMEM_LIMIT)


def _rms(x, g):
    return x * lax.rsqrt(jnp.mean(x * x, axis=-1, keepdims=True) + NORM_EPS) * g


def _norm_matmul_kernel(x_ref, g_ref, w_ref, o_ref):
    x = x_ref[...]
    scale = lax.rsqrt(jnp.mean(x * x, axis=-1, keepdims=True) + NORM_EPS)
    o_ref[...] = jnp.dot((x * g_ref[...]).astype(BF16), w_ref[...], preferred_element_type=F32) * scale


def _proj_tile(n):
    assert n % V7X_MXU_DIM == 0
    k = n // V7X_MXU_DIM
    return V7X_MXU_DIM * max(d for d in range(1, 7) if k % d == 0)


def norm_matmul(x, g, w, *, tm=512):
    T, D = x.shape
    N = w.shape[1]
    tn = _proj_tile(N)
    assert T % tm == 0 and N % tn == 0
    return pl.pallas_call(
        _norm_matmul_kernel,
        grid=(N // tn, T // tm),
        in_specs=[
            pl.BlockSpec((tm, D), lambda j, i: (i, 0)),
            pl.BlockSpec((1, D), lambda j, i: (0, 0)),
            pl.BlockSpec((D, tn), lambda j, i: (0, j)),
        ],
        out_specs=pl.BlockSpec((tm, tn), lambda j, i: (i, j)),
        out_shape=jax.ShapeDtypeStruct((T, N), F32),
        compiler_params=_params("parallel", "parallel"),
        name="norm_matmul",
    )(x, g.reshape(1, D), w)


def _mix_ffn_kernel(a_ref, b_ref, wa_ref, wb_ref, x_ref, g_ref, w1_ref, w3_ref, w2_ref, gf_ref, o_ref,
                    h_ref, acc_ref, *, final_norm):
    j = pl.program_id(1)

    @pl.when(j == 0)
    def _():
        x2 = (x_ref[...] + jnp.dot(a_ref[...], wa_ref[...], preferred_element_type=F32)
              + jnp.dot(b_ref[...], wb_ref[...], preferred_element_type=F32))
        o_ref[...] = x2
        h_ref[...] = _rms(x2, g_ref[...]).astype(BF16)
        acc_ref[...] = jnp.zeros_like(acc_ref)

    h = h_ref[...]
    a = jnp.dot(h, w1_ref[...], preferred_element_type=F32)
    b = jnp.dot(h, w3_ref[...], preferred_element_type=F32)
    act = (a * jax.nn.sigmoid(a) * b).astype(BF16)
    acc_ref[...] += jnp.dot(act, w2_ref[...], preferred_element_type=F32)

    @pl.when(j == pl.num_programs(1) - 1)
    def _():
        y = o_ref[...] + acc_ref[...]
        if final_norm:
            y = _rms(y, gf_ref[...])
        o_ref[...] = y


def mix_ffn_residual(a, b, w_out, x, g, w1, w3, w2, layer, g_final=None, *, tm=512, tf=512):
    T, D = x.shape
    K = a.shape[1]
    Fh = w1.shape[2]
    assert T % tm == 0 and Fh % tf == 0 and b.shape == a.shape and w_out.shape == (2 * K, D)
    final_norm = g_final is not None
    gf = (g_final if final_norm else g).reshape(1, D)
    once = pl.Buffered(1)
    return pl.pallas_call(
        functools.partial(_mix_ffn_kernel, final_norm=final_norm),
        grid=(T // tm, Fh // tf),
        in_specs=[
            pl.BlockSpec((tm, K), lambda i, j: (i, 0)),
            pl.BlockSpec((tm, K), lambda i, j: (i, 0)),
            pl.BlockSpec((K, D), lambda i, j: (0, 0), pipeline_mode=once),
            pl.BlockSpec((K, D), lambda i, j: (1, 0), pipeline_mode=once),
            pl.BlockSpec((tm, D), lambda i, j: (i, 0)),
            pl.BlockSpec((1, D), lambda i, j: (0, 0), pipeline_mode=once),
            pl.BlockSpec((None, D, tf), lambda i, j: (layer, 0, j)),
            pl.BlockSpec((None, D, tf), lambda i, j: (layer, 0, j)),
            pl.BlockSpec((None, tf, D), lambda i, j: (layer, j, 0)),
            pl.BlockSpec((1, D), lambda i, j: (0, 0), pipeline_mode=once),
        ],
        out_specs=pl.BlockSpec((tm, D), lambda i, j: (i, 0)),
        out_shape=jax.ShapeDtypeStruct((T, D), F32),
        scratch_shapes=[pltpu.VMEM((tm, D), BF16), pltpu.VMEM((tm, D), F32)],
        compiler_params=_params("parallel", "arbitrary"),
        name="mix_ffn_residual",
    )(a, b, w_out, w_out, x, g.reshape(1, D), w1, w3, w2, gf)


NEG_BIG = -1e30
_NT = (((1,), (1,)), ((), ()))


def _flash_steps(qas, kas, vas, masks, m_prev, acc_prev):
    hs = range(len(qas))
    s = [lax.dot_general(qas[h], kas[h], _NT, preferred_element_type=F32) for h in hs]
    s = [s[h] if masks[h] is None else jnp.where(masks[h], s[h], NEG_BIG) for h in hs]
    m_new = [jnp.maximum(m_prev[h], jnp.max(s[h], axis=1, keepdims=True)) for h in hs]
    alpha = [jnp.exp2(m_prev[h] - m_new[h]) for h in hs]
    p = [jnp.exp2(s[h] - m_new[h]) for h in hs]
    pv = [jnp.dot(p[h].astype(BF16), vas[h], preferred_element_type=F32) for h in hs]
    return m_new, [alpha[h] * acc_prev[h] + pv[h] for h in hs]


def _augment_q(q_log2, in_head, keep_t, odd):
    nblk, tq = keep_t.shape
    bias_t = jnp.where(keep_t, 0.0, NEG_BIG)
    bias = jnp.concatenate([bias_t, jnp.zeros((V7X_LANES - nblk, tq), F32)], axis=0).T
    if not odd:
        bias = pltpu.roll(bias, HEAD_DIM, axis=1)
    return jnp.where(in_head, q_log2, bias).astype(BF16)


def _key_value_tiles(k, v, blk, lane):
    low = lane < HEAD_DIM
    hot_e = 0.0 if blk is None else jnp.where(lane - HEAD_DIM == blk, 1.0, 0.0)
    hot_o = 0.0 if blk is None else jnp.where(lane == blk, 1.0, 0.0)
    k_e, k_o = jnp.where(low, k, hot_e), jnp.where(low, hot_o, k)
    v_e, v_o = jnp.where(low, v, jnp.where(lane == HEAD_DIM, 1.0, 0.0)), jnp.where(low, jnp.where(lane == 0, 1.0, 0.0), v)
    return [t.astype(BF16) for t in (k_e, k_o, v_e, v_o)]


def _normalise(acc, lane, odd):
    return acc / jnp.sum(jnp.where(lane == (0 if odd else HEAD_DIM), acc, 0.0), axis=1, keepdims=True)


LOG2E = 1.4426950408889634


MOBA_HEADS_PER_STEP = 8


def _split_bf16(x):
    hi = x.astype(BF16)
    return hi, (x - hi.astype(F32)).astype(BF16)


def _moba_kernel(q_ref, k_ref, v_ref, o_ref, ka_ref, va_ref, km_ref, acc_ref):
    L = MOBA_BLOCK
    S = k_ref.shape[1]
    nb = S // L
    HP = MOBA_HEADS_PER_STEP
    qi = pl.program_id(2)
    lane = lax.broadcasted_iota(jnp.int32, (L, V7X_LANES), 1)
    lanes_of = lambda ref, pp: ref[0, :, pp * V7X_LANES:(pp + 1) * V7X_LANES]

    @pl.when(qi == 0)
    def _():
        lane_s = lax.broadcasted_iota(jnp.int32, (S, V7X_LANES), 1)
        blk = lax.broadcasted_iota(jnp.int32, (S, V7X_LANES), 0) // L
        for pp in range(HP // 2):
            k = lanes_of(k_ref, pp)
            (ka_ref[2 * pp], ka_ref[2 * pp + 1], va_ref[2 * pp], va_ref[2 * pp + 1]) = _key_value_tiles(
                k, lanes_of(v_ref, pp), blk, lane_s)
            km_ref[pp] = jnp.concatenate(_split_bf16(jnp.mean(k.reshape(nb, L, V7X_LANES), axis=1)), axis=0)

    row = lax.broadcasted_iota(jnp.int32, (L, L), 0)
    col = lax.broadcasted_iota(jnp.int32, (L, L), 1)
    causal = col <= row
    jrow = lax.broadcasted_iota(jnp.int32, (nb, L), 0)
    past = jrow < qi
    qas = []
    for pp in range(HP // 2):
        q = lanes_of(q_ref, pp) * (HEAD_DIM ** -0.5)
        q_hi, q_lo = _split_bf16(q)
        km = km_ref[pp]
        for e in range(2):
            in_head = (lane >= e * HEAD_DIM) & (lane < (e + 1) * HEAD_DIM)
            zero = jnp.zeros_like(q_hi)
            g1 = lax.dot_general(km, jnp.where(in_head, q_hi, zero), _NT, preferred_element_type=F32)
            g2 = lax.dot_general(km, jnp.where(in_head, q_lo, zero), _NT, preferred_element_type=F32)
            gate = g1[0:nb] + g1[nb:2 * nb] + g2[0:nb]
            keep = jrow == qi
            for n in range(nb):
                g_n = gate[n:n + 1, :]
                beats = (gate > g_n) | ((gate == g_n) & (jrow < n))
                rank = jnp.sum(jnp.where(past & beats, 1.0, 0.0), axis=0, keepdims=True)
                keep = keep | ((jrow == n) & (rank < MOBA_TOPK) & past)
            qas.append(_augment_q(q * LOG2E, in_head, keep, odd=e == 1))

    tiles = lambda start: ([ka_ref[h, pl.ds(start, L), :] for h in range(HP)],
                           [va_ref[h, pl.ds(start, L), :] for h in range(HP)])

    m, acc = _flash_steps(qas, *tiles(pl.multiple_of(qi * L, L)), [causal] * HP,
                          [jnp.full((L, 1), NEG_BIG, F32)] * HP, [jnp.zeros((L, V7X_LANES), F32)] * HP)
    for h in range(HP):
        acc_ref[h] = acc[h]

    def body(n, carry):
        m2, acc2 = _flash_steps(qas, *tiles(pl.multiple_of(n * L, L)), [None] * HP, list(carry),
                                [acc_ref[h] for h in range(HP)])
        for h in range(HP):
            acc_ref[h] = acc2[h]
        return tuple(m2)

    lax.fori_loop(0, qi, body, tuple(m))
    for pp in range(HP // 2):
        o_ref[0, :, pp * V7X_LANES:(pp + 1) * V7X_LANES] = jnp.where(
            lane < HEAD_DIM, _normalise(acc_ref[2 * pp], lane, False), _normalise(acc_ref[2 * pp + 1], lane, True)
        ).astype(o_ref.dtype)


def moba_attention(z, *, n_heads=16):
    B, S, _ = z.shape
    L, HP = MOBA_BLOCK, MOBA_HEADS_PER_STEP
    W = HP * HEAD_DIM
    nb = S // L
    assert S % L == 0 and n_heads % HP == 0 and W % V7X_LANES == 0 and 2 * nb <= 16
    ngrp = n_heads // HP
    return pl.pallas_call(
        _moba_kernel,
        grid=(B, ngrp, S // L),
        in_specs=[
            pl.BlockSpec((1, L, W), lambda b, p, i: (b, i, p)),
            pl.BlockSpec((1, S, W), lambda b, p, i: (b, 0, ngrp + p)),
            pl.BlockSpec((1, S, W), lambda b, p, i: (b, 0, 2 * ngrp + p)),
        ],
        out_specs=pl.BlockSpec((1, L, W), lambda b, p, i: (b, i, p)),
        out_shape=jax.ShapeDtypeStruct((B, S, n_heads * HEAD_DIM), BF16),
        scratch_shapes=[
            pltpu.VMEM((HP, S, V7X_LANES), BF16),
            pltpu.VMEM((HP, S, V7X_LANES), BF16),
            pltpu.VMEM((HP // 2, 2 * nb, V7X_LANES), BF16),
            pltpu.VMEM((HP, L, V7X_LANES), F32),
        ],
        compiler_params=_params("parallel", "parallel", "arbitrary"),
        name="moba_attention",
    )(z, z, z)


NSA_TQ = 256
NSA_GROUPS_PER_STEP = 2
BIG = 3.0e38


def _gelu_tanh(x):
    return 0.5 * x * (1.0 + jnp.tanh(0.7978845608028654 * (x + 0.044715 * x * x * x)))


def _nsa_compress_kernel(xk0_ref, xk1_ref, xv0_ref, xv1_ref, pek_ref, pev_ref, w1k_ref, w1v_ref, w2k_ref, w2v_ref,
                         o1_ref, o2_ref):
    G, Lc, st = NSA_KV_GROUPS, NSA_CMP_BLOCK, NSA_CMP_STRIDE
    nrow = xk0_ref.shape[1] // st
    lane = lax.broadcasted_iota(jnp.int32, (nrow, G * HEAD_DIM), 1)

    def hidden(x_refs, pe_ref, w1_ref):
        acc = [jnp.zeros((G * nrow, V7X_LANES), F32) for _ in range(Lc // st)]
        for l in range(Lc):
            u, m = divmod(l, st)
            x = jnp.concatenate([r[0, pl.ds(m, nrow, stride=st), :] for r in x_refs], axis=1) + pe_ref[l:l + 1, :]
            xs = jnp.concatenate(
                [jnp.where((lane >= g * HEAD_DIM) & (lane < (g + 1) * HEAD_DIM), x, 0.0) for g in range(G)],
                axis=0).astype(BF16)
            acc[u] = acc[u] + jnp.dot(xs, w1_ref[l], preferred_element_type=F32)
        nxt = jnp.concatenate([pltpu.roll(acc[1][g * nrow:(g + 1) * nrow], nrow - 1, axis=0) for g in range(G)],
                              axis=0)
        return _gelu_tanh(acc[0] + nxt).astype(BF16)

    hk = hidden((xk0_ref, xk1_ref), pek_ref, w1k_ref)
    hv = hidden((xv0_ref, xv1_ref), pev_ref, w1v_ref)
    kc = jnp.dot(hk, w2k_ref[...], preferred_element_type=F32)
    vc = jnp.dot(hv, w2v_ref[...], preferred_element_type=F32)
    kv = kc + vc
    vk = pltpu.roll(kv, HEAD_DIM, axis=1)
    for g in range(G):
        o1_ref[0, :, g * V7X_LANES:(g + 1) * V7X_LANES] = kv[g * nrow:(g + 1) * nrow]
        o2_ref[0, :, g * V7X_LANES:(g + 1) * V7X_LANES] = vk[g * nrow:(g + 1) * nrow]


def nsa_compress(z, col_k, col_v, pe_k, w1_k, w2_k, pe_v, w1_v, w2_v):
    B, S, _ = z.shape
    G, Lc, st = NSA_KV_GROUPS, NSA_CMP_BLOCK, NSA_CMP_STRIDE
    GW = G * HEAD_DIM
    nrow = S // st
    hid = w1_k.shape[1]
    assert hid == V7X_LANES and col_k % GW == 0 and col_v % GW == 0
    tile_pe = lambda pe: jnp.tile(pe, (1, G))
    tile_w1 = lambda w: jnp.tile(w.reshape(Lc, 1, HEAD_DIM, hid), (1, G, 1, 1)).reshape(Lc, GW, hid).astype(BF16)
    w2k = jnp.pad(w2_k, ((0, 0), (0, HEAD_DIM))).astype(BF16)
    w2v = jnp.pad(w2_v, ((0, 0), (HEAD_DIM, 0))).astype(BF16)
    const = lambda shape: pl.BlockSpec(shape, lambda b: (0,) * len(shape))
    out = jax.ShapeDtypeStruct((B, nrow, G * V7X_LANES), F32)
    return pl.pallas_call(
        _nsa_compress_kernel,
        grid=(B,),
        in_specs=[
            pl.BlockSpec((1, S, V7X_LANES), lambda b: (b, 0, col_k // V7X_LANES)),
            pl.BlockSpec((1, S, V7X_LANES), lambda b: (b, 0, col_k // V7X_LANES + 1)),
            pl.BlockSpec((1, S, V7X_LANES), lambda b: (b, 0, col_v // V7X_LANES)),
            pl.BlockSpec((1, S, V7X_LANES), lambda b: (b, 0, col_v // V7X_LANES + 1)),
            const((Lc, GW)), const((Lc, GW)),
            const((Lc, GW, hid)), const((Lc, GW, hid)),
            const((hid, V7X_LANES)), const((hid, V7X_LANES)),
        ],
        out_specs=[pl.BlockSpec((1, nrow, G * V7X_LANES), lambda b: (b, 0, 0))] * 2,
        out_shape=[out, out],
        compiler_params=_params("parallel"),
        name="nsa_compress",
    )(z, z, z, z, tile_pe(pe_k), tile_pe(pe_v), tile_w1(w1_k), tile_w1(w1_v), w2k, w2v)


def _nsa_kernel(q_ref, c1_ref, c2_ref, s_ref, w_ref, g_ref, ovt_ref, o_ref, sk_ref, sv_ref, wk_ref, wv_ref, acc_ref):
    TQ = NSA_TQ
    S = s_ref.shape[1]
    R = NSA_HEADS // NSA_KV_GROUPS
    NG = NSA_GROUPS_PER_STEP
    NH = NG * R
    grp0 = pl.program_id(1) * NG
    qi = pl.program_id(2)
    lane = lax.broadcasted_iota(jnp.int32, (TQ, V7X_LANES), 1)
    tile_of = lambda ref, t: ref[0, :, t * V7X_LANES:(t + 1) * V7X_LANES]

    @pl.when(qi == 0)
    def _():
        lane_s = lax.broadcasted_iota(jnp.int32, (S, V7X_LANES), 1)
        blk = lax.broadcasted_iota(jnp.int32, (S, V7X_LANES), 0) // NSA_SLC_BLOCK
        for gg in range(NG):
            for src, k_ref, v_ref, hot in ((s_ref, sk_ref, sv_ref, blk), (w_ref, wk_ref, wv_ref, None)):
                kv = tile_of(src, gg)
                vk = pltpu.roll(kv, HEAD_DIM, axis=1)
                k_ref[2 * gg], k_ref[2 * gg + 1], v_ref[2 * gg], v_ref[2 * gg + 1] = _key_value_tiles(
                    jnp.where(lane_s < HEAD_DIM, kv, vk), jnp.where(lane_s < HEAD_DIM, vk, kv), hot, lane_s)

    q0 = pl.multiple_of(qi * TQ, TQ)
    row = lax.broadcasted_iota(jnp.int32, (TQ, TQ), 0)
    col = lax.broadcasted_iota(jnp.int32, (TQ, TQ), 1)
    causal = col <= row
    t_abs = q0 + lax.broadcasted_iota(jnp.int32, (TQ, V7X_LANES), 0)
    even_lanes = lane < HEAD_DIM

    cmask = lane * NSA_CMP_STRIDE + (NSA_CMP_BLOCK - 1) <= t_abs
    nblk = s_ref.shape[1] // NSA_SLC_BLOCK
    jrow = lax.broadcasted_iota(jnp.int32, (nblk, TQ), 0)
    own = (q0 + lax.broadcasted_iota(jnp.int32, (nblk, TQ), 1)) // NSA_SLC_BLOCK
    ovt = ovt_ref[...]
    head_lanes = [even_lanes if h % 2 == 0 else ~even_lanes for h in range(NH)]
    tiles = [tile_of(q_ref, h // 2) * (HEAD_DIM ** -0.5) for h in range(NH)]
    o_cmp, qas = [], []
    for gg in range(NG):
        c_kv_b, c_vk_b = tile_of(c1_ref, gg).astype(BF16), tile_of(c2_ref, gg).astype(BF16)
        p_sum = jnp.zeros((TQ, V7X_LANES), F32)
        for h in range(gg * R, (gg + 1) * R):
            qm = jnp.where(head_lanes[h], tiles[h], 0.0).astype(BF16)
            s = lax.dot_general(qm, c_kv_b if h % 2 == 0 else c_vk_b, _NT, preferred_element_type=F32)
            s = jnp.where(cmask, s, NEG_BIG)
            p = jnp.where(cmask, jnp.exp(s - jnp.max(s, axis=1, keepdims=True)), 0.0)
            den = jnp.sum(p, axis=1, keepdims=True)
            p = p / jnp.where(den > 0.0, den, 1.0)
            p_sum = p_sum + p
            o_cmp.append(jnp.dot(p.astype(BF16), c_vk_b if h % 2 == 0 else c_kv_b, preferred_element_type=F32))

        p_hi, p_lo = _split_bf16(p_sum)
        p_slc = (lax.dot_general(ovt, p_hi, _NT, preferred_element_type=F32)
                 + lax.dot_general(ovt, p_lo, _NT, preferred_element_type=F32))[0:nblk]
        score = jnp.where((jrow == own) | (jrow == 0), BIG, jnp.where(jrow > own, -BIG, p_slc))
        keep = jrow > nblk
        for j in range(nblk):
            s_j = score[j:j + 1, :]
            beats = (score > s_j) | ((score == s_j) & (jrow < j))
            rank = jnp.sum(jnp.where(beats, 1.0, 0.0), axis=0, keepdims=True)
            keep = keep | ((jrow == j) & (rank < NSA_SLC_TOPN) & (jrow <= own))
        qas += [_augment_q(tiles[h] * LOG2E, head_lanes[h], keep, odd=h % 2 == 1) for h in range(gg * R, (gg + 1) * R)]

    neg = [jnp.full((TQ, 1), NEG_BIG, F32)] * NH
    zacc = [jnp.zeros((TQ, V7X_LANES), F32)] * NH
    kv_index = [2 * (h // R) + h % 2 for h in range(NH)]

    def kv_blocks(k_ref, v_ref, start):
        return ([k_ref[kv_index[h], pl.ds(start, TQ), :] for h in range(NH)],
                [v_ref[kv_index[h], pl.ds(start, TQ), :] for h in range(NH)])

    m, acc = _flash_steps(qas, *kv_blocks(sk_ref, sv_ref, q0), [causal] * NH, neg, zacc)
    for h in range(NH):
        acc_ref[h] = acc[h]

    def body(kb, carry):
        m2, acc2 = _flash_steps(qas, *kv_blocks(sk_ref, sv_ref, pl.multiple_of(kb * TQ, TQ)), [None] * NH,
                                list(carry), [acc_ref[h] for h in range(NH)])
        for h in range(NH):
            acc_ref[h] = acc2[h]
        return tuple(m2)

    lax.fori_loop(0, qi, body, tuple(m))

    WK = NSA_WINDOW + TQ
    w0 = pl.multiple_of(jnp.maximum(qi - NSA_WINDOW // TQ, 0) * TQ, TQ)
    key_pos = w0 + lax.broadcasted_iota(jnp.int32, (TQ, WK), 1)
    t_win = q0 + lax.broadcasted_iota(jnp.int32, (TQ, WK), 0)
    in_window = (key_pos <= t_win) & (key_pos > t_win - NSA_WINDOW)
    _, acc = _flash_steps(qas, [wk_ref[kv_index[h], pl.ds(w0, WK), :] for h in range(NH)],
                          [wv_ref[kv_index[h], pl.ds(w0, WK), :] for h in range(NH)], [in_window] * NH, neg, zacc)

    gates = jax.nn.sigmoid(g_ref[0])
    outs = []
    for h in range(NH):
        o_slc = _normalise(acc_ref[h], lane, h % 2 == 1)
        o_win = _normalise(acc[h], lane, h % 2 == 1)
        c0 = (grp0 * R + h) * 3
        gate = lambda c: jnp.sum(jnp.where(lane == c, gates, 0.0), axis=1, keepdims=True)
        outs.append(gate(c0) * o_cmp[h] + gate(c0 + 1) * o_slc + gate(c0 + 2) * o_win)
    for p2 in range(NH // 2):
        o_ref[0, :, p2 * V7X_LANES:(p2 + 1) * V7X_LANES] = jnp.where(
            even_lanes, outs[2 * p2], outs[2 * p2 + 1]).astype(o_ref.dtype)


def nsa_attention(z, cmp_kv, cmp_vk, col_q, col_slc, col_win, col_gate):
    B, S, _ = z.shape
    G, TQ, NG = NSA_KV_GROUPS, NSA_TQ, NSA_GROUPS_PER_STEP
    R = NSA_HEADS // G
    QW = NG * R * HEAD_DIM
    KW = NG * V7X_LANES
    ncmp = cmp_kv.shape[1]
    assert S % TQ == 0 and ncmp == V7X_LANES and S // NSA_SLC_BLOCK <= V7X_LANES
    assert NSA_WINDOW % TQ == 0 and S >= NSA_WINDOW + TQ
    assert G % NG == 0 and R % 2 == 0
    assert col_q % QW == 0 and col_slc % KW == 0 and col_win % KW == 0 and col_gate % V7X_LANES == 0
    nc = (S - NSA_CMP_BLOCK) // NSA_CMP_STRIDE + 1
    c_start = np.arange(V7X_LANES) * NSA_CMP_STRIDE
    s_start = np.arange(V7X_LANES) * NSA_SLC_BLOCK
    overlap = ((c_start[:, None] <= s_start[None, :] + NSA_SLC_BLOCK - 1)
               & (c_start[:, None] + NSA_CMP_BLOCK - 1 >= s_start[None, :])
               & (np.arange(V7X_LANES)[:, None] < nc) & (np.arange(V7X_LANES)[None, :] < S // NSA_SLC_BLOCK))
    const = lambda shape: pl.BlockSpec(shape, lambda b, g, i: (0,) * len(shape))
    return pl.pallas_call(
        _nsa_kernel,
        grid=(B, G // NG, S // TQ),
        in_specs=[
            pl.BlockSpec((1, TQ, QW), lambda b, g, i: (b, i, col_q // QW + g)),
            pl.BlockSpec((1, ncmp, KW), lambda b, g, i: (b, 0, g)),
            pl.BlockSpec((1, ncmp, KW), lambda b, g, i: (b, 0, g)),
            pl.BlockSpec((1, S, KW), lambda b, g, i: (b, 0, col_slc // KW + g)),
            pl.BlockSpec((1, S, KW), lambda b, g, i: (b, 0, col_win // KW + g)),
            pl.BlockSpec((1, TQ, V7X_LANES), lambda b, g, i: (b, i, col_gate // V7X_LANES)),
            const((V7X_LANES, V7X_LANES)),
        ],
        out_specs=pl.BlockSpec((1, TQ, QW), lambda b, g, i: (b, i, g)),
        out_shape=jax.ShapeDtypeStruct((B, S, NSA_HEADS * HEAD_DIM), BF16),
        scratch_shapes=[pltpu.VMEM((2 * NG, S, V7X_LANES), BF16)] * 4 + [
            pltpu.VMEM((NG * R, TQ, V7X_LANES), F32),
        ],
        compiler_params=_params("parallel", "parallel", "arbitrary"),
        name="nsa_attention",
    )(z, cmp_kv, cmp_vk, z, z, z, jnp.asarray(overlap.T, BF16))


RWKV_CHUNK = 64
RWKV_ROWS = 512
RWKV_INTERLEAVE = 8


def _mm(a, b, dims=None):
    dims = dims or (((1,), (0,)), ((), ()))
    return lax.dot_general(a.astype(BF16), b.astype(BF16), dims, preferred_element_type=F32)


def _mm3(a, b):
    (a_hi, a_lo), (b_hi, b_lo) = _split_bf16(a), _split_bf16(b)
    return _mm(a_hi, b_hi) + (_mm(a_hi, b_lo) + _mm(a_lo, b_hi))


def _mm_onehot(a01, b):
    hi = b.astype(BF16)
    mid, lo = _split_bf16(b - hi.astype(F32))
    return _mm(a01, hi) + (_mm(a01, mid) + _mm(a01, lo))


def _head_sum(x, low):
    s0 = jnp.sum(jnp.where(low, x, 0.0), axis=1, keepdims=True)
    s1 = jnp.sum(jnp.where(low, 0.0, x), axis=1, keepdims=True)
    return jnp.where(low, s0, s1)


def _rwkv_kernel(r_ref, k_ref, v_ref, lo_ref, glo_ref, pp_ref, pl_ref, wup_ref, aup_ref, gup_ref, o_ref,
                 rs, ws, ks, vs, als, bes, gs, ys, bon, hs, rqs, ms, ns):
    S = r_ref.shape[1]
    C, RB = RWKV_CHUNK, RWKV_ROWS
    pp = pp_ref[...]
    mu_r, mu_k, mu_v, w0, a0, k_k, k_a, r_k, ln_g, ln_b = [pp[i:i + 1, :] for i in range(10)]
    mu_lo, mu_g = pl_ref[0:1, :], pl_ref[1:2, :]
    heads = lax.broadcasted_iota(jnp.int32, (RB, V7X_LANES), 1) < HEAD_DIM
    first = lax.broadcasted_iota(jnp.int32, (RB, V7X_LANES), 0) == 0

    def prologue(i):
        t0 = i * RB

        def shifted(ref, mu):
            x = ref[0, pl.ds(t0, RB), :]
            last = ref[0, pl.ds(t0 - 1, 1), :] if i > 0 else jnp.zeros((1, V7X_LANES), F32)
            prev = jnp.where(first, last, pltpu.roll(x, 1, axis=0))
            return x + (prev - x) * mu

        r, k, v = shifted(r_ref, mu_r), shifted(k_ref, mu_k), shifted(v_ref, mu_v)
        lo, glo = shifted(lo_ref, mu_lo), shifted(glo_ref, mu_g)
        wp = -(w0 + _mm(jnp.tanh(lo), wup_ref[...]))
        w = -(jnp.maximum(wp, 0.0) + jnp.log(1.0 + jnp.exp(-jnp.abs(wp)))) - 0.5
        a = jax.nn.sigmoid(a0 + _mm(lo, aup_ref[...]))
        kk = k * k_k
        kk = kk * lax.rsqrt(jnp.maximum(_head_sum(kk * kk, heads), 1e-24))
        k2 = k * (1.0 + (a - 1.0) * k_a)
        rs[pl.ds(t0, RB), :] = r
        ws[pl.ds(t0, RB), :] = -jnp.exp(w)
        ks[pl.ds(t0, RB), :] = k2
        vs[pl.ds(t0, RB), :] = v
        als[pl.ds(t0, RB), :] = -kk
        bes[pl.ds(t0, RB), :] = kk * a
        gs[pl.ds(t0, RB), :] = _mm(jax.nn.sigmoid(glo), gup_ref[...])
        bon[pl.ds(t0, RB), :] = _head_sum(r * k2 * r_k, heads) * v

    W2 = 2 * C
    row = lax.broadcasted_iota(jnp.int32, (W2, W2), 0)
    col = lax.broadcasted_iota(jnp.int32, (W2, W2), 1)
    t_idx, s_idx = row % C, col % C
    top, left = row < C, col < C
    same = top == left
    eye = jnp.where(row == col, 1.0, 0.0)
    tri = jnp.where(lax.broadcasted_iota(jnp.int32, (C, C), 1) <= lax.broadcasted_iota(jnp.int32, (C, C), 0), 1.0, 0.0)
    low_c = lax.broadcasted_iota(jnp.int32, (C, V7X_LANES), 1) < HEAD_DIM
    fold = lambda x: x[0:C] + x[C:W2]
    stack_heads = lambda x: jnp.concatenate([jnp.where(low_c, x, 0.0), jnp.where(low_c, 0.0, x)], axis=0)
    block_diag = lambda x: jnp.where(top, jnp.where(left, x, 0.0), jnp.where(left, 0.0, pltpu.roll(x, C, axis=1)))

    rows = lambda c: pl.ds(c * C if isinstance(c, int) else pl.multiple_of(c * C, C), C)

    def advance(c, H):
        ys[rows(c), :] += _mm3(rqs[c], H)
        return _mm3(ms[c], H) + ns[c]

    def transfers(i, lagged, side):
        each = lambda f, *xs: [f(*a) for a in zip(*xs)]
        cs = [i * RWKV_INTERLEAVE + u for u in range(RWKV_INTERLEAVE)]
        sls = [rows(c) for c in cs]
        state = [hs[...]] if lagged else None

        def lag(hook):
            if lagged:
                for u in range(hook * RWKV_INTERLEAVE // 8, (hook + 1) * RWKV_INTERLEAVE // 8):
                    state[0] = advance(cs[u] - RWKV_INTERLEAVE, state[0])
            if hook in side:
                side[hook]()

        r, lw, k2, v, al, be = ([ref[sl, :] for sl in sls] for ref in (rs, ws, ks, vs, als, bes))
        logp = each(lambda x: _mm_onehot(tri, x), lw)
        lag(0)
        P = each(jnp.exp, logp)
        Pinv = each(lambda x: jnp.exp(-x), logp)
        At = each(lambda a_, lp, w_: a_ * jnp.exp(lp - w_), al, logp, lw)
        Rt, Bt, Kt = each(jnp.multiply, r, P), each(jnp.multiply, be, Pinv), each(jnp.multiply, k2, Pinv)
        PC = each(lambda p: p[C - 1:C, :], P)
        A_bd, R_bd = each(stack_heads, At), each(stack_heads, Rt)
        Yt = each(lambda b, k: jnp.concatenate([b, k], axis=0), Bt, Kt)
        A1 = each(lambda a, y: jnp.where(s_idx < t_idx, _mm(a, y, dims=_NT), 0.0), A_bd, Yt)
        A2 = each(lambda a, y: jnp.where(s_idx <= t_idx, _mm(a, y, dims=_NT), 0.0), R_bd, Yt)
        X, Arb = each(block_diag, A1), each(block_diag, A2)
        T = each(lambda x: eye + x, X)
        for it in range(5):
            X = each(lambda x: _mm(x, x), X)
            T = each(lambda t, x: t + _mm(t, x), T, X)
            lag(1 + it)
        V0 = each(lambda x: jnp.concatenate([jnp.zeros_like(x), x], axis=0), v)
        TA = each(_mm, T, A_bd)
        AkV = each(lambda a, x: jnp.where(same, _mm(a, x), 0.0), A1, V0)
        lag(6)
        U0 = each(_mm, T, AkV)
        lag(7)
        AR = each(lambda a, t, u: _mm(a, jnp.concatenate([t, u], axis=1)), Arb, TA, U0)
        ArkV = each(lambda a, x: jnp.where(same, _mm(a, x), 0.0), A2, V0)
        Mx = each(lambda b, p, t: _mm((b * p).T, fold(t)), Bt, PC, TA)
        Nx = each(lambda b, k, p, u, x: _mm(jnp.concatenate([b * p, k * p], axis=0).T,
                                            jnp.concatenate([fold(u), x], axis=0)), Bt, Kt, PC, U0, v)
        for u in range(RWKV_INTERLEAVE):
            ys[sls[u], :] = fold(AR[u][:, W2:2 * W2] + ArkV[u])
            rqs[cs[u]] = Rt[u] + fold(AR[u][:, 0:W2])
            ms[cs[u]] = eye * PC[u] + jnp.where(same, Mx[u], 0.0)
            ns[cs[u]] = jnp.where(same, Nx[u], 0.0)
        if lagged:
            hs[...] = state[0]

    def epilogue(i):
        sl = pl.ds(i * RB, RB)
        y = ys[sl, :]
        d = y - _head_sum(y, heads) * (1.0 / HEAD_DIM)
        var = _head_sum(d * d, heads) * (1.0 / HEAD_DIM)
        yn = d * lax.rsqrt(var + RWKV_GN_EPS) * ln_g + ln_b
        o_ref[0, sl, :] = ((yn + bon[sl, :]) * gs[sl, :]).astype(o_ref.dtype)

    n_groups = S // RB
    assert 8 % RWKV_INTERLEAVE == 0 and RB == RWKV_INTERLEAVE * C
    hs[...] = jnp.zeros((W2, W2), F32)
    prologue(0)
    for i in range(n_groups):
        side = {}
        if i + 1 < n_groups:
            side[1] = functools.partial(prologue, i + 1)
        if i >= 2:
            side[4] = functools.partial(epilogue, i - 2)
        transfers(i, i > 0, side)
    H = hs[...]
    for u in range(RWKV_INTERLEAVE):
        H = advance((n_groups - 1) * RWKV_INTERLEAVE + u, H)
        if u == RWKV_INTERLEAVE // 2 and n_groups >= 2:
            epilogue(n_groups - 2)
    epilogue(n_groups - 1)


def rwkv7_mixer(z, shift_mu, w0, w_up, a0, a_up, g_up, k_k, k_a, r_k, ln_g, ln_b):
    B, S, _ = z.shape
    CW = RWKV_HEADS * HEAD_DIM
    npair = CW // V7X_LANES
    base = 3 * CW // V7X_LANES
    lora = w_up.shape[0] + a_up.shape[0]
    assert lora == V7X_LANES and g_up.shape[0] == V7X_LANES and S % RWKV_ROWS == 0
    pp = jnp.stack([shift_mu[0:CW], shift_mu[CW:2 * CW], shift_mu[2 * CW:3 * CW], w0, a0, k_k, k_a,
                    r_k.reshape(CW), ln_g, ln_b])
    pp = jnp.pad(pp, ((0, 16 - pp.shape[0]), (0, 0)))
    pl2 = jnp.pad(shift_mu[3 * CW:].reshape(2, V7X_LANES), ((0, 6), (0, 0)))
    wup = jnp.pad(w_up, ((0, a_up.shape[0]), (0, 0)))
    aup = jnp.pad(a_up, ((w_up.shape[0], 0), (0, 0)))
    tile = lambda off: pl.BlockSpec((1, S, V7X_LANES), lambda b, p: (b, 0, base + off * npair + p))
    fixed = lambda off: pl.BlockSpec((1, S, V7X_LANES), lambda b, p: (b, 0, base + 3 * npair + off))
    seq = pltpu.VMEM((S, V7X_LANES), F32)
    return pl.pallas_call(
        _rwkv_kernel,
        grid=(B, npair),
        in_specs=[
            tile(0), tile(1), tile(2), fixed(0), fixed(1),
            pl.BlockSpec((16, V7X_LANES), lambda b, p: (0, p)),
            pl.BlockSpec((8, V7X_LANES), lambda b, p: (0, 0)),
            pl.BlockSpec((V7X_LANES, V7X_LANES), lambda b, p: (0, p)),
            pl.BlockSpec((V7X_LANES, V7X_LANES), lambda b, p: (0, p)),
            pl.BlockSpec((V7X_LANES, V7X_LANES), lambda b, p: (0, p)),
        ],
        out_specs=pl.BlockSpec((1, S, V7X_LANES), lambda b, p: (b, 0, p)),
        out_shape=jax.ShapeDtypeStruct((B, S, CW), BF16),
        scratch_shapes=[seq] * 9 + [
            pltpu.VMEM((V7X_LANES, V7X_LANES), F32),
            pltpu.VMEM((S // RWKV_CHUNK, RWKV_CHUNK, V7X_LANES), F32),
            pltpu.VMEM((S // RWKV_CHUNK, V7X_LANES, V7X_LANES), F32),
            pltpu.VMEM((S // RWKV_CHUNK, V7X_LANES, V7X_LANES), F32),
        ],
        compiler_params=_params("parallel", "parallel"),
        name="rwkv7_mixer",
    )(z, z, z, z, z, pp, pl2, wup, aup, g_up.astype(BF16))


RET_CHUNKS_PER_STEP = 4


def _ret_kernel(q_ref, k_ref, v_ref, g_ref, cos_ref, sin_ref, din_ref, dq_ref, dk_ref, dc_ref, o_ref, st_ref):
    S = q_ref.shape[1]
    C, DV = RET_CHUNK, RET_V_DIM
    lane = lax.broadcasted_iota(jnp.int32, (C, V7X_LANES), 1)
    first_half = (lane % RET_QK_DIM) < RET_QK_DIM // 2
    st_ref[...] = jnp.zeros_like(st_ref)

    in_head = [(lane >= h * RET_QK_DIM) & (lane < (h + 1) * RET_QK_DIM) for h in range(2)]
    NCH = RET_CHUNKS_PER_STEP
    units = [(u, h) for u in range(NCH) for h in range(2)]

    def step(i, carry):
        sls = [pl.ds(pl.multiple_of((i * NCH + u) * C, C), C) for u in range(NCH)]

        def rot(z, sl):
            swapped = jnp.where(first_half, pltpu.roll(z, V7X_LANES - RET_QK_DIM // 2, axis=1),
                                pltpu.roll(z, RET_QK_DIM // 2, axis=1))
            return z * cos_ref[sl, :] + swapped * sin_ref[sl, :]

        q = [rot(q_ref[0, sl, :], sl) for sl in sls]
        k = [rot(k_ref[0, sl, :], sl) * (RET_QK_DIM ** -0.5) for sl in sls]
        qm = [jnp.where(in_head[h], q[u], 0.0) for u, h in units]
        v = [v_ref[0, sls[u], h * DV:(h + 1) * DV] for u, h in units]
        inner = [_mm(qm[n], k[u], dims=_NT) * din_ref[h] for n, (u, h) in enumerate(units)]
        upd = [_mm((jnp.where(in_head[h], k[u], 0.0) * dk_ref[h]).T, v[n]) for n, (u, h) in enumerate(units)]
        local = [_mm(inner[n], v[n]) for n in range(len(units))]
        st = [st_ref[h] for h in range(2)]
        for n, (u, h) in enumerate(units):
            o = local[n] + _mm(qm[n], st[h]) * dq_ref[h]
            st[h] = upd[n] + dc_ref[h, 0:1, :] * st[h]
            d = o - jnp.mean(o, axis=1, keepdims=True)
            on = d * lax.rsqrt(jnp.mean(d * d, axis=1, keepdims=True) + RET_GN_EPS)
            gate = g_ref[0, sls[u], h * DV:(h + 1) * DV]
            o_ref[0, sls[u], h * DV:(h + 1) * DV] = (gate * jax.nn.sigmoid(gate) * on).astype(o_ref.dtype)
        st_ref[0], st_ref[1] = st
        return carry

    lax.fori_loop(0, S // C // NCH, step, 0)


def retention_mixer(z):
    B, S, _ = z.shape
    H, C, DK, DV = RET_HEADS, RET_CHUNK, RET_QK_DIM, RET_V_DIM
    assert S % C == 0 and 2 * DK == V7X_LANES and DV == V7X_LANES
    npair = H // 2
    half = DK // 2
    inv = ROPE_BASE ** (-jnp.arange(half, dtype=F32) / half)
    ang = jnp.arange(S, dtype=F32)[:, None] * inv
    cos = jnp.tile(jnp.cos(ang), (1, 4))
    sin = jnp.tile(jnp.concatenate([-jnp.sin(ang), jnp.sin(ang)], axis=1), (1, 2))
    log_g = jnp.asarray(np.log(1.0 - 2.0 ** (-5.0 - np.arange(H))), F32)
    n = jnp.arange(C, dtype=F32)
    diff = n[:, None] - n[None, :]
    d_in = jnp.where(diff >= 0, jnp.exp(jnp.maximum(diff, 0.0) * log_g[:, None, None]), 0.0)
    lanes = lambda t: jnp.broadcast_to(t[..., None], t.shape + (V7X_LANES,))
    d_q = lanes(jnp.exp((n + 1.0) * log_g[:, None]))
    d_k = lanes(jnp.exp((C - 1.0 - n) * log_g[:, None]))
    d_c = lanes(jnp.broadcast_to(jnp.exp(C * log_g)[:, None], (H, 8)))
    qk_tiles = H * DK // V7X_LANES
    return pl.pallas_call(
        _ret_kernel,
        grid=(B, npair),
        in_specs=[
            pl.BlockSpec((1, S, V7X_LANES), lambda b, p: (b, 0, p)),
            pl.BlockSpec((1, S, V7X_LANES), lambda b, p: (b, 0, qk_tiles + p)),
            pl.BlockSpec((1, S, 2 * DV), lambda b, p: (b, 0, 2 * qk_tiles * V7X_LANES // (2 * DV) + p)),
            pl.BlockSpec((1, S, 2 * DV), lambda b, p: (b, 0, (2 * qk_tiles * V7X_LANES + H * DV) // (2 * DV) + p)),
            pl.BlockSpec((S, V7X_LANES), lambda b, p: (0, 0)),
            pl.BlockSpec((S, V7X_LANES), lambda b, p: (0, 0)),
            pl.BlockSpec((2, C, C), lambda b, p: (p, 0, 0)),
            pl.BlockSpec((2, C, V7X_LANES), lambda b, p: (p, 0, 0)),
            pl.BlockSpec((2, C, V7X_LANES), lambda b, p: (p, 0, 0)),
            pl.BlockSpec((2, 8, V7X_LANES), lambda b, p: (p, 0, 0)),
        ],
        out_specs=pl.BlockSpec((1, S, 2 * DV), lambda b, p: (b, 0, p)),
        out_shape=jax.ShapeDtypeStruct((B, S, H * DV), BF16),
        scratch_shapes=[pltpu.VMEM((2, V7X_LANES, DV), F32)],
        compiler_params=_params("parallel", "parallel"),
        name="retention_mixer",
    )(z, z, z, z, cos, sin, d_in, d_q, d_k, d_c)


def _even_mixer(x, g_norm, w_in, shift_mu, w0, w_up, a0, a_up, g_up, k_k, k_a, r_k, ln_g, ln_b):
    B, S, D = x.shape
    z = norm_matmul(x.reshape(B * S, D), g_norm, w_in.astype(BF16)).reshape(B, S, -1)
    o_a = moba_attention(z)
    o_b = rwkv7_mixer(z, shift_mu, w0, w_up, a0, a_up, g_up, k_k, k_a, r_k, ln_g, ln_b)
    return o_a, o_b


def _odd_mixer(x, g_norm, w_in, pe_k, w1_k, w2_k, pe_v, w1_v, w2_v):
    B, S, D = x.shape
    perm, col = _odd_layout()
    n_in = w_in.shape[1]
    real = perm[perm < n_in]
    cuts = [0] + [k for k in range(1, len(real)) if real[k] != real[k - 1] + 1] + [len(real)]
    w_b = w_in.astype(BF16)
    w_p = jnp.concatenate([w_b[:, int(real[a]):int(real[b - 1]) + 1] for a, b in zip(cuts[:-1], cuts[1:])]
                          + [jnp.zeros((w_in.shape[0], len(perm) - len(real)), BF16)], axis=1)
    z = norm_matmul(x.reshape(B * S, D), g_norm, w_p).reshape(B, S, -1)
    o_c = retention_mixer(z)
    cmp_kv, cmp_vk = nsa_compress(z, col["kc"], col["vc"], pe_k, w1_k, w2_k, pe_v, w1_v, w2_v)
    o_d = nsa_attention(z, cmp_kv, cmp_vk, col["nq"], col["slc"], col["win"], col["gate"])
    return o_c, o_d


def _odd_layout():
    G, Dh = NSA_KV_GROUPS, HEAD_DIM
    sizes = (RET_HEADS * RET_QK_DIM, RET_HEADS * RET_QK_DIM, RET_HEADS * RET_V_DIM, RET_HEADS * RET_V_DIM,
             NSA_HEADS * Dh) + (G * Dh,) * 6 + (3 * NSA_HEADS,)
    off = np.concatenate([[0], np.cumsum(sizes)])
    rq, rk, rv, rg, nq, kc, vc, ks, vs, kw, vw, ng = off[:-1]
    n_in = int(off[-1])
    pair = lambda a, b: np.concatenate([np.concatenate([a + g * Dh + np.arange(Dh), b + g * Dh + np.arange(Dh)])
                                        for g in range(G)])
    perm = np.concatenate([np.arange(ks), pair(ks, vs), pair(kw, vw), ng + np.arange(3 * NSA_HEADS)])
    n_pad = -(-len(perm) // (6 * V7X_MXU_DIM)) * 6 * V7X_MXU_DIM
    perm = np.concatenate([perm, np.full(n_pad - len(perm), n_in)]).astype(np.int32)
    col = {"nq": int(nq), "kc": int(kc), "vc": int(vc), "slc": int(ks), "win": int(ks) + 2 * G * Dh,
           "gate": int(ks) + 4 * G * Dh}
    return perm, col


def kernel(x, mix_norm, ffn_norm, even_w_in, even_shift_mu, even_w0, even_w_up, even_a0, even_a_up, even_g_up, even_k_k, even_k_a, even_r_k, even_ln_g, even_ln_b, even_w_out, odd_w_in, odd_cmp_pe_k, odd_cmp_w1_k, odd_cmp_w2_k, odd_cmp_pe_v, odd_cmp_w1_v, odd_cmp_w2_v, odd_w_out, ffn_w1, ffn_w3, ffn_w2, final_norm):
    B, S, D = x.shape
    depth = mix_norm.shape[0]
    w1, w3, w2 = ffn_w1.astype(BF16), ffn_w3.astype(BF16), ffn_w2.astype(BF16)
    for layer in range(depth):
        i = layer // 2
        if layer % 2 == 0:
            o1, o2 = _even_mixer(x, mix_norm[layer], even_w_in[i], even_shift_mu[i], even_w0[i], even_w_up[i],
                                 even_a0[i], even_a_up[i], even_g_up[i], even_k_k[i], even_k_a[i], even_r_k[i],
                                 even_ln_g[i], even_ln_b[i])
            w_out = even_w_out[i]
        else:
            o1, o2 = _odd_mixer(x, mix_norm[layer], odd_w_in[i], odd_cmp_pe_k[i], odd_cmp_w1_k[i], odd_cmp_w2_k[i],
                                odd_cmp_pe_v[i], odd_cmp_w1_v[i], odd_cmp_w2_v[i])
            w_out = odd_w_out[i]
        T = B * S
        x2 = mix_ffn_residual(o1.reshape(T, -1), o2.reshape(T, -1), w_out.astype(BF16), x.reshape(T, D),
                              ffn_norm[layer], w1, w3, w2, layer, final_norm if layer == depth - 1 else None)
        x = x2.reshape(B, S, D)
    return x
```

```python
import functools

import jax
import jax.numpy as jnp
import numpy as np
from jax import lax
from jax.experimental import pallas as pl
from jax.experimental.pallas import tpu as pltpu

F32 = jnp.float32
BF16 = jnp.bfloat16

V7X_LANES = 128
V7X_MXU_DIM = 256
V7X_VMEM_BYTES = 64 * 1024 * 1024
VMEM_LIMIT = V7X_VMEM_BYTES * 7 // 8

NORM_EPS = 1e-6
HEAD_DIM = 64

MOBA_BLOCK = 256
MOBA_TOPK = 3
RWKV_HEADS = 16
RWKV_GN_EPS = 6.4e-4

RET_HEADS = 8
RET_QK_DIM = 64
RET_V_DIM = 128
RET_CHUNK = 128
RET_GN_EPS = 1e-6
ROPE_BASE = 10000.0
NSA_HEADS = 16
NSA_KV_GROUPS = 4
NSA_CMP_BLOCK = 32
NSA_CMP_STRIDE = 16
NSA_SLC_BLOCK = 64
NSA_SLC_TOPN = 16
NSA_WINDOW = 512


def _params(*semantics):
    return pltpu.CompilerParams(dimension_semantics=semantics, vmem_limit_bytes=VMEM_LIMIT)


def _rms(x, g):
    return x * lax.rsqrt(jnp.mean(x * x, axis=-1, keepdims=True) + NORM_EPS) * g


def _norm_matmul_kernel(x_ref, g_ref, w_ref, o_ref):
    x = x_ref[...]
    scale = lax.rsqrt(jnp.mean(x * x, axis=-1, keepdims=True) + NORM_EPS)
    o_ref[...] = jnp.dot((x * g_ref[...]).astype(BF16), w_ref[...], preferred_element_type=F32) * scale


def _proj_tile(n):
    assert n % V7X_MXU_DIM == 0
    k = n // V7X_MXU_DIM
    return V7X_MXU_DIM * max(d for d in range(1, 7) if k % d == 0)


def norm_matmul(x, g, w, *, tm=512):
    T, D = x.shape
    N = w.shape[1]
    tn = _proj_tile(N)
    assert T % tm == 0 and N % tn == 0
    return pl.pallas_call(
        _norm_matmul_kernel,
        grid=(N // tn, T // tm),
        in_specs=[
            pl.BlockSpec((tm, D), lambda j, i: (i, 0)),
            pl.BlockSpec((1, D), lambda j, i: (0, 0)),
            pl.BlockSpec((D, tn), lambda j, i: (0, j)),
        ],
        out_specs=pl.BlockSpec((tm, tn), lambda j, i: (i, j)),
        out_shape=jax.ShapeDtypeStruct((T, N), F32),
        compiler_params=_params("parallel", "parallel"),
        name="norm_matmul",
    )(x, g.reshape(1, D), w)


def _mix_ffn_kernel(a_ref, b_ref, wa_ref, wb_ref, x_ref, g_ref, w1_ref, w3_ref, w2_ref, gf_ref, o_ref,
                    h_ref, acc_ref, *, final_norm):
    j = pl.program_id(1)

    @pl.when(j == 0)
    def _():
        x2 = (x_ref[...] + jnp.dot(a_ref[...], wa_ref[...], preferred_element_type=F32)
              + jnp.dot(b_ref[...], wb_ref[...], preferred_element_type=F32))
        o_ref[...] = x2
        h_ref[...] = _rms(x2, g_ref[...]).astype(BF16)
        acc_ref[...] = jnp.zeros_like(acc_ref)

    h = h_ref[...]
    a = jnp.dot(h, w1_ref[...], preferred_element_type=F32)
    b = jnp.dot(h, w3_ref[...], preferred_element_type=F32)
    act = (a * jax.nn.sigmoid(a) * b).astype(BF16)
    acc_ref[...] += jnp.dot(act, w2_ref[...], preferred_element_type=F32)

    @pl.when(j == pl.num_programs(1) - 1)
    def _():
        y = o_ref[...] + acc_ref[...]
        if final_norm:
            y = _rms(y, gf_ref[...])
        o_ref[...] = y


def mix_ffn_residual(a, b, w_out, x, g, w1, w3, w2, layer, g_final=None, *, tm=512, tf=512):
    T, D = x.shape
    K = a.shape[1]
    Fh = w1.shape[2]
    assert T % tm == 0 and Fh % tf == 0 and b.shape == a.shape and w_out.shape == (2 * K, D)
    final_norm = g_final is not None
    gf = (g_final if final_norm else g).reshape(1, D)
    once = pl.Buffered(1)
    return pl.pallas_call(
        functools.partial(_mix_ffn_kernel, final_norm=final_norm),
        grid=(T // tm, Fh // tf),
        in_specs=[
            pl.BlockSpec((tm, K), lambda i, j: (i, 0)),
            pl.BlockSpec((tm, K), lambda i, j: (i, 0)),
            pl.BlockSpec((K, D), lambda i, j: (0, 0), pipeline_mode=once),
            pl.BlockSpec((K, D), lambda i, j: (1, 0), pipeline_mode=once),
            pl.BlockSpec((tm, D), lambda i, j: (i, 0)),
            pl.BlockSpec((1, D), lambda i, j: (0, 0), pipeline_mode=once),
            pl.BlockSpec((None, D, tf), lambda i, j: (layer, 0, j)),
            pl.BlockSpec((None, D, tf), lambda i, j: (layer, 0, j)),
            pl.BlockSpec((None, tf, D), lambda i, j: (layer, j, 0)),
            pl.BlockSpec((1, D), lambda i, j: (0, 0), pipeline_mode=once),
        ],
        out_specs=pl.BlockSpec((tm, D), lambda i, j: (i, 0)),
        out_shape=jax.ShapeDtypeStruct((T, D), F32),
        scratch_shapes=[pltpu.VMEM((tm, D), BF16), pltpu.VMEM((tm, D), F32)],
        compiler_params=_params("parallel", "arbitrary"),
        name="mix_ffn_residual",
    )(a, b, w_out, w_out, x, g.reshape(1, D), w1, w3, w2, gf)


NEG_BIG = -1e30
_NT = (((1,), (1,)), ((), ()))


def _flash_steps(qas, kas, vas, masks, m_prev, acc_prev):
    hs = range(len(qas))
    s = [lax.dot_general(qas[h], kas[h], _NT, preferred_element_type=F32) for h in hs]
    s = [s[h] if masks[h] is None else jnp.where(masks[h], s[h], NEG_BIG) for h in hs]
    m_new = [jnp.maximum(m_prev[h], jnp.max(s[h], axis=1, keepdims=True)) for h in hs]
    alpha = [jnp.exp2(m_prev[h] - m_new[h]) for h in hs]
    p = [jnp.exp2(s[h] - m_new[h]) for h in hs]
    pv = [jnp.dot(p[h].astype(BF16), vas[h], preferred_element_type=F32) for h in hs]
    return m_new, [alpha[h] * acc_prev[h] + pv[h] for h in hs]


def _augment_q(q_log2, in_head, keep_t, odd):
    nblk, tq = keep_t.shape
    bias_t = jnp.where(keep_t, 0.0, NEG_BIG)
    bias = jnp.concatenate([bias_t, jnp.zeros((V7X_LANES - nblk, tq), F32)], axis=0).T
    if not odd:
        bias = pltpu.roll(bias, HEAD_DIM, axis=1)
    return jnp.where(in_head, q_log2, bias).astype(BF16)


def _key_value_tiles(k, v, blk, lane):
    low = lane < HEAD_DIM
    hot_e = 0.0 if blk is None else jnp.where(lane - HEAD_DIM == blk, 1.0, 0.0)
    hot_o = 0.0 if blk is None else jnp.where(lane == blk, 1.0, 0.0)
    k_e, k_o = jnp.where(low, k, hot_e), jnp.where(low, hot_o, k)
    v_e, v_o = jnp.where(low, v, 1.0), jnp.where(low, 1.0, v)
    return [t.astype(BF16) for t in (k_e, k_o, v_e, v_o)]


def _pair_normalise(acc_even, acc_odd, low):
    den = pltpu.roll(jnp.where(low, acc_odd, acc_even), HEAD_DIM, axis=1)
    return jnp.where(low, acc_even, acc_odd) / den


LOG2E = 1.4426950408889634


MOBA_HEADS_PER_STEP = 8


def _split_bf16(x):
    hi = x.astype(BF16)
    return hi, (x - hi.astype(F32)).astype(BF16)


def _moba_kernel(q_ref, k_ref, v_ref, o_ref, ka_ref, va_ref, km_ref, acc_ref):
    L = MOBA_BLOCK
    S = k_ref.shape[1]
    nb = S // L
    HP = MOBA_HEADS_PER_STEP
    qi = pl.program_id(2)
    lane = lax.broadcasted_iota(jnp.int32, (L, V7X_LANES), 1)
    lanes_of = lambda ref, pp: ref[0, :, pp * V7X_LANES:(pp + 1) * V7X_LANES]

    @pl.when(qi == 0)
    def _():
        lane_s = lax.broadcasted_iota(jnp.int32, (S, V7X_LANES), 1)
        blk = lax.broadcasted_iota(jnp.int32, (S, V7X_LANES), 0) // L
        for pp in range(HP // 2):
            k = lanes_of(k_ref, pp)
            (ka_ref[2 * pp], ka_ref[2 * pp + 1], va_ref[2 * pp], va_ref[2 * pp + 1]) = _key_value_tiles(
                k, lanes_of(v_ref, pp), blk, lane_s)
            km_ref[pp] = jnp.concatenate(_split_bf16(jnp.mean(k.reshape(nb, L, V7X_LANES), axis=1)), axis=0)

    row = lax.broadcasted_iota(jnp.int32, (L, L), 0)
    col = lax.broadcasted_iota(jnp.int32, (L, L), 1)
    causal = col <= row
    jrow = lax.broadcasted_iota(jnp.int32, (nb, L), 0)
    past = jrow < qi
    qas = []
    for pp in range(HP // 2):
        q = lanes_of(q_ref, pp) * (HEAD_DIM ** -0.5)
        q_hi, q_lo = _split_bf16(q)
        km = km_ref[pp]
        for e in range(2):
            in_head = (lane >= e * HEAD_DIM) & (lane < (e + 1) * HEAD_DIM)
            zero = jnp.zeros_like(q_hi)
            g1 = lax.dot_general(km, jnp.where(in_head, q_hi, zero), _NT, preferred_element_type=F32)
            g2 = lax.dot_general(km, jnp.where(in_head, q_lo, zero), _NT, preferred_element_type=F32)
            gate = g1[0:nb] + g1[nb:2 * nb] + g2[0:nb]
            keep = jrow == qi
            for n in range(nb):
                g_n = gate[n:n + 1, :]
                beats = (gate > g_n) | ((gate == g_n) & (jrow < n))
                rank = jnp.sum(jnp.where(past & beats, 1.0, 0.0), axis=0, keepdims=True)
                keep = keep | ((jrow == n) & (rank < MOBA_TOPK) & past)
            qas.append(_augment_q(q * LOG2E, in_head, keep, odd=e == 1))

    tiles = lambda start: ([ka_ref[h, pl.ds(start, L), :] for h in range(HP)],
                           [va_ref[h, pl.ds(start, L), :] for h in range(HP)])

    m, acc = _flash_steps(qas, *tiles(pl.multiple_of(qi * L, L)), [causal] * HP,
                          [jnp.full((L, 1), NEG_BIG, F32)] * HP, [jnp.zeros((L, V7X_LANES), F32)] * HP)
    for h in range(HP):
        acc_ref[h] = acc[h]

    def body(n, carry):
        m2, acc2 = _flash_steps(qas, *tiles(pl.multiple_of(n * L, L)), [None] * HP, list(carry),
                                [acc_ref[h] for h in range(HP)])
        for h in range(HP):
            acc_ref[h] = acc2[h]
        return tuple(m2)

    lax.fori_loop(0, qi, body, tuple(m))
    for pp in range(HP // 2):
        o_ref[0, :, pp * V7X_LANES:(pp + 1) * V7X_LANES] = _pair_normalise(
            acc_ref[2 * pp], acc_ref[2 * pp + 1], lane < HEAD_DIM).astype(o_ref.dtype)


def moba_attention(z, *, n_heads=16):
    B, S, _ = z.shape
    L, HP = MOBA_BLOCK, MOBA_HEADS_PER_STEP
    W = HP * HEAD_DIM
    nb = S // L
    assert S % L == 0 and n_heads % HP == 0 and W % V7X_LANES == 0 and 2 * nb <= 16
    ngrp = n_heads // HP
    return pl.pallas_call(
        _moba_kernel,
        grid=(B, ngrp, S // L),
        in_specs=[
            pl.BlockSpec((1, L, W), lambda b, p, i: (b, i, p)),
            pl.BlockSpec((1, S, W), lambda b, p, i: (b, 0, ngrp + p)),
            pl.BlockSpec((1, S, W), lambda b, p, i: (b, 0, 2 * ngrp + p)),
        ],
        out_specs=pl.BlockSpec((1, L, W), lambda b, p, i: (b, i, p)),
        out_shape=jax.ShapeDtypeStruct((B, S, n_heads * HEAD_DIM), BF16),
        scratch_shapes=[
            pltpu.VMEM((HP, S, V7X_LANES), BF16),
            pltpu.VMEM((HP, S, V7X_LANES), BF16),
            pltpu.VMEM((HP // 2, 2 * nb, V7X_LANES), BF16),
            pltpu.VMEM((HP, L, V7X_LANES), F32),
        ],
        compiler_params=_params("parallel", "parallel", "arbitrary"),
        name="moba_attention",
    )(z, z, z)


NSA_TQ = 256
NSA_GROUPS_PER_STEP = 2
BIG = 3.0e38


def _gelu_tanh(x):
    return 0.5 * x * (1.0 + jnp.tanh(0.7978845608028654 * (x + 0.044715 * x * x * x)))


def _nsa_compress_kernel(xk0_ref, xk1_ref, xv0_ref, xv1_ref, pek_ref, pev_ref, w1k_ref, w1v_ref, w2k_ref, w2v_ref,
                         o1_ref, o2_ref):
    G, Lc, st = NSA_KV_GROUPS, NSA_CMP_BLOCK, NSA_CMP_STRIDE
    nrow = xk0_ref.shape[1] // st
    lane = lax.broadcasted_iota(jnp.int32, (nrow, G * HEAD_DIM), 1)

    def hidden(x_refs, pe_ref, w1_ref):
        acc = [jnp.zeros((G * nrow, V7X_LANES), F32) for _ in range(Lc // st)]
        for l in range(Lc):
            u, m = divmod(l, st)
            x = jnp.concatenate([r[0, pl.ds(m, nrow, stride=st), :] for r in x_refs], axis=1) + pe_ref[l:l + 1, :]
            xs = jnp.concatenate(
                [jnp.where((lane >= g * HEAD_DIM) & (lane < (g + 1) * HEAD_DIM), x, 0.0) for g in range(G)],
                axis=0).astype(BF16)
            acc[u] = acc[u] + jnp.dot(xs, w1_ref[l], preferred_element_type=F32)
        nxt = jnp.concatenate([pltpu.roll(acc[1][g * nrow:(g + 1) * nrow], nrow - 1, axis=0) for g in range(G)],
                              axis=0)
        return _gelu_tanh(acc[0] + nxt).astype(BF16)

    hk = hidden((xk0_ref, xk1_ref), pek_ref, w1k_ref)
    hv = hidden((xv0_ref, xv1_ref), pev_ref, w1v_ref)
    kc = jnp.dot(hk, w2k_ref[...], preferred_element_type=F32)
    vc = jnp.dot(hv, w2v_ref[...], preferred_element_type=F32)
    kv = kc + vc
    vk = pltpu.roll(kv, HEAD_DIM, axis=1)
    for g in range(G):
        o1_ref[0, :, g * V7X_LANES:(g + 1) * V7X_LANES] = kv[g * nrow:(g + 1) * nrow]
        o2_ref[0, :, g * V7X_LANES:(g + 1) * V7X_LANES] = vk[g * nrow:(g + 1) * nrow]


def nsa_compress(z, col_k, col_v, pe_k, w1_k, w2_k, pe_v, w1_v, w2_v):
    B, S, _ = z.shape
    G, Lc, st = NSA_KV_GROUPS, NSA_CMP_BLOCK, NSA_CMP_STRIDE
    GW = G * HEAD_DIM
    nrow = S // st
    hid = w1_k.shape[1]
    assert hid == V7X_LANES and col_k % GW == 0 and col_v % GW == 0
    tile_pe = lambda pe: jnp.tile(pe, (1, G))
    tile_w1 = lambda w: jnp.tile(w.reshape(Lc, 1, HEAD_DIM, hid), (1, G, 1, 1)).reshape(Lc, GW, hid).astype(BF16)
    w2k = jnp.pad(w2_k, ((0, 0), (0, HEAD_DIM))).astype(BF16)
    w2v = jnp.pad(w2_v, ((0, 0), (HEAD_DIM, 0))).astype(BF16)
    const = lambda shape: pl.BlockSpec(shape, lambda b: (0,) * len(shape))
    out = jax.ShapeDtypeStruct((B, nrow, G * V7X_LANES), F32)
    return pl.pallas_call(
        _nsa_compress_kernel,
        grid=(B,),
        in_specs=[
            pl.BlockSpec((1, S, V7X_LANES), lambda b: (b, 0, col_k // V7X_LANES)),
            pl.BlockSpec((1, S, V7X_LANES), lambda b: (b, 0, col_k // V7X_LANES + 1)),
            pl.BlockSpec((1, S, V7X_LANES), lambda b: (b, 0, col_v // V7X_LANES)),
            pl.BlockSpec((1, S, V7X_LANES), lambda b: (b, 0, col_v // V7X_LANES + 1)),
            const((Lc, GW)), const((Lc, GW)),
            const((Lc, GW, hid)), const((Lc, GW, hid)),
            const((hid, V7X_LANES)), const((hid, V7X_LANES)),
        ],
        out_specs=[pl.BlockSpec((1, nrow, G * V7X_LANES), lambda b: (b, 0, 0))] * 2,
        out_shape=[out, out],
        compiler_params=_params("parallel"),
        name="nsa_compress",
    )(z, z, z, z, tile_pe(pe_k), tile_pe(pe_v), tile_w1(w1_k), tile_w1(w1_v), w2k, w2v)


def _nsa_kernel(q_ref, c1_ref, c2_ref, s_ref, w_ref, g_ref, ovt_ref, o_ref, sk_ref, sv_ref, wk_ref, wv_ref, acc_ref):
    TQ = NSA_TQ
    S = s_ref.shape[1]
    R = NSA_HEADS // NSA_KV_GROUPS
    NG = NSA_GROUPS_PER_STEP
    NH = NG * R
    grp0 = pl.program_id(1) * NG
    qi = pl.program_id(2)
    lane = lax.broadcasted_iota(jnp.int32, (TQ, V7X_LANES), 1)
    tile_of = lambda ref, t: ref[0, :, t * V7X_LANES:(t + 1) * V7X_LANES]

    @pl.when(qi == 0)
    def _():
        lane_s = lax.broadcasted_iota(jnp.int32, (S, V7X_LANES), 1)
        blk = lax.broadcasted_iota(jnp.int32, (S, V7X_LANES), 0) // NSA_SLC_BLOCK
        for gg in range(NG):
            for src, k_ref, v_ref, hot in ((s_ref, sk_ref, sv_ref, blk), (w_ref, wk_ref, wv_ref, None)):
                kv = tile_of(src, gg)
                vk = pltpu.roll(kv, HEAD_DIM, axis=1)
                k_ref[2 * gg], k_ref[2 * gg + 1], v_ref[2 * gg], v_ref[2 * gg + 1] = _key_value_tiles(
                    jnp.where(lane_s < HEAD_DIM, kv, vk), jnp.where(lane_s < HEAD_DIM, vk, kv), hot, lane_s)

    q0 = pl.multiple_of(qi * TQ, TQ)
    row = lax.broadcasted_iota(jnp.int32, (TQ, TQ), 0)
    col = lax.broadcasted_iota(jnp.int32, (TQ, TQ), 1)
    causal = col <= row
    t_abs = q0 + lax.broadcasted_iota(jnp.int32, (TQ, V7X_LANES), 0)
    even_lanes = lane < HEAD_DIM

    cmask = lane * NSA_CMP_STRIDE + (NSA_CMP_BLOCK - 1) <= t_abs
    nblk = s_ref.shape[1] // NSA_SLC_BLOCK
    jrow = lax.broadcasted_iota(jnp.int32, (nblk, TQ), 0)
    own = (q0 + lax.broadcasted_iota(jnp.int32, (nblk, TQ), 1)) // NSA_SLC_BLOCK
    ovt = ovt_ref[...]
    head_lanes = [even_lanes if h % 2 == 0 else ~even_lanes for h in range(NH)]
    tiles = [tile_of(q_ref, h // 2) * (HEAD_DIM ** -0.5) for h in range(NH)]
    o_cmp, qas = [], []
    for gg in range(NG):
        c_kv_b, c_vk_b = tile_of(c1_ref, gg).astype(BF16), tile_of(c2_ref, gg).astype(BF16)
        p_sum = jnp.zeros((TQ, V7X_LANES), F32)
        for h in range(gg * R, (gg + 1) * R):
            qm = jnp.where(head_lanes[h], tiles[h], 0.0).astype(BF16)
            s = lax.dot_general(qm, c_kv_b if h % 2 == 0 else c_vk_b, _NT, preferred_element_type=F32)
            s = jnp.where(cmask, s, NEG_BIG)
            p = jnp.where(cmask, jnp.exp(s - jnp.max(s, axis=1, keepdims=True)), 0.0)
            den = jnp.sum(p, axis=1, keepdims=True)
            p = p / jnp.where(den > 0.0, den, 1.0)
            p_sum = p_sum + p
            o_cmp.append(jnp.dot(p.astype(BF16), c_vk_b if h % 2 == 0 else c_kv_b, preferred_element_type=F32))

        p_hi, p_lo = _split_bf16(p_sum)
        p_slc = (lax.dot_general(ovt, p_hi, _NT, preferred_element_type=F32)
                 + lax.dot_general(ovt, p_lo, _NT, preferred_element_type=F32))[0:nblk]
        score = jnp.where((jrow == own) | (jrow == 0), BIG, jnp.where(jrow > own, -BIG, p_slc))
        keep = jrow > nblk
        for j in range(nblk):
            s_j = score[j:j + 1, :]
            beats = (score > s_j) | ((score == s_j) & (jrow < j))
            rank = jnp.sum(jnp.where(beats, 1.0, 0.0), axis=0, keepdims=True)
            keep = keep | ((jrow == j) & (rank < NSA_SLC_TOPN) & (jrow <= own))
        qas += [_augment_q(tiles[h] * LOG2E, head_lanes[h], keep, odd=h % 2 == 1) for h in range(gg * R, (gg + 1) * R)]

    neg = [jnp.full((TQ, 1), NEG_BIG, F32)] * NH
    zacc = [jnp.zeros((TQ, V7X_LANES), F32)] * NH
    kv_index = [2 * (h // R) + h % 2 for h in range(NH)]

    def kv_blocks(k_ref, v_ref, start):
        return ([k_ref[kv_index[h], pl.ds(start, TQ), :] for h in range(NH)],
                [v_ref[kv_index[h], pl.ds(start, TQ), :] for h in range(NH)])

    m, acc = _flash_steps(qas, *kv_blocks(sk_ref, sv_ref, q0), [causal] * NH, neg, zacc)
    for h in range(NH):
        acc_ref[h] = acc[h]

    def body(kb, carry):
        m2, acc2 = _flash_steps(qas, *kv_blocks(sk_ref, sv_ref, pl.multiple_of(kb * TQ, TQ)), [None] * NH,
                                list(carry), [acc_ref[h] for h in range(NH)])
        for h in range(NH):
            acc_ref[h] = acc2[h]
        return tuple(m2)

    lax.fori_loop(0, qi, body, tuple(m))

    WK = NSA_WINDOW + TQ
    w0 = pl.multiple_of(jnp.maximum(qi - NSA_WINDOW // TQ, 0) * TQ, TQ)
    key_pos = w0 + lax.broadcasted_iota(jnp.int32, (TQ, WK), 1)
    t_win = q0 + lax.broadcasted_iota(jnp.int32, (TQ, WK), 0)
    in_window = (key_pos <= t_win) & (key_pos > t_win - NSA_WINDOW)
    _, acc = _flash_steps(qas, [wk_ref[kv_index[h], pl.ds(w0, WK), :] for h in range(NH)],
                          [wv_ref[kv_index[h], pl.ds(w0, WK), :] for h in range(NH)], [in_window] * NH, neg, zacc)

    gates = jax.nn.sigmoid(g_ref[0])
    gate = lambda c: jnp.sum(jnp.where(lane == c, gates, 0.0), axis=1, keepdims=True)
    for p2 in range(NH // 2):
        he, ho = 2 * p2, 2 * p2 + 1
        ce, co = (grp0 * R + he) * 3, (grp0 * R + ho) * 3
        pair_gate = lambda j: jnp.where(even_lanes, gate(ce + j), gate(co + j))
        out = (pair_gate(0) * jnp.where(even_lanes, o_cmp[he], o_cmp[ho])
               + pair_gate(1) * _pair_normalise(acc_ref[he], acc_ref[ho], even_lanes)
               + pair_gate(2) * _pair_normalise(acc[he], acc[ho], even_lanes))
        o_ref[0, :, p2 * V7X_LANES:(p2 + 1) * V7X_LANES] = out.astype(o_ref.dtype)


def nsa_attention(z, cmp_kv, cmp_vk, col_q, col_slc, col_win, col_gate):
    B, S, _ = z.shape
    G, TQ, NG = NSA_KV_GROUPS, NSA_TQ, NSA_GROUPS_PER_STEP
    R = NSA_HEADS // G
    QW = NG * R * HEAD_DIM
    KW = NG * V7X_LANES
    ncmp = cmp_kv.shape[1]
    assert S % TQ == 0 and ncmp == V7X_LANES and S // NSA_SLC_BLOCK <= V7X_LANES
    assert NSA_WINDOW % TQ == 0 and S >= NSA_WINDOW + TQ
    assert G % NG == 0 and R % 2 == 0
    assert col_q % QW == 0 and col_slc % KW == 0 and col_win % KW == 0 and col_gate % V7X_LANES == 0
    nc = (S - NSA_CMP_BLOCK) // NSA_CMP_STRIDE + 1
    c_start = np.arange(V7X_LANES) * NSA_CMP_STRIDE
    s_start = np.arange(V7X_LANES) * NSA_SLC_BLOCK
    overlap = ((c_start[:, None] <= s_start[None, :] + NSA_SLC_BLOCK - 1)
               & (c_start[:, None] + NSA_CMP_BLOCK - 1 >= s_start[None, :])
               & (np.arange(V7X_LANES)[:, None] < nc) & (np.arange(V7X_LANES)[None, :] < S // NSA_SLC_BLOCK))
    const = lambda shape: pl.BlockSpec(shape, lambda b, g, i: (0,) * len(shape))
    return pl.pallas_call(
        _nsa_kernel,
        grid=(B, G // NG, S // TQ),
        in_specs=[
            pl.BlockSpec((1, TQ, QW), lambda b, g, i: (b, i, col_q // QW + g)),
            pl.BlockSpec((1, ncmp, KW), lambda b, g, i: (b, 0, g)),
            pl.BlockSpec((1, ncmp, KW), lambda b, g, i: (b, 0, g)),
            pl.BlockSpec((1, S, KW), lambda b, g, i: (b, 0, col_slc // KW + g)),
            pl.BlockSpec((1, S, KW), lambda b, g, i: (b, 0, col_win // KW + g)),
            pl.BlockSpec((1, TQ, V7X_LANES), lambda b, g, i: (b, i, col_gate // V7X_LANES)),
            const((V7X_LANES, V7X_LANES)),
        ],
        out_specs=pl.BlockSpec((1, TQ, QW), lambda b, g, i: (b, i, g)),
        out_shape=jax.ShapeDtypeStruct((B, S, NSA_HEADS * HEAD_DIM), BF16),
        scratch_shapes=[pltpu.VMEM((2 * NG, S, V7X_LANES), BF16)] * 4 + [
            pltpu.VMEM((NG * R, TQ, V7X_LANES), F32),
        ],
        compiler_params=_params("parallel", "parallel", "arbitrary"),
        name="nsa_attention",
    )(z, cmp_kv, cmp_vk, z, z, z, jnp.asarray(overlap.T, BF16))


RWKV_CHUNK = 64
RWKV_ROWS = 512
RWKV_INTERLEAVE = 8


def _mm(a, b, dims=None):
    dims = dims or (((1,), (0,)), ((), ()))
    return lax.dot_general(a.astype(BF16), b.astype(BF16), dims, preferred_element_type=F32)


def _mm3(a, b):
    (a_hi, a_lo), (b_hi, b_lo) = _split_bf16(a), _split_bf16(b)
    return _mm(a_hi, b_hi) + (_mm(a_hi, b_lo) + _mm(a_lo, b_hi))


def _mm_onehot(a01, b):
    hi = b.astype(BF16)
    mid, lo = _split_bf16(b - hi.astype(F32))
    return _mm(a01, hi) + (_mm(a01, mid) + _mm(a01, lo))


def _head_sum(x, low):
    s0 = jnp.sum(jnp.where(low, x, 0.0), axis=1, keepdims=True)
    s1 = jnp.sum(jnp.where(low, 0.0, x), axis=1, keepdims=True)
    return jnp.where(low, s0, s1)


def _rwkv_kernel(r_ref, k_ref, v_ref, lo_ref, glo_ref, pp_ref, pl_ref, wup_ref, aup_ref, gup_ref, o_ref,
                 rs, ws, ks, vs, als, bes, gs, ys, bon, hs, rqs, ms, ns):
    S = r_ref.shape[1]
    C, RB = RWKV_CHUNK, RWKV_ROWS
    pp = pp_ref[...]
    mu_r, mu_k, mu_v, w0, a0, k_k, k_a, r_k, ln_g, ln_b = [pp[i:i + 1, :] for i in range(10)]
    mu_lo, mu_g = pl_ref[0:1, :], pl_ref[1:2, :]
    heads = lax.broadcasted_iota(jnp.int32, (RB, V7X_LANES), 1) < HEAD_DIM
    first = lax.broadcasted_iota(jnp.int32, (RB, V7X_LANES), 0) == 0

    def prologue(i):
        t0 = i * RB

        def shifted(ref, mu):
            x = ref[0, pl.ds(t0, RB), :]
            last = ref[0, pl.ds(t0 - 1, 1), :] if i > 0 else jnp.zeros((1, V7X_LANES), F32)
            prev = jnp.where(first, last, pltpu.roll(x, 1, axis=0))
            return x + (prev - x) * mu

        r, k, v = shifted(r_ref, mu_r), shifted(k_ref, mu_k), shifted(v_ref, mu_v)
        lo, glo = shifted(lo_ref, mu_lo), shifted(glo_ref, mu_g)
        wp = -(w0 + _mm(jnp.tanh(lo), wup_ref[...]))
        w = -(jnp.maximum(wp, 0.0) + jnp.log(1.0 + jnp.exp(-jnp.abs(wp)))) - 0.5
        a = jax.nn.sigmoid(a0 + _mm(lo, aup_ref[...]))
        kk = k * k_k
        kk = kk * lax.rsqrt(jnp.maximum(_head_sum(kk * kk, heads), 1e-24))
        k2 = k * (1.0 + (a - 1.0) * k_a)
        rs[pl.ds(t0, RB), :] = r
        ws[pl.ds(t0, RB), :] = -jnp.exp(w)
        ks[pl.ds(t0, RB), :] = k2
        vs[pl.ds(t0, RB), :] = v
        als[pl.ds(t0, RB), :] = -kk
        bes[pl.ds(t0, RB), :] = kk * a
        gs[pl.ds(t0, RB), :] = _mm(jax.nn.sigmoid(glo), gup_ref[...])
        bon[pl.ds(t0, RB), :] = _head_sum(r * k2 * r_k, heads) * v

    W2 = 2 * C
    row = lax.broadcasted_iota(jnp.int32, (W2, W2), 0)
    col = lax.broadcasted_iota(jnp.int32, (W2, W2), 1)
    t_idx, s_idx = row % C, col % C
    top, left = row < C, col < C
    same = top == left
    eye = jnp.where(row == col, 1.0, 0.0)
    tri = jnp.where(lax.broadcasted_iota(jnp.int32, (C, C), 1) <= lax.broadcasted_iota(jnp.int32, (C, C), 0), 1.0, 0.0)
    low_c = lax.broadcasted_iota(jnp.int32, (C, V7X_LANES), 1) < HEAD_DIM
    fold = lambda x: x[0:C] + x[C:W2]
    stack_heads = lambda x: jnp.concatenate([jnp.where(low_c, x, 0.0), jnp.where(low_c, 0.0, x)], axis=0)
    block_diag = lambda x: jnp.where(top, jnp.where(left, x, 0.0), jnp.where(left, 0.0, pltpu.roll(x, C, axis=1)))

    rows = lambda c: pl.ds(c * C if isinstance(c, int) else pl.multiple_of(c * C, C), C)

    def advance(c, H):
        ys[rows(c), :] += _mm3(rqs[c], H)
        return _mm3(ms[c], H) + ns[c]

    def transfers(i, lagged, side):
        each = lambda f, *xs: [f(*a) for a in zip(*xs)]
        cs = [i * RWKV_INTERLEAVE + u for u in range(RWKV_INTERLEAVE)]
        sls = [rows(c) for c in cs]
        state = [hs[...]] if lagged else None

        def lag(hook):
            if lagged:
                for u in range(hook * RWKV_INTERLEAVE // 8, (hook + 1) * RWKV_INTERLEAVE // 8):
                    state[0] = advance(cs[u] - RWKV_INTERLEAVE, state[0])
            if hook in side:
                side[hook]()

        r, lw, k2, v, al, be = ([ref[sl, :] for sl in sls] for ref in (rs, ws, ks, vs, als, bes))
        logp = each(lambda x: _mm_onehot(tri, x), lw)
        lag(0)
        P = each(jnp.exp, logp)
        Pinv = each(lambda x: jnp.exp(-x), logp)
        At = each(lambda a_, lp, w_: a_ * jnp.exp(lp - w_), al, logp, lw)
        Rt, Bt, Kt = each(jnp.multiply, r, P), each(jnp.multiply, be, Pinv), each(jnp.multiply, k2, Pinv)
        PC = each(lambda p: p[C - 1:C, :], P)
        A_bd, R_bd = each(stack_heads, At), each(stack_heads, Rt)
        Yt = each(lambda b, k: jnp.concatenate([b, k], axis=0), Bt, Kt)
        A1 = each(lambda a, y: jnp.where(s_idx < t_idx, _mm(a, y, dims=_NT), 0.0), A_bd, Yt)
        A2 = each(lambda a, y: jnp.where(s_idx <= t_idx, _mm(a, y, dims=_NT), 0.0), R_bd, Yt)
        X, Arb = each(block_diag, A1), each(block_diag, A2)
        T = each(lambda x: eye + x, X)
        for it in range(5):
            X = each(lambda x: _mm(x, x), X)
            T = each(lambda t, x: t + _mm(t, x), T, X)
            lag(1 + it)
        V0 = each(lambda x: jnp.concatenate([jnp.zeros_like(x), x], axis=0), v)
        TA = each(_mm, T, A_bd)
        AkV = each(lambda a, x: jnp.where(same, _mm(a, x), 0.0), A1, V0)
        lag(6)
        U0 = each(_mm, T, AkV)
        lag(7)
        AR = each(lambda a, t, u: _mm(a, jnp.concatenate([t, u], axis=1)), Arb, TA, U0)
        ArkV = each(lambda a, x: jnp.where(same, _mm(a, x), 0.0), A2, V0)
        Mx = each(lambda b, p, t: _mm((b * p).T, fold(t)), Bt, PC, TA)
        Nx = each(lambda b, k, p, u, x: _mm(jnp.concatenate([b * p, k * p], axis=0).T,
                                            jnp.concatenate([fold(u), x], axis=0)), Bt, Kt, PC, U0, v)
        for u in range(RWKV_INTERLEAVE):
            ys[sls[u], :] = fold(AR[u][:, W2:2 * W2] + ArkV[u])
            rqs[cs[u]] = Rt[u] + fold(AR[u][:, 0:W2])
            ms[cs[u]] = eye * PC[u] + jnp.where(same, Mx[u], 0.0)
            ns[cs[u]] = jnp.where(same, Nx[u], 0.0)
        if lagged:
            hs[...] = state[0]

    def epilogue(i):
        sl = pl.ds(i * RB, RB)
        y = ys[sl, :]
        d = y - _head_sum(y, heads) * (1.0 / HEAD_DIM)
        var = _head_sum(d * d, heads) * (1.0 / HEAD_DIM)
        yn = d * lax.rsqrt(var + RWKV_GN_EPS) * ln_g + ln_b
        o_ref[0, sl, :] = ((yn + bon[sl, :]) * gs[sl, :]).astype(o_ref.dtype)

    n_groups = S // RB
    assert 8 % RWKV_INTERLEAVE == 0 and RB == RWKV_INTERLEAVE * C
    hs[...] = jnp.zeros((W2, W2), F32)
    prologue(0)
    for i in range(n_groups):
        side = {}
        if i + 1 < n_groups:
            side[1] = functools.partial(prologue, i + 1)
        if i >= 2:
            side[4] = functools.partial(epilogue, i - 2)
        transfers(i, i > 0, side)
    H = hs[...]
    for u in range(RWKV_INTERLEAVE):
        H = advance((n_groups - 1) * RWKV_INTERLEAVE + u, H)
        if u == RWKV_INTERLEAVE // 2 and n_groups >= 2:
            epilogue(n_groups - 2)
    epilogue(n_groups - 1)


def rwkv7_mixer(z, shift_mu, w0, w_up, a0, a_up, g_up, k_k, k_a, r_k, ln_g, ln_b):
    B, S, _ = z.shape
    CW = RWKV_HEADS * HEAD_DIM
    npair = CW // V7X_LANES
    base = 3 * CW // V7X_LANES
    lora = w_up.shape[0] + a_up.shape[0]
    assert lora == V7X_LANES and g_up.shape[0] == V7X_LANES and S % RWKV_ROWS == 0
    pp = jnp.stack([shift_mu[0:CW], shift_mu[CW:2 * CW], shift_mu[2 * CW:3 * CW], w0, a0, k_k, k_a,
                    r_k.reshape(CW), ln_g, ln_b])
    pp = jnp.pad(pp, ((0, 16 - pp.shape[0]), (0, 0)))
    pl2 = jnp.pad(shift_mu[3 * CW:].reshape(2, V7X_LANES), ((0, 6), (0, 0)))
    wup = jnp.pad(w_up, ((0, a_up.shape[0]), (0, 0)))
    aup = jnp.pad(a_up, ((w_up.shape[0], 0), (0, 0)))
    tile = lambda off: pl.BlockSpec((1, S, V7X_LANES), lambda b, p: (b, 0, base + off * npair + p))
    fixed = lambda off: pl.BlockSpec((1, S, V7X_LANES), lambda b, p: (b, 0, base + 3 * npair + off))
    seq = pltpu.VMEM((S, V7X_LANES), F32)
    return pl.pallas_call(
        _rwkv_kernel,
        grid=(B, npair),
        in_specs=[
            tile(0), tile(1), tile(2), fixed(0), fixed(1),
            pl.BlockSpec((16, V7X_LANES), lambda b, p: (0, p)),
            pl.BlockSpec((8, V7X_LANES), lambda b, p: (0, 0)),
            pl.BlockSpec((V7X_LANES, V7X_LANES), lambda b, p: (0, p)),
            pl.BlockSpec((V7X_LANES, V7X_LANES), lambda b, p: (0, p)),
            pl.BlockSpec((V7X_LANES, V7X_LANES), lambda b, p: (0, p)),
        ],
        out_specs=pl.BlockSpec((1, S, V7X_LANES), lambda b, p: (b, 0, p)),
        out_shape=jax.ShapeDtypeStruct((B, S, CW), BF16),
        scratch_shapes=[seq] * 9 + [
            pltpu.VMEM((V7X_LANES, V7X_LANES), F32),
            pltpu.VMEM((S // RWKV_CHUNK, RWKV_CHUNK, V7X_LANES), F32),
            pltpu.VMEM((S // RWKV_CHUNK, V7X_LANES, V7X_LANES), F32),
            pltpu.VMEM((S // RWKV_CHUNK, V7X_LANES, V7X_LANES), F32),
        ],
        compiler_params=_params("parallel", "parallel"),
        name="rwkv7_mixer",
    )(z, z, z, z, z, pp, pl2, wup, aup, g_up.astype(BF16))


RET_CHUNKS_PER_STEP = 4


def _ret_kernel(q_ref, k_ref, v_ref, g_ref, cos_ref, sin_ref, din_ref, dq_ref, dk_ref, dc_ref, o_ref, st_ref):
    S = q_ref.shape[1]
    C, DV = RET_CHUNK, RET_V_DIM
    lane = lax.broadcasted_iota(jnp.int32, (C, V7X_LANES), 1)
    first_half = (lane % RET_QK_DIM) < RET_QK_DIM // 2
    st_ref[...] = jnp.zeros_like(st_ref)

    in_head = [(lane >= h * RET_QK_DIM) & (lane < (h + 1) * RET_QK_DIM) for h in range(2)]
    NCH = RET_CHUNKS_PER_STEP
    units = [(u, h) for u in range(NCH) for h in range(2)]

    def step(i, carry):
        sls = [pl.ds(pl.multiple_of((i * NCH + u) * C, C), C) for u in range(NCH)]

        def rot(z, sl):
            swapped = jnp.where(first_half, pltpu.roll(z, V7X_LANES - RET_QK_DIM // 2, axis=1),
                                pltpu.roll(z, RET_QK_DIM // 2, axis=1))
            return z * cos_ref[sl, :] + swapped * sin_ref[sl, :]

        q = [rot(q_ref[0, sl, :], sl) for sl in sls]
        k = [rot(k_ref[0, sl, :], sl) * (RET_QK_DIM ** -0.5) for sl in sls]
        qm = [jnp.where(in_head[h], q[u], 0.0) for u, h in units]
        v = [v_ref[0, sls[u], h * DV:(h + 1) * DV] for u, h in units]
        inner = [_mm(qm[n], k[u], dims=_NT) * din_ref[h] for n, (u, h) in enumerate(units)]
        upd = [_mm((jnp.where(in_head[h], k[u], 0.0) * dk_ref[h]).T, v[n]) for n, (u, h) in enumerate(units)]
        local = [_mm(inner[n], v[n]) for n in range(len(units))]
        st = [st_ref[h] for h in range(2)]
        for n, (u, h) in enumerate(units):
            o = local[n] + _mm(qm[n], st[h]) * dq_ref[h]
            st[h] = upd[n] + dc_ref[h, 0:1, :] * st[h]
            d = o - jnp.mean(o, axis=1, keepdims=True)
            on = d * lax.rsqrt(jnp.mean(d * d, axis=1, keepdims=True) + RET_GN_EPS)
            gate = g_ref[0, sls[u], h * DV:(h + 1) * DV]
            o_ref[0, sls[u], h * DV:(h + 1) * DV] = (gate * jax.nn.sigmoid(gate) * on).astype(o_ref.dtype)
        st_ref[0], st_ref[1] = st
        return carry

    lax.fori_loop(0, S // C // NCH, step, 0)


def retention_mixer(z):
    B, S, _ = z.shape
    H, C, DK, DV = RET_HEADS, RET_CHUNK, RET_QK_DIM, RET_V_DIM
    assert S % C == 0 and 2 * DK == V7X_LANES and DV == V7X_LANES
    npair = H // 2
    half = DK // 2
    inv = ROPE_BASE ** (-jnp.arange(half, dtype=F32) / half)
    ang = jnp.arange(S, dtype=F32)[:, None] * inv
    cos = jnp.tile(jnp.cos(ang), (1, 4))
    sin = jnp.tile(jnp.concatenate([-jnp.sin(ang), jnp.sin(ang)], axis=1), (1, 2))
    log_g = jnp.asarray(np.log(1.0 - 2.0 ** (-5.0 - np.arange(H))), F32)
    n = jnp.arange(C, dtype=F32)
    diff = n[:, None] - n[None, :]
    d_in = jnp.where(diff >= 0, jnp.exp(jnp.maximum(diff, 0.0) * log_g[:, None, None]), 0.0)
    lanes = lambda t: jnp.broadcast_to(t[..., None], t.shape + (V7X_LANES,))
    d_q = lanes(jnp.exp((n + 1.0) * log_g[:, None]))
    d_k = lanes(jnp.exp((C - 1.0 - n) * log_g[:, None]))
    d_c = lanes(jnp.broadcast_to(jnp.exp(C * log_g)[:, None], (H, 8)))
    qk_tiles = H * DK // V7X_LANES
    return pl.pallas_call(
        _ret_kernel,
        grid=(B, npair),
        in_specs=[
            pl.BlockSpec((1, S, V7X_LANES), lambda b, p: (b, 0, p)),
            pl.BlockSpec((1, S, V7X_LANES), lambda b, p: (b, 0, qk_tiles + p)),
            pl.BlockSpec((1, S, 2 * DV), lambda b, p: (b, 0, 2 * qk_tiles * V7X_LANES // (2 * DV) + p)),
            pl.BlockSpec((1, S, 2 * DV), lambda b, p: (b, 0, (2 * qk_tiles * V7X_LANES + H * DV) // (2 * DV) + p)),
            pl.BlockSpec((S, V7X_LANES), lambda b, p: (0, 0)),
            pl.BlockSpec((S, V7X_LANES), lambda b, p: (0, 0)),
            pl.BlockSpec((2, C, C), lambda b, p: (p, 0, 0)),
            pl.BlockSpec((2, C, V7X_LANES), lambda b, p: (p, 0, 0)),
            pl.BlockSpec((2, C, V7X_LANES), lambda b, p: (p, 0, 0)),
            pl.BlockSpec((2, 8, V7X_LANES), lambda b, p: (p, 0, 0)),
        ],
        out_specs=pl.BlockSpec((1, S, 2 * DV), lambda b, p: (b, 0, p)),
        out_shape=jax.ShapeDtypeStruct((B, S, H * DV), BF16),
        scratch_shapes=[pltpu.VMEM((2, V7X_LANES, DV), F32)],
        compiler_params=_params("parallel", "parallel"),
        name="retention_mixer",
    )(z, z, z, z, cos, sin, d_in, d_q, d_k, d_c)


def _even_mixer(x, g_norm, w_in, shift_mu, w0, w_up, a0, a_up, g_up, k_k, k_a, r_k, ln_g, ln_b):
    B, S, D = x.shape
    z = norm_matmul(x.reshape(B * S, D), g_norm, w_in.astype(BF16)).reshape(B, S, -1)
    o_a = moba_attention(z)
    o_b = rwkv7_mixer(z, shift_mu, w0, w_up, a0, a_up, g_up, k_k, k_a, r_k, ln_g, ln_b)
    return o_a, o_b


def _odd_mixer(x, g_norm, w_in, pe_k, w1_k, w2_k, pe_v, w1_v, w2_v):
    B, S, D = x.shape
    perm, col = _odd_layout()
    n_in = w_in.shape[1]
    real = perm[perm < n_in]
    cuts = [0] + [k for k in range(1, len(real)) if real[k] != real[k - 1] + 1] + [len(real)]
    w_b = w_in.astype(BF16)
    w_p = jnp.concatenate([w_b[:, int(real[a]):int(real[b - 1]) + 1] for a, b in zip(cuts[:-1], cuts[1:])]
                          + [jnp.zeros((w_in.shape[0], len(perm) - len(real)), BF16)], axis=1)
    z = norm_matmul(x.reshape(B * S, D), g_norm, w_p).reshape(B, S, -1)
    o_c = retention_mixer(z)
    cmp_kv, cmp_vk = nsa_compress(z, col["kc"], col["vc"], pe_k, w1_k, w2_k, pe_v, w1_v, w2_v)
    o_d = nsa_attention(z, cmp_kv, cmp_vk, col["nq"], col["slc"], col["win"], col["gate"])
    return o_c, o_d


def _odd_layout():
    G, Dh = NSA_KV_GROUPS, HEAD_DIM
    sizes = (RET_HEADS * RET_QK_DIM, RET_HEADS * RET_QK_DIM, RET_HEADS * RET_V_DIM, RET_HEADS * RET_V_DIM,
             NSA_HEADS * Dh) + (G * Dh,) * 6 + (3 * NSA_HEADS,)
    off = np.concatenate([[0], np.cumsum(sizes)])
    rq, rk, rv, rg, nq, kc, vc, ks, vs, kw, vw, ng = off[:-1]
    n_in = int(off[-1])
    pair = lambda a, b: np.concatenate([np.concatenate([a + g * Dh + np.arange(Dh), b + g * Dh + np.arange(Dh)])
                                        for g in range(G)])
    perm = np.concatenate([np.arange(ks), pair(ks, vs), pair(kw, vw), ng + np.arange(3 * NSA_HEADS)])
    n_pad = -(-len(perm) // (6 * V7X_MXU_DIM)) * 6 * V7X_MXU_DIM
    perm = np.concatenate([perm, np.full(n_pad - len(perm), n_in)]).astype(np.int32)
    col = {"nq": int(nq), "kc": int(kc), "vc": int(vc), "slc": int(ks), "win": int(ks) + 2 * G * Dh,
           "gate": int(ks) + 4 * G * Dh}
    return perm, col


def kernel(x, mix_norm, ffn_norm, even_w_in, even_shift_mu, even_w0, even_w_up, even_a0, even_a_up, even_g_up, even_k_k, even_k_a, even_r_k, even_ln_g, even_ln_b, even_w_out, odd_w_in, odd_cmp_pe_k, odd_cmp_w1_k, odd_cmp_w2_k, odd_cmp_pe_v, odd_cmp_w1_v, odd_cmp_w2_v, odd_w_out, ffn_w1, ffn_w3, ffn_w2, final_norm):
    B, S, D = x.shape
    depth = mix_norm.shape[0]
    w1, w3, w2 = ffn_w1.astype(BF16), ffn_w3.astype(BF16), ffn_w2.astype(BF16)
    for layer in range(depth):
        i = layer // 2
        if layer % 2 == 0:
            o1, o2 = _even_mixer(x, mix_norm[layer], even_w_in[i], even_shift_mu[i], even_w0[i], even_w_up[i],
                                 even_a0[i], even_a_up[i], even_g_up[i], even_k_k[i], even_k_a[i], even_r_k[i],
                                 even_ln_g[i], even_ln_b[i])
            w_out = even_w_out[i]
        else:
            o1, o2 = _odd_mixer(x, mix_norm[layer], odd_w_in[i], odd_cmp_pe_k[i], odd_cmp_w1_k[i], odd_cmp_w2_k[i],
                                odd_cmp_pe_v[i], odd_cmp_w1_v[i], odd_cmp_w2_v[i])
            w_out = odd_w_out[i]
        T = B * S
        x2 = mix_ffn_residual(o1.reshape(T, -1), o2.reshape(T, -1), w_out.astype(BF16), x.reshape(T, D),
                              ffn_norm[layer], w1, w3, w2, layer, final_norm if layer == depth - 1 else None)
        x = x2.reshape(B, S, D)
    return x
```

```python
import functools

import jax
import jax.numpy as jnp
import numpy as np
from jax import lax
from jax.experimental import pallas as pl
from jax.experimental.pallas import tpu as pltpu

F32 = jnp.float32
BF16 = jnp.bfloat16

V7X_LANES = 128
V7X_MXU_DIM = 256
V7X_VMEM_BYTES = 64 * 1024 * 1024
VMEM_LIMIT = V7X_VMEM_BYTES * 7 // 8

NORM_EPS = 1e-6
HEAD_DIM = 64

MOBA_BLOCK = 256
MOBA_TOPK = 3
RWKV_HEADS = 16
RWKV_GN_EPS = 6.4e-4

RET_HEADS = 8
RET_QK_DIM = 64
RET_V_DIM = 128
RET_CHUNK = 128
RET_GN_EPS = 1e-6
ROPE_BASE = 10000.0
NSA_HEADS = 16
NSA_KV_GROUPS = 4
NSA_CMP_BLOCK = 32
NSA_CMP_STRIDE = 16
NSA_SLC_BLOCK = 64
NSA_SLC_TOPN = 16
NSA_WINDOW = 512


def _params(*semantics):
    return pltpu.CompilerParams(dimension_semantics=semantics, vmem_limit_bytes=VMEM_LIMIT)


def _rms(x, g):
    return x * lax.rsqrt(jnp.mean(x * x, axis=-1, keepdims=True) + NORM_EPS) * g


def _norm_matmul_kernel(x_ref, g_ref, w_ref, o_ref):
    x = x_ref[...]
    scale = lax.rsqrt(jnp.mean(x * x, axis=-1, keepdims=True) + NORM_EPS)
    o_ref[...] = jnp.dot((x * g_ref[...]).astype(BF16), w_ref[...], preferred_element_type=F32) * scale


def _proj_tile(n):
    assert n % V7X_MXU_DIM == 0
    k = n // V7X_MXU_DIM
    return V7X_MXU_DIM * max(d for d in range(1, 7) if k % d == 0)


def norm_matmul(x, g, w, *, tm=512):
    T, D = x.shape
    N = w.shape[1]
    tn = _proj_tile(N)
    assert T % tm == 0 and N % tn == 0
    return pl.pallas_call(
        _norm_matmul_kernel,
        grid=(N // tn, T // tm),
        in_specs=[
            pl.BlockSpec((tm, D), lambda j, i: (i, 0)),
            pl.BlockSpec((1, D), lambda j, i: (0, 0)),
            pl.BlockSpec((D, tn), lambda j, i: (0, j)),
        ],
        out_specs=pl.BlockSpec((tm, tn), lambda j, i: (i, j)),
        out_shape=jax.ShapeDtypeStruct((T, N), F32),
        compiler_params=_params("parallel", "parallel"),
        name="norm_matmul",
    )(x, g.reshape(1, D), w)


def _mix_ffn_kernel(a_ref, b_ref, wa_ref, wb_ref, x_ref, g_ref, w1_ref, w3_ref, w2_ref, gf_ref, o_ref,
                    h_ref, acc_ref, *, final_norm):
    j = pl.program_id(1)

    @pl.when(j == 0)
    def _():
        x2 = (x_ref[...] + jnp.dot(a_ref[...], wa_ref[...], preferred_element_type=F32)
              + jnp.dot(b_ref[...], wb_ref[...], preferred_element_type=F32))
        o_ref[...] = x2
        h_ref[...] = _rms(x2, g_ref[...]).astype(BF16)
        acc_ref[...] = jnp.zeros_like(acc_ref)

    h = h_ref[...]
    a = jnp.dot(h, w1_ref[...], preferred_element_type=F32)
    b = jnp.dot(h, w3_ref[...], preferred_element_type=F32)
    act = (a * jax.nn.sigmoid(a) * b).astype(BF16)
    acc_ref[...] += jnp.dot(act, w2_ref[...], preferred_element_type=F32)

    @pl.when(j == pl.num_programs(1) - 1)
    def _():
        y = o_ref[...] + acc_ref[...]
        if final_norm:
            y = _rms(y, gf_ref[...])
        o_ref[...] = y


def mix_ffn_residual(a, b, w_out, x, g, w1, w3, w2, layer, g_final=None, *, tm=512, tf=512):
    T, D = x.shape
    K = a.shape[1]
    Fh = w1.shape[2]
    assert T % tm == 0 and Fh % tf == 0 and b.shape == a.shape and w_out.shape == (2 * K, D)
    final_norm = g_final is not None
    gf = (g_final if final_norm else g).reshape(1, D)
    once = pl.Buffered(1)
    return pl.pallas_call(
        functools.partial(_mix_ffn_kernel, final_norm=final_norm),
        grid=(T // tm, Fh // tf),
        in_specs=[
            pl.BlockSpec((tm, K), lambda i, j: (i, 0)),
            pl.BlockSpec((tm, K), lambda i, j: (i, 0)),
            pl.BlockSpec((K, D), lambda i, j: (0, 0), pipeline_mode=once),
            pl.BlockSpec((K, D), lambda i, j: (1, 0), pipeline_mode=once),
            pl.BlockSpec((tm, D), lambda i, j: (i, 0)),
            pl.BlockSpec((1, D), lambda i, j: (0, 0), pipeline_mode=once),
            pl.BlockSpec((None, D, tf), lambda i, j: (layer, 0, j)),
            pl.BlockSpec((None, D, tf), lambda i, j: (layer, 0, j)),
            pl.BlockSpec((None, tf, D), lambda i, j: (layer, j, 0)),
            pl.BlockSpec((1, D), lambda i, j: (0, 0), pipeline_mode=once),
        ],
        out_specs=pl.BlockSpec((tm, D), lambda i, j: (i, 0)),
        out_shape=jax.ShapeDtypeStruct((T, D), F32),
        scratch_shapes=[pltpu.VMEM((tm, D), BF16), pltpu.VMEM((tm, D), F32)],
        compiler_params=_params("parallel", "arbitrary"),
        name="mix_ffn_residual",
    )(a, b, w_out, w_out, x, g.reshape(1, D), w1, w3, w2, gf)


NEG_BIG = -1e30
_NT = (((1,), (1,)), ((), ()))


def _flash_steps(qas, kas, vas, masks, m_prev, acc_prev):
    hs = range(len(qas))
    s = [lax.dot_general(qas[h], kas[h], _NT, preferred_element_type=F32) for h in hs]
    s = [s[h] if masks[h] is None else jnp.where(masks[h], s[h], NEG_BIG) for h in hs]
    m_new = [jnp.maximum(m_prev[h], jnp.max(s[h], axis=1, keepdims=True)) for h in hs]
    alpha = [jnp.exp2(m_prev[h] - m_new[h]) for h in hs]
    p = [jnp.exp2(s[h] - m_new[h]) for h in hs]
    pv = [jnp.dot(p[h].astype(BF16), vas[h], preferred_element_type=F32) for h in hs]
    return m_new, [alpha[h] * acc_prev[h] + pv[h] for h in hs]


def _augment_q(q_log2, in_head, keep_t, odd):
    nblk, tq = keep_t.shape
    bias_t = jnp.where(keep_t, 0.0, NEG_BIG)
    bias = jnp.concatenate([bias_t, jnp.zeros((V7X_LANES - nblk, tq), F32)], axis=0).T
    if not odd:
        bias = pltpu.roll(bias, HEAD_DIM, axis=1)
    return jnp.where(in_head, q_log2, bias).astype(BF16)


def _key_value_tiles(k, v, blk, lane):
    low = lane < HEAD_DIM
    hot_e = 0.0 if blk is None else jnp.where(lane - HEAD_DIM == blk, 1.0, 0.0)
    hot_o = 0.0 if blk is None else jnp.where(lane == blk, 1.0, 0.0)
    k_e, k_o = jnp.where(low, k, hot_e), jnp.where(low, hot_o, k)
    v_e, v_o = jnp.where(low, v, jnp.where(lane == HEAD_DIM, 1.0, 0.0)), jnp.where(low, jnp.where(lane == 0, 1.0, 0.0), v)
    return [t.astype(BF16) for t in (k_e, k_o, v_e, v_o)]


def _normalise(acc, lane, odd):
    return acc / jnp.sum(jnp.where(lane == (0 if odd else HEAD_DIM), acc, 0.0), axis=1, keepdims=True)


LOG2E = 1.4426950408889634


MOBA_HEADS_PER_STEP = 8


def _split_bf16(x):
    hi = x.astype(BF16)
    return hi, (x - hi.astype(F32)).astype(BF16)


def _moba_kernel(q_ref, k_ref, v_ref, o_ref, ka_ref, va_ref, km_ref, acc_ref):
    L = MOBA_BLOCK
    S = k_ref.shape[1]
    nb = S // L
    HP = MOBA_HEADS_PER_STEP
    qi = pl.program_id(2)
    lane = lax.broadcasted_iota(jnp.int32, (L, V7X_LANES), 1)
    lanes_of = lambda ref, pp: ref[0, :, pp * V7X_LANES:(pp + 1) * V7X_LANES]

    @pl.when(qi == 0)
    def _():
        lane_s = lax.broadcasted_iota(jnp.int32, (S, V7X_LANES), 1)
        blk = lax.broadcasted_iota(jnp.int32, (S, V7X_LANES), 0) // L
        for pp in range(HP // 2):
            k = lanes_of(k_ref, pp)
            (ka_ref[2 * pp], ka_ref[2 * pp + 1], va_ref[2 * pp], va_ref[2 * pp + 1]) = _key_value_tiles(
                k, lanes_of(v_ref, pp), blk, lane_s)
            km_ref[pp] = jnp.concatenate(_split_bf16(jnp.mean(k.reshape(nb, L, V7X_LANES), axis=1)), axis=0)

    row = lax.broadcasted_iota(jnp.int32, (L, L), 0)
    col = lax.broadcasted_iota(jnp.int32, (L, L), 1)
    causal = col <= row
    jrow = lax.broadcasted_iota(jnp.int32, (nb, L), 0)
    past = jrow < qi
    qas = []
    for pp in range(HP // 2):
        q = lanes_of(q_ref, pp) * (HEAD_DIM ** -0.5)
        q_hi, q_lo = _split_bf16(q)
        km = km_ref[pp]
        for e in range(2):
            in_head = (lane >= e * HEAD_DIM) & (lane < (e + 1) * HEAD_DIM)
            zero = jnp.zeros_like(q_hi)
            g1 = lax.dot_general(km, jnp.where(in_head, q_hi, zero), _NT, preferred_element_type=F32)
            g2 = lax.dot_general(km, jnp.where(in_head, q_lo, zero), _NT, preferred_element_type=F32)
            gate = g1[0:nb] + g1[nb:2 * nb] + g2[0:nb]
            keep = jrow == qi
            for n in range(nb):
                g_n = gate[n:n + 1, :]
                beats = (gate > g_n) | ((gate == g_n) & (jrow < n))
                rank = jnp.sum(jnp.where(past & beats, 1.0, 0.0), axis=0, keepdims=True)
                keep = keep | ((jrow == n) & (rank < MOBA_TOPK) & past)
            qas.append(_augment_q(q * LOG2E, in_head, keep, odd=e == 1))

    tiles = lambda start: ([ka_ref[h, pl.ds(start, L), :] for h in range(HP)],
                           [va_ref[h, pl.ds(start, L), :] for h in range(HP)])

    m, acc = _flash_steps(qas, *tiles(pl.multiple_of(qi * L, L)), [causal] * HP,
                          [jnp.full((L, 1), NEG_BIG, F32)] * HP, [jnp.zeros((L, V7X_LANES), F32)] * HP)
    for h in range(HP):
        acc_ref[h] = acc[h]

    def body(n, carry):
        m2, acc2 = _flash_steps(qas, *tiles(pl.multiple_of(n * L, L)), [None] * HP, list(carry),
                                [acc_ref[h] for h in range(HP)])
        for h in range(HP):
            acc_ref[h] = acc2[h]
        return tuple(m2)

    lax.fori_loop(0, qi, body, tuple(m))
    for pp in range(HP // 2):
        o_ref[0, :, pp * V7X_LANES:(pp + 1) * V7X_LANES] = jnp.where(
            lane < HEAD_DIM, _normalise(acc_ref[2 * pp], lane, False), _normalise(acc_ref[2 * pp + 1], lane, True)
        ).astype(o_ref.dtype)


def moba_attention(z, *, n_heads=16):
    B, S, _ = z.shape
    L, HP = MOBA_BLOCK, MOBA_HEADS_PER_STEP
    W = HP * HEAD_DIM
    nb = S // L
    assert S % L == 0 and n_heads % HP == 0 and W % V7X_LANES == 0 and 2 * nb <= 16
    ngrp = n_heads // HP
    return pl.pallas_call(
        _moba_kernel,
        grid=(B, ngrp, S // L),
        in_specs=[
            pl.BlockSpec((1, L, W), lambda b, p, i: (b, i, p)),
            pl.BlockSpec((1, S, W), lambda b, p, i: (b, 0, ngrp + p)),
            pl.BlockSpec((1, S, W), lambda b, p, i: (b, 0, 2 * ngrp + p)),
        ],
        out_specs=pl.BlockSpec((1, L, W), lambda b, p, i: (b, i, p)),
        out_shape=jax.ShapeDtypeStruct((B, S, n_heads * HEAD_DIM), BF16),
        scratch_shapes=[
            pltpu.VMEM((HP, S, V7X_LANES), BF16),
            pltpu.VMEM((HP, S, V7X_LANES), BF16),
            pltpu.VMEM((HP // 2, 2 * nb, V7X_LANES), BF16),
            pltpu.VMEM((HP, L, V7X_LANES), F32),
        ],
        compiler_params=_params("parallel", "parallel", "arbitrary"),
        name="moba_attention",
    )(z, z, z)


NSA_TQ = 256
NSA_GROUPS_PER_STEP = 2
BIG = 3.0e38


def _gelu_tanh(x):
    return 0.5 * x * (1.0 + jnp.tanh(0.7978845608028654 * (x + 0.044715 * x * x * x)))


def _nsa_compress_kernel(xk0_ref, xk1_ref, xv0_ref, xv1_ref, pek_ref, pev_ref, w1k_ref, w1v_ref, w2k_ref, w2v_ref,
                         o1_ref, o2_ref):
    G, Lc, st = NSA_KV_GROUPS, NSA_CMP_BLOCK, NSA_CMP_STRIDE
    nrow = xk0_ref.shape[1] // st
    lane = lax.broadcasted_iota(jnp.int32, (nrow, G * HEAD_DIM), 1)

    def hidden(x_refs, pe_ref, w1_ref):
        acc = [jnp.zeros((G * nrow, V7X_LANES), F32) for _ in range(Lc // st)]
        for l in range(Lc):
            u, m = divmod(l, st)
            x = jnp.concatenate([r[0, pl.ds(m, nrow, stride=st), :] for r in x_refs], axis=1) + pe_ref[l:l + 1, :]
            xs = jnp.concatenate(
                [jnp.where((lane >= g * HEAD_DIM) & (lane < (g + 1) * HEAD_DIM), x, 0.0) for g in range(G)],
                axis=0).astype(BF16)
            acc[u] = acc[u] + jnp.dot(xs, w1_ref[l], preferred_element_type=F32)
        nxt = jnp.concatenate([pltpu.roll(acc[1][g * nrow:(g + 1) * nrow], nrow - 1, axis=0) for g in range(G)],
                              axis=0)
        return _gelu_tanh(acc[0] + nxt).astype(BF16)

    hk = hidden((xk0_ref, xk1_ref), pek_ref, w1k_ref)
    hv = hidden((xv0_ref, xv1_ref), pev_ref, w1v_ref)
    kc = jnp.dot(hk, w2k_ref[...], preferred_element_type=F32)
    vc = jnp.dot(hv, w2v_ref[...], preferred_element_type=F32)
    kv = kc + vc
    vk = pltpu.roll(kv, HEAD_DIM, axis=1)
    for g in range(G):
        o1_ref[0, :, g * V7X_LANES:(g + 1) * V7X_LANES] = kv[g * nrow:(g + 1) * nrow]
        o2_ref[0, :, g * V7X_LANES:(g + 1) * V7X_LANES] = vk[g * nrow:(g + 1) * nrow]


def nsa_compress(z, col_k, col_v, pe_k, w1_k, w2_k, pe_v, w1_v, w2_v):
    B, S, _ = z.shape
    G, Lc, st = NSA_KV_GROUPS, NSA_CMP_BLOCK, NSA_CMP_STRIDE
    GW = G * HEAD_DIM
    nrow = S // st
    hid = w1_k.shape[1]
    assert hid == V7X_LANES and col_k % GW == 0 and col_v % GW == 0
    tile_pe = lambda pe: jnp.tile(pe, (1, G))
    tile_w1 = lambda w: jnp.tile(w.reshape(Lc, 1, HEAD_DIM, hid), (1, G, 1, 1)).reshape(Lc, GW, hid).astype(BF16)
    w2k = jnp.pad(w2_k, ((0, 0), (0, HEAD_DIM))).astype(BF16)
    w2v = jnp.pad(w2_v, ((0, 0), (HEAD_DIM, 0))).astype(BF16)
    const = lambda shape: pl.BlockSpec(shape, lambda b: (0,) * len(shape))
    out = jax.ShapeDtypeStruct((B, nrow, G * V7X_LANES), F32)
    return pl.pallas_call(
        _nsa_compress_kernel,
        grid=(B,),
        in_specs=[
            pl.BlockSpec((1, S, V7X_LANES), lambda b: (b, 0, col_k // V7X_LANES)),
            pl.BlockSpec((1, S, V7X_LANES), lambda b: (b, 0, col_k // V7X_LANES + 1)),
            pl.BlockSpec((1, S, V7X_LANES), lambda b: (b, 0, col_v // V7X_LANES)),
            pl.BlockSpec((1, S, V7X_LANES), lambda b: (b, 0, col_v // V7X_LANES + 1)),
            const((Lc, GW)), const((Lc, GW)),
            const((Lc, GW, hid)), const((Lc, GW, hid)),
            const((hid, V7X_LANES)), const((hid, V7X_LANES)),
        ],
        out_specs=[pl.BlockSpec((1, nrow, G * V7X_LANES), lambda b: (b, 0, 0))] * 2,
        out_shape=[out, out],
        compiler_params=_params("parallel"),
        name="nsa_compress",
    )(z, z, z, z, tile_pe(pe_k), tile_pe(pe_v), tile_w1(w1_k), tile_w1(w1_v), w2k, w2v)


def _nsa_kernel(q_ref, c1_ref, c2_ref, s_ref, w_ref, g_ref, ovt_ref, o_ref, sk_ref, sv_ref, wk_ref, wv_ref, acc_ref):
    TQ = NSA_TQ
    S = s_ref.shape[1]
    R = NSA_HEADS // NSA_KV_GROUPS
    NG = NSA_GROUPS_PER_STEP
    NH = NG * R
    grp0 = pl.program_id(1) * NG
    qi = pl.program_id(2)
    lane = lax.broadcasted_iota(jnp.int32, (TQ, V7X_LANES), 1)
    tile_of = lambda ref, t: ref[0, :, t * V7X_LANES:(t + 1) * V7X_LANES]

    @pl.when(qi == 0)
    def _():
        lane_s = lax.broadcasted_iota(jnp.int32, (S, V7X_LANES), 1)
        blk = lax.broadcasted_iota(jnp.int32, (S, V7X_LANES), 0) // NSA_SLC_BLOCK
        for gg in range(NG):
            for src, k_ref, v_ref, hot in ((s_ref, sk_ref, sv_ref, blk), (w_ref, wk_ref, wv_ref, None)):
                kv = tile_of(src, gg)
                vk = pltpu.roll(kv, HEAD_DIM, axis=1)
                k_ref[2 * gg], k_ref[2 * gg + 1], v_ref[2 * gg], v_ref[2 * gg + 1] = _key_value_tiles(
                    jnp.where(lane_s < HEAD_DIM, kv, vk), jnp.where(lane_s < HEAD_DIM, vk, kv), hot, lane_s)

    q0 = pl.multiple_of(qi * TQ, TQ)
    row = lax.broadcasted_iota(jnp.int32, (TQ, TQ), 0)
    col = lax.broadcasted_iota(jnp.int32, (TQ, TQ), 1)
    causal = col <= row
    t_abs = q0 + lax.broadcasted_iota(jnp.int32, (TQ, V7X_LANES), 0)
    even_lanes = lane < HEAD_DIM

    cmask = lane * NSA_CMP_STRIDE + (NSA_CMP_BLOCK - 1) <= t_abs
    nblk = s_ref.shape[1] // NSA_SLC_BLOCK
    jrow = lax.broadcasted_iota(jnp.int32, (nblk, TQ), 0)
    own = (q0 + lax.broadcasted_iota(jnp.int32, (nblk, TQ), 1)) // NSA_SLC_BLOCK
    ovt = ovt_ref[...]
    head_lanes = [even_lanes if h % 2 == 0 else ~even_lanes for h in range(NH)]
    tiles = [tile_of(q_ref, h // 2) * (HEAD_DIM ** -0.5) for h in range(NH)]
    o_cmp, qas = [], []
    for gg in range(NG):
        c_kv_b, c_vk_b = tile_of(c1_ref, gg).astype(BF16), tile_of(c2_ref, gg).astype(BF16)
        p_sum = jnp.zeros((TQ, V7X_LANES), F32)
        for h in range(gg * R, (gg + 1) * R):
            qm = jnp.where(head_lanes[h], tiles[h], 0.0).astype(BF16)
            s = lax.dot_general(qm, c_kv_b if h % 2 == 0 else c_vk_b, _NT, preferred_element_type=F32)
            s = jnp.where(cmask, s, NEG_BIG)
            p = jnp.where(cmask, jnp.exp(s - jnp.max(s, axis=1, keepdims=True)), 0.0)
            den = jnp.sum(p, axis=1, keepdims=True)
            p = p / jnp.where(den > 0.0, den, 1.0)
            p_sum = p_sum + p
            o_cmp.append(jnp.dot(p.astype(BF16), c_vk_b if h % 2 == 0 else c_kv_b, preferred_element_type=F32))

        p_hi, p_lo = _split_bf16(p_sum)
        p_slc = (lax.dot_general(ovt, p_hi, _NT, preferred_element_type=F32)
                 + lax.dot_general(ovt, p_lo, _NT, preferred_element_type=F32))[0:nblk]
        score = jnp.where((jrow == own) | (jrow == 0), BIG, jnp.where(jrow > own, -BIG, p_slc))
        keep = jrow > nblk
        for j in range(nblk):
            s_j = score[j:j + 1, :]
            beats = (score > s_j) | ((score == s_j) & (jrow < j))
            rank = jnp.sum(jnp.where(beats, 1.0, 0.0), axis=0, keepdims=True)
            keep = keep | ((jrow == j) & (rank < NSA_SLC_TOPN) & (jrow <= own))
        qas += [_augment_q(tiles[h] * LOG2E, head_lanes[h], keep, odd=h % 2 == 1) for h in range(gg * R, (gg + 1) * R)]

    neg = [jnp.full((TQ, 1), NEG_BIG, F32)] * NH
    zacc = [jnp.zeros((TQ, V7X_LANES), F32)] * NH
    kv_index = [2 * (h // R) + h % 2 for h in range(NH)]

    def kv_blocks(k_ref, v_ref, start):
        return ([k_ref[kv_index[h], pl.ds(start, TQ), :] for h in range(NH)],
                [v_ref[kv_index[h], pl.ds(start, TQ), :] for h in range(NH)])

    m, acc = _flash_steps(qas, *kv_blocks(sk_ref, sv_ref, q0), [causal] * NH, neg, zacc)
    for h in range(NH):
        acc_ref[h] = acc[h]

    def body(kb, carry):
        m2, acc2 = _flash_steps(qas, *kv_blocks(sk_ref, sv_ref, pl.multiple_of(kb * TQ, TQ)), [None] * NH,
                                list(carry), [acc_ref[h] for h in range(NH)])
        for h in range(NH):
            acc_ref[h] = acc2[h]
        return tuple(m2)

    lax.fori_loop(0, qi, body, tuple(m))

    WK = NSA_WINDOW + TQ
    w0 = pl.multiple_of(jnp.maximum(qi - NSA_WINDOW // TQ, 0) * TQ, TQ)
    key_pos = w0 + lax.broadcasted_iota(jnp.int32, (TQ, WK), 1)
    t_win = q0 + lax.broadcasted_iota(jnp.int32, (TQ, WK), 0)
    in_window = (key_pos <= t_win) & (key_pos > t_win - NSA_WINDOW)
    _, acc = _flash_steps(qas, [wk_ref[kv_index[h], pl.ds(w0, WK), :] for h in range(NH)],
                          [wv_ref[kv_index[h], pl.ds(w0, WK), :] for h in range(NH)], [in_window] * NH, neg, zacc)

    gates = jax.nn.sigmoid(g_ref[0])
    outs = []
    for h in range(NH):
        o_slc = _normalise(acc_ref[h], lane, h % 2 == 1)
        o_win = _normalise(acc[h], lane, h % 2 == 1)
        c0 = (grp0 * R + h) * 3
        gate = lambda c: jnp.sum(jnp.where(lane == c, gates, 0.0), axis=1, keepdims=True)
        outs.append(gate(c0) * o_cmp[h] + gate(c0 + 1) * o_slc + gate(c0 + 2) * o_win)
    for p2 in range(NH // 2):
        o_ref[0, :, p2 * V7X_LANES:(p2 + 1) * V7X_LANES] = jnp.where(
            even_lanes, outs[2 * p2], outs[2 * p2 + 1]).astype(o_ref.dtype)


def nsa_attention(z, cmp_kv, cmp_vk, col_q, col_slc, col_win, col_gate):
    B, S, _ = z.shape
    G, TQ, NG = NSA_KV_GROUPS, NSA_TQ, NSA_GROUPS_PER_STEP
    R = NSA_HEADS // G
    QW = NG * R * HEAD_DIM
    KW = NG * V7X_LANES
    ncmp = cmp_kv.shape[1]
    assert S % TQ == 0 and ncmp == V7X_LANES and S // NSA_SLC_BLOCK <= V7X_LANES
    assert NSA_WINDOW % TQ == 0 and S >= NSA_WINDOW + TQ
    assert G % NG == 0 and R % 2 == 0
    assert col_q % QW == 0 and col_slc % KW == 0 and col_win % KW == 0 and col_gate % V7X_LANES == 0
    nc = (S - NSA_CMP_BLOCK) // NSA_CMP_STRIDE + 1
    c_start = np.arange(V7X_LANES) * NSA_CMP_STRIDE
    s_start = np.arange(V7X_LANES) * NSA_SLC_BLOCK
    overlap = ((c_start[:, None] <= s_start[None, :] + NSA_SLC_BLOCK - 1)
               & (c_start[:, None] + NSA_CMP_BLOCK - 1 >= s_start[None, :])
               & (np.arange(V7X_LANES)[:, None] < nc) & (np.arange(V7X_LANES)[None, :] < S // NSA_SLC_BLOCK))
    const = lambda shape: pl.BlockSpec(shape, lambda b, g, i: (0,) * len(shape))
    return pl.pallas_call(
        _nsa_kernel,
        grid=(B, G // NG, S // TQ),
        in_specs=[
            pl.BlockSpec((1, TQ, QW), lambda b, g, i: (b, i, col_q // QW + g)),
            pl.BlockSpec((1, ncmp, KW), lambda b, g, i: (b, 0, g)),
            pl.BlockSpec((1, ncmp, KW), lambda b, g, i: (b, 0, g)),
            pl.BlockSpec((1, S, KW), lambda b, g, i: (b, 0, col_slc // KW + g)),
            pl.BlockSpec((1, S, KW), lambda b, g, i: (b, 0, col_win // KW + g)),
            pl.BlockSpec((1, TQ, V7X_LANES), lambda b, g, i: (b, i, col_gate // V7X_LANES)),
            const((V7X_LANES, V7X_LANES)),
        ],
        out_specs=pl.BlockSpec((1, TQ, QW), lambda b, g, i: (b, i, g)),
        out_shape=jax.ShapeDtypeStruct((B, S, NSA_HEADS * HEAD_DIM), BF16),
        scratch_shapes=[pltpu.VMEM((2 * NG, S, V7X_LANES), BF16)] * 4 + [
            pltpu.VMEM((NG * R, TQ, V7X_LANES), F32),
        ],
        compiler_params=_params("parallel", "parallel", "arbitrary"),
        name="nsa_attention",
    )(z, cmp_kv, cmp_vk, z, z, z, jnp.asarray(overlap.T, BF16))


RWKV_CHUNK = 64
RWKV_ROWS = 512
RWKV_INTERLEAVE = 8


def _mm(a, b, dims=None):
    dims = dims or (((1,), (0,)), ((), ()))
    return lax.dot_general(a.astype(BF16), b.astype(BF16), dims, preferred_element_type=F32)


def _mm3(a, b):
    (a_hi, a_lo), (b_hi, b_lo) = _split_bf16(a), _split_bf16(b)
    return _mm(a_hi, b_hi) + (_mm(a_hi, b_lo) + _mm(a_lo, b_hi))


def _mm_onehot(a01, b):
    hi = b.astype(BF16)
    mid, lo = _split_bf16(b - hi.astype(F32))
    return _mm(a01, hi) + (_mm(a01, mid) + _mm(a01, lo))


def _head_sum(x, low):
    s0 = jnp.sum(jnp.where(low, x, 0.0), axis=1, keepdims=True)
    s1 = jnp.sum(jnp.where(low, 0.0, x), axis=1, keepdims=True)
    return jnp.where(low, s0, s1)


def _rwkv_kernel(r_ref, k_ref, v_ref, lo_ref, glo_ref, pp_ref, pl_ref, wup_ref, aup_ref, gup_ref, o_ref,
                 rs, ws, ks, vs, als, bes, gs, ys, bon, hs, rqs, ms, ns):
    S = r_ref.shape[1]
    C, RB = RWKV_CHUNK, RWKV_ROWS
    pp = pp_ref[...]
    mu_r, mu_k, mu_v, w0, a0, k_k, k_a, r_k, ln_g, ln_b = [pp[i:i + 1, :] for i in range(10)]
    mu_lo, mu_g = pl_ref[0:1, :], pl_ref[1:2, :]
    heads = lax.broadcasted_iota(jnp.int32, (RB, V7X_LANES), 1) < HEAD_DIM
    first = lax.broadcasted_iota(jnp.int32, (RB, V7X_LANES), 0) == 0

    def prologue(i):
        t0 = i * RB

        def shifted(ref, mu):
            x = ref[0, pl.ds(t0, RB), :]
            last = ref[0, pl.ds(t0 - 1, 1), :] if i > 0 else jnp.zeros((1, V7X_LANES), F32)
            prev = jnp.where(first, last, pltpu.roll(x, 1, axis=0))
            return x + (prev - x) * mu

        r, k, v = shifted(r_ref, mu_r), shifted(k_ref, mu_k), shifted(v_ref, mu_v)
        lo, glo = shifted(lo_ref, mu_lo), shifted(glo_ref, mu_g)
        wp = -(w0 + _mm(jnp.tanh(lo), wup_ref[...]))
        w = -(jnp.maximum(wp, 0.0) + jnp.log(1.0 + jnp.exp(-jnp.abs(wp)))) - 0.5
        a = jax.nn.sigmoid(a0 + _mm(lo, aup_ref[...]))
        kk = k * k_k
        kk = kk * lax.rsqrt(jnp.maximum(_head_sum(kk * kk, heads), 1e-24))
        k2 = k * (1.0 + (a - 1.0) * k_a)
        rs[pl.ds(t0, RB), :] = r
        ws[pl.ds(t0, RB), :] = -jnp.exp(w)
        ks[pl.ds(t0, RB), :] = k2
        vs[pl.ds(t0, RB), :] = v
        als[pl.ds(t0, RB), :] = -kk
        bes[pl.ds(t0, RB), :] = kk * a
        gs[pl.ds(t0, RB), :] = _mm(jax.nn.sigmoid(glo), gup_ref[...])
        bon[pl.ds(t0, RB), :] = _head_sum(r * k2 * r_k, heads) * v

    W2 = 2 * C
    row = lax.broadcasted_iota(jnp.int32, (W2, W2), 0)
    col = lax.broadcasted_iota(jnp.int32, (W2, W2), 1)
    t_idx, s_idx = row % C, col % C
    top, left = row < C, col < C
    same = top == left
    eye = jnp.where(row == col, 1.0, 0.0)
    tri = jnp.where(lax.broadcasted_iota(jnp.int32, (C, C), 1) <= lax.broadcasted_iota(jnp.int32, (C, C), 0), 1.0, 0.0)
    low_c = lax.broadcasted_iota(jnp.int32, (C, V7X_LANES), 1) < HEAD_DIM
    lower_left = lambda b: (same & (t_idx // (2 * b) == s_idx // (2 * b))
                            & ((t_idx // b) % 2 == 1) & ((s_idx // b) % 2 == 0))
    fold = lambda x: x[0:C] + x[C:W2]
    stack_heads = lambda x: jnp.concatenate([jnp.where(low_c, x, 0.0), jnp.where(low_c, 0.0, x)], axis=0)
    block_diag = lambda x: jnp.where(top, jnp.where(left, x, 0.0), jnp.where(left, 0.0, pltpu.roll(x, C, axis=1)))

    rows = lambda c: pl.ds(c * C if isinstance(c, int) else pl.multiple_of(c * C, C), C)

    def advance(c, H):
        ys[rows(c), :] += _mm3(rqs[c], H)
        return _mm3(ms[c], H) + ns[c]

    def transfers(i, lagged, side):
        each = lambda f, *xs: [f(*a) for a in zip(*xs)]
        cs = [i * RWKV_INTERLEAVE + u for u in range(RWKV_INTERLEAVE)]
        sls = [rows(c) for c in cs]
        state = [hs[...]] if lagged else None

        def lag(hook):
            if lagged:
                for u in range(hook * RWKV_INTERLEAVE // 8, (hook + 1) * RWKV_INTERLEAVE // 8):
                    state[0] = advance(cs[u] - RWKV_INTERLEAVE, state[0])
            if hook in side:
                side[hook]()

        r, lw, k2, v, al, be = ([ref[sl, :] for sl in sls] for ref in (rs, ws, ks, vs, als, bes))
        logp = each(lambda x: _mm_onehot(tri, x), lw)
        lag(0)
        P = each(jnp.exp, logp)
        Pinv = each(lambda x: jnp.exp(-x), logp)
        At = each(lambda a_, lp, w_: a_ * jnp.exp(lp - w_), al, logp, lw)
        Rt, Bt, Kt = each(jnp.multiply, r, P), each(jnp.multiply, be, Pinv), each(jnp.multiply, k2, Pinv)
        PC = each(lambda p: p[C - 1:C, :], P)
        A_bd, R_bd = each(stack_heads, At), each(stack_heads, Rt)
        Yt = each(lambda b, k: jnp.concatenate([b, k], axis=0), Bt, Kt)
        A1 = each(lambda a, y: jnp.where(s_idx < t_idx, _mm(a, y, dims=_NT), 0.0), A_bd, Yt)
        A2 = each(lambda a, y: jnp.where(s_idx <= t_idx, _mm(a, y, dims=_NT), 0.0), R_bd, Yt)
        Aab, Arb = each(block_diag, A1), each(block_diag, A2)
        T = each(lambda a: eye + jnp.where(lower_left(1), a, 0.0), Aab)
        for it in range(5):
            b = 2 << it
            P = each(lambda a, t: _mm(jnp.where(lower_left(b), a, 0.0), t), Aab, T)
            T = each(lambda t, p: t + _mm(t, p), T, P)
            lag(1 + it)
        V0 = each(lambda x: jnp.concatenate([jnp.zeros_like(x), x], axis=0), v)
        TA = each(_mm, T, A_bd)
        AkV = each(lambda a, x: jnp.where(same, _mm(a, x), 0.0), A1, V0)
        lag(6)
        U0 = each(_mm, T, AkV)
        lag(7)
        AR = each(lambda a, t, u: _mm(a, jnp.concatenate([t, u], axis=1)), Arb, TA, U0)
        ArkV = each(lambda a, x: jnp.where(same, _mm(a, x), 0.0), A2, V0)
        Mx = each(lambda b, p, t: _mm((b * p).T, fold(t)), Bt, PC, TA)
        Nx = each(lambda b, k, p, u, x: _mm(jnp.concatenate([b * p, k * p], axis=0).T,
                                            jnp.concatenate([fold(u), x], axis=0)), Bt, Kt, PC, U0, v)
        for u in range(RWKV_INTERLEAVE):
            ys[sls[u], :] = fold(AR[u][:, W2:2 * W2] + ArkV[u])
            rqs[cs[u]] = Rt[u] + fold(AR[u][:, 0:W2])
            ms[cs[u]] = eye * PC[u] + jnp.where(same, Mx[u], 0.0)
            ns[cs[u]] = jnp.where(same, Nx[u], 0.0)
        if lagged:
            hs[...] = state[0]

    def epilogue(i):
        sl = pl.ds(i * RB, RB)
        y = ys[sl, :]
        d = y - _head_sum(y, heads) * (1.0 / HEAD_DIM)
        var = _head_sum(d * d, heads) * (1.0 / HEAD_DIM)
        yn = d * lax.rsqrt(var + RWKV_GN_EPS) * ln_g + ln_b
        o_ref[0, sl, :] = ((yn + bon[sl, :]) * gs[sl, :]).astype(o_ref.dtype)

    n_groups = S // RB
    assert 8 % RWKV_INTERLEAVE == 0 and RB == RWKV_INTERLEAVE * C
    hs[...] = jnp.zeros((W2, W2), F32)
    prologue(0)
    for i in range(n_groups):
        side = {}
        if i + 1 < n_groups:
            side[1] = functools.partial(prologue, i + 1)
        if i >= 2:
            side[4] = functools.partial(epilogue, i - 2)
        transfers(i, i > 0, side)
    H = hs[...]
    for u in range(RWKV_INTERLEAVE):
        H = advance((n_groups - 1) * RWKV_INTERLEAVE + u, H)
        if u == RWKV_INTERLEAVE // 2 and n_groups >= 2:
            epilogue(n_groups - 2)
    epilogue(n_groups - 1)


def rwkv7_mixer(z, shift_mu, w0, w_up, a0, a_up, g_up, k_k, k_a, r_k, ln_g, ln_b):
    B, S, _ = z.shape
    CW = RWKV_HEADS * HEAD_DIM
    npair = CW // V7X_LANES
    base = 3 * CW // V7X_LANES
    lora = w_up.shape[0] + a_up.shape[0]
    assert lora == V7X_LANES and g_up.shape[0] == V7X_LANES and S % RWKV_ROWS == 0
    pp = jnp.stack([shift_mu[0:CW], shift_mu[CW:2 * CW], shift_mu[2 * CW:3 * CW], w0, a0, k_k, k_a,
                    r_k.reshape(CW), ln_g, ln_b])
    pp = jnp.pad(pp, ((0, 16 - pp.shape[0]), (0, 0)))
    pl2 = jnp.pad(shift_mu[3 * CW:].reshape(2, V7X_LANES), ((0, 6), (0, 0)))
    wup = jnp.pad(w_up, ((0, a_up.shape[0]), (0, 0)))
    aup = jnp.pad(a_up, ((w_up.shape[0], 0), (0, 0)))
    tile = lambda off: pl.BlockSpec((1, S, V7X_LANES), lambda b, p: (b, 0, base + off * npair + p))
    fixed = lambda off: pl.BlockSpec((1, S, V7X_LANES), lambda b, p: (b, 0, base + 3 * npair + off))
    seq = pltpu.VMEM((S, V7X_LANES), F32)
    return pl.pallas_call(
        _rwkv_kernel,
        grid=(B, npair),
        in_specs=[
            tile(0), tile(1), tile(2), fixed(0), fixed(1),
            pl.BlockSpec((16, V7X_LANES), lambda b, p: (0, p)),
            pl.BlockSpec((8, V7X_LANES), lambda b, p: (0, 0)),
            pl.BlockSpec((V7X_LANES, V7X_LANES), lambda b, p: (0, p)),
            pl.BlockSpec((V7X_LANES, V7X_LANES), lambda b, p: (0, p)),
            pl.BlockSpec((V7X_LANES, V7X_LANES), lambda b, p: (0, p)),
        ],
        out_specs=pl.BlockSpec((1, S, V7X_LANES), lambda b, p: (b, 0, p)),
        out_shape=jax.ShapeDtypeStruct((B, S, CW), BF16),
        scratch_shapes=[seq] * 9 + [
            pltpu.VMEM((V7X_LANES, V7X_LANES), F32),
            pltpu.VMEM((S // RWKV_CHUNK, RWKV_CHUNK, V7X_LANES), F32),
            pltpu.VMEM((S // RWKV_CHUNK, V7X_LANES, V7X_LANES), F32),
            pltpu.VMEM((S // RWKV_CHUNK, V7X_LANES, V7X_LANES), F32),
        ],
        compiler_params=_params("parallel", "parallel"),
        name="rwkv7_mixer",
    )(z, z, z, z, z, pp, pl2, wup, aup, g_up.astype(BF16))


RET_CHUNKS_PER_STEP = 4


def _ret_kernel(q_ref, k_ref, v_ref, g_ref, cos_ref, sin_ref, din_ref, dq_ref, dk_ref, dc_ref, o_ref, st_ref):
    S = q_ref.shape[1]
    C, DV = RET_CHUNK, RET_V_DIM
    lane = lax.broadcasted_iota(jnp.int32, (C, V7X_LANES), 1)
    first_half = (lane % RET_QK_DIM) < RET_QK_DIM // 2
    st_ref[...] = jnp.zeros_like(st_ref)

    in_head = [(lane >= h * RET_QK_DIM) & (lane < (h + 1) * RET_QK_DIM) for h in range(2)]
    NCH = RET_CHUNKS_PER_STEP
    units = [(u, h) for u in range(NCH) for h in range(2)]

    def step(i, carry):
        sls = [pl.ds(pl.multiple_of((i * NCH + u) * C, C), C) for u in range(NCH)]

        def rot(z, sl):
            swapped = jnp.where(first_half, pltpu.roll(z, V7X_LANES - RET_QK_DIM // 2, axis=1),
                                pltpu.roll(z, RET_QK_DIM // 2, axis=1))
            return z * cos_ref[sl, :] + swapped * sin_ref[sl, :]

        q = [rot(q_ref[0, sl, :], sl) for sl in sls]
        k = [rot(k_ref[0, sl, :], sl) * (RET_QK_DIM ** -0.5) for sl in sls]
        qm = [jnp.where(in_head[h], q[u], 0.0) for u, h in units]
        v = [v_ref[0, sls[u], h * DV:(h + 1) * DV] for u, h in units]
        inner = [_mm(qm[n], k[u], dims=_NT) * din_ref[h] for n, (u, h) in enumerate(units)]
        upd = [_mm((jnp.where(in_head[h], k[u], 0.0) * dk_ref[h]).T, v[n]) for n, (u, h) in enumerate(units)]
        local = [_mm(inner[n], v[n]) for n in range(len(units))]
        st = [st_ref[h] for h in range(2)]
        for n, (u, h) in enumerate(units):
            o = local[n] + _mm(qm[n], st[h]) * dq_ref[h]
            st[h] = upd[n] + dc_ref[h, 0:1, :] * st[h]
            d = o - jnp.mean(o, axis=1, keepdims=True)
            on = d * lax.rsqrt(jnp.mean(d * d, axis=1, keepdims=True) + RET_GN_EPS)
            gate = g_ref[0, sls[u], h * DV:(h + 1) * DV]
            o_ref[0, sls[u], h * DV:(h + 1) * DV] = (gate * jax.nn.sigmoid(gate) * on).astype(o_ref.dtype)
        st_ref[0], st_ref[1] = st
        return carry

    lax.fori_loop(0, S // C // NCH, step, 0)


def retention_mixer(z):
    B, S, _ = z.shape
    H, C, DK, DV = RET_HEADS, RET_CHUNK, RET_QK_DIM, RET_V_DIM
    assert S % C == 0 and 2 * DK == V7X_LANES and DV == V7X_LANES
    npair = H // 2
    half = DK // 2
    inv = ROPE_BASE ** (-jnp.arange(half, dtype=F32) / half)
    ang = jnp.arange(S, dtype=F32)[:, None] * inv
    cos = jnp.tile(jnp.cos(ang), (1, 4))
    sin = jnp.tile(jnp.concatenate([-jnp.sin(ang), jnp.sin(ang)], axis=1), (1, 2))
    log_g = jnp.asarray(np.log(1.0 - 2.0 ** (-5.0 - np.arange(H))), F32)
    n = jnp.arange(C, dtype=F32)
    diff = n[:, None] - n[None, :]
    d_in = jnp.where(diff >= 0, jnp.exp(jnp.maximum(diff, 0.0) * log_g[:, None, None]), 0.0)
    lanes = lambda t: jnp.broadcast_to(t[..., None], t.shape + (V7X_LANES,))
    d_q = lanes(jnp.exp((n + 1.0) * log_g[:, None]))
    d_k = lanes(jnp.exp((C - 1.0 - n) * log_g[:, None]))
    d_c = lanes(jnp.broadcast_to(jnp.exp(C * log_g)[:, None], (H, 8)))
    qk_tiles = H * DK // V7X_LANES
    return pl.pallas_call(
        _ret_kernel,
        grid=(B, npair),
        in_specs=[
            pl.BlockSpec((1, S, V7X_LANES), lambda b, p: (b, 0, p)),
            pl.BlockSpec((1, S, V7X_LANES), lambda b, p: (b, 0, qk_tiles + p)),
            pl.BlockSpec((1, S, 2 * DV), lambda b, p: (b, 0, 2 * qk_tiles * V7X_LANES // (2 * DV) + p)),
            pl.BlockSpec((1, S, 2 * DV), lambda b, p: (b, 0, (2 * qk_tiles * V7X_LANES + H * DV) // (2 * DV) + p)),
            pl.BlockSpec((S, V7X_LANES), lambda b, p: (0, 0)),
            pl.BlockSpec((S, V7X_LANES), lambda b, p: (0, 0)),
            pl.BlockSpec((2, C, C), lambda b, p: (p, 0, 0)),
            pl.BlockSpec((2, C, V7X_LANES), lambda b, p: (p, 0, 0)),
            pl.BlockSpec((2, C, V7X_LANES), lambda b, p: (p, 0, 0)),
            pl.BlockSpec((2, 8, V7X_LANES), lambda b, p: (p, 0, 0)),
        ],
        out_specs=pl.BlockSpec((1, S, 2 * DV), lambda b, p: (b, 0, p)),
        out_shape=jax.ShapeDtypeStruct((B, S, H * DV), BF16),
        scratch_shapes=[pltpu.VMEM((2, V7X_LANES, DV), F32)],
        compiler_params=_params("parallel", "parallel"),
        name="retention_mixer",
    )(z, z, z, z, cos, sin, d_in, d_q, d_k, d_c)


def _even_mixer(x, g_norm, w_in, shift_mu, w0, w_up, a0, a_up, g_up, k_k, k_a, r_k, ln_g, ln_b):
    B, S, D = x.shape
    z = norm_matmul(x.reshape(B * S, D), g_norm, w_in.astype(BF16)).reshape(B, S, -1)
    o_a = moba_attention(z)
    o_b = rwkv7_mixer(z, shift_mu, w0, w_up, a0, a_up, g_up, k_k, k_a, r_k, ln_g, ln_b)
    return o_a, o_b


def _odd_mixer(x, g_norm, w_in, pe_k, w1_k, w2_k, pe_v, w1_v, w2_v):
    B, S, D = x.shape
    perm, col = _odd_layout()
    n_in = w_in.shape[1]
    real = perm[perm < n_in]
    cuts = [0] + [k for k in range(1, len(real)) if real[k] != real[k - 1] + 1] + [len(real)]
    w_b = w_in.astype(BF16)
    w_p = jnp.concatenate([w_b[:, int(real[a]):int(real[b - 1]) + 1] for a, b in zip(cuts[:-1], cuts[1:])]
                          + [jnp.zeros((w_in.shape[0], len(perm) - len(real)), BF16)], axis=1)
    z = norm_matmul(x.reshape(B * S, D), g_norm, w_p).reshape(B, S, -1)
    o_c = retention_mixer(z)
    cmp_kv, cmp_vk = nsa_compress(z, col["kc"], col["vc"], pe_k, w1_k, w2_k, pe_v, w1_v, w2_v)
    o_d = nsa_attention(z, cmp_kv, cmp_vk, col["nq"], col["slc"], col["win"], col["gate"])
    return o_c, o_d


def _odd_layout():
    G, Dh = NSA_KV_GROUPS, HEAD_DIM
    sizes = (RET_HEADS * RET_QK_DIM, RET_HEADS * RET_QK_DIM, RET_HEADS * RET_V_DIM, RET_HEADS * RET_V_DIM,
             NSA_HEADS * Dh) + (G * Dh,) * 6 + (3 * NSA_HEADS,)
    off = np.concatenate([[0], np.cumsum(sizes)])
    rq, rk, rv, rg, nq, kc, vc, ks, vs, kw, vw, ng = off[:-1]
    n_in = int(off[-1])
    pair = lambda a, b: np.concatenate([np.concatenate([a + g * Dh + np.arange(Dh), b + g * Dh + np.arange(Dh)])
                                        for g in range(G)])
    perm = np.concatenate([np.arange(ks), pair(ks, vs), pair(kw, vw), ng + np.arange(3 * NSA_HEADS)])
    n_pad = -(-len(perm) // (6 * V7X_MXU_DIM)) * 6 * V7X_MXU_DIM
    perm = np.concatenate([perm, np.full(n_pad - len(perm), n_in)]).astype(np.int32)
    col = {"nq": int(nq), "kc": int(kc), "vc": int(vc), "slc": int(ks), "win": int(ks) + 2 * G * Dh,
           "gate": int(ks) + 4 * G * Dh}
    return perm, col


def kernel(x, mix_norm, ffn_norm, even_w_in, even_shift_mu, even_w0, even_w_up, even_a0, even_a_up, even_g_up, even_k_k, even_k_a, even_r_k, even_ln_g, even_ln_b, even_w_out, odd_w_in, odd_cmp_pe_k, odd_cmp_w1_k, odd_cmp_w2_k, odd_cmp_pe_v, odd_cmp_w1_v, odd_cmp_w2_v, odd_w_out, ffn_w1, ffn_w3, ffn_w2, final_norm):
    B, S, D = x.shape
    depth = mix_norm.shape[0]
    w1, w3, w2 = ffn_w1.astype(BF16), ffn_w3.astype(BF16), ffn_w2.astype(BF16)
    for layer in range(depth):
        i = layer // 2
        if layer % 2 == 0:
            o1, o2 = _even_mixer(x, mix_norm[layer], even_w_in[i], even_shift_mu[i], even_w0[i], even_w_up[i],
                                 even_a0[i], even_a_up[i], even_g_up[i], even_k_k[i], even_k_a[i], even_r_k[i],
                                 even_ln_g[i], even_ln_b[i])
            w_out = even_w_out[i]
        else:
            o1, o2 = _odd_mixer(x, mix_norm[layer], odd_w_in[i], odd_cmp_pe_k[i], odd_cmp_w1_k[i], odd_cmp_w2_k[i],
                                odd_cmp_pe_v[i], odd_cmp_w1_v[i], odd_cmp_w2_v[i])
            w_out = odd_w_out[i]
        T = B * S
        x2 = mix_ffn_residual(o1.reshape(T, -1), o2.reshape(T, -1), w_out.astype(BF16), x.reshape(T, D),
                              ffn_norm[layer], w1, w3, w2, layer, final_norm if layer == depth - 1 else None)
        x = x2.reshape(B, S, D)
    return x
```

```python
import functools

import jax
import jax.numpy as jnp
import numpy as np
from jax import lax
from jax.experimental import pallas as pl
from jax.experimental.pallas import tpu as pltpu

F32 = jnp.float32
BF16 = jnp.bfloat16

V7X_LANES = 128
V7X_MXU_DIM = 256
V7X_VMEM_BYTES = 64 * 1024 * 1024
VMEM_LIMIT = V7X_VMEM_BYTES * 7 // 8

NORM_EPS = 1e-6
HEAD_DIM = 64

MOBA_BLOCK = 256
MOBA_TOPK = 3
RWKV_HEADS = 16
RWKV_GN_EPS = 6.4e-4

RET_HEADS = 8
RET_QK_DIM = 64
RET_V_DIM = 128
RET_CHUNK = 128
RET_GN_EPS = 1e-6
ROPE_BASE = 10000.0
NSA_HEADS = 16
NSA_KV_GROUPS = 4
NSA_CMP_BLOCK = 32
NSA_CMP_STRIDE = 16
NSA_SLC_BLOCK = 64
NSA_SLC_TOPN = 16
NSA_WINDOW = 512


def _params(*semantics):
    return pltpu.CompilerParams(dimension_semantics=semantics, vmem_limit_bytes=VMEM_LIMIT)


def _rms(x, g):
    return x * lax.rsqrt(jnp.mean(x * x, axis=-1, keepdims=True) + NORM_EPS) * g


def _norm_matmul_kernel(x_ref, g_ref, w_ref, o_ref):
    x = x_ref[...]
    scale = lax.rsqrt(jnp.mean(x * x, axis=-1, keepdims=True) + NORM_EPS)
    o_ref[...] = jnp.dot((x * g_ref[...]).astype(BF16), w_ref[...], preferred_element_type=F32) * scale


def _proj_tile(n):
    assert n % V7X_MXU_DIM == 0
    k = n // V7X_MXU_DIM
    return V7X_MXU_DIM * max(d for d in range(1, 7) if k % d == 0)


def norm_matmul(x, g, w, *, tm=512):
    T, D = x.shape
    N = w.shape[1]
    tn = _proj_tile(N)
    assert T % tm == 0 and N % tn == 0
    return pl.pallas_call(
        _norm_matmul_kernel,
        grid=(N // tn, T // tm),
        in_specs=[
            pl.BlockSpec((tm, D), lambda j, i: (i, 0)),
            pl.BlockSpec((1, D), lambda j, i: (0, 0)),
            pl.BlockSpec((D, tn), lambda j, i: (0, j)),
        ],
        out_specs=pl.BlockSpec((tm, tn), lambda j, i: (i, j)),
        out_shape=jax.ShapeDtypeStruct((T, N), F32),
        compiler_params=pltpu.CompilerParams(dimension_semantics=("parallel", "parallel"),
                                             vmem_limit_bytes=VMEM_LIMIT, allow_input_fusion=[False, False, True]),
        name="norm_matmul",
    )(x, g.reshape(1, D), w)


def _mix_ffn_kernel(a_ref, b_ref, wa_ref, wb_ref, x_ref, g_ref, w1_ref, w3_ref, w2_ref, gf_ref, o_ref,
                    h_ref, acc_ref, *, final_norm):
    j = pl.program_id(1)

    @pl.when(j == 0)
    def _():
        x2 = (x_ref[...] + jnp.dot(a_ref[...], wa_ref[...], preferred_element_type=F32)
              + jnp.dot(b_ref[...], wb_ref[...], preferred_element_type=F32))
        o_ref[...] = x2
        h_ref[...] = _rms(x2, g_ref[...]).astype(BF16)
        acc_ref[...] = jnp.zeros_like(acc_ref)

    h = h_ref[...]
    a = jnp.dot(h, w1_ref[...], preferred_element_type=F32)
    b = jnp.dot(h, w3_ref[...], preferred_element_type=F32)
    act = (a * jax.nn.sigmoid(a) * b).astype(BF16)
    acc_ref[...] += jnp.dot(act, w2_ref[...], preferred_element_type=F32)

    @pl.when(j == pl.num_programs(1) - 1)
    def _():
        y = o_ref[...] + acc_ref[...]
        if final_norm:
            y = _rms(y, gf_ref[...])
        o_ref[...] = y


def mix_ffn_residual(a, b, w_out, x, g, w1, w3, w2, layer, g_final=None, *, tm=512, tf=512):
    T, D = x.shape
    K = a.shape[1]
    Fh = w1.shape[2]
    assert T % tm == 0 and Fh % tf == 0 and b.shape == a.shape and w_out.shape == (2 * K, D)
    final_norm = g_final is not None
    gf = (g_final if final_norm else g).reshape(1, D)
    once = pl.Buffered(1)
    return pl.pallas_call(
        functools.partial(_mix_ffn_kernel, final_norm=final_norm),
        grid=(T // tm, Fh // tf),
        in_specs=[
            pl.BlockSpec((tm, K), lambda i, j: (i, 0)),
            pl.BlockSpec((tm, K), lambda i, j: (i, 0)),
            pl.BlockSpec((K, D), lambda i, j: (0, 0), pipeline_mode=once),
            pl.BlockSpec((K, D), lambda i, j: (1, 0), pipeline_mode=once),
            pl.BlockSpec((tm, D), lambda i, j: (i, 0)),
            pl.BlockSpec((1, D), lambda i, j: (0, 0), pipeline_mode=once),
            pl.BlockSpec((None, D, tf), lambda i, j: (layer, 0, j)),
            pl.BlockSpec((None, D, tf), lambda i, j: (layer, 0, j)),
            pl.BlockSpec((None, tf, D), lambda i, j: (layer, j, 0)),
            pl.BlockSpec((1, D), lambda i, j: (0, 0), pipeline_mode=once),
        ],
        out_specs=pl.BlockSpec((tm, D), lambda i, j: (i, 0)),
        out_shape=jax.ShapeDtypeStruct((T, D), F32),
        scratch_shapes=[pltpu.VMEM((tm, D), BF16), pltpu.VMEM((tm, D), F32)],
        compiler_params=_params("parallel", "arbitrary"),
        name="mix_ffn_residual",
    )(a, b, w_out, w_out, x, g.reshape(1, D), w1, w3, w2, gf)


NEG_BIG = -1e30
_NT = (((1,), (1,)), ((), ()))


def _flash_steps(qas, kas, vas, masks, m_prev, acc_prev):
    hs = range(len(qas))
    s = [lax.dot_general(qas[h], kas[h], _NT, preferred_element_type=F32) for h in hs]
    s = [s[h] if masks[h] is None else jnp.where(masks[h], s[h], NEG_BIG) for h in hs]
    m_new = [jnp.maximum(m_prev[h], jnp.max(s[h], axis=1, keepdims=True)) for h in hs]
    alpha = [jnp.exp2(m_prev[h] - m_new[h]) for h in hs]
    p = [jnp.exp2(s[h] - m_new[h]) for h in hs]
    pv = [jnp.dot(p[h].astype(BF16), vas[h], preferred_element_type=F32) for h in hs]
    return m_new, [alpha[h] * acc_prev[h] + pv[h] for h in hs]


def _augment_q(q_log2, in_head, keep_t, odd):
    nblk, tq = keep_t.shape
    bias_t = jnp.where(keep_t, 0.0, NEG_BIG)
    bias = jnp.concatenate([bias_t, jnp.zeros((V7X_LANES - nblk, tq), F32)], axis=0).T
    if not odd:
        bias = pltpu.roll(bias, HEAD_DIM, axis=1)
    return jnp.where(in_head, q_log2, bias).astype(BF16)


def _key_value_tiles(k, v, blk, lane):
    low = lane < HEAD_DIM
    hot_e = 0.0 if blk is None else jnp.where(lane - HEAD_DIM == blk, 1.0, 0.0)
    hot_o = 0.0 if blk is None else jnp.where(lane == blk, 1.0, 0.0)
    k_e, k_o = jnp.where(low, k, hot_e), jnp.where(low, hot_o, k)
    v_e, v_o = jnp.where(low, v, jnp.where(lane == HEAD_DIM, 1.0, 0.0)), jnp.where(low, jnp.where(lane == 0, 1.0, 0.0), v)
    return [t.astype(BF16) for t in (k_e, k_o, v_e, v_o)]


def _normalise(acc, lane, odd):
    return acc / jnp.sum(jnp.where(lane == (0 if odd else HEAD_DIM), acc, 0.0), axis=1, keepdims=True)


LOG2E = 1.4426950408889634


MOBA_HEADS_PER_STEP = 8


def _split_bf16(x):
    hi = x.astype(BF16)
    return hi, (x - hi.astype(F32)).astype(BF16)


def _moba_kernel(q_ref, k_ref, v_ref, o_ref, ka_ref, va_ref, km_ref, acc_ref):
    L = MOBA_BLOCK
    S = k_ref.shape[1]
    nb = S // L
    HP = MOBA_HEADS_PER_STEP
    qi = pl.program_id(2)
    lane = lax.broadcasted_iota(jnp.int32, (L, V7X_LANES), 1)
    lanes_of = lambda ref, pp: ref[0, :, pp * V7X_LANES:(pp + 1) * V7X_LANES]

    @pl.when(qi == 0)
    def _():
        lane_s = lax.broadcasted_iota(jnp.int32, (S, V7X_LANES), 1)
        blk = lax.broadcasted_iota(jnp.int32, (S, V7X_LANES), 0) // L
        for pp in range(HP // 2):
            k = lanes_of(k_ref, pp)
            (ka_ref[2 * pp], ka_ref[2 * pp + 1], va_ref[2 * pp], va_ref[2 * pp + 1]) = _key_value_tiles(
                k, lanes_of(v_ref, pp), blk, lane_s)
            km_ref[pp] = jnp.concatenate(_split_bf16(jnp.mean(k.reshape(nb, L, V7X_LANES), axis=1)), axis=0)

    row = lax.broadcasted_iota(jnp.int32, (L, L), 0)
    col = lax.broadcasted_iota(jnp.int32, (L, L), 1)
    causal = col <= row
    jrow = lax.broadcasted_iota(jnp.int32, (nb, L), 0)
    past = jrow < qi
    qas = []
    for pp in range(HP // 2):
        q = lanes_of(q_ref, pp) * (HEAD_DIM ** -0.5)
        q_hi, q_lo = _split_bf16(q)
        km = km_ref[pp]
        for e in range(2):
            in_head = (lane >= e * HEAD_DIM) & (lane < (e + 1) * HEAD_DIM)
            zero = jnp.zeros_like(q_hi)
            g1 = lax.dot_general(km, jnp.where(in_head, q_hi, zero), _NT, preferred_element_type=F32)
            g2 = lax.dot_general(km, jnp.where(in_head, q_lo, zero), _NT, preferred_element_type=F32)
            gate = g1[0:nb] + g1[nb:2 * nb] + g2[0:nb]
            keep = jrow == qi
            for n in range(nb):
                g_n = gate[n:n + 1, :]
                beats = (gate > g_n) | ((gate == g_n) & (jrow < n))
                rank = jnp.sum(jnp.where(past & beats, 1.0, 0.0), axis=0, keepdims=True)
                keep = keep | ((jrow == n) & (rank < MOBA_TOPK) & past)
            qas.append(_augment_q(q * LOG2E, in_head, keep, odd=e == 1))

    tiles = lambda start: ([ka_ref[h, pl.ds(start, L), :] for h in range(HP)],
                           [va_ref[h, pl.ds(start, L), :] for h in range(HP)])

    m, acc = _flash_steps(qas, *tiles(pl.multiple_of(qi * L, L)), [causal] * HP,
                          [jnp.full((L, 1), NEG_BIG, F32)] * HP, [jnp.zeros((L, V7X_LANES), F32)] * HP)
    for h in range(HP):
        acc_ref[h] = acc[h]

    def body(n, carry):
        m2, acc2 = _flash_steps(qas, *tiles(pl.multiple_of(n * L, L)), [None] * HP, list(carry),
                                [acc_ref[h] for h in range(HP)])
        for h in range(HP):
            acc_ref[h] = acc2[h]
        return tuple(m2)

    lax.fori_loop(0, qi, body, tuple(m))
    for pp in range(HP // 2):
        o_ref[0, :, pp * V7X_LANES:(pp + 1) * V7X_LANES] = jnp.where(
            lane < HEAD_DIM, _normalise(acc_ref[2 * pp], lane, False), _normalise(acc_ref[2 * pp + 1], lane, True)
        ).astype(o_ref.dtype)


def moba_attention(z, *, n_heads=16):
    B, S, _ = z.shape
    L, HP = MOBA_BLOCK, MOBA_HEADS_PER_STEP
    W = HP * HEAD_DIM
    nb = S // L
    assert S % L == 0 and n_heads % HP == 0 and W % V7X_LANES == 0 and 2 * nb <= 16
    ngrp = n_heads // HP
    return pl.pallas_call(
        _moba_kernel,
        grid=(B, ngrp, S // L),
        in_specs=[
            pl.BlockSpec((1, L, W), lambda b, p, i: (b, i, p)),
            pl.BlockSpec((1, S, W), lambda b, p, i: (b, 0, ngrp + p)),
            pl.BlockSpec((1, S, W), lambda b, p, i: (b, 0, 2 * ngrp + p)),
        ],
        out_specs=pl.BlockSpec((1, L, W), lambda b, p, i: (b, i, p)),
        out_shape=jax.ShapeDtypeStruct((B, S, n_heads * HEAD_DIM), BF16),
        scratch_shapes=[
            pltpu.VMEM((HP, S, V7X_LANES), BF16),
            pltpu.VMEM((HP, S, V7X_LANES), BF16),
            pltpu.VMEM((HP // 2, 2 * nb, V7X_LANES), BF16),
            pltpu.VMEM((HP, L, V7X_LANES), F32),
        ],
        compiler_params=_params("parallel", "parallel", "arbitrary"),
        name="moba_attention",
    )(z, z, z)


NSA_TQ = 256
NSA_GROUPS_PER_STEP = 2
BIG = 3.0e38


def _gelu_tanh(x):
    return 0.5 * x * (1.0 + jnp.tanh(0.7978845608028654 * (x + 0.044715 * x * x * x)))


def _nsa_compress_kernel(xk0_ref, xk1_ref, xv0_ref, xv1_ref, pek_ref, pev_ref, w1k_ref, w1v_ref, w2k_ref, w2v_ref,
                         o1_ref, o2_ref):
    G, Lc, st = NSA_KV_GROUPS, NSA_CMP_BLOCK, NSA_CMP_STRIDE
    nrow = xk0_ref.shape[1] // st
    lane = lax.broadcasted_iota(jnp.int32, (nrow, G * HEAD_DIM), 1)

    def hidden(x_refs, pe_ref, w1_ref):
        acc = [jnp.zeros((G * nrow, V7X_LANES), F32) for _ in range(Lc // st)]
        for l in range(Lc):
            u, m = divmod(l, st)
            x = jnp.concatenate([r[0, pl.ds(m, nrow, stride=st), :] for r in x_refs], axis=1) + pe_ref[l:l + 1, :]
            xs = jnp.concatenate(
                [jnp.where((lane >= g * HEAD_DIM) & (lane < (g + 1) * HEAD_DIM), x, 0.0) for g in range(G)],
                axis=0).astype(BF16)
            acc[u] = acc[u] + jnp.dot(xs, w1_ref[l], preferred_element_type=F32)
        nxt = jnp.concatenate([pltpu.roll(acc[1][g * nrow:(g + 1) * nrow], nrow - 1, axis=0) for g in range(G)],
                              axis=0)
        return _gelu_tanh(acc[0] + nxt).astype(BF16)

    hk = hidden((xk0_ref, xk1_ref), pek_ref, w1k_ref)
    hv = hidden((xv0_ref, xv1_ref), pev_ref, w1v_ref)
    kc = jnp.dot(hk, w2k_ref[...], preferred_element_type=F32)
    vc = jnp.dot(hv, w2v_ref[...], preferred_element_type=F32)
    kv = kc + vc
    vk = pltpu.roll(kv, HEAD_DIM, axis=1)
    for g in range(G):
        o1_ref[0, :, g * V7X_LANES:(g + 1) * V7X_LANES] = kv[g * nrow:(g + 1) * nrow]
        o2_ref[0, :, g * V7X_LANES:(g + 1) * V7X_LANES] = vk[g * nrow:(g + 1) * nrow]


def nsa_compress(z, col_k, col_v, pe_k, w1_k, w2_k, pe_v, w1_v, w2_v):
    B, S, _ = z.shape
    G, Lc, st = NSA_KV_GROUPS, NSA_CMP_BLOCK, NSA_CMP_STRIDE
    GW = G * HEAD_DIM
    nrow = S // st
    hid = w1_k.shape[1]
    assert hid == V7X_LANES and col_k % GW == 0 and col_v % GW == 0
    tile_pe = lambda pe: jnp.tile(pe, (1, G))
    tile_w1 = lambda w: jnp.tile(w.reshape(Lc, 1, HEAD_DIM, hid), (1, G, 1, 1)).reshape(Lc, GW, hid).astype(BF16)
    w2k = jnp.pad(w2_k, ((0, 0), (0, HEAD_DIM))).astype(BF16)
    w2v = jnp.pad(w2_v, ((0, 0), (HEAD_DIM, 0))).astype(BF16)
    const = lambda shape: pl.BlockSpec(shape, lambda b: (0,) * len(shape))
    out = jax.ShapeDtypeStruct((B, nrow, G * V7X_LANES), F32)
    return pl.pallas_call(
        _nsa_compress_kernel,
        grid=(B,),
        in_specs=[
            pl.BlockSpec((1, S, V7X_LANES), lambda b: (b, 0, col_k // V7X_LANES)),
            pl.BlockSpec((1, S, V7X_LANES), lambda b: (b, 0, col_k // V7X_LANES + 1)),
            pl.BlockSpec((1, S, V7X_LANES), lambda b: (b, 0, col_v // V7X_LANES)),
            pl.BlockSpec((1, S, V7X_LANES), lambda b: (b, 0, col_v // V7X_LANES + 1)),
            const((Lc, GW)), const((Lc, GW)),
            const((Lc, GW, hid)), const((Lc, GW, hid)),
            const((hid, V7X_LANES)), const((hid, V7X_LANES)),
        ],
        out_specs=[pl.BlockSpec((1, nrow, G * V7X_LANES), lambda b: (b, 0, 0))] * 2,
        out_shape=[out, out],
        compiler_params=_params("parallel"),
        name="nsa_compress",
    )(z, z, z, z, tile_pe(pe_k), tile_pe(pe_v), tile_w1(w1_k), tile_w1(w1_v), w2k, w2v)


def _nsa_kernel(q_ref, c1_ref, c2_ref, s_ref, w_ref, g_ref, ovt_ref, o_ref, sk_ref, sv_ref, wk_ref, wv_ref, acc_ref):
    TQ = NSA_TQ
    S = s_ref.shape[1]
    R = NSA_HEADS // NSA_KV_GROUPS
    NG = NSA_GROUPS_PER_STEP
    NH = NG * R
    grp0 = pl.program_id(1) * NG
    qi = pl.program_id(2)
    lane = lax.broadcasted_iota(jnp.int32, (TQ, V7X_LANES), 1)
    tile_of = lambda ref, t: ref[0, :, t * V7X_LANES:(t + 1) * V7X_LANES]

    @pl.when(qi == 0)
    def _():
        lane_s = lax.broadcasted_iota(jnp.int32, (S, V7X_LANES), 1)
        blk = lax.broadcasted_iota(jnp.int32, (S, V7X_LANES), 0) // NSA_SLC_BLOCK
        for gg in range(NG):
            for src, k_ref, v_ref, hot in ((s_ref, sk_ref, sv_ref, blk), (w_ref, wk_ref, wv_ref, None)):
                kv = tile_of(src, gg)
                vk = pltpu.roll(kv, HEAD_DIM, axis=1)
                k_ref[2 * gg], k_ref[2 * gg + 1], v_ref[2 * gg], v_ref[2 * gg + 1] = _key_value_tiles(
                    jnp.where(lane_s < HEAD_DIM, kv, vk), jnp.where(lane_s < HEAD_DIM, vk, kv), hot, lane_s)

    q0 = pl.multiple_of(qi * TQ, TQ)
    row = lax.broadcasted_iota(jnp.int32, (TQ, TQ), 0)
    col = lax.broadcasted_iota(jnp.int32, (TQ, TQ), 1)
    causal = col <= row
    t_abs = q0 + lax.broadcasted_iota(jnp.int32, (TQ, V7X_LANES), 0)
    even_lanes = lane < HEAD_DIM

    cmask = lane * NSA_CMP_STRIDE + (NSA_CMP_BLOCK - 1) <= t_abs
    nblk = s_ref.shape[1] // NSA_SLC_BLOCK
    jrow = lax.broadcasted_iota(jnp.int32, (nblk, TQ), 0)
    own = (q0 + lax.broadcasted_iota(jnp.int32, (nblk, TQ), 1)) // NSA_SLC_BLOCK
    ovt = ovt_ref[...]
    head_lanes = [even_lanes if h % 2 == 0 else ~even_lanes for h in range(NH)]
    tiles = [tile_of(q_ref, h // 2) * (HEAD_DIM ** -0.5) for h in range(NH)]
    o_cmp, qas = [], []
    for gg in range(NG):
        c_kv_b, c_vk_b = tile_of(c1_ref, gg).astype(BF16), tile_of(c2_ref, gg).astype(BF16)
        p_sum = jnp.zeros((TQ, V7X_LANES), F32)
        for h in range(gg * R, (gg + 1) * R):
            qm = jnp.where(head_lanes[h], tiles[h], 0.0).astype(BF16)
            s = lax.dot_general(qm, c_kv_b if h % 2 == 0 else c_vk_b, _NT, preferred_element_type=F32)
            s = jnp.where(cmask, s, NEG_BIG)
            p = jnp.where(cmask, jnp.exp(s - jnp.max(s, axis=1, keepdims=True)), 0.0)
            den = jnp.sum(p, axis=1, keepdims=True)
            p = p / jnp.where(den > 0.0, den, 1.0)
            p_sum = p_sum + p
            o_cmp.append(jnp.dot(p.astype(BF16), c_vk_b if h % 2 == 0 else c_kv_b, preferred_element_type=F32))

        p_hi, p_lo = _split_bf16(p_sum)
        p_slc = (lax.dot_general(ovt, p_hi, _NT, preferred_element_type=F32)
                 + lax.dot_general(ovt, p_lo, _NT, preferred_element_type=F32))[0:nblk]
        score = jnp.where((jrow == own) | (jrow == 0), BIG, jnp.where(jrow > own, -BIG, p_slc))
        keep = jrow > nblk
        for j in range(nblk):
            s_j = score[j:j + 1, :]
            beats = (score > s_j) | ((score == s_j) & (jrow < j))
            rank = jnp.sum(jnp.where(beats, 1.0, 0.0), axis=0, keepdims=True)
            keep = keep | ((jrow == j) & (rank < NSA_SLC_TOPN) & (jrow <= own))
        qas += [_augment_q(tiles[h] * LOG2E, head_lanes[h], keep, odd=h % 2 == 1) for h in range(gg * R, (gg + 1) * R)]

    neg = [jnp.full((TQ, 1), NEG_BIG, F32)] * NH
    zacc = [jnp.zeros((TQ, V7X_LANES), F32)] * NH
    kv_index = [2 * (h // R) + h % 2 for h in range(NH)]

    def kv_blocks(k_ref, v_ref, start):
        return ([k_ref[kv_index[h], pl.ds(start, TQ), :] for h in range(NH)],
                [v_ref[kv_index[h], pl.ds(start, TQ), :] for h in range(NH)])

    m, acc = _flash_steps(qas, *kv_blocks(sk_ref, sv_ref, q0), [causal] * NH, neg, zacc)
    for h in range(NH):
        acc_ref[h] = acc[h]

    def body(kb, carry):
        m2, acc2 = _flash_steps(qas, *kv_blocks(sk_ref, sv_ref, pl.multiple_of(kb * TQ, TQ)), [None] * NH,
                                list(carry), [acc_ref[h] for h in range(NH)])
        for h in range(NH):
            acc_ref[h] = acc2[h]
        return tuple(m2)

    lax.fori_loop(0, qi, body, tuple(m))

    WK = NSA_WINDOW + TQ
    w0 = pl.multiple_of(jnp.maximum(qi - NSA_WINDOW // TQ, 0) * TQ, TQ)
    key_pos = w0 + lax.broadcasted_iota(jnp.int32, (TQ, WK), 1)
    t_win = q0 + lax.broadcasted_iota(jnp.int32, (TQ, WK), 0)
    in_window = (key_pos <= t_win) & (key_pos > t_win - NSA_WINDOW)
    _, acc = _flash_steps(qas, [wk_ref[kv_index[h], pl.ds(w0, WK), :] for h in range(NH)],
                          [wv_ref[kv_index[h], pl.ds(w0, WK), :] for h in range(NH)], [in_window] * NH, neg, zacc)

    gates = jax.nn.sigmoid(g_ref[0])
    outs = []
    for h in range(NH):
        o_slc = _normalise(acc_ref[h], lane, h % 2 == 1)
        o_win = _normalise(acc[h], lane, h % 2 == 1)
        c0 = (grp0 * R + h) * 3
        gate = lambda c: jnp.sum(jnp.where(lane == c, gates, 0.0), axis=1, keepdims=True)
        outs.append(gate(c0) * o_cmp[h] + gate(c0 + 1) * o_slc + gate(c0 + 2) * o_win)
    for p2 in range(NH // 2):
        o_ref[0, :, p2 * V7X_LANES:(p2 + 1) * V7X_LANES] = jnp.where(
            even_lanes, outs[2 * p2], outs[2 * p2 + 1]).astype(o_ref.dtype)


def nsa_attention(z, cmp_kv, cmp_vk, col_q, col_slc, col_win, col_gate):
    B, S, _ = z.shape
    G, TQ, NG = NSA_KV_GROUPS, NSA_TQ, NSA_GROUPS_PER_STEP
    R = NSA_HEADS // G
    QW = NG * R * HEAD_DIM
    KW = NG * V7X_LANES
    ncmp = cmp_kv.shape[1]
    assert S % TQ == 0 and ncmp == V7X_LANES and S // NSA_SLC_BLOCK <= V7X_LANES
    assert NSA_WINDOW % TQ == 0 and S >= NSA_WINDOW + TQ
    assert G % NG == 0 and R % 2 == 0
    assert col_q % QW == 0 and col_slc % KW == 0 and col_win % KW == 0 and col_gate % V7X_LANES == 0
    nc = (S - NSA_CMP_BLOCK) // NSA_CMP_STRIDE + 1
    c_start = np.arange(V7X_LANES) * NSA_CMP_STRIDE
    s_start = np.arange(V7X_LANES) * NSA_SLC_BLOCK
    overlap = ((c_start[:, None] <= s_start[None, :] + NSA_SLC_BLOCK - 1)
               & (c_start[:, None] + NSA_CMP_BLOCK - 1 >= s_start[None, :])
               & (np.arange(V7X_LANES)[:, None] < nc) & (np.arange(V7X_LANES)[None, :] < S // NSA_SLC_BLOCK))
    const = lambda shape: pl.BlockSpec(shape, lambda b, g, i: (0,) * len(shape))
    return pl.pallas_call(
        _nsa_kernel,
        grid=(B, G // NG, S // TQ),
        in_specs=[
            pl.BlockSpec((1, TQ, QW), lambda b, g, i: (b, i, col_q // QW + g)),
            pl.BlockSpec((1, ncmp, KW), lambda b, g, i: (b, 0, g)),
            pl.BlockSpec((1, ncmp, KW), lambda b, g, i: (b, 0, g)),
            pl.BlockSpec((1, S, KW), lambda b, g, i: (b, 0, col_slc // KW + g)),
            pl.BlockSpec((1, S, KW), lambda b, g, i: (b, 0, col_win // KW + g)),
            pl.BlockSpec((1, TQ, V7X_LANES), lambda b, g, i: (b, i, col_gate // V7X_LANES)),
            const((V7X_LANES, V7X_LANES)),
        ],
        out_specs=pl.BlockSpec((1, TQ, QW), lambda b, g, i: (b, i, g)),
        out_shape=jax.ShapeDtypeStruct((B, S, NSA_HEADS * HEAD_DIM), BF16),
        scratch_shapes=[pltpu.VMEM((2 * NG, S, V7X_LANES), BF16)] * 4 + [
            pltpu.VMEM((NG * R, TQ, V7X_LANES), F32),
        ],
        compiler_params=_params("parallel", "parallel", "arbitrary"),
        name="nsa_attention",
    )(z, cmp_kv, cmp_vk, z, z, z, jnp.asarray(overlap.T, BF16))


RWKV_CHUNK = 64
RWKV_ROWS = 512
RWKV_INTERLEAVE = 8


def _mm(a, b, dims=None):
    dims = dims or (((1,), (0,)), ((), ()))
    return lax.dot_general(a.astype(BF16), b.astype(BF16), dims, preferred_element_type=F32)


def _mm3(a, b):
    (a_hi, a_lo), (b_hi, b_lo) = _split_bf16(a), _split_bf16(b)
    return _mm(a_hi, b_hi) + (_mm(a_hi, b_lo) + _mm(a_lo, b_hi))


def _mm_onehot(a01, b):
    hi = b.astype(BF16)
    mid, lo = _split_bf16(b - hi.astype(F32))
    return _mm(a01, hi) + (_mm(a01, mid) + _mm(a01, lo))


def _head_sum(x, low):
    s0 = jnp.sum(jnp.where(low, x, 0.0), axis=1, keepdims=True)
    s1 = jnp.sum(jnp.where(low, 0.0, x), axis=1, keepdims=True)
    return jnp.where(low, s0, s1)


def _rwkv_kernel(r_ref, k_ref, v_ref, lo_ref, glo_ref, pp_ref, pl_ref, wup_ref, aup_ref, gup_ref, o_ref,
                 rs, ws, ks, vs, als, bes, gs, ys, bon, hs, rqs, ms, ns):
    S = r_ref.shape[1]
    C, RB = RWKV_CHUNK, RWKV_ROWS
    pp = pp_ref[...]
    mu_r, mu_k, mu_v, w0, a0, k_k, k_a, r_k, ln_g, ln_b = [pp[i:i + 1, :] for i in range(10)]
    mu_lo, mu_g = pl_ref[0:1, :], pl_ref[1:2, :]
    heads = lax.broadcasted_iota(jnp.int32, (RB, V7X_LANES), 1) < HEAD_DIM
    first = lax.broadcasted_iota(jnp.int32, (RB, V7X_LANES), 0) == 0

    def prologue(i):
        t0 = i * RB

        def shifted(ref, mu):
            x = ref[0, pl.ds(t0, RB), :]
            last = ref[0, pl.ds(t0 - 1, 1), :] if i > 0 else jnp.zeros((1, V7X_LANES), F32)
            prev = jnp.where(first, last, pltpu.roll(x, 1, axis=0))
            return x + (prev - x) * mu

        r, k, v = shifted(r_ref, mu_r), shifted(k_ref, mu_k), shifted(v_ref, mu_v)
        lo, glo = shifted(lo_ref, mu_lo), shifted(glo_ref, mu_g)
        wp = -(w0 + _mm(jnp.tanh(lo), wup_ref[...]))
        w = -(jnp.maximum(wp, 0.0) + jnp.log(1.0 + jnp.exp(-jnp.abs(wp)))) - 0.5
        a = jax.nn.sigmoid(a0 + _mm(lo, aup_ref[...]))
        kk = k * k_k
        kk = kk * lax.rsqrt(jnp.maximum(_head_sum(kk * kk, heads), 1e-24))
        k2 = k * (1.0 + (a - 1.0) * k_a)
        rs[pl.ds(t0, RB), :] = r
        ws[pl.ds(t0, RB), :] = -jnp.exp(w)
        ks[pl.ds(t0, RB), :] = k2
        vs[pl.ds(t0, RB), :] = v
        als[pl.ds(t0, RB), :] = -kk
        bes[pl.ds(t0, RB), :] = kk * a
        gs[pl.ds(t0, RB), :] = _mm(jax.nn.sigmoid(glo), gup_ref[...])
        bon[pl.ds(t0, RB), :] = _head_sum(r * k2 * r_k, heads) * v

    W2 = 2 * C
    row = lax.broadcasted_iota(jnp.int32, (W2, W2), 0)
    col = lax.broadcasted_iota(jnp.int32, (W2, W2), 1)
    t_idx, s_idx = row % C, col % C
    top, left = row < C, col < C
    same = top == left
    eye = jnp.where(row == col, 1.0, 0.0)
    tri = jnp.where(lax.broadcasted_iota(jnp.int32, (C, C), 1) <= lax.broadcasted_iota(jnp.int32, (C, C), 0), 1.0, 0.0)
    low_c = lax.broadcasted_iota(jnp.int32, (C, V7X_LANES), 1) < HEAD_DIM
    lower_left = lambda b: (same & (t_idx // (2 * b) == s_idx // (2 * b))
                            & ((t_idx // b) % 2 == 1) & ((s_idx // b) % 2 == 0))
    fold = lambda x: x[0:C] + x[C:W2]
    stack_heads = lambda x: jnp.concatenate([jnp.where(low_c, x, 0.0), jnp.where(low_c, 0.0, x)], axis=0)
    block_diag = lambda x: jnp.where(top, jnp.where(left, x, 0.0), jnp.where(left, 0.0, pltpu.roll(x, C, axis=1)))

    rows = lambda c: pl.ds(c * C if isinstance(c, int) else pl.multiple_of(c * C, C), C)

    def advance(c, H):
        ys[rows(c), :] += _mm3(rqs[c], H)
        return _mm3(ms[c], H) + ns[c]

    def transfers(i, lagged, side):
        each = lambda f, *xs: [f(*a) for a in zip(*xs)]
        cs = [i * RWKV_INTERLEAVE + u for u in range(RWKV_INTERLEAVE)]
        sls = [rows(c) for c in cs]
        state = [hs[...]] if lagged else None

        def lag(hook):
            if lagged:
                for u in range(hook * RWKV_INTERLEAVE // 8, (hook + 1) * RWKV_INTERLEAVE // 8):
                    state[0] = advance(cs[u] - RWKV_INTERLEAVE, state[0])
            if hook in side:
                side[hook]()

        r, lw, k2, v, al, be = ([ref[sl, :] for sl in sls] for ref in (rs, ws, ks, vs, als, bes))
        logp = each(lambda x: _mm_onehot(tri, x), lw)
        lag(0)
        P = each(jnp.exp, logp)
        Pinv = each(lambda x: jnp.exp(-x), logp)
        At = each(lambda a_, lp, w_: a_ * jnp.exp(lp - w_), al, logp, lw)
        Rt, Bt, Kt = each(jnp.multiply, r, P), each(jnp.multiply, be, Pinv), each(jnp.multiply, k2, Pinv)
        PC = each(lambda p: p[C - 1:C, :], P)
        A_bd, R_bd = each(stack_heads, At), each(stack_heads, Rt)
        Yt = each(lambda b, k: jnp.concatenate([b, k], axis=0), Bt, Kt)
        A1 = each(lambda a, y: jnp.where(s_idx < t_idx, _mm(a, y, dims=_NT), 0.0), A_bd, Yt)
        A2 = each(lambda a, y: jnp.where(s_idx <= t_idx, _mm(a, y, dims=_NT), 0.0), R_bd, Yt)
        Aab, Arb = each(block_diag, A1), each(block_diag, A2)
        T = each(lambda a: eye + jnp.where(lower_left(1), a, 0.0), Aab)
        for it in range(5):
            b = 2 << it
            P = each(lambda a, t: _mm(jnp.where(lower_left(b), a, 0.0), t), Aab, T)
            T = each(lambda t, p: t + _mm(t, p), T, P)
            lag(1 + it)
        V0 = each(lambda x: jnp.concatenate([jnp.zeros_like(x), x], axis=0), v)
        TA = each(_mm, T, A_bd)
        AkV = each(lambda a, x: jnp.where(same, _mm(a, x), 0.0), A1, V0)
        lag(6)
        U0 = each(_mm, T, AkV)
        lag(7)
        AR = each(lambda a, t, u: _mm(a, jnp.concatenate([t, u], axis=1)), Arb, TA, U0)
        ArkV = each(lambda a, x: jnp.where(same, _mm(a, x), 0.0), A2, V0)
        Mx = each(lambda b, p, t: _mm((b * p).T, fold(t)), Bt, PC, TA)
        Nx = each(lambda b, k, p, u, x: _mm(jnp.concatenate([b * p, k * p], axis=0).T,
                                            jnp.concatenate([fold(u), x], axis=0)), Bt, Kt, PC, U0, v)
        for u in range(RWKV_INTERLEAVE):
            ys[sls[u], :] = fold(AR[u][:, W2:2 * W2] + ArkV[u])
            rqs[cs[u]] = Rt[u] + fold(AR[u][:, 0:W2])
            ms[cs[u]] = eye * PC[u] + jnp.where(same, Mx[u], 0.0)
            ns[cs[u]] = jnp.where(same, Nx[u], 0.0)
        if lagged:
            hs[...] = state[0]

    def epilogue(i):
        sl = pl.ds(i * RB, RB)
        y = ys[sl, :]
        d = y - _head_sum(y, heads) * (1.0 / HEAD_DIM)
        var = _head_sum(d * d, heads) * (1.0 / HEAD_DIM)
        yn = d * lax.rsqrt(var + RWKV_GN_EPS) * ln_g + ln_b
        o_ref[0, sl, :] = ((yn + bon[sl, :]) * gs[sl, :]).astype(o_ref.dtype)

    n_groups = S // RB
    assert 8 % RWKV_INTERLEAVE == 0 and RB == RWKV_INTERLEAVE * C
    hs[...] = jnp.zeros((W2, W2), F32)
    prologue(0)
    for i in range(n_groups):
        side = {}
        if i + 1 < n_groups:
            side[1] = functools.partial(prologue, i + 1)
        if i >= 2:
            side[4] = functools.partial(epilogue, i - 2)
        transfers(i, i > 0, side)
    H = hs[...]
    for u in range(RWKV_INTERLEAVE):
        H = advance((n_groups - 1) * RWKV_INTERLEAVE + u, H)
        if u == RWKV_INTERLEAVE // 2 and n_groups >= 2:
            epilogue(n_groups - 2)
    epilogue(n_groups - 1)


def rwkv7_mixer(z, shift_mu, w0, w_up, a0, a_up, g_up, k_k, k_a, r_k, ln_g, ln_b):
    B, S, _ = z.shape
    CW = RWKV_HEADS * HEAD_DIM
    npair = CW // V7X_LANES
    base = 3 * CW // V7X_LANES
    lora = w_up.shape[0] + a_up.shape[0]
    assert lora == V7X_LANES and g_up.shape[0] == V7X_LANES and S % RWKV_ROWS == 0
    pp = jnp.stack([shift_mu[0:CW], shift_mu[CW:2 * CW], shift_mu[2 * CW:3 * CW], w0, a0, k_k, k_a,
                    r_k.reshape(CW), ln_g, ln_b])
    pp = jnp.pad(pp, ((0, 16 - pp.shape[0]), (0, 0)))
    pl2 = jnp.pad(shift_mu[3 * CW:].reshape(2, V7X_LANES), ((0, 6), (0, 0)))
    wup = jnp.pad(w_up, ((0, a_up.shape[0]), (0, 0)))
    aup = jnp.pad(a_up, ((w_up.shape[0], 0), (0, 0)))
    tile = lambda off: pl.BlockSpec((1, S, V7X_LANES), lambda b, p: (b, 0, base + off * npair + p))
    fixed = lambda off: pl.BlockSpec((1, S, V7X_LANES), lambda b, p: (b, 0, base + 3 * npair + off))
    seq = pltpu.VMEM((S, V7X_LANES), F32)
    return pl.pallas_call(
        _rwkv_kernel,
        grid=(B, npair),
        in_specs=[
            tile(0), tile(1), tile(2), fixed(0), fixed(1),
            pl.BlockSpec((16, V7X_LANES), lambda b, p: (0, p)),
            pl.BlockSpec((8, V7X_LANES), lambda b, p: (0, 0)),
            pl.BlockSpec((V7X_LANES, V7X_LANES), lambda b, p: (0, p)),
            pl.BlockSpec((V7X_LANES, V7X_LANES), lambda b, p: (0, p)),
            pl.BlockSpec((V7X_LANES, V7X_LANES), lambda b, p: (0, p)),
        ],
        out_specs=pl.BlockSpec((1, S, V7X_LANES), lambda b, p: (b, 0, p)),
        out_shape=jax.ShapeDtypeStruct((B, S, CW), BF16),
        scratch_shapes=[seq] * 9 + [
            pltpu.VMEM((V7X_LANES, V7X_LANES), F32),
            pltpu.VMEM((S // RWKV_CHUNK, RWKV_CHUNK, V7X_LANES), F32),
            pltpu.VMEM((S // RWKV_CHUNK, V7X_LANES, V7X_LANES), F32),
            pltpu.VMEM((S // RWKV_CHUNK, V7X_LANES, V7X_LANES), F32),
        ],
        compiler_params=_params("parallel", "parallel"),
        name="rwkv7_mixer",
    )(z, z, z, z, z, pp, pl2, wup, aup, g_up.astype(BF16))


RET_CHUNKS_PER_STEP = 4


def _ret_kernel(q_ref, k_ref, v_ref, g_ref, cos_ref, sin_ref, din_ref, dq_ref, dk_ref, dc_ref, o_ref, st_ref):
    S = q_ref.shape[1]
    C, DV = RET_CHUNK, RET_V_DIM
    lane = lax.broadcasted_iota(jnp.int32, (C, V7X_LANES), 1)
    first_half = (lane % RET_QK_DIM) < RET_QK_DIM // 2
    st_ref[...] = jnp.zeros_like(st_ref)

    in_head = [(lane >= h * RET_QK_DIM) & (lane < (h + 1) * RET_QK_DIM) for h in range(2)]
    NCH = RET_CHUNKS_PER_STEP
    units = [(u, h) for u in range(NCH) for h in range(2)]

    def step(i, carry):
        sls = [pl.ds(pl.multiple_of((i * NCH + u) * C, C), C) for u in range(NCH)]

        def rot(z, sl):
            swapped = jnp.where(first_half, pltpu.roll(z, V7X_LANES - RET_QK_DIM // 2, axis=1),
                                pltpu.roll(z, RET_QK_DIM // 2, axis=1))
            return z * cos_ref[sl, :] + swapped * sin_ref[sl, :]

        q = [rot(q_ref[0, sl, :], sl) for sl in sls]
        k = [rot(k_ref[0, sl, :], sl) * (RET_QK_DIM ** -0.5) for sl in sls]
        qm = [jnp.where(in_head[h], q[u], 0.0) for u, h in units]
        v = [v_ref[0, sls[u], h * DV:(h + 1) * DV] for u, h in units]
        inner = [_mm(qm[n], k[u], dims=_NT) * din_ref[h] for n, (u, h) in enumerate(units)]
        upd = [_mm((jnp.where(in_head[h], k[u], 0.0) * dk_ref[h]).T, v[n]) for n, (u, h) in enumerate(units)]
        local = [_mm(inner[n], v[n]) for n in range(len(units))]
        st = [st_ref[h] for h in range(2)]
        for n, (u, h) in enumerate(units):
            o = local[n] + _mm(qm[n], st[h]) * dq_ref[h]
            st[h] = upd[n] + dc_ref[h, 0:1, :] * st[h]
            d = o - jnp.mean(o, axis=1, keepdims=True)
            on = d * lax.rsqrt(jnp.mean(d * d, axis=1, keepdims=True) + RET_GN_EPS)
            gate = g_ref[0, sls[u], h * DV:(h + 1) * DV]
            o_ref[0, sls[u], h * DV:(h + 1) * DV] = (gate * jax.nn.sigmoid(gate) * on).astype(o_ref.dtype)
        st_ref[0], st_ref[1] = st
        return carry

    lax.fori_loop(0, S // C // NCH, step, 0)


def retention_mixer(z):
    B, S, _ = z.shape
    H, C, DK, DV = RET_HEADS, RET_CHUNK, RET_QK_DIM, RET_V_DIM
    assert S % C == 0 and 2 * DK == V7X_LANES and DV == V7X_LANES
    npair = H // 2
    half = DK // 2
    inv = ROPE_BASE ** (-jnp.arange(half, dtype=F32) / half)
    ang = jnp.arange(S, dtype=F32)[:, None] * inv
    cos = jnp.tile(jnp.cos(ang), (1, 4))
    sin = jnp.tile(jnp.concatenate([-jnp.sin(ang), jnp.sin(ang)], axis=1), (1, 2))
    log_g = jnp.asarray(np.log(1.0 - 2.0 ** (-5.0 - np.arange(H))), F32)
    n = jnp.arange(C, dtype=F32)
    diff = n[:, None] - n[None, :]
    d_in = jnp.where(diff >= 0, jnp.exp(jnp.maximum(diff, 0.0) * log_g[:, None, None]), 0.0)
    lanes = lambda t: jnp.broadcast_to(t[..., None], t.shape + (V7X_LANES,))
    d_q = lanes(jnp.exp((n + 1.0) * log_g[:, None]))
    d_k = lanes(jnp.exp((C - 1.0 - n) * log_g[:, None]))
    d_c = lanes(jnp.broadcast_to(jnp.exp(C * log_g)[:, None], (H, 8)))
    qk_tiles = H * DK // V7X_LANES
    return pl.pallas_call(
        _ret_kernel,
        grid=(B, npair),
        in_specs=[
            pl.BlockSpec((1, S, V7X_LANES), lambda b, p: (b, 0, p)),
            pl.BlockSpec((1, S, V7X_LANES), lambda b, p: (b, 0, qk_tiles + p)),
            pl.BlockSpec((1, S, 2 * DV), lambda b, p: (b, 0, 2 * qk_tiles * V7X_LANES // (2 * DV) + p)),
            pl.BlockSpec((1, S, 2 * DV), lambda b, p: (b, 0, (2 * qk_tiles * V7X_LANES + H * DV) // (2 * DV) + p)),
            pl.BlockSpec((S, V7X_LANES), lambda b, p: (0, 0)),
            pl.BlockSpec((S, V7X_LANES), lambda b, p: (0, 0)),
            pl.BlockSpec((2, C, C), lambda b, p: (p, 0, 0)),
            pl.BlockSpec((2, C, V7X_LANES), lambda b, p: (p, 0, 0)),
            pl.BlockSpec((2, C, V7X_LANES), lambda b, p: (p, 0, 0)),
            pl.BlockSpec((2, 8, V7X_LANES), lambda b, p: (p, 0, 0)),
        ],
        out_specs=pl.BlockSpec((1, S, 2 * DV), lambda b, p: (b, 0, p)),
        out_shape=jax.ShapeDtypeStruct((B, S, H * DV), BF16),
        scratch_shapes=[pltpu.VMEM((2, V7X_LANES, DV), F32)],
        compiler_params=_params("parallel", "parallel"),
        name="retention_mixer",
    )(z, z, z, z, cos, sin, d_in, d_q, d_k, d_c)


def _even_mixer(x, g_norm, w_in, shift_mu, w0, w_up, a0, a_up, g_up, k_k, k_a, r_k, ln_g, ln_b):
    B, S, D = x.shape
    z = norm_matmul(x.reshape(B * S, D), g_norm, w_in.astype(BF16)).reshape(B, S, -1)
    o_a = moba_attention(z)
    o_b = rwkv7_mixer(z, shift_mu, w0, w_up, a0, a_up, g_up, k_k, k_a, r_k, ln_g, ln_b)
    return o_a, o_b


def _odd_mixer(x, g_norm, w_in, pe_k, w1_k, w2_k, pe_v, w1_v, w2_v):
    B, S, D = x.shape
    perm, col = _odd_layout()
    n_in = w_in.shape[1]
    real = perm[perm < n_in]
    cuts = [0] + [k for k in range(1, len(real)) if real[k] != real[k - 1] + 1] + [len(real)]
    w_b = w_in.astype(BF16)
    w_p = jnp.concatenate([w_b[:, int(real[a]):int(real[b - 1]) + 1] for a, b in zip(cuts[:-1], cuts[1:])]
                          + [jnp.zeros((w_in.shape[0], len(perm) - len(real)), BF16)], axis=1)
    z = norm_matmul(x.reshape(B * S, D), g_norm, w_p).reshape(B, S, -1)
    o_c = retention_mixer(z)
    cmp_kv, cmp_vk = nsa_compress(z, col["kc"], col["vc"], pe_k, w1_k, w2_k, pe_v, w1_v, w2_v)
    o_d = nsa_attention(z, cmp_kv, cmp_vk, col["nq"], col["slc"], col["win"], col["gate"])
    return o_c, o_d


def _odd_layout():
    G, Dh = NSA_KV_GROUPS, HEAD_DIM
    sizes = (RET_HEADS * RET_QK_DIM, RET_HEADS * RET_QK_DIM, RET_HEADS * RET_V_DIM, RET_HEADS * RET_V_DIM,
             NSA_HEADS * Dh) + (G * Dh,) * 6 + (3 * NSA_HEADS,)
    off = np.concatenate([[0], np.cumsum(sizes)])
    rq, rk, rv, rg, nq, kc, vc, ks, vs, kw, vw, ng = off[:-1]
    n_in = int(off[-1])
    pair = lambda a, b: np.concatenate([np.concatenate([a + g * Dh + np.arange(Dh), b + g * Dh + np.arange(Dh)])
                                        for g in range(G)])
    perm = np.concatenate([np.arange(ks), pair(ks, vs), pair(kw, vw), ng + np.arange(3 * NSA_HEADS)])
    n_pad = -(-len(perm) // (6 * V7X_MXU_DIM)) * 6 * V7X_MXU_DIM
    perm = np.concatenate([perm, np.full(n_pad - len(perm), n_in)]).astype(np.int32)
    col = {"nq": int(nq), "kc": int(kc), "vc": int(vc), "slc": int(ks), "win": int(ks) + 2 * G * Dh,
           "gate": int(ks) + 4 * G * Dh}
    return perm, col


def kernel(x, mix_norm, ffn_norm, even_w_in, even_shift_mu, even_w0, even_w_up, even_a0, even_a_up, even_g_up, even_k_k, even_k_a, even_r_k, even_ln_g, even_ln_b, even_w_out, odd_w_in, odd_cmp_pe_k, odd_cmp_w1_k, odd_cmp_w2_k, odd_cmp_pe_v, odd_cmp_w1_v, odd_cmp_w2_v, odd_w_out, ffn_w1, ffn_w3, ffn_w2, final_norm):
    B, S, D = x.shape
    depth = mix_norm.shape[0]
    w1, w3, w2 = ffn_w1.astype(BF16), ffn_w3.astype(BF16), ffn_w2.astype(BF16)
    for layer in range(depth):
        i = layer // 2
        if layer % 2 == 0:
            o1, o2 = _even_mixer(x, mix_norm[layer], even_w_in[i], even_shift_mu[i], even_w0[i], even_w_up[i],
                                 even_a0[i], even_a_up[i], even_g_up[i], even_k_k[i], even_k_a[i], even_r_k[i],
                                 even_ln_g[i], even_ln_b[i])
            w_out = even_w_out[i]
        else:
            o1, o2 = _odd_mixer(x, mix_norm[layer], odd_w_in[i], odd_cmp_pe_k[i], odd_cmp_w1_k[i], odd_cmp_w2_k[i],
                                odd_cmp_pe_v[i], odd_cmp_w1_v[i], odd_cmp_w2_v[i])
            w_out = odd_w_out[i]
        T = B * S
        x2 = mix_ffn_residual(o1.reshape(T, -1), o2.reshape(T, -1), w_out.astype(BF16), x.reshape(T, D),
                              ffn_norm[layer], w1, w3, w2, layer, final_norm if layer == depth - 1 else None)
        x = x2.reshape(B, S, D)
    return x
```
